```python
import jax, jax.numpy as jnp
from jax import lax
import numpy as np

D_MODEL = 1024
BATCH = 8
SEQ = 8192
DEPTH = 2

D_MIX = D_MODEL
NORM_EPS = 1e-6
ATT_HEADS = 4
QK_NOPE_DIM = 128
QK_ROPE_DIM = 64
V_HEAD_DIM = 128
ATT_WIDTH = ATT_HEADS * V_HEAD_DIM
Q_LORA_RANK = D_MODEL // 4
KV_LORA_RANK = D_MODEL // 8
ROPE_THETA = 10000.0
Q_BLOCK = 128
CONV_WIDTH = D_MIX // 4
CONV_KERNEL = 31
SGU_WIDTH = D_MIX // 4
SGU_GROUPS = 4
SGU_GROUP_DIM = SGU_WIDTH // SGU_GROUPS
SGU_CHUNK = 128
IN_SPLITS = (Q_LORA_RANK, KV_LORA_RANK, QK_ROPE_DIM, ATT_WIDTH,
             CONV_WIDTH, CONV_WIDTH, CONV_WIDTH,
             SGU_WIDTH, SGU_WIDTH, SGU_WIDTH)
D_IN = Q_LORA_RANK + KV_LORA_RANK + QK_ROPE_DIM + ATT_WIDTH + 3 * CONV_WIDTH + 3 * SGU_WIDTH

kernel_name = "hymba_style_mla_conformer_sgu_hybrid"


def rms_norm(x, g):
    xf = x.astype(jnp.float32)
    y = xf * lax.rsqrt(jnp.mean(xf * xf, axis=-1, keepdims=True) + NORM_EPS)
    return (y * g.astype(jnp.float32)).astype(x.dtype)


def layer_norm(x, g, b):
    xf = x.astype(jnp.float32)
    mu = jnp.mean(xf, axis=-1, keepdims=True)
    var = jnp.mean(jnp.square(xf - mu), axis=-1, keepdims=True)
    y = (xf - mu) * lax.rsqrt(var + NORM_EPS)
    return (y * g.astype(jnp.float32) + b.astype(jnp.float32)).astype(x.dtype)


def apply_rope(x, cos, sin):
    half = x.shape[-1] // 2
    x1, x2 = x[..., :half], x[..., half:]
    return jnp.concatenate([x1 * cos - x2 * sin, x2 * cos + x1 * sin], axis=-1)


def mla_mixer(q_lat, kv_lat, k_rope, positions, q_norm_g, w_uq, kv_norm_g, w_ukv):
    B, S, _ = q_lat.shape
    q = (rms_norm(q_lat, q_norm_g) @ w_uq).reshape(B, S, ATT_HEADS, QK_NOPE_DIM + QK_ROPE_DIM)
    q_nope, q_rope = q[..., :QK_NOPE_DIM], q[..., QK_NOPE_DIM:]
    kv = (rms_norm(kv_lat, kv_norm_g) @ w_ukv).reshape(B, S, ATT_HEADS, QK_NOPE_DIM + V_HEAD_DIM)
    k_nope, v = kv[..., :QK_NOPE_DIM], kv[..., QK_NOPE_DIM:]
    inv_freq = ROPE_THETA ** (-jnp.arange(0, QK_ROPE_DIM, 2, dtype=jnp.float32) / QK_ROPE_DIM)
    ang = positions.astype(jnp.float32)[..., None] * inv_freq
    cos, sin = jnp.cos(ang).astype(q.dtype), jnp.sin(ang).astype(q.dtype)
    q_rope = apply_rope(q_rope, cos[:, :, None, :], sin[:, :, None, :])
    k_rope = apply_rope(k_rope, cos, sin)
    scale = (QK_NOPE_DIM + QK_ROPE_DIM) ** -0.5
    nb = S // Q_BLOCK
    qn_blocks = q_nope.reshape(B, nb, Q_BLOCK, ATT_HEADS, QK_NOPE_DIM).transpose(1, 0, 2, 3, 4)
    qr_blocks = q_rope.reshape(B, nb, Q_BLOCK, ATT_HEADS, QK_ROPE_DIM).transpose(1, 0, 2, 3, 4)
    k_pos = jnp.arange(S)

    def one_block(args):
        qn, qr, blk = args
        s = (jnp.einsum('bqhd,bkhd->bhqk', qn, k_nope)
             + jnp.einsum('bqhr,bkr->bhqk', qr, k_rope)).astype(jnp.float32) * scale
        q_pos = blk * Q_BLOCK + jnp.arange(Q_BLOCK)
        s = jnp.where(k_pos[None, :] <= q_pos[:, None], s, -jnp.inf)
        p = jax.nn.softmax(s, axis=-1).astype(v.dtype)
        return jnp.einsum('bhqk,bkhd->bqhd', p, v)

    out = lax.map(one_block, (qn_blocks, qr_blocks, jnp.arange(nb)))
    return out.transpose(1, 0, 2, 3, 4).reshape(B, S, ATT_WIDTH)


def conv_mixer(a, b, conv_w, conv_b, ln_g, ln_b, w_pw2):
    h = a * jax.nn.sigmoid(b)
    h = lax.conv_general_dilated(
        h, conv_w[:, None, :].astype(h.dtype), window_strides=(1,),
        padding=[(CONV_KERNEL - 1, 0)],
        dimension_numbers=('NWC', 'WIO', 'NWC'),
        feature_group_count=CONV_WIDTH) + conv_b
    h = jax.nn.silu(layer_norm(h, ln_g, ln_b))
    return h @ w_pw2


def sgu_mixer(u, v, ln_g, ln_b, w_s, b_s):
    u = jax.nn.gelu(u, approximate=False)
    v = layer_norm(jax.nn.gelu(v, approximate=False), ln_g, ln_b)
    B, S, _ = v.shape
    nc = S // SGU_CHUNK
    v = v.reshape(B, nc, SGU_CHUNK, SGU_GROUPS, SGU_GROUP_DIM)
    causal = jnp.tril(jnp.ones((SGU_CHUNK, SGU_CHUNK), dtype=bool))
    w = jnp.where(causal[None], w_s, jnp.zeros_like(w_s))
    sv = jnp.einsum('gts,bnsgc->bntgc', w, v) + b_s.T[None, None, :, :, None]
    return u * sv.reshape(B, S, SGU_WIDTH)


def hybrid_layer(x, c, positions, w_ada, b_ada, g_pre, g_post, w_in, q_norm_g, w_uq,
                 kv_norm_g, w_ukv, conv_w, conv_b, conv_ln_g, conv_ln_b, w_pw2,
                 sgu_ln_g, sgu_ln_b, w_s, b_s, w_out):
    mod = jax.nn.silu(c) @ w_ada + b_ada
    shift, scale, gate = jnp.split(mod, 3, axis=-1)
    h = rms_norm(x, g_pre) * (1 + scale[:, None, :]) + shift[:, None, :]
    z = h @ w_in
    (q_lat, kv_lat, k_rope, g_att, conv_a, conv_b_in, g_conv,
     sgu_u, sgu_v, g_sgu) = jnp.split(z, np.cumsum(IN_SPLITS)[:-1].tolist(), axis=-1)
    y_att = mla_mixer(q_lat, kv_lat, k_rope, positions, q_norm_g, w_uq, kv_norm_g, w_ukv)
    y_conv = conv_mixer(conv_a, conv_b_in, conv_w, conv_b, conv_ln_g, conv_ln_b, w_pw2)
    y_sgu = sgu_mixer(sgu_u, sgu_v, sgu_ln_g, sgu_ln_b, w_s, b_s)
    y = jnp.concatenate([y_att * jax.nn.silu(g_att),
                         y_conv * jax.nn.silu(g_conv),
                         y_sgu * jax.nn.silu(g_sgu)], axis=-1) @ w_out
    return x + gate[:, None, :] * rms_norm(y, g_post)


def _fwd_setup_inputs(seed: int = 0) -> dict:
    key = jax.random.key(seed)
    ks = jax.random.split(key, 32)
    f32 = jnp.float32
    L = DEPTH

    def nrm(k, shape, s):
        return jax.random.normal(k, shape, f32) * s

    def gain(k, shape):
        return 1.0 + 0.02 * jax.random.normal(k, shape, f32)

    b_ada = jnp.concatenate([nrm(ks[3], (L, D_MODEL), 0.02),
                             nrm(ks[4], (L, D_MODEL), 0.02),
                             gain(ks[5], (L, D_MODEL))], axis=-1)
    return {
        "x": jax.random.normal(ks[0], (BATCH, SEQ, D_MODEL), f32),
        "c": jax.random.normal(ks[1], (BATCH, D_MODEL), f32),
        "positions": jnp.tile(jnp.arange(SEQ, dtype=jnp.int32)[None, :], (BATCH, 1)),
        "w_ada": nrm(ks[2], (L, D_MODEL, 3 * D_MODEL), 0.1 * D_MODEL ** -0.5),
        "b_ada": b_ada,
        "g_pre": gain(ks[6], (L, D_MODEL)),
        "g_post": gain(ks[7], (L, D_MODEL)),
        "w_in": nrm(ks[8], (L, D_MODEL, D_IN), D_MODEL ** -0.5),
        "q_norm_g": gain(ks[9], (L, Q_LORA_RANK)),
        "w_uq": nrm(ks[10], (L, Q_LORA_RANK, ATT_HEADS * (QK_NOPE_DIM + QK_ROPE_DIM)), Q_LORA_RANK ** -0.5),
        "kv_norm_g": gain(ks[11], (L, KV_LORA_RANK)),
        "w_ukv": nrm(ks[12], (L, KV_LORA_RANK, ATT_HEADS * (QK_NOPE_DIM + V_HEAD_DIM)), KV_LORA_RANK ** -0.5),
        "conv_w": nrm(ks[13], (L, CONV_KERNEL, CONV_WIDTH), CONV_KERNEL ** -0.5),
        "conv_b": nrm(ks[14], (L, CONV_WIDTH), 0.02),
        "conv_ln_g": gain(ks[15], (L, CONV_WIDTH)),
        "conv_ln_b": nrm(ks[16], (L, CONV_WIDTH), 0.02),
        "w_pw2": nrm(ks[17], (L, CONV_WIDTH, CONV_WIDTH), CONV_WIDTH ** -0.5),
        "sgu_ln_g": gain(ks[18], (L, SGU_WIDTH)),
        "sgu_ln_b": nrm(ks[19], (L, SGU_WIDTH), 0.02),
        "w_s": nrm(ks[20], (L, SGU_GROUPS, SGU_CHUNK, SGU_CHUNK), SGU_CHUNK ** -0.5),
        "b_s": gain(ks[21], (L, SGU_GROUPS, SGU_CHUNK)),
        "w_out": nrm(ks[22], (L, D_MIX, D_MODEL), D_MIX ** -0.5),
    }


def _fwd_reference(x, c, positions, w_ada, b_ada, g_pre, g_post, w_in, q_norm_g, w_uq,
              kv_norm_g, w_ukv, conv_w, conv_b, conv_ln_g, conv_ln_b, w_pw2,
              sgu_ln_g, sgu_ln_b, w_s, b_s, w_out):
    for l in range(DEPTH):
        x = hybrid_layer(x, c, positions, w_ada[l], b_ada[l], g_pre[l], g_post[l], w_in[l],
                         q_norm_g[l], w_uq[l], kv_norm_g[l], w_ukv[l], conv_w[l], conv_b[l],
                         conv_ln_g[l], conv_ln_b[l], w_pw2[l], sgu_ln_g[l], sgu_ln_b[l],
                         w_s[l], b_s[l], w_out[l])
    return x


import jax as _jax
import jax.numpy as _jnp

TWIN_FORMAT = 'train_step'
FWD_PARAMS = ['x', 'c', 'positions', 'w_ada', 'b_ada', 'g_pre', 'g_post', 'w_in', 'q_norm_g', 'w_uq', 'kv_norm_g', 'w_ukv', 'conv_w', 'conv_b', 'conv_ln_g', 'conv_ln_b', 'w_pw2', 'sgu_ln_g', 'sgu_ln_b', 'w_s', 'b_s', 'w_out']
TWIN_WEIGHTS = ['w_ada', 'b_ada', 'g_pre', 'g_post', 'w_in', 'q_norm_g', 'w_uq', 'kv_norm_g', 'w_ukv', 'conv_w', 'conv_b', 'conv_ln_g', 'conv_ln_b', 'w_pw2', 'sgu_ln_g', 'sgu_ln_b', 'w_s', 'b_s', 'w_out']
TWIN_DIFF_INPUT = 'x'
TWIN_INPUTS = ['x', 'c', 'positions', 'w_ada', 'b_ada', 'g_pre', 'g_post', 'w_in', 'q_norm_g', 'w_uq', 'kv_norm_g', 'w_ukv', 'conv_w', 'conv_b', 'conv_ln_g', 'conv_ln_b', 'w_pw2', 'sgu_ln_g', 'sgu_ln_b', 'w_s', 'b_s', 'w_out', 'loss_target', 'm_w_ada', 'm_b_ada', 'm_g_pre', 'm_g_post', 'm_w_in', 'm_q_norm_g', 'm_w_uq', 'm_kv_norm_g', 'm_w_ukv', 'm_conv_w', 'm_conv_b', 'm_conv_ln_g', 'm_conv_ln_b', 'm_w_pw2', 'm_sgu_ln_g', 'm_sgu_ln_b', 'm_w_s', 'm_b_s', 'm_w_out', 'v_w_ada', 'v_b_ada', 'v_g_pre', 'v_g_post', 'v_w_in', 'v_q_norm_g', 'v_w_uq', 'v_kv_norm_g', 'v_w_ukv', 'v_conv_w', 'v_conv_b', 'v_conv_ln_g', 'v_conv_ln_b', 'v_w_pw2', 'v_sgu_ln_g', 'v_sgu_ln_b', 'v_w_s', 'v_b_s', 'v_w_out']
TWIN_OUTPUTS = ['loss', 'grad_x', 'grad_w_ada', 'grad_b_ada', 'grad_g_pre', 'grad_g_post', 'grad_w_in', 'grad_q_norm_g', 'grad_w_uq', 'grad_kv_norm_g', 'grad_w_ukv', 'grad_conv_w', 'grad_conv_b', 'grad_conv_ln_g', 'grad_conv_ln_b', 'grad_w_pw2', 'grad_sgu_ln_g', 'grad_sgu_ln_b', 'grad_w_s', 'grad_b_s', 'grad_w_out', 'delta_w_ada', 'delta_b_ada', 'delta_g_pre', 'delta_g_post', 'delta_w_in', 'delta_q_norm_g', 'delta_w_uq', 'delta_kv_norm_g', 'delta_w_ukv', 'delta_conv_w', 'delta_conv_b', 'delta_conv_ln_g', 'delta_conv_ln_b', 'delta_w_pw2', 'delta_sgu_ln_g', 'delta_sgu_ln_b', 'delta_w_s', 'delta_b_s', 'delta_w_out', 'new_m_w_ada', 'new_m_b_ada', 'new_m_g_pre', 'new_m_g_post', 'new_m_w_in', 'new_m_q_norm_g', 'new_m_w_uq', 'new_m_kv_norm_g', 'new_m_w_ukv', 'new_m_conv_w', 'new_m_conv_b', 'new_m_conv_ln_g', 'new_m_conv_ln_b', 'new_m_w_pw2', 'new_m_sgu_ln_g', 'new_m_sgu_ln_b', 'new_m_w_s', 'new_m_b_s', 'new_m_w_out', 'new_v_w_ada', 'new_v_b_ada', 'new_v_g_pre', 'new_v_g_post', 'new_v_w_in', 'new_v_q_norm_g', 'new_v_w_uq', 'new_v_kv_norm_g', 'new_v_w_ukv', 'new_v_conv_w', 'new_v_conv_b', 'new_v_conv_ln_g', 'new_v_conv_ln_b', 'new_v_w_pw2', 'new_v_sgu_ln_g', 'new_v_sgu_ln_b', 'new_v_w_s', 'new_v_b_s', 'new_v_w_out']
TWIN_LEAF_KINDS = {'loss': 'loss', 'grad_x': 'grad_x', 'grad_w_ada': 'grad_w', 'grad_b_ada': 'grad_w', 'grad_g_pre': 'grad_w', 'grad_g_post': 'grad_w', 'grad_w_in': 'grad_w', 'grad_q_norm_g': 'grad_w', 'grad_w_uq': 'grad_w', 'grad_kv_norm_g': 'grad_w', 'grad_w_ukv': 'grad_w', 'grad_conv_w': 'grad_w', 'grad_conv_b': 'grad_w', 'grad_conv_ln_g': 'grad_w', 'grad_conv_ln_b': 'grad_w', 'grad_w_pw2': 'grad_w', 'grad_sgu_ln_g': 'grad_w', 'grad_sgu_ln_b': 'grad_w', 'grad_w_s': 'grad_w', 'grad_b_s': 'grad_w', 'grad_w_out': 'grad_w', 'delta_w_ada': 'delta_w', 'delta_b_ada': 'delta_w', 'delta_g_pre': 'delta_w', 'delta_g_post': 'delta_w', 'delta_w_in': 'delta_w', 'delta_q_norm_g': 'delta_w', 'delta_w_uq': 'delta_w', 'delta_kv_norm_g': 'delta_w', 'delta_w_ukv': 'delta_w', 'delta_conv_w': 'delta_w', 'delta_conv_b': 'delta_w', 'delta_conv_ln_g': 'delta_w', 'delta_conv_ln_b': 'delta_w', 'delta_w_pw2': 'delta_w', 'delta_sgu_ln_g': 'delta_w', 'delta_sgu_ln_b': 'delta_w', 'delta_w_s': 'delta_w', 'delta_b_s': 'delta_w', 'delta_w_out': 'delta_w', 'new_m_w_ada': 'new_m', 'new_m_b_ada': 'new_m', 'new_m_g_pre': 'new_m', 'new_m_g_post': 'new_m', 'new_m_w_in': 'new_m', 'new_m_q_norm_g': 'new_m', 'new_m_w_uq': 'new_m', 'new_m_kv_norm_g': 'new_m', 'new_m_w_ukv': 'new_m', 'new_m_conv_w': 'new_m', 'new_m_conv_b': 'new_m', 'new_m_conv_ln_g': 'new_m', 'new_m_conv_ln_b': 'new_m', 'new_m_w_pw2': 'new_m', 'new_m_sgu_ln_g': 'new_m', 'new_m_sgu_ln_b': 'new_m', 'new_m_w_s': 'new_m', 'new_m_b_s': 'new_m', 'new_m_w_out': 'new_m', 'new_v_w_ada': 'new_v', 'new_v_b_ada': 'new_v', 'new_v_g_pre': 'new_v', 'new_v_g_post': 'new_v', 'new_v_w_in': 'new_v', 'new_v_q_norm_g': 'new_v', 'new_v_w_uq': 'new_v', 'new_v_kv_norm_g': 'new_v', 'new_v_w_ukv': 'new_v', 'new_v_conv_w': 'new_v', 'new_v_conv_b': 'new_v', 'new_v_conv_ln_g': 'new_v', 'new_v_conv_ln_b': 'new_v', 'new_v_w_pw2': 'new_v', 'new_v_sgu_ln_g': 'new_v', 'new_v_sgu_ln_b': 'new_v', 'new_v_w_s': 'new_v', 'new_v_b_s': 'new_v', 'new_v_w_out': 'new_v'}


def _forward(args):
    return _fwd_reference(*[args[k] for k in FWD_PARAMS])


def _output_shape():
    def fwd():
        inp = _fwd_setup_inputs(0)
        return _fwd_reference(*[inp[k] for k in FWD_PARAMS])
    out = _jax.eval_shape(fwd)
    return out.shape, out.dtype

N_MICROBATCH = 1
ADAM_LR = 0.001
ADAM_B1 = 0.9
ADAM_B2 = 0.999
ADAM_EPS = 1e-08
ADAM_WD = 0.01
ADAM_STEP = 10
PER_EXAMPLE_BATCH_AXIS = {'x': 0, 'c': 0, 'positions': 0, 'loss_target': 0}
SHARED_INPUTS = []
_WEIGHT_DTYPES = {'w_ada': _jnp.float32, 'b_ada': _jnp.float32, 'g_pre': _jnp.float32, 'g_post': _jnp.float32, 'w_in': _jnp.float32, 'q_norm_g': _jnp.float32, 'w_uq': _jnp.float32, 'kv_norm_g': _jnp.float32, 'w_ukv': _jnp.float32, 'conv_w': _jnp.float32, 'conv_b': _jnp.float32, 'conv_ln_g': _jnp.float32, 'conv_ln_b': _jnp.float32, 'w_pw2': _jnp.float32, 'sgu_ln_g': _jnp.float32, 'sgu_ln_b': _jnp.float32, 'w_s': _jnp.float32, 'b_s': _jnp.float32, 'w_out': _jnp.float32}
MOMENT_SCALE = {'w_ada': 1.039053e+01, 'b_ada': 3.698170e+01, 'g_pre': 1.050515e+00, 'g_post': 6.436689e+01, 'w_in': 6.948428e-01, 'q_norm_g': 3.097202e-01, 'w_uq': 1.682736e-01, 'kv_norm_g': 7.876495e-01, 'w_ukv': 2.312714e-01, 'conv_w': 7.947283e-01, 'conv_b': 5.121109e+00, 'conv_ln_g': 1.789949e+00, 'conv_ln_b': 3.155212e+00, 'w_pw2': 1.246563e+00, 'sgu_ln_g': 4.954709e-01, 'sgu_ln_b': 6.274314e-01, 'w_s': 3.463392e-01, 'b_s': 4.647986e-01, 'w_out': 1.260148e+00}


def _to_microbatches(a, axis):
    t = _jnp.moveaxis(a, axis, 0)
    t = t.reshape((N_MICROBATCH, t.shape[0] // N_MICROBATCH) + t.shape[1:])
    return _jnp.moveaxis(t, 1, axis + 1)


def setup_inputs(seed: int = 0) -> dict:
    inp = _fwd_setup_inputs(seed)
    key = _jax.random.fold_in(_jax.random.key(seed), 7919)
    shape, _ = _output_shape()
    out = dict(inp)
    out["loss_target"] = _jax.random.normal(_jax.random.fold_in(key, 0), shape, _jnp.float32)
    for i, name in enumerate(TWIN_WEIGHTS):
        w = inp[name].astype(_jnp.float32)
        if MOMENT_SCALE is None:
            s = _jnp.sqrt(_jnp.mean(_jnp.square(w)) + 1e-30)
        else:
            s = MOMENT_SCALE[name]
        km, kv = _jax.random.split(_jax.random.fold_in(key, i + 1))
        out[name] = w
        out["m_" + name] = s * _jax.random.normal(km, w.shape, _jnp.float32)
        out["v_" + name] = (s * s) * _jax.random.uniform(kv, w.shape, _jnp.float32, 0.5, 1.5)
    if N_MICROBATCH > 1:
        for name, axis in PER_EXAMPLE_BATCH_AXIS.items():
            out[name] = _to_microbatches(out[name], axis)
    return {'x': out['x'], 'c': out['c'], 'positions': out['positions'], 'w_ada': out['w_ada'], 'b_ada': out['b_ada'], 'g_pre': out['g_pre'], 'g_post': out['g_post'], 'w_in': out['w_in'], 'q_norm_g': out['q_norm_g'], 'w_uq': out['w_uq'], 'kv_norm_g': out['kv_norm_g'], 'w_ukv': out['w_ukv'], 'conv_w': out['conv_w'], 'conv_b': out['conv_b'], 'conv_ln_g': out['conv_ln_g'], 'conv_ln_b': out['conv_ln_b'], 'w_pw2': out['w_pw2'], 'sgu_ln_g': out['sgu_ln_g'], 'sgu_ln_b': out['sgu_ln_b'], 'w_s': out['w_s'], 'b_s': out['b_s'], 'w_out': out['w_out'], 'loss_target': out['loss_target'], 'm_w_ada': out['m_w_ada'], 'm_b_ada': out['m_b_ada'], 'm_g_pre': out['m_g_pre'], 'm_g_post': out['m_g_post'], 'm_w_in': out['m_w_in'], 'm_q_norm_g': out['m_q_norm_g'], 'm_w_uq': out['m_w_uq'], 'm_kv_norm_g': out['m_kv_norm_g'], 'm_w_ukv': out['m_w_ukv'], 'm_conv_w': out['m_conv_w'], 'm_conv_b': out['m_conv_b'], 'm_conv_ln_g': out['m_conv_ln_g'], 'm_conv_ln_b': out['m_conv_ln_b'], 'm_w_pw2': out['m_w_pw2'], 'm_sgu_ln_g': out['m_sgu_ln_g'], 'm_sgu_ln_b': out['m_sgu_ln_b'], 'm_w_s': out['m_w_s'], 'm_b_s': out['m_b_s'], 'm_w_out': out['m_w_out'], 'v_w_ada': out['v_w_ada'], 'v_b_ada': out['v_b_ada'], 'v_g_pre': out['v_g_pre'], 'v_g_post': out['v_g_post'], 'v_w_in': out['v_w_in'], 'v_q_norm_g': out['v_q_norm_g'], 'v_w_uq': out['v_w_uq'], 'v_kv_norm_g': out['v_kv_norm_g'], 'v_w_ukv': out['v_w_ukv'], 'v_conv_w': out['v_conv_w'], 'v_conv_b': out['v_conv_b'], 'v_conv_ln_g': out['v_conv_ln_g'], 'v_conv_ln_b': out['v_conv_ln_b'], 'v_w_pw2': out['v_w_pw2'], 'v_sgu_ln_g': out['v_sgu_ln_g'], 'v_sgu_ln_b': out['v_sgu_ln_b'], 'v_w_s': out['v_w_s'], 'v_b_s': out['v_b_s'], 'v_w_out': out['v_w_out']}


def _loss(weights, diff, rest, loss_target):
    with _jax.named_scope("forward"):
        args = {**rest, TWIN_DIFF_INPUT: diff, **{k: w.astype(_WEIGHT_DTYPES[k]) for k, w in weights.items()}}
        y = _forward(args)
    with _jax.named_scope("loss_head"):
        err = _jnp.square(y.astype(_jnp.float32) - loss_target)
        return 0.5 * _jnp.sum(_jnp.mean(err, axis=-1)) if err.ndim else 0.5 * err


def _adamw(w, g, m, v):
    m = ADAM_B1 * m + (1.0 - ADAM_B1) * g
    v = ADAM_B2 * v + (1.0 - ADAM_B2) * _jnp.square(g)
    m_hat = m / (1.0 - ADAM_B1 ** ADAM_STEP)
    v_hat = v / (1.0 - ADAM_B2 ** ADAM_STEP)
    delta = -ADAM_LR * (m_hat / (_jnp.sqrt(v_hat) + ADAM_EPS) + ADAM_WD * w)
    return delta, m, v


def reference(x, c, positions, w_ada, b_ada, g_pre, g_post, w_in, q_norm_g, w_uq, kv_norm_g, w_ukv, conv_w, conv_b, conv_ln_g, conv_ln_b, w_pw2, sgu_ln_g, sgu_ln_b, w_s, b_s, w_out, loss_target, m_w_ada, m_b_ada, m_g_pre, m_g_post, m_w_in, m_q_norm_g, m_w_uq, m_kv_norm_g, m_w_ukv, m_conv_w, m_conv_b, m_conv_ln_g, m_conv_ln_b, m_w_pw2, m_sgu_ln_g, m_sgu_ln_b, m_w_s, m_b_s, m_w_out, v_w_ada, v_b_ada, v_g_pre, v_g_post, v_w_in, v_q_norm_g, v_w_uq, v_kv_norm_g, v_w_ukv, v_conv_w, v_conv_b, v_conv_ln_g, v_conv_ln_b, v_w_pw2, v_sgu_ln_g, v_sgu_ln_b, v_w_s, v_b_s, v_w_out):
    given = dict(x=x, c=c, positions=positions, w_ada=w_ada, b_ada=b_ada, g_pre=g_pre, g_post=g_post, w_in=w_in, q_norm_g=q_norm_g, w_uq=w_uq, kv_norm_g=kv_norm_g, w_ukv=w_ukv, conv_w=conv_w, conv_b=conv_b, conv_ln_g=conv_ln_g, conv_ln_b=conv_ln_b, w_pw2=w_pw2, sgu_ln_g=sgu_ln_g, sgu_ln_b=sgu_ln_b, w_s=w_s, b_s=b_s, w_out=w_out, loss_target=loss_target, m_w_ada=m_w_ada, m_b_ada=m_b_ada, m_g_pre=m_g_pre, m_g_post=m_g_post, m_w_in=m_w_in, m_q_norm_g=m_q_norm_g, m_w_uq=m_w_uq, m_kv_norm_g=m_kv_norm_g, m_w_ukv=m_w_ukv, m_conv_w=m_conv_w, m_conv_b=m_conv_b, m_conv_ln_g=m_conv_ln_g, m_conv_ln_b=m_conv_ln_b, m_w_pw2=m_w_pw2, m_sgu_ln_g=m_sgu_ln_g, m_sgu_ln_b=m_sgu_ln_b, m_w_s=m_w_s, m_b_s=m_b_s, m_w_out=m_w_out, v_w_ada=v_w_ada, v_b_ada=v_b_ada, v_g_pre=v_g_pre, v_g_post=v_g_post, v_w_in=v_w_in, v_q_norm_g=v_q_norm_g, v_w_uq=v_w_uq, v_kv_norm_g=v_kv_norm_g, v_w_ukv=v_w_ukv, v_conv_w=v_conv_w, v_conv_b=v_conv_b, v_conv_ln_g=v_conv_ln_g, v_conv_ln_b=v_conv_ln_b, v_w_pw2=v_w_pw2, v_sgu_ln_g=v_sgu_ln_g, v_sgu_ln_b=v_sgu_ln_b, v_w_s=v_w_s, v_b_s=v_b_s, v_w_out=v_w_out)
    weights = {n: given[n] for n in TWIN_WEIGHTS}
    shared = {n: given[n] for n in SHARED_INPUTS}
    per_example = {n: given[n] for n in ['x', 'c', 'positions']}
    grad_fn = _jax.value_and_grad(_loss, argnums=(0, 1))

    def one_microbatch(ex, loss_target):
        ex = dict(ex)
        diff = ex.pop(TWIN_DIFF_INPUT)
        return grad_fn(weights, diff, {**shared, **ex}, loss_target)

    if N_MICROBATCH == 1:
        loss, (grad_w, grad_x) = one_microbatch(per_example, given["loss_target"])
    else:
        def body(carry, xs):
            loss_sum, grad_sum = carry
            l_k, (gw_k, gx_k) = one_microbatch(xs[0], xs[1])
            with _jax.named_scope("update"):
                return (loss_sum + l_k, _jax.tree.map(_jnp.add, grad_sum, gw_k)), gx_k

        init = (_jnp.zeros((), _jnp.float32), _jax.tree.map(_jnp.zeros_like, weights))
        (loss, grad_w), grad_x = _jax.lax.scan(body, init, (per_example, given["loss_target"]))
    with _jax.named_scope("update"):
        delta_w, new_m, new_v = {}, {}, {}
        for n in TWIN_WEIGHTS:
            delta_w[n], new_m[n], new_v[n] = _adamw(weights[n], grad_w[n], given["m_" + n], given["v_" + n])
    return (loss, grad_x, *[grad_w[n] for n in TWIN_WEIGHTS], *[delta_w[n] for n in TWIN_WEIGHTS],
            *[new_m[n] for n in TWIN_WEIGHTS], *[new_v[n] for n in TWIN_WEIGHTS])
```

```python
import functools
import math

import numpy as np
import jax
import jax.numpy as jnp
from jax import lax
from jax.experimental import pallas as pl
from jax.experimental.pallas import tpu as pltpu

F32 = jnp.float32
BF16 = jnp.bfloat16

N_DEV = 8
DEPTH = 2
D = 1024
HEADS = 4
NOPE = 128
ROPE = 64
VDIM = 128
QK = NOPE + ROPE
Q_RANK = 256
KV_RANK = 128
ATT_W = HEADS * VDIM
CONV_W = 256
CONV_K = 31
SGU_W = 256
SGU_G = 4
SGU_GD = SGU_W // SGU_G
SGU_T = 128
D_IN = 2496
ATT_IN = Q_RANK + KV_RANK + ROPE
PAD_IN = 64
DZ = D_IN + PAD_IN
HQ = 2 * NOPE
EPS = 1e-6
ROPE_THETA = 10000.0
ATT_SCALE = QK ** -0.5
NEG_INF = float("-inf")

ADAM_LR = 0.001
ADAM_B1 = 0.9
ADAM_B2 = 0.999
ADAM_EPS = 1e-08
ADAM_WD = 0.01
ADAM_STEP = 10

VMEM_LIMIT = 56 * 1024 * 1024
ROW_TILE = 256
ATT_TILE = 512
HALO = 32
PACK_LANES = 512

MESH = pl.DeviceIdType.MESH
NT = (((1,), (1,)), ((), ()))
TN = (((0,), (0,)), ((), ()))


def _cparams(*sem):
    return pltpu.CompilerParams(dimension_semantics=sem, vmem_limit_bytes=VMEM_LIMIT)


def _sds(shape, dtype=F32):
    return jax.ShapeDtypeStruct(tuple(shape), dtype)


def _rows(tm, width, col=0):
    return pl.BlockSpec((tm, width), lambda i: (i, col))


def _full(shape):
    nd = len(shape)
    return pl.BlockSpec(tuple(shape), lambda *_: (0,) * nd)


def _sigmoid(x):
    return 1.0 / (1.0 + jnp.exp(-x))


def _silu_and_grad(g):
    s = _sigmoid(g)
    return g * s, s * (1.0 + g * (1.0 - s))


def _gelu_and_grad(x):
    cdf = 0.5 * (1.0 + lax.erf(x * (1.0 / math.sqrt(2.0))))
    pdf = jnp.exp(-0.5 * x * x) * (1.0 / math.sqrt(2.0 * math.pi))
    return x * cdf, cdf + x * pdf


def _swap_halves(a):
    lane = lax.broadcasted_iota(jnp.int32, a.shape, 1)
    up = pltpu.roll(a, 32, 1)
    down = pltpu.roll(a, 96, 1)
    return jnp.where(lane < 32, down, jnp.where(lane < 64, up, 0.0))


def _rope_tables(pos_ref, rope_ref):
    ang = pos_ref[...].astype(F32) * rope_ref[0:1, :]
    return jnp.cos(ang) * rope_ref[1:2, :], jnp.sin(ang) * rope_ref[2:3, :]


def _exchange(src, *, all_to_all, name):
    shape = src.shape[-2:]

    def body(src_ref, out_ref, send_sems, recv_sems, local_sem):
        x, y, c = lax.axis_index("x"), lax.axis_index("y"), lax.axis_index("c")
        me = 4 * x + 2 * y + c

        def block_for(dest):
            return src_ref.at[dest] if all_to_all else src_ref

        local = pltpu.make_async_copy(block_for(me), out_ref.at[me], local_sem.at[0])
        local.start()
        sends, recvs = [], []
        for r in range(1, N_DEV):
            px = 1 - x if (r >> 2) & 1 else x
            py = 1 - y if (r >> 1) & 1 else y
            pc = 1 - c if r & 1 else c
            peer = 4 * px + 2 * py + pc
            sends.append(pltpu.make_async_remote_copy(
                src_ref=block_for(peer), dst_ref=out_ref.at[me],
                send_sem=send_sems.at[r - 1], recv_sem=recv_sems.at[r - 1],
                device_id=(px, py, pc), device_id_type=MESH))
            recvs.append(pltpu.make_async_remote_copy(
                src_ref=block_for(me), dst_ref=out_ref.at[peer],
                send_sem=send_sems.at[r - 1], recv_sem=recv_sems.at[r - 1],
                device_id=(px, py, pc), device_id_type=MESH))
        for cp in sends:
            cp.start()
        for cp in recvs:
            cp.wait_recv()
        for cp in sends:
            cp.wait_send()
        local.wait()

    return pl.pallas_call(
        body, name=name,
        out_shape=_sds((N_DEV,) + tuple(shape), src.dtype),
        in_specs=[pl.BlockSpec(memory_space=pl.ANY)],
        out_specs=pl.BlockSpec(memory_space=pl.ANY),
        scratch_shapes=[pltpu.SemaphoreType.DMA((N_DEV - 1,)), pltpu.SemaphoreType.DMA((N_DEV - 1,)),
                        pltpu.SemaphoreType.DMA((1,))],
    )(src)


class _Packer:
    def __init__(self, entries, row_multiple):
        self.entries = entries
        self.offsets = {}
        off = 0
        for name, shape in entries:
            self.offsets[name] = off
            off += int(np.prod(shape))
        quantum = PACK_LANES * row_multiple
        self.total = -(-off // quantum) * quantum
        self.used = off
        self.rows = self.total // PACK_LANES

    def pack(self, arrays, dtype, lead=()):
        n = len(lead)
        flat = [arrays[name].astype(dtype).reshape(lead + (-1,)) for name, _ in self.entries]
        flat.append(jnp.zeros(lead + (self.total - self.used,), dtype))
        return jnp.concatenate(flat, axis=n).reshape(lead + (self.rows, PACK_LANES))

    def unpack(self, buf, lead=()):
        flat = buf.reshape(lead + (self.total,))
        out = {}
        for name, shape in self.entries:
            o = self.offsets[name]
            out[name] = lax.slice_in_dim(flat, o, o + int(np.prod(shape)), axis=len(lead)).reshape(lead + tuple(shape))
        return out


def _ada_forward(c_rows, w_ada, b_ada_cols):
    cols = w_ada.shape[-1]
    rows = c_rows.shape[0]

    def body(c_ref, w_ref, b_ref, sc_ref, part_ref):
        cv = c_ref[...]
        sc = cv * _sigmoid(cv)
        sc_ref[...] = sc
        scb = sc.astype(BF16)
        for l in range(DEPTH):
            part_ref[l] = jnp.dot(scb, w_ref[l].astype(BF16), preferred_element_type=F32) + b_ref[l:l + 1, :]

    return pl.pallas_call(
        body, name="ada_forward",
        out_shape=(_sds((rows, D)), _sds((DEPTH, rows, cols))),
        compiler_params=pltpu.CompilerParams(vmem_limit_bytes=VMEM_LIMIT),
    )(c_rows, w_ada, b_ada_cols)


def _ada_backward(sc_t, dmod_cols):
    cols = dmod_cols.shape[-1]

    def body(sc_ref, dm_ref, gw_ref):
        scb = sc_ref[...].astype(BF16)
        for l in range(DEPTH):
            gw_ref[l] = jnp.dot(scb, dm_ref[l].astype(BF16), preferred_element_type=F32)

    return pl.pallas_call(
        body, name="ada_backward",
        out_shape=_sds((DEPTH, D, cols)),
        compiler_params=pltpu.CompilerParams(vmem_limit_bytes=VMEM_LIMIT),
    )(sc_t, dmod_cols)


def _prenorm_inproj(x, g_pre, mod, w_in_p, name):
    S = x.shape[0]
    tm = ROW_TILE

    def body(x_ref, g_ref, mod_ref, w_ref, z_ref):
        xv = x_ref[...]
        rstd = lax.rsqrt(jnp.mean(xv * xv, axis=-1, keepdims=True) + EPS)
        h = (xv * rstd * g_ref[...]) * (1.0 + mod_ref[1:2, :]) + mod_ref[0:1, :]
        z_ref[...] = jnp.dot(h.astype(BF16), w_ref[...], preferred_element_type=F32)

    return pl.pallas_call(
        body, name=name, grid=(S // tm,),
        in_specs=[_rows(tm, D), _full((1, D)), _full((3, D)), _full((D, DZ))],
        out_specs=_rows(tm, DZ), out_shape=_sds((S, DZ)),
        compiler_params=_cparams("parallel"),
    )(x, g_pre, mod, w_in_p)


def _att_prep(z, pos, q_g, kv_g, wq_p, w_ukv, rope_rows, name):
    S = z.shape[0]
    tm = ROW_TILE

    def body(z_ref, pos_ref, qg_ref, kvg_ref, wq_ref, wkv_ref, rope_ref, q_ref, k_ref, v_ref):
        zz = z_ref[...]
        ql, kvl, ka = zz[:, 0:Q_RANK], zz[:, Q_RANK:Q_RANK + KV_RANK], zz[:, Q_RANK + KV_RANK:]
        qn = ql * lax.rsqrt(jnp.mean(ql * ql, axis=-1, keepdims=True) + EPS) * qg_ref[...]
        kvn = kvl * lax.rsqrt(jnp.mean(kvl * kvl, axis=-1, keepdims=True) + EPS) * kvg_ref[...]
        q = jnp.dot(qn.astype(BF16), wq_ref[...], preferred_element_type=F32)
        kv = jnp.dot(kvn.astype(BF16), wkv_ref[...], preferred_element_type=F32)
        ct, st = _rope_tables(pos_ref, rope_ref)
        krot = (ka * ct + _swap_halves(ka) * st).astype(BF16)
        for h in range(HEADS):
            b = h * HQ
            q_ref[:, b:b + NOPE] = q[:, b:b + NOPE].astype(BF16)
            a = q[:, b + NOPE:b + HQ]
            q_ref[:, b + NOPE:b + HQ] = (a * ct + _swap_halves(a) * st).astype(BF16)
            k_ref[:, b:b + NOPE] = kv[:, b:b + NOPE].astype(BF16)
            k_ref[:, b + NOPE:b + HQ] = krot
            v_ref[:, h * VDIM:(h + 1) * VDIM] = kv[:, b + NOPE:b + HQ].astype(BF16)

    return pl.pallas_call(
        body, name=name, grid=(S // tm,),
        in_specs=[_rows(tm, 512, 0), _rows(tm, 1), _full((1, Q_RANK)), _full((1, KV_RANK)),
                  _full((Q_RANK, HEADS * HQ)), _full((KV_RANK, HEADS * HQ)), _full((8, 128))],
        out_specs=(_rows(tm, HEADS * HQ), _rows(tm, HEADS * HQ), _rows(tm, ATT_W)),
        out_shape=(_sds((S, HEADS * HQ), BF16), _sds((S, HEADS * HQ), BF16), _sds((S, ATT_W), BF16)),
        compiler_params=_cparams("parallel"),
    )(z, pos, q_g, kv_g, wq_p, w_ukv, rope_rows)


def _flash_forward(q, k, v, name):
    S = q.shape[0]
    t = ATT_TILE
    nq = S // t

    def body(q_ref, k_ref, v_ref, o_ref, lse_ref, m_sc, l_sc, acc_sc):
        qb = pl.program_id(1)
        qv = q_ref[...]
        m_sc[...] = jnp.full(m_sc.shape, NEG_INF, F32)
        l_sc[...] = jnp.zeros(l_sc.shape, F32)
        acc_sc[...] = jnp.zeros(acc_sc.shape, F32)

        def step(kb, diagonal):
            rows = pl.ds(pl.multiple_of(kb * t, t), t)
            s = lax.dot_general(qv, k_ref[rows, :], NT, preferred_element_type=F32) * ATT_SCALE
            if diagonal:
                ri = lax.broadcasted_iota(jnp.int32, (t, t), 0)
                ci = lax.broadcasted_iota(jnp.int32, (t, t), 1)
                s = jnp.where(ci <= ri, s, NEG_INF)
            m_prev = m_sc[...]
            m_new = jnp.maximum(m_prev, jnp.max(s, axis=-1, keepdims=True))
            alpha = jnp.exp(m_prev - m_new)
            p = jnp.exp(s - m_new)
            l_sc[...] = alpha * l_sc[...] + jnp.sum(p, axis=-1, keepdims=True)
            acc_sc[...] = alpha * acc_sc[...] + jnp.dot(p.astype(BF16), v_ref[rows, :], preferred_element_type=F32)
            m_sc[...] = m_new

        def loop_body(kb, carry):
            step(kb, False)
            return carry

        lax.fori_loop(0, qb, loop_body, 0)
        step(qb, True)
        o_ref[...] = acc_sc[...] / l_sc[...]
        lse_ref[0] = m_sc[...] + jnp.log(l_sc[...])

    return pl.pallas_call(
        body, name=name, grid=(HEADS, nq),
        in_specs=[pl.BlockSpec((t, HQ), lambda h, i: (i, h)),
                  pl.BlockSpec((S, HQ), lambda h, i: (0, h)),
                  pl.BlockSpec((S, VDIM), lambda h, i: (0, h))],
        out_specs=(pl.BlockSpec((t, VDIM), lambda h, i: (i, h)),
                   pl.BlockSpec((1, t, 1), lambda h, i: (h, i, 0))),
        out_shape=(_sds((S, ATT_W)), _sds((HEADS, S, 1))),
        scratch_shapes=[pltpu.VMEM((t, 1), F32), pltpu.VMEM((t, 1), F32), pltpu.VMEM((t, VDIM), F32)],
        compiler_params=_cparams("parallel", "parallel"),
    )(q, k, v)


def _conv_window(win_ref, a_prev, b_prev, a_cur, b_cur, first):
    hp = a_prev * _sigmoid(b_prev)
    win_ref[0:HALO, :] = jnp.where(first, 0.0, hp)
    win_ref[HALO:, :] = a_cur * _sigmoid(b_cur)


def _conv_in_specs(tm):
    per = tm // HALO
    prev = lambda col: pl.BlockSpec((HALO, CONV_W), lambda i: (jnp.maximum(i * per - 1, 0), col))
    return [_rows(tm, CONV_W, 4), _rows(tm, CONV_W, 5), prev(4), prev(5)]


def _conv_forward(z, conv_w_p, conv_b, ln_g, ln_b, w_pw2, name):
    S = z.shape[0]
    tm = ROW_TILE

    def body(a_ref, b_ref, ap_ref, bp_ref, w_ref, cb_ref, g_ref, be_ref, pw_ref, cv_ref, y_ref, win):
        _conv_window(win, ap_ref[...], bp_ref[...], a_ref[...], b_ref[...], pl.program_id(0) == 0)
        acc = jnp.zeros((tm, CONV_W), F32)
        for kk in range(CONV_K):
            acc = acc + w_ref[kk:kk + 1, :] * win[pl.ds(HALO - (CONV_K - 1) + kk, tm), :]
        cv = acc + cb_ref[...]
        cv_ref[...] = cv
        mu = jnp.mean(cv, axis=-1, keepdims=True)
        cc = cv - mu
        rstd = lax.rsqrt(jnp.mean(cc * cc, axis=-1, keepdims=True) + EPS)
        n = cc * rstd * g_ref[...] + be_ref[...]
        sl = n * _sigmoid(n)
        y_ref[...] = jnp.dot(sl.astype(BF16), pw_ref[...], preferred_element_type=F32)

    return pl.pallas_call(
        body, name=name, grid=(S // tm,),
        in_specs=_conv_in_specs(tm) + [_full((HALO, CONV_W)), _full((1, CONV_W)), _full((1, CONV_W)),
                                       _full((1, CONV_W)), _full((CONV_W, CONV_W))],
        out_specs=(_rows(tm, CONV_W), _rows(tm, CONV_W)),
        out_shape=(_sds((S, CONV_W)), _sds((S, CONV_W))),
        scratch_shapes=[pltpu.VMEM((tm + HALO, CONV_W), F32)],
        compiler_params=_cparams("parallel"),
    )(z, z, z, z, conv_w_p, conv_b, ln_g, ln_b, w_pw2)


def _sgu_common(u, v, g_ref, be_ref):
    gu, dgu = _gelu_and_grad(u)
    gv, dgv = _gelu_and_grad(v)
    mu = jnp.mean(gv, axis=-1, keepdims=True)
    cc = gv - mu
    rstd = lax.rsqrt(jnp.mean(cc * cc, axis=-1, keepdims=True) + EPS)
    nh = cc * rstd
    vn = nh * g_ref[...] + be_ref[...]
    return gu, dgu, dgv, rstd, nh, vn


def _sgu_masks():
    lane_group = lax.broadcasted_iota(jnp.int32, (1, SGU_W), 1) // SGU_GD
    ri = lax.broadcasted_iota(jnp.int32, (SGU_T, SGU_T), 0)
    ci = lax.broadcasted_iota(jnp.int32, (SGU_T, SGU_T), 1)
    return [lane_group == g for g in range(SGU_G)], ci <= ri


def _sgu_forward(z, ln_g, ln_b, w_s, bias_full, name):
    S = z.shape[0]
    tm = ROW_TILE

    def body(u_ref, v_ref, g_ref, be_ref, ws_ref, bias_ref, y_ref):
        gmask, tril = _sgu_masks()
        wm = [jnp.where(tril, ws_ref[g], 0.0).astype(BF16) for g in range(SGU_G)]
        for ch in range(tm // SGU_T):
            rows = slice(ch * SGU_T, (ch + 1) * SGU_T)
            gu, _, _, _, _, vn = _sgu_common(u_ref[rows, :], v_ref[rows, :], g_ref, be_ref)
            vb = vn.astype(BF16)
            sv = bias_ref[...]
            for g in range(SGU_G):
                sv = sv + jnp.where(gmask[g], jnp.dot(wm[g], vb, preferred_element_type=F32), 0.0)
            y_ref[rows, :] = gu * sv

    return pl.pallas_call(
        body, name=name, grid=(S // tm,),
        in_specs=[_rows(tm, SGU_W, 7), _rows(tm, SGU_W, 8), _full((1, SGU_W)), _full((1, SGU_W)),
                  _full((SGU_G, SGU_T, SGU_T)), _full((SGU_T, SGU_W))],
        out_specs=_rows(tm, SGU_W), out_shape=_sds((S, SGU_W)),
        compiler_params=_cparams("parallel"),
    )(z, z, ln_g, ln_b, w_s, bias_full)


def _out_proj(x, z, y_att, y_conv, y_sgu, w_out, g_post, mod, name):
    S = x.shape[0]
    tm = ROW_TILE

    def body(x_ref, ga_ref, gc_ref, gs_ref, ya_ref, yc_ref, ys_ref, w_ref, gp_ref, mod_ref, xn_ref, y_ref, cat_ref):
        ca = (ya_ref[...] * _silu_and_grad(ga_ref[...])[0]).astype(BF16)
        cc = (yc_ref[...] * _silu_and_grad(gc_ref[...])[0]).astype(BF16)
        cs = (ys_ref[...] * _silu_and_grad(gs_ref[...])[0]).astype(BF16)
        cat_ref[:, 0:ATT_W] = ca
        cat_ref[:, ATT_W:ATT_W + CONV_W] = cc
        cat_ref[:, ATT_W + CONV_W:] = cs
        y = (jnp.dot(ca, w_ref[0:ATT_W, :], preferred_element_type=F32)
             + jnp.dot(cc, w_ref[ATT_W:ATT_W + CONV_W, :], preferred_element_type=F32)
             + jnp.dot(cs, w_ref[ATT_W + CONV_W:, :], preferred_element_type=F32))
        y_ref[...] = y
        rstd = lax.rsqrt(jnp.mean(y * y, axis=-1, keepdims=True) + EPS)
        xn_ref[...] = x_ref[...] + mod_ref[2:3, :] * (y * rstd * gp_ref[...])

    return pl.pallas_call(
        body, name=name, grid=(S // tm,),
        in_specs=[_rows(tm, D), _rows(tm, 512, 1), _rows(tm, 256, 6), _rows(tm, 256, 9),
                  _rows(tm, ATT_W), _rows(tm, CONV_W), _rows(tm, SGU_W),
                  _full((D, D)), _full((1, D)), _full((3, D))],
        out_specs=(_rows(tm, D), _rows(tm, D), _rows(tm, D)),
        out_shape=(_sds((S, D)), _sds((S, D)), _sds((S, D), BF16)),
        compiler_params=_cparams("parallel"),
    )(x, z, z, z, y_att, y_conv, y_sgu, w_out, g_post, mod)


def _loss_head(y, target):
    S = y.shape[0]
    tm = ROW_TILE

    def body(y_ref, t_ref, loss_ref, dy_ref):
        @pl.when(pl.program_id(0) == 0)
        def _():
            loss_ref[...] = jnp.zeros(loss_ref.shape, F32)

        err = y_ref[...] - t_ref[...]
        dy_ref[...] = err * (1.0 / D)
        row = jnp.sum(err * err, axis=-1, keepdims=True) * (1.0 / D)
        loss_ref[...] += 0.5 * jnp.sum(row, axis=0, keepdims=True)

    return pl.pallas_call(
        body, name="loss_head", grid=(S // tm,),
        in_specs=[_rows(tm, D), _rows(tm, D)],
        out_specs=(_full((1, 1)), _rows(tm, D)),
        out_shape=(_sds((1, 1)), _sds((S, D))),
        compiler_params=_cparams("arbitrary"),
    )(y, target)


def _matmul_tn(a, b, name):
    S, M = a.shape
    N = b.shape[1]
    bk = min(512, S)

    def body(a_ref, b_ref, o_ref):
        @pl.when(pl.program_id(0) == 0)
        def _():
            o_ref[...] = jnp.zeros(o_ref.shape, F32)

        o_ref[...] += lax.dot_general(a_ref[...], b_ref[...], TN, preferred_element_type=F32)

    return pl.pallas_call(
        body, name=name, grid=(S // bk,),
        in_specs=[pl.BlockSpec((bk, M), lambda k: (k, 0)), pl.BlockSpec((bk, N), lambda k: (k, 0))],
        out_specs=_full((M, N)), out_shape=_sds((M, N)),
        compiler_params=_cparams("arbitrary"),
    )(a, b)


def _out_proj_backward(dxo, y, z, y_att, y_conv, y_sgu, lse, w_out, g_post, mod, name):
    S = dxo.shape[0]
    tm = ROW_TILE

    def body(dxo_ref, y_ref, ga_ref, gc_ref, gs_ref, ya_ref, yc_ref, ys_ref, lse_ref, w_ref, gp_ref, mod_ref,
             dyb_ref, dob_ref, st_ref, dga_ref, dyc_ref, dgc_ref, dys_ref, dgs_ref, dgate_ref, dgp_ref):
        @pl.when(pl.program_id(0) == 0)
        def _():
            dgate_ref[...] = jnp.zeros(dgate_ref.shape, F32)
            dgp_ref[...] = jnp.zeros(dgp_ref.shape, F32)

        dxo_v = dxo_ref[...]
        yv = y_ref[...]
        gp = gp_ref[...]
        rstd = lax.rsqrt(jnp.mean(yv * yv, axis=-1, keepdims=True) + EPS)
        yhat = yv * rstd
        dgate_ref[...] += jnp.sum(dxo_v * (yhat * gp), axis=0, keepdims=True)
        dr = dxo_v * mod_ref[2:3, :]
        dgp_ref[...] += jnp.sum(dr * yhat, axis=0, keepdims=True)
        dyh = dr * gp
        dy = rstd * (dyh - yhat * jnp.mean(dyh * yhat, axis=-1, keepdims=True))
        dyb = dy.astype(BF16)
        dyb_ref[...] = dyb
        dcat = lax.dot_general(dyb, w_ref[...], NT, preferred_element_type=F32)

        ya = ya_ref[...]
        sil, dsil = _silu_and_grad(ga_ref[...])
        da = dcat[:, 0:ATT_W]
        do = da * sil
        dob_ref[...] = do.astype(BF16)
        dga_ref[...] = da * ya * dsil
        lane = lax.broadcasted_iota(jnp.int32, (1, 128), 1)
        stats = jnp.zeros((tm, 128), F32)
        for h in range(HEADS):
            cols = slice(h * VDIM, (h + 1) * VDIM)
            delta = jnp.sum(do[:, cols] * ya[:, cols], axis=-1, keepdims=True)
            stats = stats + jnp.where(lane == 2 * h, lse_ref[h], 0.0) + jnp.where(lane == 2 * h + 1, delta, 0.0)
        st_ref[...] = stats

        sil, dsil = _silu_and_grad(gc_ref[...])
        dc = dcat[:, ATT_W:ATT_W + CONV_W]
        dyc_ref[...] = dc * sil
        dgc_ref[...] = dc * yc_ref[...] * dsil
        sil, dsil = _silu_and_grad(gs_ref[...])
        dsg = dcat[:, ATT_W + CONV_W:]
        dys_ref[...] = dsg * sil
        dgs_ref[...] = dsg * ys_ref[...] * dsil

    return pl.pallas_call(
        body, name=name, grid=(S // tm,),
        in_specs=[_rows(tm, D), _rows(tm, D), _rows(tm, 512, 1), _rows(tm, 256, 6), _rows(tm, 256, 9),
                  _rows(tm, ATT_W), _rows(tm, CONV_W), _rows(tm, SGU_W),
                  pl.BlockSpec((HEADS, tm, 1), lambda i: (0, i, 0)),
                  _full((D, D)), _full((1, D)), _full((3, D))],
        out_specs=(_rows(tm, D), _rows(tm, ATT_W), _rows(tm, 128), _rows(tm, ATT_W),
                   _rows(tm, CONV_W), _rows(tm, CONV_W), _rows(tm, SGU_W), _rows(tm, SGU_W),
                   _full((1, D)), _full((1, D))),
        out_shape=(_sds((S, D), BF16), _sds((S, ATT_W), BF16), _sds((S, 128)), _sds((S, ATT_W)),
                   _sds((S, CONV_W)), _sds((S, CONV_W)), _sds((S, SGU_W)), _sds((S, SGU_W)),
                   _sds((1, D)), _sds((1, D))),
        compiler_params=_cparams("arbitrary"),
    )(dxo, y, z, z, z, y_att, y_conv, y_sgu, lse, w_out, g_post, mod)


def _flash_backward(q, k, v, do, stats, name):
    S = q.shape[0]
    t = ATT_TILE
    nq = S // t

    def body(q_ref, do_ref, st_ref, k_ref, v_ref, dq_ref, dk_ref, dv_ref, dk_sc, dv_sc):
        h = pl.program_id(0)
        kb = pl.program_id(1)

        @pl.when(kb == 0)
        def _():
            dq_ref[...] = jnp.zeros(dq_ref.shape, F32)

        kv_k = k_ref[...]
        kv_v = v_ref[...]
        dk_sc[...] = jnp.zeros(dk_sc.shape, F32)
        dv_sc[...] = jnp.zeros(dv_sc.shape, F32)
        lane = lax.broadcasted_iota(jnp.int32, (1, 128), 1)

        def step(qb, diagonal):
            rows = pl.ds(pl.multiple_of(qb * t, t), t)
            qv = q_ref[rows, :]
            dov = do_ref[rows, :]
            st = st_ref[rows, :]
            lse = jnp.sum(jnp.where(lane == 2 * h, st, 0.0), axis=-1, keepdims=True)
            delta = jnp.sum(jnp.where(lane == 2 * h + 1, st, 0.0), axis=-1, keepdims=True)
            s = lax.dot_general(qv, kv_k, NT, preferred_element_type=F32) * ATT_SCALE
            p = jnp.exp(s - lse)
            if diagonal:
                ri = lax.broadcasted_iota(jnp.int32, (t, t), 0)
                ci = lax.broadcasted_iota(jnp.int32, (t, t), 1)
                p = jnp.where(ci <= ri, p, 0.0)
            dv_sc[...] += lax.dot_general(p.astype(BF16), dov, TN, preferred_element_type=F32)
            dp = lax.dot_general(dov, kv_v, NT, preferred_element_type=F32)
            ds = (p * (dp - delta) * ATT_SCALE).astype(BF16)
            dk_sc[...] += lax.dot_general(ds, qv, TN, preferred_element_type=F32)
            dq_ref[rows, :] += jnp.dot(ds, kv_k, preferred_element_type=F32)

        step(kb, True)

        def loop_body(qb, carry):
            step(qb, False)
            return carry

        lax.fori_loop(kb + 1, nq, loop_body, 0)
        dk_ref[...] = dk_sc[...]
        dv_ref[...] = dv_sc[...]

    return pl.pallas_call(
        body, name=name, grid=(HEADS, nq),
        in_specs=[pl.BlockSpec((S, HQ), lambda h, j: (0, h)),
                  pl.BlockSpec((S, VDIM), lambda h, j: (0, h)),
                  pl.BlockSpec((S, 128), lambda h, j: (0, 0)),
                  pl.BlockSpec((t, HQ), lambda h, j: (j, h)),
                  pl.BlockSpec((t, VDIM), lambda h, j: (j, h))],
        out_specs=(pl.BlockSpec((S, HQ), lambda h, j: (0, h)),
                   pl.BlockSpec((t, HQ), lambda h, j: (j, h)),
                   pl.BlockSpec((t, VDIM), lambda h, j: (j, h))),
        out_shape=(_sds((S, HEADS * HQ)), _sds((S, HEADS * HQ)), _sds((S, ATT_W))),
        scratch_shapes=[pltpu.VMEM((t, HQ), F32), pltpu.VMEM((t, VDIM), F32)],
        compiler_params=_cparams("parallel", "arbitrary"),
    )(q, do, stats, k, v)


def _att_prep_backward(z, pos, dq, dk, dv, q_g, kv_g, wq_p, w_ukv, rope_rows, name):
    S = z.shape[0]
    tm = ROW_TILE

    def body(z_ref, pos_ref, dq_ref, dk_ref, dv_ref, qg_ref, kvg_ref, wq_ref, wkv_ref, rope_ref,
             dz_ref, qn_ref, dqp_ref, kvn_ref, dkv_ref, dqg_ref, dkvg_ref):
        @pl.when(pl.program_id(0) == 0)
        def _():
            dqg_ref[...] = jnp.zeros(dqg_ref.shape, F32)
            dkvg_ref[...] = jnp.zeros(dkvg_ref.shape, F32)

        zz = z_ref[...]
        ql, kvl = zz[:, 0:Q_RANK], zz[:, Q_RANK:Q_RANK + KV_RANK]
        q_rstd = lax.rsqrt(jnp.mean(ql * ql, axis=-1, keepdims=True) + EPS)
        kv_rstd = lax.rsqrt(jnp.mean(kvl * kvl, axis=-1, keepdims=True) + EPS)
        qhat, kvhat = ql * q_rstd, kvl * kv_rstd
        qg, kvg = qg_ref[...], kvg_ref[...]
        qn_ref[...] = (qhat * qg).astype(BF16)
        kvn_ref[...] = (kvhat * kvg).astype(BF16)
        ct, st = _rope_tables(pos_ref, rope_ref)

        def unrotate(d):
            return d * ct + _swap_halves(d * st)

        dkrot = jnp.zeros((tm, NOPE), F32)
        for h in range(HEADS):
            b = h * HQ
            dqp_ref[:, b:b + NOPE] = dq_ref[:, b:b + NOPE].astype(BF16)
            dqp_ref[:, b + NOPE:b + HQ] = unrotate(dq_ref[:, b + NOPE:b + HQ]).astype(BF16)
            dkv_ref[:, b:b + NOPE] = dk_ref[:, b:b + NOPE].astype(BF16)
            dkv_ref[:, b + NOPE:b + HQ] = dv_ref[:, h * VDIM:(h + 1) * VDIM].astype(BF16)
            dkrot = dkrot + dk_ref[:, b + NOPE:b + HQ]
        dqn = lax.dot_general(dqp_ref[...], wq_ref[...], NT, preferred_element_type=F32)
        dkvn = lax.dot_general(dkv_ref[...], wkv_ref[...], NT, preferred_element_type=F32)
        dqg_ref[...] += jnp.sum(dqn * qhat, axis=0, keepdims=True)
        dkvg_ref[...] += jnp.sum(dkvn * kvhat, axis=0, keepdims=True)
        dqh, dkvh = dqn * qg, dkvn * kvg
        dz_ref[:, 0:Q_RANK] = q_rstd * (dqh - qhat * jnp.mean(dqh * qhat, axis=-1, keepdims=True))
        dz_ref[:, Q_RANK:Q_RANK + KV_RANK] = kv_rstd * (dkvh - kvhat * jnp.mean(dkvh * kvhat, axis=-1, keepdims=True))
        dz_ref[:, Q_RANK + KV_RANK:] = unrotate(dkrot)

    W = HEADS * HQ
    return pl.pallas_call(
        body, name=name, grid=(S // tm,),
        in_specs=[_rows(tm, 512, 0), _rows(tm, 1), _rows(tm, W), _rows(tm, W), _rows(tm, ATT_W),
                  _full((1, Q_RANK)), _full((1, KV_RANK)), _full((Q_RANK, W)), _full((KV_RANK, W)), _full((8, 128))],
        out_specs=(_rows(tm, 512), _rows(tm, Q_RANK), _rows(tm, W), _rows(tm, KV_RANK), _rows(tm, W),
                   _full((1, Q_RANK)), _full((1, KV_RANK))),
        out_shape=(_sds((S, 512)), _sds((S, Q_RANK), BF16), _sds((S, W), BF16), _sds((S, KV_RANK), BF16),
                   _sds((S, W), BF16), _sds((1, Q_RANK)), _sds((1, KV_RANK))),
        compiler_params=_cparams("arbitrary"),
    )(z, pos, dq, dk, dv, q_g, kv_g, wq_p, w_ukv, rope_rows)


def _conv_norm_backward(dyc, cv, ln_g, ln_b, w_pw2, name):
    S = cv.shape[0]
    tm = ROW_TILE

    def body(dy_ref, cv_ref, g_ref, be_ref, pw_ref, dcv_ref, sl_ref, dyb_ref, dg_ref, db_ref, dcb_ref):
        @pl.when(pl.program_id(0) == 0)
        def _():
            dg_ref[...] = jnp.zeros(dg_ref.shape, F32)
            db_ref[...] = jnp.zeros(db_ref.shape, F32)
            dcb_ref[...] = jnp.zeros(dcb_ref.shape, F32)

        cv_v = cv_ref[...]
        mu = jnp.mean(cv_v, axis=-1, keepdims=True)
        cc = cv_v - mu
        rstd = lax.rsqrt(jnp.mean(cc * cc, axis=-1, keepdims=True) + EPS)
        nh = cc * rstd
        g = g_ref[...]
        n = nh * g + be_ref[...]
        sil, dsil = _silu_and_grad(n)
        sl_ref[...] = sil.astype(BF16)
        dyb = dy_ref[...].astype(BF16)
        dyb_ref[...] = dyb
        dn = lax.dot_general(dyb, pw_ref[...], NT, preferred_element_type=F32) * dsil
        db_ref[...] += jnp.sum(dn, axis=0, keepdims=True)
        dg_ref[...] += jnp.sum(dn * nh, axis=0, keepdims=True)
        dnh = dn * g
        dcv = rstd * (dnh - jnp.mean(dnh, axis=-1, keepdims=True) - nh * jnp.mean(dnh * nh, axis=-1, keepdims=True))
        dcv_ref[...] = dcv
        dcb_ref[...] += jnp.sum(dcv, axis=0, keepdims=True)

    vec = _full((1, CONV_W))
    return pl.pallas_call(
        body, name=name, grid=(S // tm,),
        in_specs=[_rows(tm, CONV_W), _rows(tm, CONV_W), vec, vec, _full((CONV_W, CONV_W))],
        out_specs=(_rows(tm, CONV_W), _rows(tm, CONV_W), _rows(tm, CONV_W), vec, vec, vec),
        out_shape=(_sds((S, CONV_W)), _sds((S, CONV_W), BF16), _sds((S, CONV_W), BF16),
                   _sds((1, CONV_W)), _sds((1, CONV_W)), _sds((1, CONV_W))),
        compiler_params=_cparams("arbitrary"),
    )(dyc, cv, ln_g, ln_b, w_pw2)


def _conv_backward(z, dcv, conv_w_p, name):
    S = z.shape[0]
    tm = ROW_TILE
    per = tm // HALO
    last_halo = S // HALO - 1

    def body(a_ref, b_ref, ap_ref, bp_ref, d_ref, dn_ref, w_ref, da_ref, db_ref, dw_ref, win, dwin):
        i = pl.program_id(0)

        @pl.when(i == 0)
        def _():
            dw_ref[...] = jnp.zeros(dw_ref.shape, F32)

        av, bv = a_ref[...], b_ref[...]
        _conv_window(win, ap_ref[...], bp_ref[...], av, bv, i == 0)
        dcur = d_ref[...]
        dwin[0:tm, :] = dcur
        dwin[tm:, :] = jnp.where(i == pl.num_programs(0) - 1, 0.0, dn_ref[...])
        dh = jnp.zeros((tm, CONV_W), F32)
        for kk in range(CONV_K):
            dh = dh + w_ref[kk:kk + 1, :] * dwin[pl.ds(CONV_K - 1 - kk, tm), :]
            dw_ref[kk:kk + 1, :] += jnp.sum(dcur * win[pl.ds(HALO - (CONV_K - 1) + kk, tm), :], axis=0, keepdims=True)
        sb = _sigmoid(bv)
        da_ref[...] = dh * sb
        db_ref[...] = dh * av * sb * (1.0 - sb)

    nxt = pl.BlockSpec((HALO, CONV_W), lambda i: (jnp.minimum((i + 1) * per, last_halo), 0))
    return pl.pallas_call(
        body, name=name, grid=(S // tm,),
        in_specs=_conv_in_specs(tm) + [_rows(tm, CONV_W), nxt, _full((HALO, CONV_W))],
        out_specs=(_rows(tm, CONV_W), _rows(tm, CONV_W), _full((HALO, CONV_W))),
        out_shape=(_sds((S, CONV_W)), _sds((S, CONV_W)), _sds((HALO, CONV_W))),
        scratch_shapes=[pltpu.VMEM((tm + HALO, CONV_W), F32), pltpu.VMEM((tm + HALO, CONV_W), F32)],
        compiler_params=_cparams("arbitrary"),
    )(z, z, z, z, dcv, dcv, conv_w_p)


def _sgu_backward(z, dy, ln_g, ln_b, w_s, bias_full, name):
    S = z.shape[0]
    tm = ROW_TILE

    def body(u_ref, v_ref, dy_ref, g_ref, be_ref, ws_ref, bias_ref, du_ref, dv_ref, dws_ref, dbs_ref, dg_ref, db_ref):
        @pl.when(pl.program_id(0) == 0)
        def _():
            dws_ref[...] = jnp.zeros(dws_ref.shape, F32)
            dbs_ref[...] = jnp.zeros(dbs_ref.shape, F32)
            dg_ref[...] = jnp.zeros(dg_ref.shape, F32)
            db_ref[...] = jnp.zeros(db_ref.shape, F32)

        gmask, tril = _sgu_masks()
        lane = lax.broadcasted_iota(jnp.int32, (1, 128), 1)
        wm = [jnp.where(tril, ws_ref[g], 0.0).astype(BF16) for g in range(SGU_G)]
        gain = g_ref[...]
        for ch in range(tm // SGU_T):
            rows = slice(ch * SGU_T, (ch + 1) * SGU_T)
            gu, dgu, dgv, rstd, nh, vn = _sgu_common(u_ref[rows, :], v_ref[rows, :], g_ref, be_ref)
            vb = vn.astype(BF16)
            sv = bias_ref[...]
            for g in range(SGU_G):
                sv = sv + jnp.where(gmask[g], jnp.dot(wm[g], vb, preferred_element_type=F32), 0.0)
            dyv = dy_ref[rows, :]
            du_ref[rows, :] = dyv * sv * dgu
            dsv = dyv * gu
            dsvb = dsv.astype(BF16)
            dvn = jnp.zeros((SGU_T, SGU_W), F32)
            for g in range(SGU_G):
                dsg = jnp.where(gmask[g], dsv, 0.0)
                dwg = lax.dot_general(dsg.astype(BF16), vb, NT, preferred_element_type=F32)
                dws_ref[g] += jnp.where(tril, dwg, 0.0)
                dvn = dvn + jnp.where(gmask[g], lax.dot_general(wm[g], dsvb, TN, preferred_element_type=F32), 0.0)
                dbs_ref[...] += jnp.where(lane == g, jnp.sum(dsg, axis=-1, keepdims=True), 0.0)
            db_ref[...] += jnp.sum(dvn, axis=0, keepdims=True)
            dg_ref[...] += jnp.sum(dvn * nh, axis=0, keepdims=True)
            dnh = dvn * gain
            dgvv = rstd * (dnh - jnp.mean(dnh, axis=-1, keepdims=True) - nh * jnp.mean(dnh * nh, axis=-1, keepdims=True))
            dv_ref[rows, :] = dgvv * dgv

    vec = _full((1, SGU_W))
    return pl.pallas_call(
        body, name=name, grid=(S // tm,),
        in_specs=[_rows(tm, SGU_W, 7), _rows(tm, SGU_W, 8), _rows(tm, SGU_W), vec, vec,
                  _full((SGU_G, SGU_T, SGU_T)), _full((SGU_T, SGU_W))],
        out_specs=(_rows(tm, SGU_W), _rows(tm, SGU_W), _full((SGU_G, SGU_T, SGU_T)), _full((SGU_T, 128)), vec, vec),
        out_shape=(_sds((S, SGU_W)), _sds((S, SGU_W)), _sds((SGU_G, SGU_T, SGU_T)), _sds((SGU_T, 128)),
                   _sds((1, SGU_W)), _sds((1, SGU_W))),
        compiler_params=_cparams("arbitrary"),
    )(z, z, dy, ln_g, ln_b, w_s, bias_full)


def _inproj_backward(x, dxo, dz_att, dga, dca, dcb, dgc, dsu, dsv, dgs, g_pre, mod, w_in_p, name):
    S = x.shape[0]
    tm = ROW_TILE

    def body(x_ref, dxo_ref, p0, p1, p2, p3, p4, p5, p6, p7, g_ref, mod_ref, w_ref,
             dx_ref, hb_ref, dzb_ref, dmod_ref, dg_ref):
        @pl.when(pl.program_id(0) == 0)
        def _():
            dmod_ref[...] = jnp.zeros(dmod_ref.shape, F32)
            dg_ref[...] = jnp.zeros(dg_ref.shape, F32)

        off = 0
        for piece in (p0, p1, p2, p3, p4, p5, p6, p7):
            wdt = piece.shape[1]
            dzb_ref[:, off:off + wdt] = piece[...].astype(BF16)
            off += wdt
        dh = lax.dot_general(dzb_ref[...], w_ref[...], NT, preferred_element_type=F32)
        xv = x_ref[...]
        g = g_ref[...]
        one_scale = 1.0 + mod_ref[1:2, :]
        rstd = lax.rsqrt(jnp.mean(xv * xv, axis=-1, keepdims=True) + EPS)
        xhat = xv * rstd
        xg = xhat * g
        hb_ref[...] = (xg * one_scale + mod_ref[0:1, :]).astype(BF16)
        dmod_ref[0:1, :] += jnp.sum(dh, axis=0, keepdims=True)
        dmod_ref[1:2, :] += jnp.sum(dh * xg, axis=0, keepdims=True)
        dhs = dh * one_scale
        dg_ref[...] += jnp.sum(dhs * xhat, axis=0, keepdims=True)
        dxh = dhs * g
        dx_ref[...] = dxo_ref[...] + rstd * (dxh - xhat * jnp.mean(dxh * xhat, axis=-1, keepdims=True))

    widths = (512, 512, 256, 256, 256, 256, 256, 256)
    return pl.pallas_call(
        body, name=name, grid=(S // tm,),
        in_specs=[_rows(tm, D), _rows(tm, D)] + [_rows(tm, w) for w in widths]
                 + [_full((1, D)), _full((3, D)), _full((D, DZ))],
        out_specs=(_rows(tm, D), _rows(tm, D), _rows(tm, DZ), _full((2, D)), _full((1, D))),
        out_shape=(_sds((S, D)), _sds((S, D), BF16), _sds((S, DZ), BF16), _sds((2, D)), _sds((1, D))),
        compiler_params=_cparams("arbitrary"),
    )(x, dxo, dz_att, dga, dca, dcb, dgc, dsu, dsv, dgs, g_pre, mod, w_in_p)


def _adamw(w, gparts, m, v, name):
    shape = w.shape
    cols = shape[-1]
    rows = int(np.prod(shape[:-1]))
    parts = gparts.shape[0]
    w2, m2, v2 = (a.reshape(rows, cols) for a in (w, m, v))
    g3 = gparts.reshape(parts, rows, cols)
    tr = rows
    for cand in (256, 128):
        if rows > cand and rows % cand == 0:
            tr = cand
            break

    def body(w_ref, g_ref, m_ref, v_ref, go_ref, d_ref, mo_ref, vo_ref):
        g = g_ref[0].astype(F32)
        for p in range(1, parts):
            g = g + g_ref[p].astype(F32)
        wv = w_ref[...]
        mn = ADAM_B1 * m_ref[...] + (1.0 - ADAM_B1) * g
        vn = ADAM_B2 * v_ref[...] + (1.0 - ADAM_B2) * (g * g)
        m_hat = mn / (1.0 - ADAM_B1 ** ADAM_STEP)
        v_hat = vn / (1.0 - ADAM_B2 ** ADAM_STEP)
        go_ref[...] = g
        d_ref[...] = -ADAM_LR * (m_hat / (jnp.sqrt(v_hat) + ADAM_EPS) + ADAM_WD * wv)
        mo_ref[...] = mn
        vo_ref[...] = vn

    blk = pl.BlockSpec((tr, cols), lambda i: (i, 0))
    outs = pl.pallas_call(
        body, name=name, grid=(rows // tr,),
        in_specs=[blk, pl.BlockSpec((parts, tr, cols), lambda i: (0, i, 0)), blk, blk],
        out_specs=(blk, blk, blk, blk),
        out_shape=tuple(_sds((rows, cols)) for _ in range(4)),
        compiler_params=_cparams("parallel"),
    )(w2, g3, m2, v2)
    return tuple(o.reshape(shape) for o in outs)


_GATHERED = (("w_in", (DEPTH, D, D_IN // N_DEV)), ("w_out", (DEPTH, D // N_DEV, D)),
             ("w_uq", (DEPTH, Q_RANK, HEADS * QK // N_DEV)), ("w_ukv", (DEPTH, KV_RANK, HEADS * HQ // N_DEV)),
             ("w_pw2", (DEPTH, CONV_W // N_DEV, CONV_W)), ("conv_w", (DEPTH, CONV_K, CONV_W // N_DEV)))
_COL_SHARDED = ("w_in", "w_uq", "w_ukv", "conv_w")

_SMALL = (("dmod", (DEPTH, 3 * D)), ("g_pre", (DEPTH, D)), ("g_post", (DEPTH, D)), ("q_norm_g", (DEPTH, Q_RANK)),
          ("kv_norm_g", (DEPTH, KV_RANK)), ("conv_b", (DEPTH, CONV_W)), ("conv_ln_g", (DEPTH, CONV_W)),
          ("conv_ln_b", (DEPTH, CONV_W)), ("sgu_ln_g", (DEPTH, SGU_W)), ("sgu_ln_b", (DEPTH, SGU_W)),
          ("w_s", (DEPTH, SGU_G, SGU_T, SGU_T)), ("b_s", (DEPTH, SGU_G, SGU_T)))


def _assemble(name, parts):
    if name in _COL_SHARDED:
        p = jnp.moveaxis(parts, 0, -2)
        return p.reshape(p.shape[:-2] + (p.shape[-2] * p.shape[-1],))
    p = jnp.moveaxis(parts, 0, 1)
    return p.reshape((p.shape[0], p.shape[1] * p.shape[2], p.shape[3]))


def _scatter_layout(name, full):
    if name in _COL_SHARDED:
        p = full.reshape(full.shape[:-1] + (N_DEV, full.shape[-1] // N_DEV))
        return jnp.moveaxis(p, -2, 0)
    p = full.reshape((full.shape[0], N_DEV, full.shape[1] // N_DEV, full.shape[2]))
    return jnp.moveaxis(p, 1, 0)


def kernel(x, c, positions, w_ada, b_ada, g_pre, g_post, w_in, q_norm_g, w_uq, kv_norm_g, w_ukv, conv_w, conv_b, conv_ln_g, conv_ln_b, w_pw2, sgu_ln_g, sgu_ln_b, w_s, b_s, w_out, loss_target, m_w_ada, m_b_ada, m_g_pre, m_g_post, m_w_in, m_q_norm_g, m_w_uq, m_kv_norm_g, m_w_ukv, m_conv_w, m_conv_b, m_conv_ln_g, m_conv_ln_b, m_w_pw2, m_sgu_ln_g, m_sgu_ln_b, m_w_s, m_b_s, m_w_out, v_w_ada, v_b_ada, v_g_pre, v_g_post, v_w_in, v_q_norm_g, v_w_uq, v_kv_norm_g, v_w_ukv, v_conv_w, v_conv_b, v_conv_ln_g, v_conv_ln_b, v_w_pw2, v_sgu_ln_g, v_sgu_ln_b, v_w_s, v_b_s, v_w_out):
    weights = dict(w_ada=w_ada, b_ada=b_ada, g_pre=g_pre, g_post=g_post, w_in=w_in, q_norm_g=q_norm_g, w_uq=w_uq,
                   kv_norm_g=kv_norm_g, w_ukv=w_ukv, conv_w=conv_w, conv_b=conv_b, conv_ln_g=conv_ln_g,
                   conv_ln_b=conv_ln_b, w_pw2=w_pw2, sgu_ln_g=sgu_ln_g, sgu_ln_b=sgu_ln_b, w_s=w_s, b_s=b_s, w_out=w_out)
    m_in = dict(w_ada=m_w_ada, b_ada=m_b_ada, g_pre=m_g_pre, g_post=m_g_post, w_in=m_w_in, q_norm_g=m_q_norm_g,
                w_uq=m_w_uq, kv_norm_g=m_kv_norm_g, w_ukv=m_w_ukv, conv_w=m_conv_w, conv_b=m_conv_b,
                conv_ln_g=m_conv_ln_g, conv_ln_b=m_conv_ln_b, w_pw2=m_w_pw2, sgu_ln_g=m_sgu_ln_g,
                sgu_ln_b=m_sgu_ln_b, w_s=m_w_s, b_s=m_b_s, w_out=m_w_out)
    v_in = dict(w_ada=v_w_ada, b_ada=v_b_ada, g_pre=v_g_pre, g_post=v_g_post, w_in=v_w_in, q_norm_g=v_q_norm_g,
                w_uq=v_w_uq, kv_norm_g=v_kv_norm_g, w_ukv=v_w_ukv, conv_w=v_conv_w, conv_b=v_conv_b,
                conv_ln_g=v_conv_ln_g, conv_ln_b=v_conv_ln_b, w_pw2=v_w_pw2, sgu_ln_g=v_sgu_ln_g,
                sgu_ln_b=v_sgu_ln_b, w_s=v_w_s, b_s=v_b_s, w_out=v_w_out)
    order = list(weights)

    S = x.shape[1]
    me = 4 * lax.axis_index("x") + 2 * lax.axis_index("y") + lax.axis_index("c")
    x0 = x.reshape(S, D)
    target = loss_target.reshape(S, D)
    pos = positions.reshape(S, 1)

    wpack = _Packer(_GATHERED, 16)
    gathered = _exchange(wpack.pack(weights, BF16), all_to_all=False, name="gather_weights")
    full = {n: _assemble(n, p) for n, p in wpack.unpack(gathered, (N_DEV,)).items()}
    c_all = _exchange(c.reshape(8, D // 8), all_to_all=False, name="gather_c").reshape(N_DEV, D)

    ada_cols = w_ada.shape[-1]
    b_cols = lax.dynamic_slice_in_dim(b_ada, me * ada_cols, ada_cols, axis=1)
    sc_rows, mod_part = _ada_forward(jnp.pad(c_all, ((0, 8), (0, 0))), w_ada, b_cols)
    mod_recv = _exchange(jnp.moveaxis(mod_part[:, :N_DEV], 1, 0), all_to_all=True, name="exchange_mod")
    mod = jnp.moveaxis(mod_recv, 0, 1).reshape(DEPTH, 3, D)

    w_in_f = full["w_in"]
    w_in_p = jnp.concatenate([w_in_f[:, :, :ATT_IN], jnp.zeros((DEPTH, D, PAD_IN), BF16), w_in_f[:, :, ATT_IN:]], axis=2)
    wq_p = jnp.pad(full["w_uq"].reshape(DEPTH, Q_RANK, HEADS, QK), ((0, 0), (0, 0), (0, 0), (0, HQ - QK)))
    wq_p = wq_p.reshape(DEPTH, Q_RANK, HEADS * HQ)
    conv_w_p = jnp.pad(full["conv_w"].astype(F32), ((0, 0), (0, HALO - CONV_K), (0, 0)))
    bias_full = jnp.repeat(jnp.swapaxes(b_s, 1, 2), SGU_GD, axis=2)
    inv_freq = ROPE_THETA ** (-jnp.arange(0, ROPE, 2, dtype=F32) / ROPE)
    zeros32 = jnp.zeros((ROPE // 2,), F32)
    ones32 = jnp.ones((ROPE // 2,), F32)
    rope_rows = jnp.zeros((8, 128), F32)
    rope_rows = rope_rows.at[0].set(jnp.concatenate([inv_freq, inv_freq, zeros32, zeros32]))
    rope_rows = rope_rows.at[1].set(jnp.concatenate([ones32, ones32, zeros32, zeros32]))
    rope_rows = rope_rows.at[2].set(jnp.concatenate([-ones32, ones32, zeros32, zeros32]))

    def vec(a, l):
        return a[l].reshape(1, -1)

    saved = []
    xl = x0
    for l in range(DEPTH):
        z = _prenorm_inproj(xl, vec(g_pre, l), mod[l], w_in_p[l], name=f"prenorm_inproj_{l}")
        q, k, v = _att_prep(z, pos, vec(q_norm_g, l), vec(kv_norm_g, l), wq_p[l], full["w_ukv"][l], rope_rows,
                            name=f"att_prep_{l}")
        y_att, lse = _flash_forward(q, k, v, name=f"flash_forward_{l}")
        cv, y_conv = _conv_forward(z, conv_w_p[l], vec(conv_b, l), vec(conv_ln_g, l), vec(conv_ln_b, l),
                                   full["w_pw2"][l], name=f"conv_forward_{l}")
        y_sgu = _sgu_forward(z, vec(sgu_ln_g, l), vec(sgu_ln_b, l), w_s[l], bias_full[l], name=f"sgu_forward_{l}")
        x_next, y, ycat = _out_proj(xl, z, y_att, y_conv, y_sgu, full["w_out"][l], vec(g_post, l), mod[l],
                                    name=f"out_proj_{l}")
        saved.append(dict(x=xl, z=z, q=q, k=k, v=v, y_att=y_att, lse=lse, cv=cv, y_conv=y_conv, y_sgu=y_sgu, y=y, ycat=ycat))
        xl = x_next

    loss_part, dx = _loss_head(xl, target)
    loss = lax.psum(loss_part.reshape(()), ("x", "y", "c"))

    big = {n: [None] * DEPTH for n, _ in _GATHERED}
    small = {n: [None] * DEPTH for n, _ in _SMALL}
    for l in reversed(range(DEPTH)):
        sv = saved[l]
        (dyb, dob, stats, dga, dyc, dgc, dys, dgs, dgate, dgpost) = _out_proj_backward(
            dx, sv["y"], sv["z"], sv["y_att"], sv["y_conv"], sv["y_sgu"], sv["lse"], full["w_out"][l],
            vec(g_post, l), mod[l], name=f"out_proj_backward_{l}")
        big["w_out"][l] = _matmul_tn(sv["ycat"], dyb, name=f"grad_w_out_{l}")
        dq, dk, dv = _flash_backward(sv["q"], sv["k"], sv["v"], dob, stats, name=f"flash_backward_{l}")
        dz_att, qn_b, dqp_b, kvn_b, dkv_b, dqg, dkvg = _att_prep_backward(
            sv["z"], pos, dq, dk, dv, vec(q_norm_g, l), vec(kv_norm_g, l), wq_p[l], full["w_ukv"][l], rope_rows,
            name=f"att_prep_backward_{l}")
        dwq_p = _matmul_tn(qn_b, dqp_b, name=f"grad_w_uq_{l}")
        big["w_uq"][l] = dwq_p.reshape(Q_RANK, HEADS, HQ)[:, :, :QK].reshape(Q_RANK, HEADS * QK)
        big["w_ukv"][l] = _matmul_tn(kvn_b, dkv_b, name=f"grad_w_ukv_{l}")
        dcv, sl_b, dyc_b, dclg, dclb, dcb = _conv_norm_backward(dyc, sv["cv"], vec(conv_ln_g, l), vec(conv_ln_b, l),
                                                              full["w_pw2"][l], name=f"conv_norm_backward_{l}")
        big["w_pw2"][l] = _matmul_tn(sl_b, dyc_b, name=f"grad_w_pw2_{l}")
        dca, dcbb, dconvw = _conv_backward(sv["z"], dcv, conv_w_p[l], name=f"conv_backward_{l}")
        big["conv_w"][l] = dconvw[:CONV_K]
        dsu, dsvv, dws, dbs, dslg, dslb = _sgu_backward(sv["z"], dys, vec(sgu_ln_g, l), vec(sgu_ln_b, l), w_s[l],
                                                       bias_full[l], name=f"sgu_backward_{l}")
        dx, h_b, dz_b, dmod2, dgpre = _inproj_backward(
            sv["x"], dx, dz_att, dga, dca, dcbb, dgc, dsu, dsvv, dgs, vec(g_pre, l), mod[l], w_in_p[l],
            name=f"inproj_backward_{l}")
        dwin_p = _matmul_tn(h_b, dz_b, name=f"grad_w_in_{l}")
        big["w_in"][l] = jnp.concatenate([dwin_p[:, :ATT_IN], dwin_p[:, ATT_IN + PAD_IN:]], axis=1)
        small["dmod"][l] = jnp.concatenate([dmod2.reshape(-1), dgate.reshape(-1)])
        small["g_pre"][l] = dgpre.reshape(-1)
        small["g_post"][l] = dgpost.reshape(-1)
        small["q_norm_g"][l] = dqg.reshape(-1)
        small["kv_norm_g"][l] = dkvg.reshape(-1)
        small["conv_b"][l] = dcb.reshape(-1)
        small["conv_ln_g"][l] = dclg.reshape(-1)
        small["conv_ln_b"][l] = dclb.reshape(-1)
        small["sgu_ln_g"][l] = dslg.reshape(-1)
        small["sgu_ln_b"][l] = dslb.reshape(-1)
        small["w_s"][l] = dws
        small["b_s"][l] = jnp.swapaxes(dbs[:, :SGU_G], 0, 1)
    grad_x = dx.reshape(1, S, D)

    gpack = _Packer(_GATHERED, 16)
    send = gpack.pack({n: _scatter_layout(n, jnp.stack(big[n])) for n, _ in _GATHERED}, BF16, lead=(N_DEV,))
    gparts = gpack.unpack(_exchange(send, all_to_all=True, name="exchange_grads"), (N_DEV,))
    spack = _Packer(_SMALL, 8)
    sparts = spack.unpack(_exchange(spack.pack({n: jnp.stack(small[n]) for n, _ in _SMALL}, F32),
                                    all_to_all=False, name="gather_small_grads"), (N_DEV,))
    dmod_all = sparts["dmod"]
    dmod_cols = lax.dynamic_slice_in_dim(dmod_all, me * ada_cols, ada_cols, axis=2)
    sc_t = jnp.pad(sc_rows[:N_DEV].T, ((0, 0), (0, 128 - N_DEV)))
    dmod_rows = jnp.pad(jnp.moveaxis(dmod_cols, 0, 1), ((0, 0), (0, 128 - N_DEV), (0, 0)))
    gparts["w_ada"] = _ada_backward(sc_t, dmod_rows)[None]
    gparts["b_ada"] = dmod_all
    for n, _ in _SMALL[1:]:
        gparts[n] = sparts[n]

    grads, deltas, new_m, new_v = {}, {}, {}, {}
    for n in order:
        grads[n], deltas[n], new_m[n], new_v[n] = _adamw(weights[n], gparts[n], m_in[n], v_in[n], name=f"adamw_{n}")
    return (loss, grad_x, *[grads[n] for n in order], *[deltas[n] for n in order],
            *[new_m[n] for n in order], *[new_v[n] for n in order])
```

```python
import functools
import math

import numpy as np
import jax
import jax.numpy as jnp
from jax import lax
from jax.experimental import pallas as pl
from jax.experimental.pallas import tpu as pltpu

F32 = jnp.float32
BF16 = jnp.bfloat16

N_DEV = 8
DEPTH = 2
D = 1024
HEADS = 4
NOPE = 128
ROPE = 64
VDIM = 128
QK = NOPE + ROPE
Q_RANK = 256
KV_RANK = 128
ATT_W = HEADS * VDIM
CONV_W = 256
CONV_K = 31
SGU_W = 256
SGU_G = 4
SGU_GD = SGU_W // SGU_G
SGU_T = 128
D_IN = 2496
ATT_IN = Q_RANK + KV_RANK + ROPE
PAD_IN = 64
DZ = D_IN + PAD_IN
HQ = 2 * NOPE
EPS = 1e-6
ROPE_THETA = 10000.0
ATT_SCALE = QK ** -0.5
LOG2E = math.log2(math.e)
EXP2_SCALE = ATT_SCALE * LOG2E
NEG_INF = float("-inf")

ADAM_LR = 0.001
ADAM_B1 = 0.9
ADAM_B2 = 0.999
ADAM_EPS = 1e-08
ADAM_WD = 0.01
ADAM_STEP = 10

VMEM_LIMIT = 56 * 1024 * 1024
ROW_TILE = 256
ATT_TILE = 512
HALO = 32
PACK_LANES = 128

MESH = pl.DeviceIdType.MESH
NT = (((1,), (1,)), ((), ()))
TN = (((0,), (0,)), ((), ()))


def _cparams(*sem):
    return pltpu.CompilerParams(dimension_semantics=sem, vmem_limit_bytes=VMEM_LIMIT)


def _sds(shape, dtype=F32):
    return jax.ShapeDtypeStruct(tuple(shape), dtype)


def _rows(tm, width, col=0):
    return pl.BlockSpec((tm, width), lambda i: (i, col))


def _full(shape):
    nd = len(shape)
    return pl.BlockSpec(tuple(shape), lambda *_: (0,) * nd)


def _sigmoid(x):
    return 1.0 / (1.0 + jnp.exp(-x))


def _silu_and_grad(g):
    s = _sigmoid(g)
    return g * s, s * (1.0 + g * (1.0 - s))


def _gelu_and_grad(x):
    cdf = 0.5 * (1.0 + lax.erf(x * (1.0 / math.sqrt(2.0))))
    pdf = jnp.exp(-0.5 * x * x) * (1.0 / math.sqrt(2.0 * math.pi))
    return x * cdf, cdf + x * pdf


def _swap_halves(a):
    lane = lax.broadcasted_iota(jnp.int32, a.shape, 1)
    up = pltpu.roll(a, 32, 1)
    down = pltpu.roll(a, 96, 1)
    return jnp.where(lane < 32, down, jnp.where(lane < 64, up, 0.0))


def _rope_tables(pos_ref, rope_ref):
    ang = pos_ref[...].astype(F32) * rope_ref[0:1, :]
    return jnp.cos(ang) * rope_ref[1:2, :], jnp.sin(ang) * rope_ref[2:3, :]


def _exchange(srcs, *, all_to_all, name):
    n = len(srcs)
    shapes = [tuple(s.shape[1:]) if all_to_all else tuple(s.shape) for s in srcs]

    def body(*refs):
        src_refs, out_refs = refs[:n], refs[n:2 * n]
        send_sems, recv_sems, local_sems = refs[2 * n:]
        x, y, c = lax.axis_index("x"), lax.axis_index("y"), lax.axis_index("c")
        me = 4 * x + 2 * y + c
        local, sends, recvs = [], [], []
        for a in range(n):
            def block_for(dest, src_ref=src_refs[a]):
                return src_ref.at[dest] if all_to_all else src_ref

            local.append(pltpu.make_async_copy(block_for(me), out_refs[a].at[me], local_sems.at[a]))
            for r in range(1, N_DEV):
                px = 1 - x if (r >> 2) & 1 else x
                py = 1 - y if (r >> 1) & 1 else y
                pc = 1 - c if r & 1 else c
                peer = 4 * px + 2 * py + pc
                sends.append(pltpu.make_async_remote_copy(
                    src_ref=block_for(peer), dst_ref=out_refs[a].at[me],
                    send_sem=send_sems.at[a, r - 1], recv_sem=recv_sems.at[a, r - 1],
                    device_id=(px, py, pc), device_id_type=MESH))
                recvs.append(pltpu.make_async_remote_copy(
                    src_ref=block_for(me), dst_ref=out_refs[a].at[peer],
                    send_sem=send_sems.at[a, r - 1], recv_sem=recv_sems.at[a, r - 1],
                    device_id=(px, py, pc), device_id_type=MESH))
        for cp in local + sends:
            cp.start()
        for cp in recvs:
            cp.wait_recv()
        for cp in sends:
            cp.wait_send()
        for cp in local:
            cp.wait()

    return pl.pallas_call(
        body, name=name,
        out_shape=[_sds((N_DEV,) + shp, s.dtype) for shp, s in zip(shapes, srcs)],
        in_specs=[pl.BlockSpec(memory_space=pl.ANY)] * n,
        out_specs=[pl.BlockSpec(memory_space=pl.ANY)] * n,
        scratch_shapes=[pltpu.SemaphoreType.DMA((n, N_DEV - 1)), pltpu.SemaphoreType.DMA((n, N_DEV - 1)),
                        pltpu.SemaphoreType.DMA((n,))],
    )(*srcs)


class _Packer:
    def __init__(self, entries, row_multiple):
        self.entries = entries
        self.offsets = {}
        off = 0
        for name, shape in entries:
            self.offsets[name] = off
            off += int(np.prod(shape))
        quantum = PACK_LANES * row_multiple
        self.total = -(-off // quantum) * quantum
        self.used = off
        self.rows = self.total // PACK_LANES

    def pack(self, arrays, dtype, lead=()):
        n = len(lead)
        flat = [arrays[name].astype(dtype).reshape(lead + (-1,)) for name, _ in self.entries]
        flat.append(jnp.zeros(lead + (self.total - self.used,), dtype))
        return jnp.concatenate(flat, axis=n).reshape(lead + (self.rows, PACK_LANES))

    def unpack(self, buf, lead=()):
        flat = buf.reshape(lead + (self.total,))
        out = {}
        for name, shape in self.entries:
            o = self.offsets[name]
            out[name] = lax.slice_in_dim(flat, o, o + int(np.prod(shape)), axis=len(lead)).reshape(lead + tuple(shape))
        return out


def _ada_forward(c_rows, w_ada, b_ada_cols):
    cols = w_ada.shape[-1]
    rows = c_rows.shape[0]

    def body(c_ref, w_ref, b_ref, sc_ref, part_ref):
        cv = c_ref[...]
        sc = cv * _sigmoid(cv)
        sc_ref[...] = sc
        scb = sc.astype(BF16)
        for l in range(DEPTH):
            part_ref[l] = jnp.dot(scb, w_ref[l].astype(BF16), preferred_element_type=F32) + b_ref[l:l + 1, :]

    return pl.pallas_call(
        body, name="ada_forward",
        out_shape=(_sds((rows, D)), _sds((DEPTH, rows, cols))),
        compiler_params=pltpu.CompilerParams(vmem_limit_bytes=VMEM_LIMIT),
    )(c_rows, w_ada, b_ada_cols)


def _ada_backward(sc_t, dmod_cols):
    cols = dmod_cols.shape[-1]

    def body(sc_ref, dm_ref, gw_ref):
        scb = sc_ref[...].astype(BF16)
        for l in range(DEPTH):
            gw_ref[l] = jnp.dot(scb, dm_ref[l].astype(BF16), preferred_element_type=F32)

    return pl.pallas_call(
        body, name="ada_backward",
        out_shape=_sds((DEPTH, D, cols)),
        compiler_params=pltpu.CompilerParams(vmem_limit_bytes=VMEM_LIMIT),
    )(sc_t, dmod_cols)


def _prenorm_inproj(x, g_pre, mod, w_in_p, name):
    S = x.shape[0]
    tm = ROW_TILE

    def body(x_ref, g_ref, mod_ref, w_ref, z_ref):
        xv = x_ref[...]
        rstd = lax.rsqrt(jnp.mean(xv * xv, axis=-1, keepdims=True) + EPS)
        h = (xv * rstd * g_ref[...]) * (1.0 + mod_ref[1:2, :]) + mod_ref[0:1, :]
        z_ref[...] = jnp.dot(h.astype(BF16), w_ref[...], preferred_element_type=F32)

    return pl.pallas_call(
        body, name=name, grid=(S // tm,),
        in_specs=[_rows(tm, D), _full((1, D)), _full((3, D)), _full((D, DZ))],
        out_specs=_rows(tm, DZ), out_shape=_sds((S, DZ)),
        compiler_params=_cparams("parallel"),
    )(x, g_pre, mod, w_in_p)


def _att_prep(z, pos, q_g, kv_g, wq_p, w_ukv, rope_rows, name):
    S = z.shape[0]
    tm = ROW_TILE

    def body(z_ref, pos_ref, qg_ref, kvg_ref, wq_ref, wkv_ref, rope_ref, q_ref, k_ref, v_ref):
        zz = z_ref[...]
        ql, kvl, ka = zz[:, 0:Q_RANK], zz[:, Q_RANK:Q_RANK + KV_RANK], zz[:, Q_RANK + KV_RANK:]
        qn = ql * lax.rsqrt(jnp.mean(ql * ql, axis=-1, keepdims=True) + EPS) * qg_ref[...]
        kvn = kvl * lax.rsqrt(jnp.mean(kvl * kvl, axis=-1, keepdims=True) + EPS) * kvg_ref[...]
        q = jnp.dot(qn.astype(BF16), wq_ref[...], preferred_element_type=F32)
        kv = jnp.dot(kvn.astype(BF16), wkv_ref[...], preferred_element_type=F32)
        ct, st = _rope_tables(pos_ref, rope_ref)
        krot = (ka * ct + _swap_halves(ka) * st).astype(BF16)
        for h in range(HEADS):
            b = h * HQ
            q_ref[:, b:b + NOPE] = q[:, b:b + NOPE].astype(BF16)
            a = q[:, b + NOPE:b + HQ]
            q_ref[:, b + NOPE:b + HQ] = (a * ct + _swap_halves(a) * st).astype(BF16)
            k_ref[:, b:b + NOPE] = kv[:, b:b + NOPE].astype(BF16)
            k_ref[:, b + NOPE:b + HQ] = krot
            v_ref[:, h * VDIM:(h + 1) * VDIM] = kv[:, b + NOPE:b + HQ].astype(BF16)

    return pl.pallas_call(
        body, name=name, grid=(S // tm,),
        in_specs=[_rows(tm, 512, 0), _rows(tm, 1), _full((1, Q_RANK)), _full((1, KV_RANK)),
                  _full((Q_RANK, HEADS * HQ)), _full((KV_RANK, HEADS * HQ)), _full((8, 128))],
        out_specs=(_rows(tm, HEADS * HQ), _rows(tm, HEADS * HQ), _rows(tm, ATT_W)),
        out_shape=(_sds((S, HEADS * HQ), BF16), _sds((S, HEADS * HQ), BF16), _sds((S, ATT_W), BF16)),
        compiler_params=_cparams("parallel"),
    )(z, pos, q_g, kv_g, wq_p, w_ukv, rope_rows)


def _flash_forward(q, k, v, name):
    S = q.shape[0]
    t = ATT_TILE
    tq = 2 * t
    nq = S // tq

    def body(q_ref, k_ref, v_ref, o_ref, lse_ref, m_sc, l_sc, acc_sc):
        qb = pl.program_id(1)
        m_sc[...] = jnp.full(m_sc.shape, NEG_INF, F32)
        l_sc[...] = jnp.zeros(l_sc.shape, F32)
        acc_sc[...] = jnp.zeros(acc_sc.shape, F32)

        def half_step(hf, kb, diagonal):
            rows = pl.ds(pl.multiple_of(kb * t, t), t)
            s = lax.dot_general(q_ref[hf * t:(hf + 1) * t, :], k_ref[rows, :], NT, preferred_element_type=F32)
            if diagonal:
                ri = lax.broadcasted_iota(jnp.int32, (t, t), 0)
                ci = lax.broadcasted_iota(jnp.int32, (t, t), 1)
                s = jnp.where(ci <= ri, s, NEG_INF)
            m_prev = m_sc[hf]
            m_new = jnp.maximum(m_prev, jnp.max(s, axis=-1, keepdims=True))
            alpha = jnp.exp2((m_prev - m_new) * EXP2_SCALE)
            p = jnp.exp2(s * EXP2_SCALE - m_new * EXP2_SCALE)
            l_sc[hf] = alpha * l_sc[hf] + jnp.sum(p, axis=-1, keepdims=True)
            acc_sc[hf] = alpha * acc_sc[hf] + jnp.dot(p.astype(BF16), v_ref[rows, :], preferred_element_type=F32)
            m_sc[hf] = m_new

        def loop_body(kb, carry):
            half_step(0, kb, False)
            half_step(1, kb, False)
            return carry

        lax.fori_loop(0, 2 * qb, loop_body, 0)
        half_step(0, 2 * qb, True)
        half_step(1, 2 * qb, False)
        half_step(1, 2 * qb + 1, True)
        for hf in range(2):
            o_ref[hf * t:(hf + 1) * t, :] = acc_sc[hf] / l_sc[hf]
            lse_ref[0, hf * t:(hf + 1) * t, :] = m_sc[hf] * ATT_SCALE + jnp.log(l_sc[hf])

    return pl.pallas_call(
        body, name=name, grid=(HEADS, nq),
        in_specs=[pl.BlockSpec((tq, HQ), lambda h, i: (i, h)),
                  pl.BlockSpec((S, HQ), lambda h, i: (0, h)),
                  pl.BlockSpec((S, VDIM), lambda h, i: (0, h))],
        out_specs=(pl.BlockSpec((tq, VDIM), lambda h, i: (i, h)),
                   pl.BlockSpec((1, tq, 1), lambda h, i: (h, i, 0))),
        out_shape=(_sds((S, ATT_W)), _sds((HEADS, S, 1))),
        scratch_shapes=[pltpu.VMEM((2, t, 1), F32), pltpu.VMEM((2, t, 1), F32), pltpu.VMEM((2, t, VDIM), F32)],
        compiler_params=_cparams("parallel", "parallel"),
    )(q, k, v)


def _conv_window(win_ref, a_prev, b_prev, a_cur, b_cur, first):
    hp = a_prev * _sigmoid(b_prev)
    win_ref[0:HALO, :] = jnp.where(first, 0.0, hp)
    win_ref[HALO:, :] = a_cur * _sigmoid(b_cur)


def _conv_in_specs(tm):
    per = tm // HALO
    prev = lambda col: pl.BlockSpec((HALO, CONV_W), lambda i: (jnp.maximum(i * per - 1, 0), col))
    return [_rows(tm, CONV_W, 4), _rows(tm, CONV_W, 5), prev(4), prev(5)]


def _conv_forward(z, conv_w_p, conv_b, ln_g, ln_b, w_pw2, name):
    S = z.shape[0]
    tm = ROW_TILE

    def body(a_ref, b_ref, ap_ref, bp_ref, w_ref, cb_ref, g_ref, be_ref, pw_ref, cv_ref, y_ref, win):
        _conv_window(win, ap_ref[...], bp_ref[...], a_ref[...], b_ref[...], pl.program_id(0) == 0)
        acc = jnp.zeros((tm, CONV_W), F32)
        for kk in range(CONV_K):
            acc = acc + w_ref[kk:kk + 1, :] * win[pl.ds(HALO - (CONV_K - 1) + kk, tm), :]
        cv = acc + cb_ref[...]
        cv_ref[...] = cv
        mu = jnp.mean(cv, axis=-1, keepdims=True)
        cc = cv - mu
        rstd = lax.rsqrt(jnp.mean(cc * cc, axis=-1, keepdims=True) + EPS)
        n = cc * rstd * g_ref[...] + be_ref[...]
        sl = n * _sigmoid(n)
        y_ref[...] = jnp.dot(sl.astype(BF16), pw_ref[...], preferred_element_type=F32)

    return pl.pallas_call(
        body, name=name, grid=(S // tm,),
        in_specs=_conv_in_specs(tm) + [_full((HALO, CONV_W)), _full((1, CONV_W)), _full((1, CONV_W)),
                                       _full((1, CONV_W)), _full((CONV_W, CONV_W))],
        out_specs=(_rows(tm, CONV_W), _rows(tm, CONV_W)),
        out_shape=(_sds((S, CONV_W)), _sds((S, CONV_W))),
        scratch_shapes=[pltpu.VMEM((tm + HALO, CONV_W), F32)],
        compiler_params=_cparams("parallel"),
    )(z, z, z, z, conv_w_p, conv_b, ln_g, ln_b, w_pw2)


def _sgu_common(u, v, g_ref, be_ref):
    gu, dgu = _gelu_and_grad(u)
    gv, dgv = _gelu_and_grad(v)
    mu = jnp.mean(gv, axis=-1, keepdims=True)
    cc = gv - mu
    rstd = lax.rsqrt(jnp.mean(cc * cc, axis=-1, keepdims=True) + EPS)
    nh = cc * rstd
    vn = nh * g_ref[...] + be_ref[...]
    return gu, dgu, dgv, rstd, nh, vn


def _sgu_masks():
    lane_group = lax.broadcasted_iota(jnp.int32, (1, SGU_W), 1) // SGU_GD
    ri = lax.broadcasted_iota(jnp.int32, (SGU_T, SGU_T), 0)
    ci = lax.broadcasted_iota(jnp.int32, (SGU_T, SGU_T), 1)
    return [lane_group == g for g in range(SGU_G)], ci <= ri


def _sgu_forward(z, ln_g, ln_b, w_s, bias_full, name):
    S = z.shape[0]
    tm = ROW_TILE

    def body(u_ref, v_ref, g_ref, be_ref, ws_ref, bias_ref, y_ref):
        gmask, tril = _sgu_masks()
        wm = [jnp.where(tril, ws_ref[g], 0.0).astype(BF16) for g in range(SGU_G)]
        for ch in range(tm // SGU_T):
            rows = slice(ch * SGU_T, (ch + 1) * SGU_T)
            gu, _, _, _, _, vn = _sgu_common(u_ref[rows, :], v_ref[rows, :], g_ref, be_ref)
            vb = vn.astype(BF16)
            sv = bias_ref[...]
            for g in range(SGU_G):
                sv = sv + jnp.where(gmask[g], jnp.dot(wm[g], vb, preferred_element_type=F32), 0.0)
            y_ref[rows, :] = gu * sv

    return pl.pallas_call(
        body, name=name, grid=(S // tm,),
        in_specs=[_rows(tm, SGU_W, 7), _rows(tm, SGU_W, 8), _full((1, SGU_W)), _full((1, SGU_W)),
                  _full((SGU_G, SGU_T, SGU_T)), _full((SGU_T, SGU_W))],
        out_specs=_rows(tm, SGU_W), out_shape=_sds((S, SGU_W)),
        compiler_params=_cparams("parallel"),
    )(z, z, ln_g, ln_b, w_s, bias_full)


def _out_proj(x, z, y_att, y_conv, y_sgu, w_out, g_post, mod, name):
    S = x.shape[0]
    tm = ROW_TILE

    def body(x_ref, ga_ref, gc_ref, gs_ref, ya_ref, yc_ref, ys_ref, w_ref, gp_ref, mod_ref, xn_ref, y_ref, cat_ref):
        ca = (ya_ref[...] * _silu_and_grad(ga_ref[...])[0]).astype(BF16)
        cc = (yc_ref[...] * _silu_and_grad(gc_ref[...])[0]).astype(BF16)
        cs = (ys_ref[...] * _silu_and_grad(gs_ref[...])[0]).astype(BF16)
        cat_ref[:, 0:ATT_W] = ca
        cat_ref[:, ATT_W:ATT_W + CONV_W] = cc
        cat_ref[:, ATT_W + CONV_W:] = cs
        y = (jnp.dot(ca, w_ref[0:ATT_W, :], preferred_element_type=F32)
             + jnp.dot(cc, w_ref[ATT_W:ATT_W + CONV_W, :], preferred_element_type=F32)
             + jnp.dot(cs, w_ref[ATT_W + CONV_W:, :], preferred_element_type=F32))
        y_ref[...] = y
        rstd = lax.rsqrt(jnp.mean(y * y, axis=-1, keepdims=True) + EPS)
        xn_ref[...] = x_ref[...] + mod_ref[2:3, :] * (y * rstd * gp_ref[...])

    return pl.pallas_call(
        body, name=name, grid=(S // tm,),
        in_specs=[_rows(tm, D), _rows(tm, 512, 1), _rows(tm, 256, 6), _rows(tm, 256, 9),
                  _rows(tm, ATT_W), _rows(tm, CONV_W), _rows(tm, SGU_W),
                  _full((D, D)), _full((1, D)), _full((3, D))],
        out_specs=(_rows(tm, D), _rows(tm, D), _rows(tm, D)),
        out_shape=(_sds((S, D)), _sds((S, D)), _sds((S, D), BF16)),
        compiler_params=_cparams("parallel"),
    )(x, z, z, z, y_att, y_conv, y_sgu, w_out, g_post, mod)


def _loss_head(y, target):
    S = y.shape[0]
    tm = ROW_TILE

    def body(y_ref, t_ref, loss_ref, dy_ref):
        @pl.when(pl.program_id(0) == 0)
        def _():
            loss_ref[...] = jnp.zeros(loss_ref.shape, F32)

        err = y_ref[...] - t_ref[...]
        dy_ref[...] = err * (1.0 / D)
        row = jnp.sum(err * err, axis=-1, keepdims=True) * (1.0 / D)
        loss_ref[...] += 0.5 * jnp.sum(row, axis=0, keepdims=True)

    return pl.pallas_call(
        body, name="loss_head", grid=(S // tm,),
        in_specs=[_rows(tm, D), _rows(tm, D)],
        out_specs=(_full((1, 1)), _rows(tm, D)),
        out_shape=(_sds((1, 1)), _sds((S, D))),
        compiler_params=_cparams("arbitrary"),
    )(y, target)


def _matmul_tn(a, b, name):
    S, M = a.shape
    N = b.shape[1]
    bk = min(512, S)

    def body(a_ref, b_ref, o_ref):
        @pl.when(pl.program_id(0) == 0)
        def _():
            o_ref[...] = jnp.zeros(o_ref.shape, F32)

        o_ref[...] += lax.dot_general(a_ref[...], b_ref[...], TN, preferred_element_type=F32)

    return pl.pallas_call(
        body, name=name, grid=(S // bk,),
        in_specs=[pl.BlockSpec((bk, M), lambda k: (k, 0)), pl.BlockSpec((bk, N), lambda k: (k, 0))],
        out_specs=_full((M, N)), out_shape=_sds((M, N)),
        compiler_params=_cparams("arbitrary"),
    )(a, b)


def _out_proj_backward(dxo, y, z, y_att, y_conv, y_sgu, lse, w_out, g_post, mod, name):
    S = dxo.shape[0]
    tm = ROW_TILE

    def body(dxo_ref, y_ref, ga_ref, gc_ref, gs_ref, ya_ref, yc_ref, ys_ref, lse_ref, w_ref, gp_ref, mod_ref,
             dyb_ref, dob_ref, st_ref, dga_ref, dyc_ref, dgc_ref, dys_ref, dgs_ref, dgate_ref, dgp_ref):
        @pl.when(pl.program_id(0) == 0)
        def _():
            dgate_ref[...] = jnp.zeros(dgate_ref.shape, F32)
            dgp_ref[...] = jnp.zeros(dgp_ref.shape, F32)

        dxo_v = dxo_ref[...]
        yv = y_ref[...]
        gp = gp_ref[...]
        rstd = lax.rsqrt(jnp.mean(yv * yv, axis=-1, keepdims=True) + EPS)
        yhat = yv * rstd
        dgate_ref[...] += jnp.sum(dxo_v * (yhat * gp), axis=0, keepdims=True)
        dr = dxo_v * mod_ref[2:3, :]
        dgp_ref[...] += jnp.sum(dr * yhat, axis=0, keepdims=True)
        dyh = dr * gp
        dy = rstd * (dyh - yhat * jnp.mean(dyh * yhat, axis=-1, keepdims=True))
        dyb = dy.astype(BF16)
        dyb_ref[...] = dyb
        dcat = lax.dot_general(dyb, w_ref[...], NT, preferred_element_type=F32)

        ya = ya_ref[...]
        sil, dsil = _silu_and_grad(ga_ref[...])
        da = dcat[:, 0:ATT_W]
        do = da * sil
        dob_ref[...] = do.astype(BF16)
        dga_ref[...] = da * ya * dsil
        lane = lax.broadcasted_iota(jnp.int32, (1, 128), 1)
        stats = jnp.zeros((tm, 128), F32)
        for h in range(HEADS):
            cols = slice(h * VDIM, (h + 1) * VDIM)
            delta = jnp.sum(do[:, cols] * ya[:, cols], axis=-1, keepdims=True)
            stats = stats + jnp.where(lane == 2 * h, lse_ref[h], 0.0) + jnp.where(lane == 2 * h + 1, delta, 0.0)
        st_ref[...] = stats

        sil, dsil = _silu_and_grad(gc_ref[...])
        dc = dcat[:, ATT_W:ATT_W + CONV_W]
        dyc_ref[...] = dc * sil
        dgc_ref[...] = dc * yc_ref[...] * dsil
        sil, dsil = _silu_and_grad(gs_ref[...])
        dsg = dcat[:, ATT_W + CONV_W:]
        dys_ref[...] = dsg * sil
        dgs_ref[...] = dsg * ys_ref[...] * dsil

    return pl.pallas_call(
        body, name=name, grid=(S // tm,),
        in_specs=[_rows(tm, D), _rows(tm, D), _rows(tm, 512, 1), _rows(tm, 256, 6), _rows(tm, 256, 9),
                  _rows(tm, ATT_W), _rows(tm, CONV_W), _rows(tm, SGU_W),
                  pl.BlockSpec((HEADS, tm, 1), lambda i: (0, i, 0)),
                  _full((D, D)), _full((1, D)), _full((3, D))],
        out_specs=(_rows(tm, D), _rows(tm, ATT_W), _rows(tm, 128), _rows(tm, ATT_W),
                   _rows(tm, CONV_W), _rows(tm, CONV_W), _rows(tm, SGU_W), _rows(tm, SGU_W),
                   _full((1, D)), _full((1, D))),
        out_shape=(_sds((S, D), BF16), _sds((S, ATT_W), BF16), _sds((S, 128)), _sds((S, ATT_W)),
                   _sds((S, CONV_W)), _sds((S, CONV_W)), _sds((S, SGU_W)), _sds((S, SGU_W)),
                   _sds((1, D)), _sds((1, D))),
        compiler_params=_cparams("arbitrary"),
    )(dxo, y, z, z, z, y_att, y_conv, y_sgu, lse, w_out, g_post, mod)


def _flash_backward(q, k, v, do, stats, name):
    S = q.shape[0]
    t = ATT_TILE
    tk = 2 * t
    nq = S // t

    def body(q_ref, do_ref, st_ref, k_ref, v_ref, dq_ref, dk_ref, dv_ref, dk_sc, dv_sc):
        h = pl.program_id(0)
        j = pl.program_id(1)

        @pl.when(j == 0)
        def _():
            dq_ref[...] = jnp.zeros(dq_ref.shape, F32)

        dk_sc[...] = jnp.zeros(dk_sc.shape, F32)
        dv_sc[...] = jnp.zeros(dv_sc.shape, F32)
        lane = lax.broadcasted_iota(jnp.int32, (1, 128), 1)

        def chain(hf, qv, dov, lse2, delta, diagonal):
            kt = k_ref[hf * t:(hf + 1) * t, :]
            s = lax.dot_general(qv, kt, NT, preferred_element_type=F32)
            p = jnp.exp2(s * EXP2_SCALE - lse2)
            if diagonal:
                ri = lax.broadcasted_iota(jnp.int32, (t, t), 0)
                ci = lax.broadcasted_iota(jnp.int32, (t, t), 1)
                p = jnp.where(ci <= ri, p, 0.0)
            dv_sc[hf] += lax.dot_general(p.astype(BF16), dov, TN, preferred_element_type=F32)
            dp = lax.dot_general(dov, v_ref[hf * t:(hf + 1) * t, :], NT, preferred_element_type=F32)
            ds = (p * (dp - delta) * ATT_SCALE).astype(BF16)
            dk_sc[hf] += lax.dot_general(ds, qv, TN, preferred_element_type=F32)
            return jnp.dot(ds, kt, preferred_element_type=F32)

        def q_tile(qb, modes):
            rows = pl.ds(pl.multiple_of(qb * t, t), t)
            qv = q_ref[rows, :]
            dov = do_ref[rows, :]
            st = st_ref[rows, :]
            lse2 = jnp.sum(jnp.where(lane == 2 * h, st, 0.0), axis=-1, keepdims=True) * LOG2E
            delta = jnp.sum(jnp.where(lane == 2 * h + 1, st, 0.0), axis=-1, keepdims=True)
            parts = [chain(hf, qv, dov, lse2, delta, modes[hf]) for hf in range(2) if modes[hf] is not None]
            dq_ref[rows, :] += parts[0] if len(parts) == 1 else parts[0] + parts[1]

        q_tile(2 * j, (True, None))
        q_tile(2 * j + 1, (False, True))

        def loop_body(qb, carry):
            q_tile(qb, (False, False))
            return carry

        lax.fori_loop(2 * j + 2, nq, loop_body, 0)
        for hf in range(2):
            dk_ref[hf * t:(hf + 1) * t, :] = dk_sc[hf]
            dv_ref[hf * t:(hf + 1) * t, :] = dv_sc[hf]

    return pl.pallas_call(
        body, name=name, grid=(HEADS, S // tk),
        in_specs=[pl.BlockSpec((S, HQ), lambda h, j: (0, h)),
                  pl.BlockSpec((S, VDIM), lambda h, j: (0, h)),
                  pl.BlockSpec((S, 128), lambda h, j: (0, 0)),
                  pl.BlockSpec((tk, HQ), lambda h, j: (j, h)),
                  pl.BlockSpec((tk, VDIM), lambda h, j: (j, h))],
        out_specs=(pl.BlockSpec((S, HQ), lambda h, j: (0, h)),
                   pl.BlockSpec((tk, HQ), lambda h, j: (j, h)),
                   pl.BlockSpec((tk, VDIM), lambda h, j: (j, h))),
        out_shape=(_sds((S, HEADS * HQ)), _sds((S, HEADS * HQ)), _sds((S, ATT_W))),
        scratch_shapes=[pltpu.VMEM((2, t, HQ), F32), pltpu.VMEM((2, t, VDIM), F32)],
        compiler_params=_cparams("parallel", "arbitrary"),
    )(q, do, stats, k, v)


def _att_prep_backward(z, pos, dq, dk, dv, q_g, kv_g, wq_p, w_ukv, rope_rows, name):
    S = z.shape[0]
    tm = ROW_TILE

    def body(z_ref, pos_ref, dq_ref, dk_ref, dv_ref, qg_ref, kvg_ref, wq_ref, wkv_ref, rope_ref,
             dz_ref, qn_ref, dqp_ref, kvn_ref, dkv_ref, dqg_ref, dkvg_ref):
        @pl.when(pl.program_id(0) == 0)
        def _():
            dqg_ref[...] = jnp.zeros(dqg_ref.shape, F32)
            dkvg_ref[...] = jnp.zeros(dkvg_ref.shape, F32)

        zz = z_ref[...]
        ql, kvl = zz[:, 0:Q_RANK], zz[:, Q_RANK:Q_RANK + KV_RANK]
        q_rstd = lax.rsqrt(jnp.mean(ql * ql, axis=-1, keepdims=True) + EPS)
        kv_rstd = lax.rsqrt(jnp.mean(kvl * kvl, axis=-1, keepdims=True) + EPS)
        qhat, kvhat = ql * q_rstd, kvl * kv_rstd
        qg, kvg = qg_ref[...], kvg_ref[...]
        qn_ref[...] = (qhat * qg).astype(BF16)
        kvn_ref[...] = (kvhat * kvg).astype(BF16)
        ct, st = _rope_tables(pos_ref, rope_ref)

        def unrotate(d):
            return d * ct + _swap_halves(d * st)

        dkrot = jnp.zeros((tm, NOPE), F32)
        for h in range(HEADS):
            b = h * HQ
            dqp_ref[:, b:b + NOPE] = dq_ref[:, b:b + NOPE].astype(BF16)
            dqp_ref[:, b + NOPE:b + HQ] = unrotate(dq_ref[:, b + NOPE:b + HQ]).astype(BF16)
            dkv_ref[:, b:b + NOPE] = dk_ref[:, b:b + NOPE].astype(BF16)
            dkv_ref[:, b + NOPE:b + HQ] = dv_ref[:, h * VDIM:(h + 1) * VDIM].astype(BF16)
            dkrot = dkrot + dk_ref[:, b + NOPE:b + HQ]
        dqn = lax.dot_general(dqp_ref[...], wq_ref[...], NT, preferred_element_type=F32)
        dkvn = lax.dot_general(dkv_ref[...], wkv_ref[...], NT, preferred_element_type=F32)
        dqg_ref[...] += jnp.sum(dqn * qhat, axis=0, keepdims=True)
        dkvg_ref[...] += jnp.sum(dkvn * kvhat, axis=0, keepdims=True)
        dqh, dkvh = dqn * qg, dkvn * kvg
        dz_ref[:, 0:Q_RANK] = q_rstd * (dqh - qhat * jnp.mean(dqh * qhat, axis=-1, keepdims=True))
        dz_ref[:, Q_RANK:Q_RANK + KV_RANK] = kv_rstd * (dkvh - kvhat * jnp.mean(dkvh * kvhat, axis=-1, keepdims=True))
        dz_ref[:, Q_RANK + KV_RANK:] = unrotate(dkrot)

    W = HEADS * HQ
    return pl.pallas_call(
        body, name=name, grid=(S // tm,),
        in_specs=[_rows(tm, 512, 0), _rows(tm, 1), _rows(tm, W), _rows(tm, W), _rows(tm, ATT_W),
                  _full((1, Q_RANK)), _full((1, KV_RANK)), _full((Q_RANK, W)), _full((KV_RANK, W)), _full((8, 128))],
        out_specs=(_rows(tm, 512), _rows(tm, Q_RANK), _rows(tm, W), _rows(tm, KV_RANK), _rows(tm, W),
                   _full((1, Q_RANK)), _full((1, KV_RANK))),
        out_shape=(_sds((S, 512)), _sds((S, Q_RANK), BF16), _sds((S, W), BF16), _sds((S, KV_RANK), BF16),
                   _sds((S, W), BF16), _sds((1, Q_RANK)), _sds((1, KV_RANK))),
        compiler_params=_cparams("arbitrary"),
    )(z, pos, dq, dk, dv, q_g, kv_g, wq_p, w_ukv, rope_rows)


def _conv_norm_backward(dyc, cv, ln_g, ln_b, w_pw2, name):
    S = cv.shape[0]
    tm = ROW_TILE

    def body(dy_ref, cv_ref, g_ref, be_ref, pw_ref, dcv_ref, sl_ref, dyb_ref, dg_ref, db_ref, dcb_ref):
        @pl.when(pl.program_id(0) == 0)
        def _():
            dg_ref[...] = jnp.zeros(dg_ref.shape, F32)
            db_ref[...] = jnp.zeros(db_ref.shape, F32)
            dcb_ref[...] = jnp.zeros(dcb_ref.shape, F32)

        cv_v = cv_ref[...]
        mu = jnp.mean(cv_v, axis=-1, keepdims=True)
        cc = cv_v - mu
        rstd = lax.rsqrt(jnp.mean(cc * cc, axis=-1, keepdims=True) + EPS)
        nh = cc * rstd
        g = g_ref[...]
        n = nh * g + be_ref[...]
        sil, dsil = _silu_and_grad(n)
        sl_ref[...] = sil.astype(BF16)
        dyb = dy_ref[...].astype(BF16)
        dyb_ref[...] = dyb
        dn = lax.dot_general(dyb, pw_ref[...], NT, preferred_element_type=F32) * dsil
        db_ref[...] += jnp.sum(dn, axis=0, keepdims=True)
        dg_ref[...] += jnp.sum(dn * nh, axis=0, keepdims=True)
        dnh = dn * g
        dcv = rstd * (dnh - jnp.mean(dnh, axis=-1, keepdims=True) - nh * jnp.mean(dnh * nh, axis=-1, keepdims=True))
        dcv_ref[...] = dcv
        dcb_ref[...] += jnp.sum(dcv, axis=0, keepdims=True)

    vec = _full((1, CONV_W))
    return pl.pallas_call(
        body, name=name, grid=(S // tm,),
        in_specs=[_rows(tm, CONV_W), _rows(tm, CONV_W), vec, vec, _full((CONV_W, CONV_W))],
        out_specs=(_rows(tm, CONV_W), _rows(tm, CONV_W), _rows(tm, CONV_W), vec, vec, vec),
        out_shape=(_sds((S, CONV_W)), _sds((S, CONV_W), BF16), _sds((S, CONV_W), BF16),
                   _sds((1, CONV_W)), _sds((1, CONV_W)), _sds((1, CONV_W))),
        compiler_params=_cparams("arbitrary"),
    )(dyc, cv, ln_g, ln_b, w_pw2)


def _conv_backward(z, dcv, conv_w_p, name):
    S = z.shape[0]
    tm = ROW_TILE
    per = tm // HALO
    last_halo = S // HALO - 1

    def body(a_ref, b_ref, ap_ref, bp_ref, d_ref, dn_ref, w_ref, da_ref, db_ref, dw_ref, win, dwin):
        i = pl.program_id(0)

        @pl.when(i == 0)
        def _():
            dw_ref[...] = jnp.zeros(dw_ref.shape, F32)

        av, bv = a_ref[...], b_ref[...]
        _conv_window(win, ap_ref[...], bp_ref[...], av, bv, i == 0)
        dcur = d_ref[...]
        dwin[0:tm, :] = dcur
        dwin[tm:, :] = jnp.where(i == pl.num_programs(0) - 1, 0.0, dn_ref[...])
        dh = jnp.zeros((tm, CONV_W), F32)
        for kk in range(CONV_K):
            dh = dh + w_ref[kk:kk + 1, :] * dwin[pl.ds(CONV_K - 1 - kk, tm), :]
            dw_ref[kk:kk + 1, :] += jnp.sum(dcur * win[pl.ds(HALO - (CONV_K - 1) + kk, tm), :], axis=0, keepdims=True)
        sb = _sigmoid(bv)
        da_ref[...] = dh * sb
        db_ref[...] = dh * av * sb * (1.0 - sb)

    nxt = pl.BlockSpec((HALO, CONV_W), lambda i: (jnp.minimum((i + 1) * per, last_halo), 0))
    return pl.pallas_call(
        body, name=name, grid=(S // tm,),
        in_specs=_conv_in_specs(tm) + [_rows(tm, CONV_W), nxt, _full((HALO, CONV_W))],
        out_specs=(_rows(tm, CONV_W), _rows(tm, CONV_W), _full((HALO, CONV_W))),
        out_shape=(_sds((S, CONV_W)), _sds((S, CONV_W)), _sds((HALO, CONV_W))),
        scratch_shapes=[pltpu.VMEM((tm + HALO, CONV_W), F32), pltpu.VMEM((tm + HALO, CONV_W), F32)],
        compiler_params=_cparams("arbitrary"),
    )(z, z, z, z, dcv, dcv, conv_w_p)


def _sgu_backward(z, dy, ln_g, ln_b, w_s, bias_full, name):
    S = z.shape[0]
    tm = ROW_TILE

    def body(u_ref, v_ref, dy_ref, g_ref, be_ref, ws_ref, bias_ref, du_ref, dv_ref, dws_ref, dbs_ref, dg_ref, db_ref):
        @pl.when(pl.program_id(0) == 0)
        def _():
            dws_ref[...] = jnp.zeros(dws_ref.shape, F32)
            dbs_ref[...] = jnp.zeros(dbs_ref.shape, F32)
            dg_ref[...] = jnp.zeros(dg_ref.shape, F32)
            db_ref[...] = jnp.zeros(db_ref.shape, F32)

        gmask, tril = _sgu_masks()
        lane = lax.broadcasted_iota(jnp.int32, (1, 128), 1)
        wm = [jnp.where(tril, ws_ref[g], 0.0).astype(BF16) for g in range(SGU_G)]
        gain = g_ref[...]
        for ch in range(tm // SGU_T):
            rows = slice(ch * SGU_T, (ch + 1) * SGU_T)
            gu, dgu, dgv, rstd, nh, vn = _sgu_common(u_ref[rows, :], v_ref[rows, :], g_ref, be_ref)
            vb = vn.astype(BF16)
            sv = bias_ref[...]
            for g in range(SGU_G):
                sv = sv + jnp.where(gmask[g], jnp.dot(wm[g], vb, preferred_element_type=F32), 0.0)
            dyv = dy_ref[rows, :]
            du_ref[rows, :] = dyv * sv * dgu
            dsv = dyv * gu
            dsvb = dsv.astype(BF16)
            dvn = jnp.zeros((SGU_T, SGU_W), F32)
            for g in range(SGU_G):
                dsg = jnp.where(gmask[g], dsv, 0.0)
                dwg = lax.dot_general(dsg.astype(BF16), vb, NT, preferred_element_type=F32)
                dws_ref[g] += jnp.where(tril, dwg, 0.0)
                dvn = dvn + jnp.where(gmask[g], lax.dot_general(wm[g], dsvb, TN, preferred_element_type=F32), 0.0)
                dbs_ref[...] += jnp.where(lane == g, jnp.sum(dsg, axis=-1, keepdims=True), 0.0)
            db_ref[...] += jnp.sum(dvn, axis=0, keepdims=True)
            dg_ref[...] += jnp.sum(dvn * nh, axis=0, keepdims=True)
            dnh = dvn * gain
            dgvv = rstd * (dnh - jnp.mean(dnh, axis=-1, keepdims=True) - nh * jnp.mean(dnh * nh, axis=-1, keepdims=True))
            dv_ref[rows, :] = dgvv * dgv

    vec = _full((1, SGU_W))
    return pl.pallas_call(
        body, name=name, grid=(S // tm,),
        in_specs=[_rows(tm, SGU_W, 7), _rows(tm, SGU_W, 8), _rows(tm, SGU_W), vec, vec,
                  _full((SGU_G, SGU_T, SGU_T)), _full((SGU_T, SGU_W))],
        out_specs=(_rows(tm, SGU_W), _rows(tm, SGU_W), _full((SGU_G, SGU_T, SGU_T)), _full((SGU_T, 128)), vec, vec),
        out_shape=(_sds((S, SGU_W)), _sds((S, SGU_W)), _sds((SGU_G, SGU_T, SGU_T)), _sds((SGU_T, 128)),
                   _sds((1, SGU_W)), _sds((1, SGU_W))),
        compiler_params=_cparams("arbitrary"),
    )(z, z, dy, ln_g, ln_b, w_s, bias_full)


def _inproj_backward(x, dxo, dz_att, dga, dca, dcb, dgc, dsu, dsv, dgs, g_pre, mod, w_in_p, name):
    S = x.shape[0]
    tm = ROW_TILE

    def body(x_ref, dxo_ref, p0, p1, p2, p3, p4, p5, p6, p7, g_ref, mod_ref, w_ref,
             dx_ref, hb_ref, dzb_ref, dmod_ref, dg_ref):
        @pl.when(pl.program_id(0) == 0)
        def _():
            dmod_ref[...] = jnp.zeros(dmod_ref.shape, F32)
            dg_ref[...] = jnp.zeros(dg_ref.shape, F32)

        off = 0
        for piece in (p0, p1, p2, p3, p4, p5, p6, p7):
            wdt = piece.shape[1]
            dzb_ref[:, off:off + wdt] = piece[...].astype(BF16)
            off += wdt
        dh = lax.dot_general(dzb_ref[...], w_ref[...], NT, preferred_element_type=F32)
        xv = x_ref[...]
        g = g_ref[...]
        one_scale = 1.0 + mod_ref[1:2, :]
        rstd = lax.rsqrt(jnp.mean(xv * xv, axis=-1, keepdims=True) + EPS)
        xhat = xv * rstd
        xg = xhat * g
        hb_ref[...] = (xg * one_scale + mod_ref[0:1, :]).astype(BF16)
        dmod_ref[0:1, :] += jnp.sum(dh, axis=0, keepdims=True)
        dmod_ref[1:2, :] += jnp.sum(dh * xg, axis=0, keepdims=True)
        dhs = dh * one_scale
        dg_ref[...] += jnp.sum(dhs * xhat, axis=0, keepdims=True)
        dxh = dhs * g
        dx_ref[...] = dxo_ref[...] + rstd * (dxh - xhat * jnp.mean(dxh * xhat, axis=-1, keepdims=True))

    widths = (512, 512, 256, 256, 256, 256, 256, 256)
    return pl.pallas_call(
        body, name=name, grid=(S // tm,),
        in_specs=[_rows(tm, D), _rows(tm, D)] + [_rows(tm, w) for w in widths]
                 + [_full((1, D)), _full((3, D)), _full((D, DZ))],
        out_specs=(_rows(tm, D), _rows(tm, D), _rows(tm, DZ), _full((2, D)), _full((1, D))),
        out_shape=(_sds((S, D)), _sds((S, D), BF16), _sds((S, DZ), BF16), _sds((2, D)), _sds((1, D))),
        compiler_params=_cparams("arbitrary"),
    )(x, dxo, dz_att, dga, dca, dcb, dgc, dsu, dsv, dgs, g_pre, mod, w_in_p)


def _adamw(w, gparts, m, v, name):
    shape = w.shape
    cols = shape[-1]
    rows = int(np.prod(shape[:-1]))
    parts = gparts.shape[0]
    w2, m2, v2 = (a.reshape(rows, cols) for a in (w, m, v))
    g3 = gparts.reshape(parts, rows, cols)
    tr = rows
    for cand in (256, 128):
        if rows > cand and rows % cand == 0:
            tr = cand
            break

    def body(w_ref, g_ref, m_ref, v_ref, go_ref, d_ref, mo_ref, vo_ref):
        g = g_ref[0].astype(F32)
        for p in range(1, parts):
            g = g + g_ref[p].astype(F32)
        wv = w_ref[...]
        mn = ADAM_B1 * m_ref[...] + (1.0 - ADAM_B1) * g
        vn = ADAM_B2 * v_ref[...] + (1.0 - ADAM_B2) * (g * g)
        m_hat = mn / (1.0 - ADAM_B1 ** ADAM_STEP)
        v_hat = vn / (1.0 - ADAM_B2 ** ADAM_STEP)
        go_ref[...] = g
        d_ref[...] = -ADAM_LR * (m_hat / (jnp.sqrt(v_hat) + ADAM_EPS) + ADAM_WD * wv)
        mo_ref[...] = mn
        vo_ref[...] = vn

    blk = pl.BlockSpec((tr, cols), lambda i: (i, 0))
    outs = pl.pallas_call(
        body, name=name, grid=(rows // tr,),
        in_specs=[blk, pl.BlockSpec((parts, tr, cols), lambda i: (0, i, 0)), blk, blk],
        out_specs=(blk, blk, blk, blk),
        out_shape=tuple(_sds((rows, cols)) for _ in range(4)),
        compiler_params=_cparams("parallel"),
    )(w2, g3, m2, v2)
    return tuple(o.reshape(shape) for o in outs)


_GATHERED = (("w_in", (DEPTH, D, D_IN // N_DEV)), ("w_out", (DEPTH, D // N_DEV, D)),
             ("w_uq", (DEPTH, Q_RANK, HEADS * QK // N_DEV)), ("w_ukv", (DEPTH, KV_RANK, HEADS * HQ // N_DEV)),
             ("w_pw2", (DEPTH, CONV_W // N_DEV, CONV_W)), ("conv_w", (DEPTH, CONV_K, CONV_W // N_DEV)))
_COL_SHARDED = ("w_in", "w_uq", "w_ukv", "conv_w")

_SMALL = (("dmod", (DEPTH, 3 * D)), ("g_pre", (DEPTH, D)), ("g_post", (DEPTH, D)), ("q_norm_g", (DEPTH, Q_RANK)),
          ("kv_norm_g", (DEPTH, KV_RANK)), ("conv_b", (DEPTH, CONV_W)), ("conv_ln_g", (DEPTH, CONV_W)),
          ("conv_ln_b", (DEPTH, CONV_W)), ("sgu_ln_g", (DEPTH, SGU_W)), ("sgu_ln_b", (DEPTH, SGU_W)),
          ("w_s", (DEPTH, SGU_G, SGU_T, SGU_T)), ("b_s", (DEPTH, SGU_G, SGU_T)))


def _assemble(name, parts):
    if name in _COL_SHARDED:
        p = jnp.moveaxis(parts, 0, -2)
        return p.reshape(p.shape[:-2] + (p.shape[-2] * p.shape[-1],))
    p = jnp.moveaxis(parts, 0, 1)
    return p.reshape((p.shape[0], p.shape[1] * p.shape[2], p.shape[3]))


def _scatter_layout(name, full):
    if name in _COL_SHARDED:
        p = full.reshape(full.shape[:-1] + (N_DEV, full.shape[-1] // N_DEV))
        return jnp.moveaxis(p, -2, 0)
    p = full.reshape((full.shape[0], N_DEV, full.shape[1] // N_DEV, full.shape[2]))
    return jnp.moveaxis(p, 1, 0)


def kernel(x, c, positions, w_ada, b_ada, g_pre, g_post, w_in, q_norm_g, w_uq, kv_norm_g, w_ukv, conv_w, conv_b, conv_ln_g, conv_ln_b, w_pw2, sgu_ln_g, sgu_ln_b, w_s, b_s, w_out, loss_target, m_w_ada, m_b_ada, m_g_pre, m_g_post, m_w_in, m_q_norm_g, m_w_uq, m_kv_norm_g, m_w_ukv, m_conv_w, m_conv_b, m_conv_ln_g, m_conv_ln_b, m_w_pw2, m_sgu_ln_g, m_sgu_ln_b, m_w_s, m_b_s, m_w_out, v_w_ada, v_b_ada, v_g_pre, v_g_post, v_w_in, v_q_norm_g, v_w_uq, v_kv_norm_g, v_w_ukv, v_conv_w, v_conv_b, v_conv_ln_g, v_conv_ln_b, v_w_pw2, v_sgu_ln_g, v_sgu_ln_b, v_w_s, v_b_s, v_w_out):
    weights = dict(w_ada=w_ada, b_ada=b_ada, g_pre=g_pre, g_post=g_post, w_in=w_in, q_norm_g=q_norm_g, w_uq=w_uq,
                   kv_norm_g=kv_norm_g, w_ukv=w_ukv, conv_w=conv_w, conv_b=conv_b, conv_ln_g=conv_ln_g,
                   conv_ln_b=conv_ln_b, w_pw2=w_pw2, sgu_ln_g=sgu_ln_g, sgu_ln_b=sgu_ln_b, w_s=w_s, b_s=b_s, w_out=w_out)
    m_in = dict(w_ada=m_w_ada, b_ada=m_b_ada, g_pre=m_g_pre, g_post=m_g_post, w_in=m_w_in, q_norm_g=m_q_norm_g,
                w_uq=m_w_uq, kv_norm_g=m_kv_norm_g, w_ukv=m_w_ukv, conv_w=m_conv_w, conv_b=m_conv_b,
                conv_ln_g=m_conv_ln_g, conv_ln_b=m_conv_ln_b, w_pw2=m_w_pw2, sgu_ln_g=m_sgu_ln_g,
                sgu_ln_b=m_sgu_ln_b, w_s=m_w_s, b_s=m_b_s, w_out=m_w_out)
    v_in = dict(w_ada=v_w_ada, b_ada=v_b_ada, g_pre=v_g_pre, g_post=v_g_post, w_in=v_w_in, q_norm_g=v_q_norm_g,
                w_uq=v_w_uq, kv_norm_g=v_kv_norm_g, w_ukv=v_w_ukv, conv_w=v_conv_w, conv_b=v_conv_b,
                conv_ln_g=v_conv_ln_g, conv_ln_b=v_conv_ln_b, w_pw2=v_w_pw2, sgu_ln_g=v_sgu_ln_g,
                sgu_ln_b=v_sgu_ln_b, w_s=v_w_s, b_s=v_b_s, w_out=v_w_out)
    order = list(weights)

    S = x.shape[1]
    me = 4 * lax.axis_index("x") + 2 * lax.axis_index("y") + lax.axis_index("c")
    x0 = x.reshape(S, D)
    target = loss_target.reshape(S, D)
    pos = positions.reshape(S, 1)

    gathered = _exchange([weights[n].astype(BF16) for n, _ in _GATHERED], all_to_all=False, name="gather_weights")
    full = {n: _assemble(n, p) for (n, _), p in zip(_GATHERED, gathered)}
    c_all = _exchange([c.reshape(8, D // 8)], all_to_all=False, name="gather_c")[0].reshape(N_DEV, D)

    ada_cols = w_ada.shape[-1]
    b_cols = lax.dynamic_slice_in_dim(b_ada, me * ada_cols, ada_cols, axis=1)
    sc_rows, mod_part = _ada_forward(jnp.pad(c_all, ((0, 8), (0, 0))), w_ada, b_cols)
    mod_recv = _exchange([jnp.moveaxis(mod_part[:, :N_DEV], 1, 0)], all_to_all=True, name="exchange_mod")[0]
    mod = jnp.moveaxis(mod_recv, 0, 1).reshape(DEPTH, 3, D)

    w_in_f = full["w_in"]
    w_in_p = jnp.concatenate([w_in_f[:, :, :ATT_IN], jnp.zeros((DEPTH, D, PAD_IN), BF16), w_in_f[:, :, ATT_IN:]], axis=2)
    wq_p = jnp.pad(full["w_uq"].reshape(DEPTH, Q_RANK, HEADS, QK), ((0, 0), (0, 0), (0, 0), (0, HQ - QK)))
    wq_p = wq_p.reshape(DEPTH, Q_RANK, HEADS * HQ)
    conv_w_p = jnp.pad(full["conv_w"].astype(F32), ((0, 0), (0, HALO - CONV_K), (0, 0)))
    bias_full = jnp.repeat(jnp.swapaxes(b_s, 1, 2), SGU_GD, axis=2)
    inv_freq = ROPE_THETA ** (-jnp.arange(0, ROPE, 2, dtype=F32) / ROPE)
    zeros32 = jnp.zeros((ROPE // 2,), F32)
    ones32 = jnp.ones((ROPE // 2,), F32)
    rope_rows = jnp.zeros((8, 128), F32)
    rope_rows = rope_rows.at[0].set(jnp.concatenate([inv_freq, inv_freq, zeros32, zeros32]))
    rope_rows = rope_rows.at[1].set(jnp.concatenate([ones32, ones32, zeros32, zeros32]))
    rope_rows = rope_rows.at[2].set(jnp.concatenate([-ones32, ones32, zeros32, zeros32]))

    def vec(a, l):
        return a[l].reshape(1, -1)

    saved = []
    xl = x0
    for l in range(DEPTH):
        z = _prenorm_inproj(xl, vec(g_pre, l), mod[l], w_in_p[l], name=f"prenorm_inproj_{l}")
        q, k, v = _att_prep(z, pos, vec(q_norm_g, l), vec(kv_norm_g, l), wq_p[l], full["w_ukv"][l], rope_rows,
                            name=f"att_prep_{l}")
        y_att, lse = _flash_forward(q, k, v, name=f"flash_forward_{l}")
        cv, y_conv = _conv_forward(z, conv_w_p[l], vec(conv_b, l), vec(conv_ln_g, l), vec(conv_ln_b, l),
                                   full["w_pw2"][l], name=f"conv_forward_{l}")
        y_sgu = _sgu_forward(z, vec(sgu_ln_g, l), vec(sgu_ln_b, l), w_s[l], bias_full[l], name=f"sgu_forward_{l}")
        x_next, y, ycat = _out_proj(xl, z, y_att, y_conv, y_sgu, full["w_out"][l], vec(g_post, l), mod[l],
                                    name=f"out_proj_{l}")
        saved.append(dict(x=xl, z=z, q=q, k=k, v=v, y_att=y_att, lse=lse, cv=cv, y_conv=y_conv, y_sgu=y_sgu, y=y, ycat=ycat))
        xl = x_next

    loss_part, dx = _loss_head(xl, target)
    loss = lax.psum(loss_part.reshape(()), ("x", "y", "c"))

    big = {n: [None] * DEPTH for n, _ in _GATHERED}
    small = {n: [None] * DEPTH for n, _ in _SMALL}
    for l in reversed(range(DEPTH)):
        sv = saved[l]
        (dyb, dob, stats, dga, dyc, dgc, dys, dgs, dgate, dgpost) = _out_proj_backward(
            dx, sv["y"], sv["z"], sv["y_att"], sv["y_conv"], sv["y_sgu"], sv["lse"], full["w_out"][l],
            vec(g_post, l), mod[l], name=f"out_proj_backward_{l}")
        big["w_out"][l] = _matmul_tn(sv["ycat"], dyb, name=f"grad_w_out_{l}")
        dq, dk, dv = _flash_backward(sv["q"], sv["k"], sv["v"], dob, stats, name=f"flash_backward_{l}")
        dz_att, qn_b, dqp_b, kvn_b, dkv_b, dqg, dkvg = _att_prep_backward(
            sv["z"], pos, dq, dk, dv, vec(q_norm_g, l), vec(kv_norm_g, l), wq_p[l], full["w_ukv"][l], rope_rows,
            name=f"att_prep_backward_{l}")
        dwq_p = _matmul_tn(qn_b, dqp_b, name=f"grad_w_uq_{l}")
        big["w_uq"][l] = dwq_p.reshape(Q_RANK, HEADS, HQ)[:, :, :QK].reshape(Q_RANK, HEADS * QK)
        big["w_ukv"][l] = _matmul_tn(kvn_b, dkv_b, name=f"grad_w_ukv_{l}")
        dcv, sl_b, dyc_b, dclg, dclb, dcb = _conv_norm_backward(dyc, sv["cv"], vec(conv_ln_g, l), vec(conv_ln_b, l),
                                                              full["w_pw2"][l], name=f"conv_norm_backward_{l}")
        big["w_pw2"][l] = _matmul_tn(sl_b, dyc_b, name=f"grad_w_pw2_{l}")
        dca, dcbb, dconvw = _conv_backward(sv["z"], dcv, conv_w_p[l], name=f"conv_backward_{l}")
        big["conv_w"][l] = dconvw[:CONV_K]
        dsu, dsvv, dws, dbs, dslg, dslb = _sgu_backward(sv["z"], dys, vec(sgu_ln_g, l), vec(sgu_ln_b, l), w_s[l],
                                                       bias_full[l], name=f"sgu_backward_{l}")
        dx, h_b, dz_b, dmod2, dgpre = _inproj_backward(
            sv["x"], dx, dz_att, dga, dca, dcbb, dgc, dsu, dsvv, dgs, vec(g_pre, l), mod[l], w_in_p[l],
            name=f"inproj_backward_{l}")
        dwin_p = _matmul_tn(h_b, dz_b, name=f"grad_w_in_{l}")
        big["w_in"][l] = jnp.concatenate([dwin_p[:, :ATT_IN], dwin_p[:, ATT_IN + PAD_IN:]], axis=1)
        small["dmod"][l] = jnp.concatenate([dmod2.reshape(-1), dgate.reshape(-1)])
        small["g_pre"][l] = dgpre.reshape(-1)
        small["g_post"][l] = dgpost.reshape(-1)
        small["q_norm_g"][l] = dqg.reshape(-1)
        small["kv_norm_g"][l] = dkvg.reshape(-1)
        small["conv_b"][l] = dcb.reshape(-1)
        small["conv_ln_g"][l] = dclg.reshape(-1)
        small["conv_ln_b"][l] = dclb.reshape(-1)
        small["sgu_ln_g"][l] = dslg.reshape(-1)
        small["sgu_ln_b"][l] = dslb.reshape(-1)
        small["w_s"][l] = dws
        small["b_s"][l] = jnp.swapaxes(dbs[:, :SGU_G], 0, 1)
    grad_x = dx.reshape(1, S, D)

    send = [_scatter_layout(n, jnp.stack(big[n])).astype(BF16) for n, _ in _GATHERED]
    gparts = dict(zip([n for n, _ in _GATHERED], _exchange(send, all_to_all=True, name="exchange_grads")))
    spack = _Packer(_SMALL, 8)
    sparts = spack.unpack(_exchange([spack.pack({n: jnp.stack(small[n]) for n, _ in _SMALL}, F32)],
                                    all_to_all=False, name="gather_small_grads")[0], (N_DEV,))
    dmod_all = sparts["dmod"]
    dmod_cols = lax.dynamic_slice_in_dim(dmod_all, me * ada_cols, ada_cols, axis=2)
    sc_t = jnp.pad(sc_rows[:N_DEV].T, ((0, 0), (0, 128 - N_DEV)))
    dmod_rows = jnp.pad(jnp.moveaxis(dmod_cols, 0, 1), ((0, 0), (0, 128 - N_DEV), (0, 0)))
    gparts["w_ada"] = _ada_backward(sc_t, dmod_rows)[None]
    gparts["b_ada"] = dmod_all
    for n, _ in _SMALL[1:]:
        gparts[n] = sparts[n]

    grads, deltas, new_m, new_v = {}, {}, {}, {}
    for n in order:
        grads[n], deltas[n], new_m[n], new_v[n] = _adamw(weights[n], gparts[n], m_in[n], v_in[n], name=f"adamw_{n}")
    return (loss, grad_x, *[grads[n] for n in order], *[deltas[n] for n in order],
            *[new_m[n] for n in order], *[new_v[n] for n in order])
```

```python
import functools
import math

import numpy as np
import jax
import jax.numpy as jnp
from jax import lax
from jax.experimental import pallas as pl
from jax.experimental.pallas import tpu as pltpu

F32 = jnp.float32
BF16 = jnp.bfloat16

N_DEV = 8
DEPTH = 2
D = 1024
HEADS = 4
NOPE = 128
ROPE = 64
VDIM = 128
QK = NOPE + ROPE
Q_RANK = 256
KV_RANK = 128
ATT_W = HEADS * VDIM
CONV_W = 256
CONV_K = 31
SGU_W = 256
SGU_G = 4
SGU_GD = SGU_W // SGU_G
SGU_T = 128
D_IN = 2496
ATT_IN = Q_RANK + KV_RANK + ROPE
PAD_IN = 64
DZ = D_IN + PAD_IN
HQ = 2 * NOPE
EPS = 1e-6
ROPE_THETA = 10000.0
ATT_SCALE = QK ** -0.5
LOG2E = math.log2(math.e)
EXP2_SCALE = ATT_SCALE * LOG2E
NEG_INF = float("-inf")

ADAM_LR = 0.001
ADAM_B1 = 0.9
ADAM_B2 = 0.999
ADAM_EPS = 1e-08
ADAM_WD = 0.01
ADAM_STEP = 10

VMEM_LIMIT = 56 * 1024 * 1024
ROW_TILE = 256
ATT_TILE = 512
HALO = 32
PACK_LANES = 128

MESH = pl.DeviceIdType.MESH
NT = (((1,), (1,)), ((), ()))
TN = (((0,), (0,)), ((), ()))


def _cparams(*sem):
    return pltpu.CompilerParams(dimension_semantics=sem, vmem_limit_bytes=VMEM_LIMIT)


def _sds(shape, dtype=F32):
    return jax.ShapeDtypeStruct(tuple(shape), dtype)


def _rows(tm, width, col=0):
    return pl.BlockSpec((tm, width), lambda i: (i, col))


def _full(shape):
    nd = len(shape)
    return pl.BlockSpec(tuple(shape), lambda *_: (0,) * nd)


def _sigmoid(x):
    return 1.0 / (1.0 + jnp.exp(-x))


def _silu_and_grad(g):
    s = _sigmoid(g)
    return g * s, s * (1.0 + g * (1.0 - s))


def _gelu_and_grad(x):
    cdf = 0.5 * (1.0 + lax.erf(x * (1.0 / math.sqrt(2.0))))
    pdf = jnp.exp(-0.5 * x * x) * (1.0 / math.sqrt(2.0 * math.pi))
    return x * cdf, cdf + x * pdf


def _swap_halves(a):
    lane = lax.broadcasted_iota(jnp.int32, a.shape, 1)
    up = pltpu.roll(a, 32, 1)
    down = pltpu.roll(a, 96, 1)
    return jnp.where(lane < 32, down, jnp.where(lane < 64, up, 0.0))


def _rope_tables(pos_ref, rope_ref):
    ang = pos_ref[...].astype(F32) * rope_ref[0:1, :]
    return jnp.cos(ang) * rope_ref[1:2, :], jnp.sin(ang) * rope_ref[2:3, :]


def _exchange(srcs, *, all_to_all, name):
    n = len(srcs)
    shapes = [tuple(s.shape[1:]) if all_to_all else tuple(s.shape) for s in srcs]

    def body(*refs):
        src_refs, out_refs = refs[:n], refs[n:2 * n]
        send_sems, recv_sems, local_sems = refs[2 * n:]
        x, y, c = lax.axis_index("x"), lax.axis_index("y"), lax.axis_index("c")
        me = 4 * x + 2 * y + c
        local, sends, recvs = [], [], []
        for a in range(n):
            def block_for(dest, src_ref=src_refs[a]):
                return src_ref.at[dest] if all_to_all else src_ref

            local.append(pltpu.make_async_copy(block_for(me), out_refs[a].at[me], local_sems.at[a]))
            for r in range(1, N_DEV):
                px = 1 - x if (r >> 2) & 1 else x
                py = 1 - y if (r >> 1) & 1 else y
                pc = 1 - c if r & 1 else c
                peer = 4 * px + 2 * py + pc
                sends.append(pltpu.make_async_remote_copy(
                    src_ref=block_for(peer), dst_ref=out_refs[a].at[me],
                    send_sem=send_sems.at[a, r - 1], recv_sem=recv_sems.at[a, r - 1],
                    device_id=(px, py, pc), device_id_type=MESH))
                recvs.append(pltpu.make_async_remote_copy(
                    src_ref=block_for(me), dst_ref=out_refs[a].at[peer],
                    send_sem=send_sems.at[a, r - 1], recv_sem=recv_sems.at[a, r - 1],
                    device_id=(px, py, pc), device_id_type=MESH))
        for cp in local + sends:
            cp.start()
        for cp in recvs:
            cp.wait_recv()
        for cp in sends:
            cp.wait_send()
        for cp in local:
            cp.wait()

    return pl.pallas_call(
        body, name=name,
        out_shape=[_sds((N_DEV,) + shp, s.dtype) for shp, s in zip(shapes, srcs)],
        in_specs=[pl.BlockSpec(memory_space=pl.ANY)] * n,
        out_specs=[pl.BlockSpec(memory_space=pl.ANY)] * n,
        scratch_shapes=[pltpu.SemaphoreType.DMA((n, N_DEV - 1)), pltpu.SemaphoreType.DMA((n, N_DEV - 1)),
                        pltpu.SemaphoreType.DMA((n,))],
    )(*srcs)


class _Packer:
    def __init__(self, entries, row_multiple):
        self.entries = entries
        self.offsets = {}
        off = 0
        for name, shape in entries:
            self.offsets[name] = off
            off += int(np.prod(shape))
        quantum = PACK_LANES * row_multiple
        self.total = -(-off // quantum) * quantum
        self.used = off
        self.rows = self.total // PACK_LANES

    def pack(self, arrays, dtype, lead=()):
        n = len(lead)
        flat = [arrays[name].astype(dtype).reshape(lead + (-1,)) for name, _ in self.entries]
        flat.append(jnp.zeros(lead + (self.total - self.used,), dtype))
        return jnp.concatenate(flat, axis=n).reshape(lead + (self.rows, PACK_LANES))

    def unpack(self, buf, lead=()):
        flat = buf.reshape(lead + (self.total,))
        out = {}
        for name, shape in self.entries:
            o = self.offsets[name]
            out[name] = lax.slice_in_dim(flat, o, o + int(np.prod(shape)), axis=len(lead)).reshape(lead + tuple(shape))
        return out


def _ada_forward(c_rows, w_ada, b_ada_cols):
    cols = w_ada.shape[-1]
    rows = c_rows.shape[0]

    def body(c_ref, w_ref, b_ref, sc_ref, part_ref):
        cv = c_ref[...]
        sc = cv * _sigmoid(cv)
        sc_ref[...] = sc
        scb = sc.astype(BF16)
        for l in range(DEPTH):
            part_ref[l] = jnp.dot(scb, w_ref[l].astype(BF16), preferred_element_type=F32) + b_ref[l:l + 1, :]

    return pl.pallas_call(
        body, name="ada_forward",
        out_shape=(_sds((rows, D)), _sds((DEPTH, rows, cols))),
        compiler_params=pltpu.CompilerParams(vmem_limit_bytes=VMEM_LIMIT),
    )(c_rows, w_ada, b_ada_cols)


def _ada_backward(sc_t, dmod_cols):
    cols = dmod_cols.shape[-1]

    def body(sc_ref, dm_ref, gw_ref):
        scb = sc_ref[...].astype(BF16)
        for l in range(DEPTH):
            gw_ref[l] = jnp.dot(scb, dm_ref[l].astype(BF16), preferred_element_type=F32)

    return pl.pallas_call(
        body, name="ada_backward",
        out_shape=_sds((DEPTH, D, cols)),
        compiler_params=pltpu.CompilerParams(vmem_limit_bytes=VMEM_LIMIT),
    )(sc_t, dmod_cols)


def _prenorm_inproj(x, g_pre, mod, w_in_p, name):
    S = x.shape[0]
    tm = ROW_TILE

    def body(x_ref, g_ref, mod_ref, w_ref, z_ref):
        xv = x_ref[...]
        rstd = lax.rsqrt(jnp.mean(xv * xv, axis=-1, keepdims=True) + EPS)
        h = (xv * rstd * g_ref[...]) * (1.0 + mod_ref[1:2, :]) + mod_ref[0:1, :]
        z_ref[...] = jnp.dot(h.astype(BF16), w_ref[...], preferred_element_type=F32)

    return pl.pallas_call(
        body, name=name, grid=(S // tm,),
        in_specs=[_rows(tm, D), _full((1, D)), _full((3, D)), _full((D, DZ))],
        out_specs=_rows(tm, DZ), out_shape=_sds((S, DZ)),
        compiler_params=_cparams("parallel"),
    )(x, g_pre, mod, w_in_p)


def _att_prep(z, pos, q_g, kv_g, wq_p, w_ukv, rope_rows, name):
    S = z.shape[0]
    tm = ROW_TILE

    def body(z_ref, pos_ref, qg_ref, kvg_ref, wq_ref, wkv_ref, rope_ref, q_ref, k_ref, v_ref):
        zz = z_ref[...]
        ql, kvl, ka = zz[:, 0:Q_RANK], zz[:, Q_RANK:Q_RANK + KV_RANK], zz[:, Q_RANK + KV_RANK:]
        qn = ql * lax.rsqrt(jnp.mean(ql * ql, axis=-1, keepdims=True) + EPS) * qg_ref[...]
        kvn = kvl * lax.rsqrt(jnp.mean(kvl * kvl, axis=-1, keepdims=True) + EPS) * kvg_ref[...]
        q = jnp.dot(qn.astype(BF16), wq_ref[...], preferred_element_type=F32)
        kv = jnp.dot(kvn.astype(BF16), wkv_ref[...], preferred_element_type=F32)
        ct, st = _rope_tables(pos_ref, rope_ref)
        krot = (ka * ct + _swap_halves(ka) * st).astype(BF16)
        for h in range(HEADS):
            b = h * HQ
            q_ref[:, b:b + NOPE] = q[:, b:b + NOPE].astype(BF16)
            a = q[:, b + NOPE:b + HQ]
            q_ref[:, b + NOPE:b + HQ] = (a * ct + _swap_halves(a) * st).astype(BF16)
            k_ref[:, b:b + NOPE] = kv[:, b:b + NOPE].astype(BF16)
            k_ref[:, b + NOPE:b + HQ] = krot
            v_ref[:, h * VDIM:(h + 1) * VDIM] = kv[:, b + NOPE:b + HQ].astype(BF16)

    return pl.pallas_call(
        body, name=name, grid=(S // tm,),
        in_specs=[_rows(tm, 512, 0), _rows(tm, 1), _full((1, Q_RANK)), _full((1, KV_RANK)),
                  _full((Q_RANK, HEADS * HQ)), _full((KV_RANK, HEADS * HQ)), _full((8, 128))],
        out_specs=(_rows(tm, HEADS * HQ), _rows(tm, HEADS * HQ), _rows(tm, ATT_W)),
        out_shape=(_sds((S, HEADS * HQ), BF16), _sds((S, HEADS * HQ), BF16), _sds((S, ATT_W), BF16)),
        compiler_params=_cparams("parallel"),
    )(z, pos, q_g, kv_g, wq_p, w_ukv, rope_rows)


def _flash_forward(q, k, v, name):
    S = q.shape[0]
    t = ATT_TILE
    tq = 2 * t
    nq = S // tq

    nl = t // 128

    def body(q_ref, k_ref, v_ref, o_ref, lse_ref, m_sc, l_sc, acc_sc):
        qb = pl.program_id(1)
        m_sc[...] = jnp.full(m_sc.shape, NEG_INF, F32)
        l_sc[...] = jnp.zeros(l_sc.shape, F32)
        acc_sc[...] = jnp.zeros(acc_sc.shape, F32)

        def scores(hf, rows, diagonal):
            s = lax.dot_general(q_ref[hf * t:(hf + 1) * t, :], k_ref[rows, :], NT, preferred_element_type=F32)
            if diagonal:
                ri = lax.broadcasted_iota(jnp.int32, (t, t), 0)
                ci = lax.broadcasted_iota(jnp.int32, (t, t), 1)
                s = jnp.where(ci <= ri, s, NEG_INF)
            return s

        def max_step(hf, kb, diagonal):
            s = scores(hf, pl.ds(pl.multiple_of(kb * t, t), t), diagonal)
            mp = m_sc[hf]
            for c in range(nl):
                mp = jnp.maximum(mp, s[:, c * 128:(c + 1) * 128])
            m_sc[hf] = mp

        def sum_step(hf, kb, diagonal):
            rows = pl.ds(pl.multiple_of(kb * t, t), t)
            s = scores(hf, rows, diagonal)
            p = jnp.exp2(s * EXP2_SCALE - jnp.tile(m_sc[hf], (1, nl)))
            lp = l_sc[hf]
            for c in range(nl):
                lp = lp + p[:, c * 128:(c + 1) * 128]
            l_sc[hf] = lp
            acc_sc[hf] += jnp.dot(p.astype(BF16), v_ref[rows, :], preferred_element_type=F32)

        def sweep(step):
            def loop_body(kb, carry):
                step(0, kb, False)
                step(1, kb, False)
                return carry

            lax.fori_loop(0, 2 * qb, loop_body, 0)
            step(0, 2 * qb, True)
            step(1, 2 * qb, False)
            step(1, 2 * qb + 1, True)

        sweep(max_step)
        for hf in range(2):
            m = jnp.max(m_sc[hf], axis=-1, keepdims=True)
            m_sc[hf] = jnp.broadcast_to(m * EXP2_SCALE, (t, 128))
        sweep(sum_step)
        for hf in range(2):
            l = jnp.sum(l_sc[hf], axis=-1, keepdims=True)
            o_ref[hf * t:(hf + 1) * t, :] = acc_sc[hf] / l
            m_scaled = jnp.max(m_sc[hf], axis=-1, keepdims=True) * (1.0 / LOG2E)
            lse_ref[0, hf * t:(hf + 1) * t, :] = m_scaled + jnp.log(l)

    return pl.pallas_call(
        body, name=name, grid=(HEADS, nq),
        in_specs=[pl.BlockSpec((tq, HQ), lambda h, i: (i, h)),
                  pl.BlockSpec((S, HQ), lambda h, i: (0, h)),
                  pl.BlockSpec((S, VDIM), lambda h, i: (0, h))],
        out_specs=(pl.BlockSpec((tq, VDIM), lambda h, i: (i, h)),
                   pl.BlockSpec((1, tq, 1), lambda h, i: (h, i, 0))),
        out_shape=(_sds((S, ATT_W)), _sds((HEADS, S, 1))),
        scratch_shapes=[pltpu.VMEM((2, t, 128), F32), pltpu.VMEM((2, t, 128), F32), pltpu.VMEM((2, t, VDIM), F32)],
        compiler_params=_cparams("parallel", "parallel"),
    )(q, k, v)


def _conv_window(win_ref, a_prev, b_prev, a_cur, b_cur, first):
    hp = a_prev * _sigmoid(b_prev)
    win_ref[0:HALO, :] = jnp.where(first, 0.0, hp)
    win_ref[HALO:, :] = a_cur * _sigmoid(b_cur)


def _conv_in_specs(tm):
    per = tm // HALO
    prev = lambda col: pl.BlockSpec((HALO, CONV_W), lambda i: (jnp.maximum(i * per - 1, 0), col))
    return [_rows(tm, CONV_W, 4), _rows(tm, CONV_W, 5), prev(4), prev(5)]


def _conv_forward(z, conv_w_p, conv_b, ln_g, ln_b, w_pw2, name):
    S = z.shape[0]
    tm = ROW_TILE

    def body(a_ref, b_ref, ap_ref, bp_ref, w_ref, cb_ref, g_ref, be_ref, pw_ref, cv_ref, y_ref, win):
        _conv_window(win, ap_ref[...], bp_ref[...], a_ref[...], b_ref[...], pl.program_id(0) == 0)
        acc = jnp.zeros((tm, CONV_W), F32)
        for kk in range(CONV_K):
            acc = acc + w_ref[kk:kk + 1, :] * win[pl.ds(HALO - (CONV_K - 1) + kk, tm), :]
        cv = acc + cb_ref[...]
        cv_ref[...] = cv
        mu = jnp.mean(cv, axis=-1, keepdims=True)
        cc = cv - mu
        rstd = lax.rsqrt(jnp.mean(cc * cc, axis=-1, keepdims=True) + EPS)
        n = cc * rstd * g_ref[...] + be_ref[...]
        sl = n * _sigmoid(n)
        y_ref[...] = jnp.dot(sl.astype(BF16), pw_ref[...], preferred_element_type=F32)

    return pl.pallas_call(
        body, name=name, grid=(S // tm,),
        in_specs=_conv_in_specs(tm) + [_full((HALO, CONV_W)), _full((1, CONV_W)), _full((1, CONV_W)),
                                       _full((1, CONV_W)), _full((CONV_W, CONV_W))],
        out_specs=(_rows(tm, CONV_W), _rows(tm, CONV_W)),
        out_shape=(_sds((S, CONV_W)), _sds((S, CONV_W))),
        scratch_shapes=[pltpu.VMEM((tm + HALO, CONV_W), F32)],
        compiler_params=_cparams("parallel"),
    )(z, z, z, z, conv_w_p, conv_b, ln_g, ln_b, w_pw2)


def _sgu_common(u, v, g_ref, be_ref):
    gu, dgu = _gelu_and_grad(u)
    gv, dgv = _gelu_and_grad(v)
    mu = jnp.mean(gv, axis=-1, keepdims=True)
    cc = gv - mu
    rstd = lax.rsqrt(jnp.mean(cc * cc, axis=-1, keepdims=True) + EPS)
    nh = cc * rstd
    vn = nh * g_ref[...] + be_ref[...]
    return gu, dgu, dgv, rstd, nh, vn


def _sgu_masks():
    lane_group = lax.broadcasted_iota(jnp.int32, (1, SGU_W), 1) // SGU_GD
    ri = lax.broadcasted_iota(jnp.int32, (SGU_T, SGU_T), 0)
    ci = lax.broadcasted_iota(jnp.int32, (SGU_T, SGU_T), 1)
    return [lane_group == g for g in range(SGU_G)], ci <= ri


def _sgu_forward(z, ln_g, ln_b, w_s, bias_full, name):
    S = z.shape[0]
    tm = ROW_TILE

    def body(u_ref, v_ref, g_ref, be_ref, ws_ref, bias_ref, y_ref):
        gmask, tril = _sgu_masks()
        wm = [jnp.where(tril, ws_ref[g], 0.0).astype(BF16) for g in range(SGU_G)]
        for ch in range(tm // SGU_T):
            rows = slice(ch * SGU_T, (ch + 1) * SGU_T)
            gu, _, _, _, _, vn = _sgu_common(u_ref[rows, :], v_ref[rows, :], g_ref, be_ref)
            vb = vn.astype(BF16)
            sv = bias_ref[...]
            for g in range(SGU_G):
                sv = sv + jnp.where(gmask[g], jnp.dot(wm[g], vb, preferred_element_type=F32), 0.0)
            y_ref[rows, :] = gu * sv

    return pl.pallas_call(
        body, name=name, grid=(S // tm,),
        in_specs=[_rows(tm, SGU_W, 7), _rows(tm, SGU_W, 8), _full((1, SGU_W)), _full((1, SGU_W)),
                  _full((SGU_G, SGU_T, SGU_T)), _full((SGU_T, SGU_W))],
        out_specs=_rows(tm, SGU_W), out_shape=_sds((S, SGU_W)),
        compiler_params=_cparams("parallel"),
    )(z, z, ln_g, ln_b, w_s, bias_full)


def _out_proj(x, z, y_att, y_conv, y_sgu, w_out, g_post, mod, name):
    S = x.shape[0]
    tm = ROW_TILE

    def body(x_ref, ga_ref, gc_ref, gs_ref, ya_ref, yc_ref, ys_ref, w_ref, gp_ref, mod_ref, xn_ref, y_ref, cat_ref):
        ca = (ya_ref[...] * _silu_and_grad(ga_ref[...])[0]).astype(BF16)
        cc = (yc_ref[...] * _silu_and_grad(gc_ref[...])[0]).astype(BF16)
        cs = (ys_ref[...] * _silu_and_grad(gs_ref[...])[0]).astype(BF16)
        cat_ref[:, 0:ATT_W] = ca
        cat_ref[:, ATT_W:ATT_W + CONV_W] = cc
        cat_ref[:, ATT_W + CONV_W:] = cs
        y = (jnp.dot(ca, w_ref[0:ATT_W, :], preferred_element_type=F32)
             + jnp.dot(cc, w_ref[ATT_W:ATT_W + CONV_W, :], preferred_element_type=F32)
             + jnp.dot(cs, w_ref[ATT_W + CONV_W:, :], preferred_element_type=F32))
        y_ref[...] = y
        rstd = lax.rsqrt(jnp.mean(y * y, axis=-1, keepdims=True) + EPS)
        xn_ref[...] = x_ref[...] + mod_ref[2:3, :] * (y * rstd * gp_ref[...])

    return pl.pallas_call(
        body, name=name, grid=(S // tm,),
        in_specs=[_rows(tm, D), _rows(tm, 512, 1), _rows(tm, 256, 6), _rows(tm, 256, 9),
                  _rows(tm, ATT_W), _rows(tm, CONV_W), _rows(tm, SGU_W),
                  _full((D, D)), _full((1, D)), _full((3, D))],
        out_specs=(_rows(tm, D), _rows(tm, D), _rows(tm, D)),
        out_shape=(_sds((S, D)), _sds((S, D)), _sds((S, D), BF16)),
        compiler_params=_cparams("parallel"),
    )(x, z, z, z, y_att, y_conv, y_sgu, w_out, g_post, mod)


def _loss_head(y, target):
    S = y.shape[0]
    tm = ROW_TILE

    def body(y_ref, t_ref, loss_ref, dy_ref):
        @pl.when(pl.program_id(0) == 0)
        def _():
            loss_ref[...] = jnp.zeros(loss_ref.shape, F32)

        err = y_ref[...] - t_ref[...]
        dy_ref[...] = err * (1.0 / D)
        row = jnp.sum(err * err, axis=-1, keepdims=True) * (1.0 / D)
        loss_ref[...] += 0.5 * jnp.sum(row, axis=0, keepdims=True)

    return pl.pallas_call(
        body, name="loss_head", grid=(S // tm,),
        in_specs=[_rows(tm, D), _rows(tm, D)],
        out_specs=(_full((1, 1)), _rows(tm, D)),
        out_shape=(_sds((1, 1)), _sds((S, D))),
        compiler_params=_cparams("arbitrary"),
    )(y, target)


def _matmul_tn(a, b, name):
    S, M = a.shape
    N = b.shape[1]
    bk = min(512, S)

    def body(a_ref, b_ref, o_ref):
        @pl.when(pl.program_id(0) == 0)
        def _():
            o_ref[...] = jnp.zeros(o_ref.shape, F32)

        o_ref[...] += lax.dot_general(a_ref[...], b_ref[...], TN, preferred_element_type=F32)

    return pl.pallas_call(
        body, name=name, grid=(S // bk,),
        in_specs=[pl.BlockSpec((bk, M), lambda k: (k, 0)), pl.BlockSpec((bk, N), lambda k: (k, 0))],
        out_specs=_full((M, N)), out_shape=_sds((M, N)),
        compiler_params=_cparams("arbitrary"),
    )(a, b)


def _out_proj_backward(dxo, y, z, y_att, y_conv, y_sgu, lse, w_out, g_post, mod, name):
    S = dxo.shape[0]
    tm = ROW_TILE

    def body(dxo_ref, y_ref, ga_ref, gc_ref, gs_ref, ya_ref, yc_ref, ys_ref, lse_ref, w_ref, gp_ref, mod_ref,
             dyb_ref, dob_ref, st_ref, dga_ref, dyc_ref, dgc_ref, dys_ref, dgs_ref, dgate_ref, dgp_ref):
        @pl.when(pl.program_id(0) == 0)
        def _():
            dgate_ref[...] = jnp.zeros(dgate_ref.shape, F32)
            dgp_ref[...] = jnp.zeros(dgp_ref.shape, F32)

        dxo_v = dxo_ref[...]
        yv = y_ref[...]
        gp = gp_ref[...]
        rstd = lax.rsqrt(jnp.mean(yv * yv, axis=-1, keepdims=True) + EPS)
        yhat = yv * rstd
        dgate_ref[...] += jnp.sum(dxo_v * (yhat * gp), axis=0, keepdims=True)
        dr = dxo_v * mod_ref[2:3, :]
        dgp_ref[...] += jnp.sum(dr * yhat, axis=0, keepdims=True)
        dyh = dr * gp
        dy = rstd * (dyh - yhat * jnp.mean(dyh * yhat, axis=-1, keepdims=True))
        dyb = dy.astype(BF16)
        dyb_ref[...] = dyb
        dcat = lax.dot_general(dyb, w_ref[...], NT, preferred_element_type=F32)

        ya = ya_ref[...]
        sil, dsil = _silu_and_grad(ga_ref[...])
        da = dcat[:, 0:ATT_W]
        do = da * sil
        dob_ref[...] = do.astype(BF16)
        dga_ref[...] = da * ya * dsil
        lane = lax.broadcasted_iota(jnp.int32, (1, 128), 1)
        stats = jnp.zeros((tm, 128), F32)
        for h in range(HEADS):
            cols = slice(h * VDIM, (h + 1) * VDIM)
            delta = jnp.sum(do[:, cols] * ya[:, cols], axis=-1, keepdims=True)
            stats = stats + jnp.where(lane == 2 * h, lse_ref[h], 0.0) + jnp.where(lane == 2 * h + 1, delta, 0.0)
        st_ref[...] = stats

        sil, dsil = _silu_and_grad(gc_ref[...])
        dc = dcat[:, ATT_W:ATT_W + CONV_W]
        dyc_ref[...] = dc * sil
        dgc_ref[...] = dc * yc_ref[...] * dsil
        sil, dsil = _silu_and_grad(gs_ref[...])
        dsg = dcat[:, ATT_W + CONV_W:]
        dys_ref[...] = dsg * sil
        dgs_ref[...] = dsg * ys_ref[...] * dsil

    return pl.pallas_call(
        body, name=name, grid=(S // tm,),
        in_specs=[_rows(tm, D), _rows(tm, D), _rows(tm, 512, 1), _rows(tm, 256, 6), _rows(tm, 256, 9),
                  _rows(tm, ATT_W), _rows(tm, CONV_W), _rows(tm, SGU_W),
                  pl.BlockSpec((HEADS, tm, 1), lambda i: (0, i, 0)),
                  _full((D, D)), _full((1, D)), _full((3, D))],
        out_specs=(_rows(tm, D), _rows(tm, ATT_W), _rows(tm, 128), _rows(tm, ATT_W),
                   _rows(tm, CONV_W), _rows(tm, CONV_W), _rows(tm, SGU_W), _rows(tm, SGU_W),
                   _full((1, D)), _full((1, D))),
        out_shape=(_sds((S, D), BF16), _sds((S, ATT_W), BF16), _sds((S, 128)), _sds((S, ATT_W)),
                   _sds((S, CONV_W)), _sds((S, CONV_W)), _sds((S, SGU_W)), _sds((S, SGU_W)),
                   _sds((1, D)), _sds((1, D))),
        compiler_params=_cparams("arbitrary"),
    )(dxo, y, z, z, z, y_att, y_conv, y_sgu, lse, w_out, g_post, mod)


def _flash_backward(q, k, v, do, stats, name):
    S = q.shape[0]
    t = ATT_TILE
    tk = 2 * t
    nq = S // t

    def body(q_ref, do_ref, st_ref, k_ref, v_ref, dq_ref, dk_ref, dv_ref, dk_sc, dv_sc):
        h = pl.program_id(0)
        j = pl.program_id(1)

        @pl.when(j == 0)
        def _():
            dq_ref[...] = jnp.zeros(dq_ref.shape, F32)

        dk_sc[...] = jnp.zeros(dk_sc.shape, F32)
        dv_sc[...] = jnp.zeros(dv_sc.shape, F32)
        lane = lax.broadcasted_iota(jnp.int32, (1, 128), 1)

        def chain(hf, qv, dov, lse2, delta, diagonal):
            kt = k_ref[hf * t:(hf + 1) * t, :]
            s = lax.dot_general(qv, kt, NT, preferred_element_type=F32)
            p = jnp.exp2(s * EXP2_SCALE - lse2)
            if diagonal:
                ri = lax.broadcasted_iota(jnp.int32, (t, t), 0)
                ci = lax.broadcasted_iota(jnp.int32, (t, t), 1)
                p = jnp.where(ci <= ri, p, 0.0)
            dv_sc[hf] += lax.dot_general(p.astype(BF16), dov, TN, preferred_element_type=F32)
            dp = lax.dot_general(dov, v_ref[hf * t:(hf + 1) * t, :], NT, preferred_element_type=F32)
            ds = (p * (dp - delta) * ATT_SCALE).astype(BF16)
            dk_sc[hf] += lax.dot_general(ds, qv, TN, preferred_element_type=F32)
            return jnp.dot(ds, kt, preferred_element_type=F32)

        def q_tile(qb, modes):
            rows = pl.ds(pl.multiple_of(qb * t, t), t)
            qv = q_ref[rows, :]
            dov = do_ref[rows, :]
            st = st_ref[rows, :]
            lse2 = jnp.sum(jnp.where(lane == 2 * h, st, 0.0), axis=-1, keepdims=True) * LOG2E
            delta = jnp.sum(jnp.where(lane == 2 * h + 1, st, 0.0), axis=-1, keepdims=True)
            parts = [chain(hf, qv, dov, lse2, delta, modes[hf]) for hf in range(2) if modes[hf] is not None]
            dq_ref[rows, :] += parts[0] if len(parts) == 1 else parts[0] + parts[1]

        q_tile(2 * j, (True, None))
        q_tile(2 * j + 1, (False, True))

        def loop_body(qb, carry):
            q_tile(qb, (False, False))
            return carry

        lax.fori_loop(2 * j + 2, nq, loop_body, 0)
        for hf in range(2):
            dk_ref[hf * t:(hf + 1) * t, :] = dk_sc[hf]
            dv_ref[hf * t:(hf + 1) * t, :] = dv_sc[hf]

    return pl.pallas_call(
        body, name=name, grid=(HEADS, S // tk),
        in_specs=[pl.BlockSpec((S, HQ), lambda h, j: (0, h)),
                  pl.BlockSpec((S, VDIM), lambda h, j: (0, h)),
                  pl.BlockSpec((S, 128), lambda h, j: (0, 0)),
                  pl.BlockSpec((tk, HQ), lambda h, j: (j, h)),
                  pl.BlockSpec((tk, VDIM), lambda h, j: (j, h))],
        out_specs=(pl.BlockSpec((S, HQ), lambda h, j: (0, h)),
                   pl.BlockSpec((tk, HQ), lambda h, j: (j, h)),
                   pl.BlockSpec((tk, VDIM), lambda h, j: (j, h))),
        out_shape=(_sds((S, HEADS * HQ)), _sds((S, HEADS * HQ)), _sds((S, ATT_W))),
        scratch_shapes=[pltpu.VMEM((2, t, HQ), F32), pltpu.VMEM((2, t, VDIM), F32)],
        compiler_params=_cparams("parallel", "arbitrary"),
    )(q, do, stats, k, v)


def _att_prep_backward(z, pos, dq, dk, dv, q_g, kv_g, wq_p, w_ukv, rope_rows, name):
    S = z.shape[0]
    tm = ROW_TILE

    def body(z_ref, pos_ref, dq_ref, dk_ref, dv_ref, qg_ref, kvg_ref, wq_ref, wkv_ref, rope_ref,
             dz_ref, qn_ref, dqp_ref, kvn_ref, dkv_ref, dqg_ref, dkvg_ref):
        @pl.when(pl.program_id(0) == 0)
        def _():
            dqg_ref[...] = jnp.zeros(dqg_ref.shape, F32)
            dkvg_ref[...] = jnp.zeros(dkvg_ref.shape, F32)

        zz = z_ref[...]
        ql, kvl = zz[:, 0:Q_RANK], zz[:, Q_RANK:Q_RANK + KV_RANK]
        q_rstd = lax.rsqrt(jnp.mean(ql * ql, axis=-1, keepdims=True) + EPS)
        kv_rstd = lax.rsqrt(jnp.mean(kvl * kvl, axis=-1, keepdims=True) + EPS)
        qhat, kvhat = ql * q_rstd, kvl * kv_rstd
        qg, kvg = qg_ref[...], kvg_ref[...]
        qn_ref[...] = (qhat * qg).astype(BF16)
        kvn_ref[...] = (kvhat * kvg).astype(BF16)
        ct, st = _rope_tables(pos_ref, rope_ref)

        def unrotate(d):
            return d * ct + _swap_halves(d * st)

        dkrot = jnp.zeros((tm, NOPE), F32)
        for h in range(HEADS):
            b = h * HQ
            dqp_ref[:, b:b + NOPE] = dq_ref[:, b:b + NOPE].astype(BF16)
            dqp_ref[:, b + NOPE:b + HQ] = unrotate(dq_ref[:, b + NOPE:b + HQ]).astype(BF16)
            dkv_ref[:, b:b + NOPE] = dk_ref[:, b:b + NOPE].astype(BF16)
            dkv_ref[:, b + NOPE:b + HQ] = dv_ref[:, h * VDIM:(h + 1) * VDIM].astype(BF16)
            dkrot = dkrot + dk_ref[:, b + NOPE:b + HQ]
        dqn = lax.dot_general(dqp_ref[...], wq_ref[...], NT, preferred_element_type=F32)
        dkvn = lax.dot_general(dkv_ref[...], wkv_ref[...], NT, preferred_element_type=F32)
        dqg_ref[...] += jnp.sum(dqn * qhat, axis=0, keepdims=True)
        dkvg_ref[...] += jnp.sum(dkvn * kvhat, axis=0, keepdims=True)
        dqh, dkvh = dqn * qg, dkvn * kvg
        dz_ref[:, 0:Q_RANK] = q_rstd * (dqh - qhat * jnp.mean(dqh * qhat, axis=-1, keepdims=True))
        dz_ref[:, Q_RANK:Q_RANK + KV_RANK] = kv_rstd * (dkvh - kvhat * jnp.mean(dkvh * kvhat, axis=-1, keepdims=True))
        dz_ref[:, Q_RANK + KV_RANK:] = unrotate(dkrot)

    W = HEADS * HQ
    return pl.pallas_call(
        body, name=name, grid=(S // tm,),
        in_specs=[_rows(tm, 512, 0), _rows(tm, 1), _rows(tm, W), _rows(tm, W), _rows(tm, ATT_W),
                  _full((1, Q_RANK)), _full((1, KV_RANK)), _full((Q_RANK, W)), _full((KV_RANK, W)), _full((8, 128))],
        out_specs=(_rows(tm, 512), _rows(tm, Q_RANK), _rows(tm, W), _rows(tm, KV_RANK), _rows(tm, W),
                   _full((1, Q_RANK)), _full((1, KV_RANK))),
        out_shape=(_sds((S, 512)), _sds((S, Q_RANK), BF16), _sds((S, W), BF16), _sds((S, KV_RANK), BF16),
                   _sds((S, W), BF16), _sds((1, Q_RANK)), _sds((1, KV_RANK))),
        compiler_params=_cparams("arbitrary"),
    )(z, pos, dq, dk, dv, q_g, kv_g, wq_p, w_ukv, rope_rows)


def _conv_norm_backward(dyc, cv, ln_g, ln_b, w_pw2, name):
    S = cv.shape[0]
    tm = ROW_TILE

    def body(dy_ref, cv_ref, g_ref, be_ref, pw_ref, dcv_ref, sl_ref, dyb_ref, dg_ref, db_ref, dcb_ref):
        @pl.when(pl.program_id(0) == 0)
        def _():
            dg_ref[...] = jnp.zeros(dg_ref.shape, F32)
            db_ref[...] = jnp.zeros(db_ref.shape, F32)
            dcb_ref[...] = jnp.zeros(dcb_ref.shape, F32)

        cv_v = cv_ref[...]
        mu = jnp.mean(cv_v, axis=-1, keepdims=True)
        cc = cv_v - mu
        rstd = lax.rsqrt(jnp.mean(cc * cc, axis=-1, keepdims=True) + EPS)
        nh = cc * rstd
        g = g_ref[...]
        n = nh * g + be_ref[...]
        sil, dsil = _silu_and_grad(n)
        sl_ref[...] = sil.astype(BF16)
        dyb = dy_ref[...].astype(BF16)
        dyb_ref[...] = dyb
        dn = lax.dot_general(dyb, pw_ref[...], NT, preferred_element_type=F32) * dsil
        db_ref[...] += jnp.sum(dn, axis=0, keepdims=True)
        dg_ref[...] += jnp.sum(dn * nh, axis=0, keepdims=True)
        dnh = dn * g
        dcv = rstd * (dnh - jnp.mean(dnh, axis=-1, keepdims=True) - nh * jnp.mean(dnh * nh, axis=-1, keepdims=True))
        dcv_ref[...] = dcv
        dcb_ref[...] += jnp.sum(dcv, axis=0, keepdims=True)

    vec = _full((1, CONV_W))
    return pl.pallas_call(
        body, name=name, grid=(S // tm,),
        in_specs=[_rows(tm, CONV_W), _rows(tm, CONV_W), vec, vec, _full((CONV_W, CONV_W))],
        out_specs=(_rows(tm, CONV_W), _rows(tm, CONV_W), _rows(tm, CONV_W), vec, vec, vec),
        out_shape=(_sds((S, CONV_W)), _sds((S, CONV_W), BF16), _sds((S, CONV_W), BF16),
                   _sds((1, CONV_W)), _sds((1, CONV_W)), _sds((1, CONV_W))),
        compiler_params=_cparams("arbitrary"),
    )(dyc, cv, ln_g, ln_b, w_pw2)


def _conv_backward(z, dcv, conv_w_p, name):
    S = z.shape[0]
    tm = ROW_TILE
    per = tm // HALO
    last_halo = S // HALO - 1

    def body(a_ref, b_ref, ap_ref, bp_ref, d_ref, dn_ref, w_ref, da_ref, db_ref, dw_ref, win, dwin):
        i = pl.program_id(0)

        @pl.when(i == 0)
        def _():
            dw_ref[...] = jnp.zeros(dw_ref.shape, F32)

        av, bv = a_ref[...], b_ref[...]
        _conv_window(win, ap_ref[...], bp_ref[...], av, bv, i == 0)
        dcur = d_ref[...]
        dwin[0:tm, :] = dcur
        dwin[tm:, :] = jnp.where(i == pl.num_programs(0) - 1, 0.0, dn_ref[...])
        dh = jnp.zeros((tm, CONV_W), F32)
        for kk in range(CONV_K):
            dh = dh + w_ref[kk:kk + 1, :] * dwin[pl.ds(CONV_K - 1 - kk, tm), :]
            dw_ref[kk:kk + 1, :] += jnp.sum(dcur * win[pl.ds(HALO - (CONV_K - 1) + kk, tm), :], axis=0, keepdims=True)
        sb = _sigmoid(bv)
        da_ref[...] = dh * sb
        db_ref[...] = dh * av * sb * (1.0 - sb)

    nxt = pl.BlockSpec((HALO, CONV_W), lambda i: (jnp.minimum((i + 1) * per, last_halo), 0))
    return pl.pallas_call(
        body, name=name, grid=(S // tm,),
        in_specs=_conv_in_specs(tm) + [_rows(tm, CONV_W), nxt, _full((HALO, CONV_W))],
        out_specs=(_rows(tm, CONV_W), _rows(tm, CONV_W), _full((HALO, CONV_W))),
        out_shape=(_sds((S, CONV_W)), _sds((S, CONV_W)), _sds((HALO, CONV_W))),
        scratch_shapes=[pltpu.VMEM((tm + HALO, CONV_W), F32), pltpu.VMEM((tm + HALO, CONV_W), F32)],
        compiler_params=_cparams("arbitrary"),
    )(z, z, z, z, dcv, dcv, conv_w_p)


def _sgu_backward(z, dy, ln_g, ln_b, w_s, bias_full, name):
    S = z.shape[0]
    tm = ROW_TILE

    def body(u_ref, v_ref, dy_ref, g_ref, be_ref, ws_ref, bias_ref, du_ref, dv_ref, dws_ref, dbs_ref, dg_ref, db_ref):
        @pl.when(pl.program_id(0) == 0)
        def _():
            dws_ref[...] = jnp.zeros(dws_ref.shape, F32)
            dbs_ref[...] = jnp.zeros(dbs_ref.shape, F32)
            dg_ref[...] = jnp.zeros(dg_ref.shape, F32)
            db_ref[...] = jnp.zeros(db_ref.shape, F32)

        gmask, tril = _sgu_masks()
        lane = lax.broadcasted_iota(jnp.int32, (1, 128), 1)
        wm = [jnp.where(tril, ws_ref[g], 0.0).astype(BF16) for g in range(SGU_G)]
        gain = g_ref[...]
        for ch in range(tm // SGU_T):
            rows = slice(ch * SGU_T, (ch + 1) * SGU_T)
            gu, dgu, dgv, rstd, nh, vn = _sgu_common(u_ref[rows, :], v_ref[rows, :], g_ref, be_ref)
            vb = vn.astype(BF16)
            sv = bias_ref[...]
            for g in range(SGU_G):
                sv = sv + jnp.where(gmask[g], jnp.dot(wm[g], vb, preferred_element_type=F32), 0.0)
            dyv = dy_ref[rows, :]
            du_ref[rows, :] = dyv * sv * dgu
            dsv = dyv * gu
            dsvb = dsv.astype(BF16)
            dvn = jnp.zeros((SGU_T, SGU_W), F32)
            for g in range(SGU_G):
                dsg = jnp.where(gmask[g], dsv, 0.0)
                dwg = lax.dot_general(dsg.astype(BF16), vb, NT, preferred_element_type=F32)
                dws_ref[g] += jnp.where(tril, dwg, 0.0)
                dvn = dvn + jnp.where(gmask[g], lax.dot_general(wm[g], dsvb, TN, preferred_element_type=F32), 0.0)
                dbs_ref[...] += jnp.where(lane == g, jnp.sum(dsg, axis=-1, keepdims=True), 0.0)
            db_ref[...] += jnp.sum(dvn, axis=0, keepdims=True)
            dg_ref[...] += jnp.sum(dvn * nh, axis=0, keepdims=True)
            dnh = dvn * gain
            dgvv = rstd * (dnh - jnp.mean(dnh, axis=-1, keepdims=True) - nh * jnp.mean(dnh * nh, axis=-1, keepdims=True))
            dv_ref[rows, :] = dgvv * dgv

    vec = _full((1, SGU_W))
    return pl.pallas_call(
        body, name=name, grid=(S // tm,),
        in_specs=[_rows(tm, SGU_W, 7), _rows(tm, SGU_W, 8), _rows(tm, SGU_W), vec, vec,
                  _full((SGU_G, SGU_T, SGU_T)), _full((SGU_T, SGU_W))],
        out_specs=(_rows(tm, SGU_W), _rows(tm, SGU_W), _full((SGU_G, SGU_T, SGU_T)), _full((SGU_T, 128)), vec, vec),
        out_shape=(_sds((S, SGU_W)), _sds((S, SGU_W)), _sds((SGU_G, SGU_T, SGU_T)), _sds((SGU_T, 128)),
                   _sds((1, SGU_W)), _sds((1, SGU_W))),
        compiler_params=_cparams("arbitrary"),
    )(z, z, dy, ln_g, ln_b, w_s, bias_full)


def _inproj_backward(x, dxo, dz_att, dga, dca, dcb, dgc, dsu, dsv, dgs, g_pre, mod, w_in_p, name):
    S = x.shape[0]
    tm = ROW_TILE

    def body(x_ref, dxo_ref, p0, p1, p2, p3, p4, p5, p6, p7, g_ref, mod_ref, w_ref,
             dx_ref, hb_ref, dzb_ref, dmod_ref, dg_ref):
        @pl.when(pl.program_id(0) == 0)
        def _():
            dmod_ref[...] = jnp.zeros(dmod_ref.shape, F32)
            dg_ref[...] = jnp.zeros(dg_ref.shape, F32)

        off = 0
        for piece in (p0, p1, p2, p3, p4, p5, p6, p7):
            wdt = piece.shape[1]
            dzb_ref[:, off:off + wdt] = piece[...].astype(BF16)
            off += wdt
        dh = lax.dot_general(dzb_ref[...], w_ref[...], NT, preferred_element_type=F32)
        xv = x_ref[...]
        g = g_ref[...]
        one_scale = 1.0 + mod_ref[1:2, :]
        rstd = lax.rsqrt(jnp.mean(xv * xv, axis=-1, keepdims=True) + EPS)
        xhat = xv * rstd
        xg = xhat * g
        hb_ref[...] = (xg * one_scale + mod_ref[0:1, :]).astype(BF16)
        dmod_ref[0:1, :] += jnp.sum(dh, axis=0, keepdims=True)
        dmod_ref[1:2, :] += jnp.sum(dh * xg, axis=0, keepdims=True)
        dhs = dh * one_scale
        dg_ref[...] += jnp.sum(dhs * xhat, axis=0, keepdims=True)
        dxh = dhs * g
        dx_ref[...] = dxo_ref[...] + rstd * (dxh - xhat * jnp.mean(dxh * xhat, axis=-1, keepdims=True))

    widths = (512, 512, 256, 256, 256, 256, 256, 256)
    return pl.pallas_call(
        body, name=name, grid=(S // tm,),
        in_specs=[_rows(tm, D), _rows(tm, D)] + [_rows(tm, w) for w in widths]
                 + [_full((1, D)), _full((3, D)), _full((D, DZ))],
        out_specs=(_rows(tm, D), _rows(tm, D), _rows(tm, DZ), _full((2, D)), _full((1, D))),
        out_shape=(_sds((S, D)), _sds((S, D), BF16), _sds((S, DZ), BF16), _sds((2, D)), _sds((1, D))),
        compiler_params=_cparams("arbitrary"),
    )(x, dxo, dz_att, dga, dca, dcb, dgc, dsu, dsv, dgs, g_pre, mod, w_in_p)


def _adamw(w, gparts, m, v, name):
    shape = w.shape
    cols = shape[-1]
    rows = int(np.prod(shape[:-1]))
    parts = gparts.shape[0]
    w2, m2, v2 = (a.reshape(rows, cols) for a in (w, m, v))
    g3 = gparts.reshape(parts, rows, cols)
    tr = rows
    for cand in (256, 128):
        if rows > cand and rows % cand == 0:
            tr = cand
            break

    def body(w_ref, g_ref, m_ref, v_ref, go_ref, d_ref, mo_ref, vo_ref):
        g = g_ref[0].astype(F32)
        for p in range(1, parts):
            g = g + g_ref[p].astype(F32)
        wv = w_ref[...]
        mn = ADAM_B1 * m_ref[...] + (1.0 - ADAM_B1) * g
        vn = ADAM_B2 * v_ref[...] + (1.0 - ADAM_B2) * (g * g)
        m_hat = mn / (1.0 - ADAM_B1 ** ADAM_STEP)
        v_hat = vn / (1.0 - ADAM_B2 ** ADAM_STEP)
        go_ref[...] = g
        d_ref[...] = -ADAM_LR * (m_hat / (jnp.sqrt(v_hat) + ADAM_EPS) + ADAM_WD * wv)
        mo_ref[...] = mn
        vo_ref[...] = vn

    blk = pl.BlockSpec((tr, cols), lambda i: (i, 0))
    outs = pl.pallas_call(
        body, name=name, grid=(rows // tr,),
        in_specs=[blk, pl.BlockSpec((parts, tr, cols), lambda i: (0, i, 0)), blk, blk],
        out_specs=(blk, blk, blk, blk),
        out_shape=tuple(_sds((rows, cols)) for _ in range(4)),
        compiler_params=_cparams("parallel"),
    )(w2, g3, m2, v2)
    return tuple(o.reshape(shape) for o in outs)


_GATHERED = (("w_in", (DEPTH, D, D_IN // N_DEV)), ("w_out", (DEPTH, D // N_DEV, D)),
             ("w_uq", (DEPTH, Q_RANK, HEADS * QK // N_DEV)), ("w_ukv", (DEPTH, KV_RANK, HEADS * HQ // N_DEV)),
             ("w_pw2", (DEPTH, CONV_W // N_DEV, CONV_W)), ("conv_w", (DEPTH, CONV_K, CONV_W // N_DEV)))
_COL_SHARDED = ("w_in", "w_uq", "w_ukv", "conv_w")

_SMALL = (("dmod", (DEPTH, 3 * D)), ("g_pre", (DEPTH, D)), ("g_post", (DEPTH, D)), ("q_norm_g", (DEPTH, Q_RANK)),
          ("kv_norm_g", (DEPTH, KV_RANK)), ("conv_b", (DEPTH, CONV_W)), ("conv_ln_g", (DEPTH, CONV_W)),
          ("conv_ln_b", (DEPTH, CONV_W)), ("sgu_ln_g", (DEPTH, SGU_W)), ("sgu_ln_b", (DEPTH, SGU_W)),
          ("w_s", (DEPTH, SGU_G, SGU_T, SGU_T)), ("b_s", (DEPTH, SGU_G, SGU_T)))


def _assemble(name, parts):
    if name in _COL_SHARDED:
        p = jnp.moveaxis(parts, 0, -2)
        return p.reshape(p.shape[:-2] + (p.shape[-2] * p.shape[-1],))
    p = jnp.moveaxis(parts, 0, 1)
    return p.reshape((p.shape[0], p.shape[1] * p.shape[2], p.shape[3]))


def _scatter_layout(name, full):
    if name in _COL_SHARDED:
        p = full.reshape(full.shape[:-1] + (N_DEV, full.shape[-1] // N_DEV))
        return jnp.moveaxis(p, -2, 0)
    p = full.reshape((full.shape[0], N_DEV, full.shape[1] // N_DEV, full.shape[2]))
    return jnp.moveaxis(p, 1, 0)


def kernel(x, c, positions, w_ada, b_ada, g_pre, g_post, w_in, q_norm_g, w_uq, kv_norm_g, w_ukv, conv_w, conv_b, conv_ln_g, conv_ln_b, w_pw2, sgu_ln_g, sgu_ln_b, w_s, b_s, w_out, loss_target, m_w_ada, m_b_ada, m_g_pre, m_g_post, m_w_in, m_q_norm_g, m_w_uq, m_kv_norm_g, m_w_ukv, m_conv_w, m_conv_b, m_conv_ln_g, m_conv_ln_b, m_w_pw2, m_sgu_ln_g, m_sgu_ln_b, m_w_s, m_b_s, m_w_out, v_w_ada, v_b_ada, v_g_pre, v_g_post, v_w_in, v_q_norm_g, v_w_uq, v_kv_norm_g, v_w_ukv, v_conv_w, v_conv_b, v_conv_ln_g, v_conv_ln_b, v_w_pw2, v_sgu_ln_g, v_sgu_ln_b, v_w_s, v_b_s, v_w_out):
    weights = dict(w_ada=w_ada, b_ada=b_ada, g_pre=g_pre, g_post=g_post, w_in=w_in, q_norm_g=q_norm_g, w_uq=w_uq,
                   kv_norm_g=kv_norm_g, w_ukv=w_ukv, conv_w=conv_w, conv_b=conv_b, conv_ln_g=conv_ln_g,
                   conv_ln_b=conv_ln_b, w_pw2=w_pw2, sgu_ln_g=sgu_ln_g, sgu_ln_b=sgu_ln_b, w_s=w_s, b_s=b_s, w_out=w_out)
    m_in = dict(w_ada=m_w_ada, b_ada=m_b_ada, g_pre=m_g_pre, g_post=m_g_post, w_in=m_w_in, q_norm_g=m_q_norm_g,
                w_uq=m_w_uq, kv_norm_g=m_kv_norm_g, w_ukv=m_w_ukv, conv_w=m_conv_w, conv_b=m_conv_b,
                conv_ln_g=m_conv_ln_g, conv_ln_b=m_conv_ln_b, w_pw2=m_w_pw2, sgu_ln_g=m_sgu_ln_g,
                sgu_ln_b=m_sgu_ln_b, w_s=m_w_s, b_s=m_b_s, w_out=m_w_out)
    v_in = dict(w_ada=v_w_ada, b_ada=v_b_ada, g_pre=v_g_pre, g_post=v_g_post, w_in=v_w_in, q_norm_g=v_q_norm_g,
                w_uq=v_w_uq, kv_norm_g=v_kv_norm_g, w_ukv=v_w_ukv, conv_w=v_conv_w, conv_b=v_conv_b,
                conv_ln_g=v_conv_ln_g, conv_ln_b=v_conv_ln_b, w_pw2=v_w_pw2, sgu_ln_g=v_sgu_ln_g,
                sgu_ln_b=v_sgu_ln_b, w_s=v_w_s, b_s=v_b_s, w_out=v_w_out)
    order = list(weights)

    S = x.shape[1]
    me = 4 * lax.axis_index("x") + 2 * lax.axis_index("y") + lax.axis_index("c")
    x0 = x.reshape(S, D)
    target = loss_target.reshape(S, D)
    pos = positions.reshape(S, 1)

    gathered = _exchange([weights[n].astype(BF16) for n, _ in _GATHERED], all_to_all=False, name="gather_weights")
    full = {n: _assemble(n, p) for (n, _), p in zip(_GATHERED, gathered)}
    c_all = _exchange([c.reshape(8, D // 8)], all_to_all=False, name="gather_c")[0].reshape(N_DEV, D)

    ada_cols = w_ada.shape[-1]
    b_cols = lax.dynamic_slice_in_dim(b_ada, me * ada_cols, ada_cols, axis=1)
    sc_rows, mod_part = _ada_forward(jnp.pad(c_all, ((0, 8), (0, 0))), w_ada, b_cols)
    mod_recv = _exchange([jnp.moveaxis(mod_part[:, :N_DEV], 1, 0)], all_to_all=True, name="exchange_mod")[0]
    mod = jnp.moveaxis(mod_recv, 0, 1).reshape(DEPTH, 3, D)

    w_in_f = full["w_in"]
    w_in_p = jnp.concatenate([w_in_f[:, :, :ATT_IN], jnp.zeros((DEPTH, D, PAD_IN), BF16), w_in_f[:, :, ATT_IN:]], axis=2)
    wq_p = jnp.pad(full["w_uq"].reshape(DEPTH, Q_RANK, HEADS, QK), ((0, 0), (0, 0), (0, 0), (0, HQ - QK)))
    wq_p = wq_p.reshape(DEPTH, Q_RANK, HEADS * HQ)
    conv_w_p = jnp.pad(full["conv_w"].astype(F32), ((0, 0), (0, HALO - CONV_K), (0, 0)))
    bias_full = jnp.repeat(jnp.swapaxes(b_s, 1, 2), SGU_GD, axis=2)
    inv_freq = ROPE_THETA ** (-jnp.arange(0, ROPE, 2, dtype=F32) / ROPE)
    zeros32 = jnp.zeros((ROPE // 2,), F32)
    ones32 = jnp.ones((ROPE // 2,), F32)
    rope_rows = jnp.zeros((8, 128), F32)
    rope_rows = rope_rows.at[0].set(jnp.concatenate([inv_freq, inv_freq, zeros32, zeros32]))
    rope_rows = rope_rows.at[1].set(jnp.concatenate([ones32, ones32, zeros32, zeros32]))
    rope_rows = rope_rows.at[2].set(jnp.concatenate([-ones32, ones32, zeros32, zeros32]))

    def vec(a, l):
        return a[l].reshape(1, -1)

    saved = []
    xl = x0
    for l in range(DEPTH):
        z = _prenorm_inproj(xl, vec(g_pre, l), mod[l], w_in_p[l], name=f"prenorm_inproj_{l}")
        q, k, v = _att_prep(z, pos, vec(q_norm_g, l), vec(kv_norm_g, l), wq_p[l], full["w_ukv"][l], rope_rows,
                            name=f"att_prep_{l}")
        y_att, lse = _flash_forward(q, k, v, name=f"flash_forward_{l}")
        cv, y_conv = _conv_forward(z, conv_w_p[l], vec(conv_b, l), vec(conv_ln_g, l), vec(conv_ln_b, l),
                                   full["w_pw2"][l], name=f"conv_forward_{l}")
        y_sgu = _sgu_forward(z, vec(sgu_ln_g, l), vec(sgu_ln_b, l), w_s[l], bias_full[l], name=f"sgu_forward_{l}")
        x_next, y, ycat = _out_proj(xl, z, y_att, y_conv, y_sgu, full["w_out"][l], vec(g_post, l), mod[l],
                                    name=f"out_proj_{l}")
        saved.append(dict(x=xl, z=z, q=q, k=k, v=v, y_att=y_att, lse=lse, cv=cv, y_conv=y_conv, y_sgu=y_sgu, y=y, ycat=ycat))
        xl = x_next

    loss_part, dx = _loss_head(xl, target)
    loss = lax.psum(loss_part.reshape(()), ("x", "y", "c"))

    big = {n: [None] * DEPTH for n, _ in _GATHERED}
    small = {n: [None] * DEPTH for n, _ in _SMALL}
    for l in reversed(range(DEPTH)):
        sv = saved[l]
        (dyb, dob, stats, dga, dyc, dgc, dys, dgs, dgate, dgpost) = _out_proj_backward(
            dx, sv["y"], sv["z"], sv["y_att"], sv["y_conv"], sv["y_sgu"], sv["lse"], full["w_out"][l],
            vec(g_post, l), mod[l], name=f"out_proj_backward_{l}")
        big["w_out"][l] = _matmul_tn(sv["ycat"], dyb, name=f"grad_w_out_{l}")
        dq, dk, dv = _flash_backward(sv["q"], sv["k"], sv["v"], dob, stats, name=f"flash_backward_{l}")
        dz_att, qn_b, dqp_b, kvn_b, dkv_b, dqg, dkvg = _att_prep_backward(
            sv["z"], pos, dq, dk, dv, vec(q_norm_g, l), vec(kv_norm_g, l), wq_p[l], full["w_ukv"][l], rope_rows,
            name=f"att_prep_backward_{l}")
        dwq_p = _matmul_tn(qn_b, dqp_b, name=f"grad_w_uq_{l}")
        big["w_uq"][l] = dwq_p.reshape(Q_RANK, HEADS, HQ)[:, :, :QK].reshape(Q_RANK, HEADS * QK)
        big["w_ukv"][l] = _matmul_tn(kvn_b, dkv_b, name=f"grad_w_ukv_{l}")
        dcv, sl_b, dyc_b, dclg, dclb, dcb = _conv_norm_backward(dyc, sv["cv"], vec(conv_ln_g, l), vec(conv_ln_b, l),
                                                              full["w_pw2"][l], name=f"conv_norm_backward_{l}")
        big["w_pw2"][l] = _matmul_tn(sl_b, dyc_b, name=f"grad_w_pw2_{l}")
        dca, dcbb, dconvw = _conv_backward(sv["z"], dcv, conv_w_p[l], name=f"conv_backward_{l}")
        big["conv_w"][l] = dconvw[:CONV_K]
        dsu, dsvv, dws, dbs, dslg, dslb = _sgu_backward(sv["z"], dys, vec(sgu_ln_g, l), vec(sgu_ln_b, l), w_s[l],
                                                       bias_full[l], name=f"sgu_backward_{l}")
        dx, h_b, dz_b, dmod2, dgpre = _inproj_backward(
            sv["x"], dx, dz_att, dga, dca, dcbb, dgc, dsu, dsvv, dgs, vec(g_pre, l), mod[l], w_in_p[l],
            name=f"inproj_backward_{l}")
        dwin_p = _matmul_tn(h_b, dz_b, name=f"grad_w_in_{l}")
        big["w_in"][l] = jnp.concatenate([dwin_p[:, :ATT_IN], dwin_p[:, ATT_IN + PAD_IN:]], axis=1)
        small["dmod"][l] = jnp.concatenate([dmod2.reshape(-1), dgate.reshape(-1)])
        small["g_pre"][l] = dgpre.reshape(-1)
        small["g_post"][l] = dgpost.reshape(-1)
        small["q_norm_g"][l] = dqg.reshape(-1)
        small["kv_norm_g"][l] = dkvg.reshape(-1)
        small["conv_b"][l] = dcb.reshape(-1)
        small["conv_ln_g"][l] = dclg.reshape(-1)
        small["conv_ln_b"][l] = dclb.reshape(-1)
        small["sgu_ln_g"][l] = dslg.reshape(-1)
        small["sgu_ln_b"][l] = dslb.reshape(-1)
        small["w_s"][l] = dws
        small["b_s"][l] = jnp.swapaxes(dbs[:, :SGU_G], 0, 1)
    grad_x = dx.reshape(1, S, D)

    send = [_scatter_layout(n, jnp.stack(big[n])).astype(BF16) for n, _ in _GATHERED]
    gparts = dict(zip([n for n, _ in _GATHERED], _exchange(send, all_to_all=True, name="exchange_grads")))
    spack = _Packer(_SMALL, 8)
    sparts = spack.unpack(_exchange([spack.pack({n: jnp.stack(small[n]) for n, _ in _SMALL}, F32)],
                                    all_to_all=False, name="gather_small_grads")[0], (N_DEV,))
    dmod_all = sparts["dmod"]
    dmod_cols = lax.dynamic_slice_in_dim(dmod_all, me * ada_cols, ada_cols, axis=2)
    sc_t = jnp.pad(sc_rows[:N_DEV].T, ((0, 0), (0, 128 - N_DEV)))
    dmod_rows = jnp.pad(jnp.moveaxis(dmod_cols, 0, 1), ((0, 0), (0, 128 - N_DEV), (0, 0)))
    gparts["w_ada"] = _ada_backward(sc_t, dmod_rows)[None]
    gparts["b_ada"] = dmod_all
    for n, _ in _SMALL[1:]:
        gparts[n] = sparts[n]

    grads, deltas, new_m, new_v = {}, {}, {}, {}
    for n in order:
        grads[n], deltas[n], new_m[n], new_v[n] = _adamw(weights[n], gparts[n], m_in[n], v_in[n], name=f"adamw_{n}")
    return (loss, grad_x, *[grads[n] for n in order], *[deltas[n] for n in order],
            *[new_m[n] for n in order], *[new_v[n] for n in order])
```

```python
import functools
import math

import numpy as np
import jax
import jax.numpy as jnp
from jax import lax
from jax.experimental import pallas as pl
from jax.experimental.pallas import tpu as pltpu

F32 = jnp.float32
BF16 = jnp.bfloat16

N_DEV = 8
DEPTH = 2
D = 1024
HEADS = 4
NOPE = 128
ROPE = 64
VDIM = 128
QK = NOPE + ROPE
Q_RANK = 256
KV_RANK = 128
ATT_W = HEADS * VDIM
CONV_W = 256
CONV_K = 31
SGU_W = 256
SGU_G = 4
SGU_GD = SGU_W // SGU_G
SGU_T = 128
D_IN = 2496
ATT_IN = Q_RANK + KV_RANK + ROPE
PAD_IN = 64
DZ = D_IN + PAD_IN
HQ = 2 * NOPE
EPS = 1e-6
ROPE_THETA = 10000.0
ATT_SCALE = QK ** -0.5
LOG2E = math.log2(math.e)
EXP2_SCALE = ATT_SCALE * LOG2E
NEG_INF = float("-inf")

ADAM_LR = 0.001
ADAM_B1 = 0.9
ADAM_B2 = 0.999
ADAM_EPS = 1e-08
ADAM_WD = 0.01
ADAM_STEP = 10

VMEM_LIMIT = 56 * 1024 * 1024
ROW_TILE = 256
ATT_TILE = 512
HALO = 32
PACK_LANES = 128

MESH = pl.DeviceIdType.MESH
NT = (((1,), (1,)), ((), ()))
TN = (((0,), (0,)), ((), ()))


def _cparams(*sem):
    return pltpu.CompilerParams(dimension_semantics=sem, vmem_limit_bytes=VMEM_LIMIT)


def _sds(shape, dtype=F32):
    return jax.ShapeDtypeStruct(tuple(shape), dtype)


def _rows(tm, width, col=0):
    return pl.BlockSpec((tm, width), lambda i: (i, col))


def _full(shape):
    nd = len(shape)
    return pl.BlockSpec(tuple(shape), lambda *_: (0,) * nd)


def _sigmoid(x):
    return 1.0 / (1.0 + jnp.exp(-x))


def _silu_and_grad(g):
    s = _sigmoid(g)
    return g * s, s * (1.0 + g * (1.0 - s))


def _gelu_and_grad(x):
    cdf = 0.5 * (1.0 + lax.erf(x * (1.0 / math.sqrt(2.0))))
    pdf = jnp.exp(-0.5 * x * x) * (1.0 / math.sqrt(2.0 * math.pi))
    return x * cdf, cdf + x * pdf


def _swap_halves(a):
    lane = lax.broadcasted_iota(jnp.int32, a.shape, 1)
    up = pltpu.roll(a, 32, 1)
    down = pltpu.roll(a, 96, 1)
    return jnp.where(lane < 32, down, jnp.where(lane < 64, up, 0.0))


def _rope_tables(pos_ref, rope_ref):
    ang = pos_ref[...].astype(F32) * rope_ref[0:1, :]
    return jnp.cos(ang) * rope_ref[1:2, :], jnp.sin(ang) * rope_ref[2:3, :]


class _Comm:
    def __init__(self, srcs, kinds):
        self.srcs = list(srcs)
        self.kinds = list(kinds)
        self.n = len(self.srcs)
        self.out_shape = [_sds((N_DEV,) + tuple(s.shape[1:] if k else s.shape), s.dtype)
                          for s, k in zip(self.srcs, self.kinds)]
        self.specs = [pl.BlockSpec(memory_space=pl.ANY)] * self.n
        self.scratch = [pltpu.SemaphoreType.DMA((self.n, N_DEV - 1)), pltpu.SemaphoreType.DMA((self.n, N_DEV - 1)),
                        pltpu.SemaphoreType.DMA((self.n,))] if self.n else []

    def _copies(self, src_refs, out_refs, sems):
        send_sems, recv_sems, local_sems = sems
        x, y, c = lax.axis_index("x"), lax.axis_index("y"), lax.axis_index("c")
        me = 4 * x + 2 * y + c
        local, sends, recvs = [], [], []
        for a in range(self.n):
            def block_for(dest, src_ref=src_refs[a], a2a=self.kinds[a]):
                return src_ref.at[dest] if a2a else src_ref

            local.append(pltpu.make_async_copy(block_for(me), out_refs[a].at[me], local_sems.at[a]))
            for r in range(1, N_DEV):
                px = 1 - x if (r >> 2) & 1 else x
                py = 1 - y if (r >> 1) & 1 else y
                pc = 1 - c if r & 1 else c
                peer = 4 * px + 2 * py + pc
                sends.append(pltpu.make_async_remote_copy(
                    src_ref=block_for(peer), dst_ref=out_refs[a].at[me],
                    send_sem=send_sems.at[a, r - 1], recv_sem=recv_sems.at[a, r - 1],
                    device_id=(px, py, pc), device_id_type=MESH))
                recvs.append(pltpu.make_async_remote_copy(
                    src_ref=block_for(me), dst_ref=out_refs[a].at[peer],
                    send_sem=send_sems.at[a, r - 1], recv_sem=recv_sems.at[a, r - 1],
                    device_id=(px, py, pc), device_id_type=MESH))
        return local, sends, recvs

    def start(self, src_refs, out_refs, sems):
        local, sends, _ = self._copies(src_refs, out_refs, sems)
        for cp in local + sends:
            cp.start()

    def finish(self, src_refs, out_refs, sems):
        local, sends, recvs = self._copies(src_refs, out_refs, sems)
        for cp in recvs:
            cp.wait_recv()
        for cp in sends:
            cp.wait_send()
        for cp in local:
            cp.wait()


def _exchange(srcs, kinds, name):
    comm = _Comm(srcs, kinds)
    n = comm.n

    def body(*refs):
        src_refs, out_refs, sems = refs[:n], refs[n:2 * n], refs[2 * n:]
        comm.start(src_refs, out_refs, sems)
        comm.finish(src_refs, out_refs, sems)

    return pl.pallas_call(
        body, name=name, out_shape=comm.out_shape, in_specs=comm.specs, out_specs=comm.specs,
        scratch_shapes=comm.scratch,
    )(*srcs)


class _Packer:
    def __init__(self, entries, row_multiple):
        self.entries = entries
        self.offsets = {}
        off = 0
        for name, shape in entries:
            self.offsets[name] = off
            off += int(np.prod(shape))
        quantum = PACK_LANES * row_multiple
        self.total = -(-off // quantum) * quantum
        self.used = off
        self.rows = self.total // PACK_LANES

    def pack(self, arrays, dtype, lead=()):
        n = len(lead)
        flat = [arrays[name].astype(dtype).reshape(lead + (-1,)) for name, _ in self.entries]
        flat.append(jnp.zeros(lead + (self.total - self.used,), dtype))
        return jnp.concatenate(flat, axis=n).reshape(lead + (self.rows, PACK_LANES))

    def unpack(self, buf, lead=()):
        flat = buf.reshape(lead + (self.total,))
        out = {}
        for name, shape in self.entries:
            o = self.offsets[name]
            out[name] = lax.slice_in_dim(flat, o, o + int(np.prod(shape)), axis=len(lead)).reshape(lead + tuple(shape))
        return out


def _ada_forward(c_rows, w_ada, b_ada_cols):
    cols = w_ada.shape[-1]
    rows = c_rows.shape[0]

    def body(c_ref, w_ref, b_ref, sc_ref, part_ref):
        cv = c_ref[...]
        sc = cv * _sigmoid(cv)
        sc_ref[...] = sc
        scb = sc.astype(BF16)
        for l in range(DEPTH):
            part_ref[l] = jnp.dot(scb, w_ref[l].astype(BF16), preferred_element_type=F32) + b_ref[l:l + 1, :]

    return pl.pallas_call(
        body, name="ada_forward",
        out_shape=(_sds((rows, D)), _sds((DEPTH, rows, cols))),
        compiler_params=pltpu.CompilerParams(vmem_limit_bytes=VMEM_LIMIT),
    )(c_rows, w_ada, b_ada_cols)


def _ada_backward(sc_t, dmod_cols):
    cols = dmod_cols.shape[-1]

    def body(sc_ref, dm_ref, gw_ref):
        scb = sc_ref[...].astype(BF16)
        for l in range(DEPTH):
            gw_ref[l] = jnp.dot(scb, dm_ref[l].astype(BF16), preferred_element_type=F32)

    return pl.pallas_call(
        body, name="ada_backward",
        out_shape=_sds((DEPTH, D, cols)),
        compiler_params=pltpu.CompilerParams(vmem_limit_bytes=VMEM_LIMIT),
    )(sc_t, dmod_cols)


def _prenorm_inproj(x, g_pre, mod, w_in_p, name):
    S = x.shape[0]
    tm = ROW_TILE

    def body(x_ref, g_ref, mod_ref, w_ref, z_ref):
        xv = x_ref[...]
        rstd = lax.rsqrt(jnp.mean(xv * xv, axis=-1, keepdims=True) + EPS)
        h = (xv * rstd * g_ref[...]) * (1.0 + mod_ref[1:2, :]) + mod_ref[0:1, :]
        z_ref[...] = jnp.dot(h.astype(BF16), w_ref[...], preferred_element_type=F32)

    return pl.pallas_call(
        body, name=name, grid=(S // tm,),
        in_specs=[_rows(tm, D), _full((1, D)), _full((3, D)), _full((D, DZ))],
        out_specs=_rows(tm, DZ), out_shape=_sds((S, DZ)),
        compiler_params=_cparams("parallel"),
    )(x, g_pre, mod, w_in_p)


def _att_prep(z, pos, q_g, kv_g, wq_p, w_ukv, rope_rows, name):
    S = z.shape[0]
    tm = ROW_TILE

    def body(z_ref, pos_ref, qg_ref, kvg_ref, wq_ref, wkv_ref, rope_ref, q_ref, k_ref, v_ref):
        zz = z_ref[...]
        ql, kvl, ka = zz[:, 0:Q_RANK], zz[:, Q_RANK:Q_RANK + KV_RANK], zz[:, Q_RANK + KV_RANK:]
        qn = ql * lax.rsqrt(jnp.mean(ql * ql, axis=-1, keepdims=True) + EPS) * qg_ref[...]
        kvn = kvl * lax.rsqrt(jnp.mean(kvl * kvl, axis=-1, keepdims=True) + EPS) * kvg_ref[...]
        q = jnp.dot(qn.astype(BF16), wq_ref[...], preferred_element_type=F32)
        kv = jnp.dot(kvn.astype(BF16), wkv_ref[...], preferred_element_type=F32)
        ct, st = _rope_tables(pos_ref, rope_ref)
        krot = (ka * ct + _swap_halves(ka) * st).astype(BF16)
        for h in range(HEADS):
            b = h * HQ
            q_ref[:, b:b + NOPE] = q[:, b:b + NOPE].astype(BF16)
            a = q[:, b + NOPE:b + HQ]
            q_ref[:, b + NOPE:b + HQ] = (a * ct + _swap_halves(a) * st).astype(BF16)
            k_ref[:, b:b + NOPE] = kv[:, b:b + NOPE].astype(BF16)
            k_ref[:, b + NOPE:b + HQ] = krot
            v_ref[:, h * VDIM:(h + 1) * VDIM] = kv[:, b + NOPE:b + HQ].astype(BF16)

    return pl.pallas_call(
        body, name=name, grid=(S // tm,),
        in_specs=[_rows(tm, 512, 0), _rows(tm, 1), _full((1, Q_RANK)), _full((1, KV_RANK)),
                  _full((Q_RANK, HEADS * HQ)), _full((KV_RANK, HEADS * HQ)), _full((8, 128))],
        out_specs=(_rows(tm, HEADS * HQ), _rows(tm, HEADS * HQ), _rows(tm, ATT_W)),
        out_shape=(_sds((S, HEADS * HQ), BF16), _sds((S, HEADS * HQ), BF16), _sds((S, ATT_W), BF16)),
        compiler_params=_cparams("parallel"),
    )(z, pos, q_g, kv_g, wq_p, w_ukv, rope_rows)


def _flash_forward(q, k, v, name, comm=None):
    S = q.shape[0]
    t = ATT_TILE
    tq = 2 * t
    nq = S // tq
    nl = t // 128
    comm = comm or _Comm([], [])
    nc = comm.n

    def body(*refs):
        q_ref, k_ref, v_ref = refs[:3]
        c_src = refs[3:3 + nc]
        o_ref, lse_ref = refs[3 + nc:5 + nc]
        c_out = refs[5 + nc:5 + 2 * nc]
        m_sc, l_sc, acc_sc = refs[5 + 2 * nc:8 + 2 * nc]
        c_sems = refs[8 + 2 * nc:]
        if nc:
            @pl.when((pl.program_id(0) == 0) & (pl.program_id(1) == 0))
            def _():
                comm.start(c_src, c_out, c_sems)

        qb = pl.program_id(1)
        m_sc[...] = jnp.full(m_sc.shape, NEG_INF, F32)
        l_sc[...] = jnp.zeros(l_sc.shape, F32)
        acc_sc[...] = jnp.zeros(acc_sc.shape, F32)

        def scores(hf, rows, diagonal):
            s = lax.dot_general(q_ref[hf * t:(hf + 1) * t, :], k_ref[rows, :], NT, preferred_element_type=F32)
            if diagonal:
                ri = lax.broadcasted_iota(jnp.int32, (t, t), 0)
                ci = lax.broadcasted_iota(jnp.int32, (t, t), 1)
                s = jnp.where(ci <= ri, s, NEG_INF)
            return s

        def max_step(hf, kb, diagonal):
            s = scores(hf, pl.ds(pl.multiple_of(kb * t, t), t), diagonal)
            mp = m_sc[hf]
            for c in range(nl):
                mp = jnp.maximum(mp, s[:, c * 128:(c + 1) * 128])
            m_sc[hf] = mp

        def sum_step(hf, kb, diagonal):
            rows = pl.ds(pl.multiple_of(kb * t, t), t)
            s = scores(hf, rows, diagonal)
            p = jnp.exp2(s * EXP2_SCALE - jnp.tile(m_sc[hf], (1, nl)))
            lp = l_sc[hf]
            for c in range(nl):
                lp = lp + p[:, c * 128:(c + 1) * 128]
            l_sc[hf] = lp
            acc_sc[hf] += jnp.dot(p.astype(BF16), v_ref[rows, :], preferred_element_type=F32)

        def sweep(step):
            def loop_body(kb, carry):
                step(0, kb, False)
                step(1, kb, False)
                return carry

            lax.fori_loop(0, 2 * qb, loop_body, 0)
            step(0, 2 * qb, True)
            step(1, 2 * qb, False)
            step(1, 2 * qb + 1, True)

        sweep(max_step)
        for hf in range(2):
            m = jnp.max(m_sc[hf], axis=-1, keepdims=True)
            m_sc[hf] = jnp.broadcast_to(m * EXP2_SCALE, (t, 128))
        sweep(sum_step)
        for hf in range(2):
            l = jnp.sum(l_sc[hf], axis=-1, keepdims=True)
            o_ref[hf * t:(hf + 1) * t, :] = acc_sc[hf] / l
            m_scaled = jnp.max(m_sc[hf], axis=-1, keepdims=True) * (1.0 / LOG2E)
            lse_ref[0, hf * t:(hf + 1) * t, :] = m_scaled + jnp.log(l)

        if nc:
            @pl.when((pl.program_id(0) == HEADS - 1) & (pl.program_id(1) == nq - 1))
            def _():
                comm.finish(c_src, c_out, c_sems)

    outs = pl.pallas_call(
        body, name=name, grid=(HEADS, nq),
        in_specs=[pl.BlockSpec((tq, HQ), lambda h, i: (i, h)),
                  pl.BlockSpec((S, HQ), lambda h, i: (0, h)),
                  pl.BlockSpec((S, VDIM), lambda h, i: (0, h))] + comm.specs,
        out_specs=[pl.BlockSpec((tq, VDIM), lambda h, i: (i, h)),
                   pl.BlockSpec((1, tq, 1), lambda h, i: (h, i, 0))] + comm.specs,
        out_shape=[_sds((S, ATT_W)), _sds((HEADS, S, 1))] + comm.out_shape,
        scratch_shapes=[pltpu.VMEM((2, t, 128), F32), pltpu.VMEM((2, t, 128), F32), pltpu.VMEM((2, t, VDIM), F32)]
                       + comm.scratch,
        compiler_params=_cparams("arbitrary", "arbitrary"),
    )(q, k, v, *comm.srcs)
    return outs[0], outs[1], outs[2:]


def _conv_window(win_ref, a_prev, b_prev, a_cur, b_cur, first):
    hp = a_prev * _sigmoid(b_prev)
    win_ref[0:HALO, :] = jnp.where(first, 0.0, hp)
    win_ref[HALO:, :] = a_cur * _sigmoid(b_cur)


def _conv_in_specs(tm):
    per = tm // HALO
    prev = lambda col: pl.BlockSpec((HALO, CONV_W), lambda i: (jnp.maximum(i * per - 1, 0), col))
    return [_rows(tm, CONV_W, 4), _rows(tm, CONV_W, 5), prev(4), prev(5)]


def _conv_forward(z, conv_w_p, conv_b, ln_g, ln_b, w_pw2, name):
    S = z.shape[0]
    tm = ROW_TILE

    def body(a_ref, b_ref, ap_ref, bp_ref, w_ref, cb_ref, g_ref, be_ref, pw_ref, cv_ref, y_ref, win):
        _conv_window(win, ap_ref[...], bp_ref[...], a_ref[...], b_ref[...], pl.program_id(0) == 0)
        acc = jnp.zeros((tm, CONV_W), F32)
        for kk in range(CONV_K):
            acc = acc + w_ref[kk:kk + 1, :] * win[pl.ds(HALO - (CONV_K - 1) + kk, tm), :]
        cv = acc + cb_ref[...]
        cv_ref[...] = cv
        mu = jnp.mean(cv, axis=-1, keepdims=True)
        cc = cv - mu
        rstd = lax.rsqrt(jnp.mean(cc * cc, axis=-1, keepdims=True) + EPS)
        n = cc * rstd * g_ref[...] + be_ref[...]
        sl = n * _sigmoid(n)
        y_ref[...] = jnp.dot(sl.astype(BF16), pw_ref[...], preferred_element_type=F32)

    return pl.pallas_call(
        body, name=name, grid=(S // tm,),
        in_specs=_conv_in_specs(tm) + [_full((HALO, CONV_W)), _full((1, CONV_W)), _full((1, CONV_W)),
                                       _full((1, CONV_W)), _full((CONV_W, CONV_W))],
        out_specs=(_rows(tm, CONV_W), _rows(tm, CONV_W)),
        out_shape=(_sds((S, CONV_W)), _sds((S, CONV_W))),
        scratch_shapes=[pltpu.VMEM((tm + HALO, CONV_W), F32)],
        compiler_params=_cparams("parallel"),
    )(z, z, z, z, conv_w_p, conv_b, ln_g, ln_b, w_pw2)


def _sgu_common(u, v, g_ref, be_ref):
    gu, dgu = _gelu_and_grad(u)
    gv, dgv = _gelu_and_grad(v)
    mu = jnp.mean(gv, axis=-1, keepdims=True)
    cc = gv - mu
    rstd = lax.rsqrt(jnp.mean(cc * cc, axis=-1, keepdims=True) + EPS)
    nh = cc * rstd
    vn = nh * g_ref[...] + be_ref[...]
    return gu, dgu, dgv, rstd, nh, vn


def _sgu_masks():
    lane_group = lax.broadcasted_iota(jnp.int32, (1, SGU_W), 1) // SGU_GD
    ri = lax.broadcasted_iota(jnp.int32, (SGU_T, SGU_T), 0)
    ci = lax.broadcasted_iota(jnp.int32, (SGU_T, SGU_T), 1)
    return [lane_group == g for g in range(SGU_G)], ci <= ri


def _sgu_forward(z, ln_g, ln_b, w_s, bias_full, name):
    S = z.shape[0]
    tm = ROW_TILE

    def body(u_ref, v_ref, g_ref, be_ref, ws_ref, bias_ref, y_ref):
        gmask, tril = _sgu_masks()
        wm = [jnp.where(tril, ws_ref[g], 0.0).astype(BF16) for g in range(SGU_G)]
        for ch in range(tm // SGU_T):
            rows = slice(ch * SGU_T, (ch + 1) * SGU_T)
            gu, _, _, _, _, vn = _sgu_common(u_ref[rows, :], v_ref[rows, :], g_ref, be_ref)
            vb = vn.astype(BF16)
            sv = bias_ref[...]
            for g in range(SGU_G):
                sv = sv + jnp.where(gmask[g], jnp.dot(wm[g], vb, preferred_element_type=F32), 0.0)
            y_ref[rows, :] = gu * sv

    return pl.pallas_call(
        body, name=name, grid=(S // tm,),
        in_specs=[_rows(tm, SGU_W, 7), _rows(tm, SGU_W, 8), _full((1, SGU_W)), _full((1, SGU_W)),
                  _full((SGU_G, SGU_T, SGU_T)), _full((SGU_T, SGU_W))],
        out_specs=_rows(tm, SGU_W), out_shape=_sds((S, SGU_W)),
        compiler_params=_cparams("parallel"),
    )(z, z, ln_g, ln_b, w_s, bias_full)


def _out_proj(x, z, y_att, y_conv, y_sgu, w_out, g_post, mod, name):
    S = x.shape[0]
    tm = ROW_TILE

    def body(x_ref, ga_ref, gc_ref, gs_ref, ya_ref, yc_ref, ys_ref, w_ref, gp_ref, mod_ref, xn_ref, y_ref, cat_ref):
        ca = (ya_ref[...] * _silu_and_grad(ga_ref[...])[0]).astype(BF16)
        cc = (yc_ref[...] * _silu_and_grad(gc_ref[...])[0]).astype(BF16)
        cs = (ys_ref[...] * _silu_and_grad(gs_ref[...])[0]).astype(BF16)
        cat_ref[:, 0:ATT_W] = ca
        cat_ref[:, ATT_W:ATT_W + CONV_W] = cc
        cat_ref[:, ATT_W + CONV_W:] = cs
        y = (jnp.dot(ca, w_ref[0:ATT_W, :], preferred_element_type=F32)
             + jnp.dot(cc, w_ref[ATT_W:ATT_W + CONV_W, :], preferred_element_type=F32)
             + jnp.dot(cs, w_ref[ATT_W + CONV_W:, :], preferred_element_type=F32))
        y_ref[...] = y
        rstd = lax.rsqrt(jnp.mean(y * y, axis=-1, keepdims=True) + EPS)
        xn_ref[...] = x_ref[...] + mod_ref[2:3, :] * (y * rstd * gp_ref[...])

    return pl.pallas_call(
        body, name=name, grid=(S // tm,),
        in_specs=[_rows(tm, D), _rows(tm, 512, 1), _rows(tm, 256, 6), _rows(tm, 256, 9),
                  _rows(tm, ATT_W), _rows(tm, CONV_W), _rows(tm, SGU_W),
                  _full((D, D)), _full((1, D)), _full((3, D))],
        out_specs=(_rows(tm, D), _rows(tm, D), _rows(tm, D)),
        out_shape=(_sds((S, D)), _sds((S, D)), _sds((S, D), BF16)),
        compiler_params=_cparams("parallel"),
    )(x, z, z, z, y_att, y_conv, y_sgu, w_out, g_post, mod)


def _loss_head(y, target):
    S = y.shape[0]
    tm = ROW_TILE

    def body(y_ref, t_ref, loss_ref, dy_ref):
        @pl.when(pl.program_id(0) == 0)
        def _():
            loss_ref[...] = jnp.zeros(loss_ref.shape, F32)

        err = y_ref[...] - t_ref[...]
        dy_ref[...] = err * (1.0 / D)
        row = jnp.sum(err * err, axis=-1, keepdims=True) * (1.0 / D)
        loss_ref[...] += 0.5 * jnp.sum(row, axis=0, keepdims=True)

    return pl.pallas_call(
        body, name="loss_head", grid=(S // tm,),
        in_specs=[_rows(tm, D), _rows(tm, D)],
        out_specs=(_full((1, 1)), _rows(tm, D)),
        out_shape=(_sds((1, 1)), _sds((S, D))),
        compiler_params=_cparams("arbitrary"),
    )(y, target)


def _matmul_tn(a, b, name):
    S, M = a.shape
    N = b.shape[1]
    bk = min(512, S)

    def body(a_ref, b_ref, o_ref):
        @pl.when(pl.program_id(0) == 0)
        def _():
            o_ref[...] = jnp.zeros(o_ref.shape, F32)

        o_ref[...] += lax.dot_general(a_ref[...], b_ref[...], TN, preferred_element_type=F32)

    return pl.pallas_call(
        body, name=name, grid=(S // bk,),
        in_specs=[pl.BlockSpec((bk, M), lambda k: (k, 0)), pl.BlockSpec((bk, N), lambda k: (k, 0))],
        out_specs=_full((M, N)), out_shape=_sds((M, N)),
        compiler_params=_cparams("arbitrary"),
    )(a, b)


def _out_proj_backward(dxo, y, z, y_att, y_conv, y_sgu, lse, w_out, g_post, mod, name):
    S = dxo.shape[0]
    tm = ROW_TILE

    def body(dxo_ref, y_ref, ga_ref, gc_ref, gs_ref, ya_ref, yc_ref, ys_ref, lse_ref, w_ref, gp_ref, mod_ref,
             dyb_ref, dob_ref, st_ref, dga_ref, dyc_ref, dgc_ref, dys_ref, dgs_ref, dgate_ref, dgp_ref):
        @pl.when(pl.program_id(0) == 0)
        def _():
            dgate_ref[...] = jnp.zeros(dgate_ref.shape, F32)
            dgp_ref[...] = jnp.zeros(dgp_ref.shape, F32)

        dxo_v = dxo_ref[...]
        yv = y_ref[...]
        gp = gp_ref[...]
        rstd = lax.rsqrt(jnp.mean(yv * yv, axis=-1, keepdims=True) + EPS)
        yhat = yv * rstd
        dgate_ref[...] += jnp.sum(dxo_v * (yhat * gp), axis=0, keepdims=True)
        dr = dxo_v * mod_ref[2:3, :]
        dgp_ref[...] += jnp.sum(dr * yhat, axis=0, keepdims=True)
        dyh = dr * gp
        dy = rstd * (dyh - yhat * jnp.mean(dyh * yhat, axis=-1, keepdims=True))
        dyb = dy.astype(BF16)
        dyb_ref[...] = dyb
        dcat = lax.dot_general(dyb, w_ref[...], NT, preferred_element_type=F32)

        ya = ya_ref[...]
        sil, dsil = _silu_and_grad(ga_ref[...])
        da = dcat[:, 0:ATT_W]
        do = da * sil
        dob_ref[...] = do.astype(BF16)
        dga_ref[...] = (da * ya * dsil).astype(BF16)
        lane = lax.broadcasted_iota(jnp.int32, (1, 128), 1)
        stats = jnp.zeros((tm, 128), F32)
        for h in range(HEADS):
            cols = slice(h * VDIM, (h + 1) * VDIM)
            delta = jnp.sum(do[:, cols] * ya[:, cols], axis=-1, keepdims=True)
            stats = stats + jnp.where(lane == 2 * h, lse_ref[h], 0.0) + jnp.where(lane == 2 * h + 1, delta, 0.0)
        st_ref[...] = stats

        sil, dsil = _silu_and_grad(gc_ref[...])
        dc = dcat[:, ATT_W:ATT_W + CONV_W]
        dyc_ref[...] = dc * sil
        dgc_ref[...] = (dc * yc_ref[...] * dsil).astype(BF16)
        sil, dsil = _silu_and_grad(gs_ref[...])
        dsg = dcat[:, ATT_W + CONV_W:]
        dys_ref[...] = dsg * sil
        dgs_ref[...] = (dsg * ys_ref[...] * dsil).astype(BF16)

    return pl.pallas_call(
        body, name=name, grid=(S // tm,),
        in_specs=[_rows(tm, D), _rows(tm, D), _rows(tm, 512, 1), _rows(tm, 256, 6), _rows(tm, 256, 9),
                  _rows(tm, ATT_W), _rows(tm, CONV_W), _rows(tm, SGU_W),
                  pl.BlockSpec((HEADS, tm, 1), lambda i: (0, i, 0)),
                  _full((D, D)), _full((1, D)), _full((3, D))],
        out_specs=(_rows(tm, D), _rows(tm, ATT_W), _rows(tm, 128), _rows(tm, ATT_W),
                   _rows(tm, CONV_W), _rows(tm, CONV_W), _rows(tm, SGU_W), _rows(tm, SGU_W),
                   _full((1, D)), _full((1, D))),
        out_shape=(_sds((S, D), BF16), _sds((S, ATT_W), BF16), _sds((S, 128)), _sds((S, ATT_W), BF16),
                   _sds((S, CONV_W)), _sds((S, CONV_W), BF16), _sds((S, SGU_W)), _sds((S, SGU_W), BF16),
                   _sds((1, D)), _sds((1, D))),
        compiler_params=_cparams("arbitrary"),
    )(dxo, y, z, z, z, y_att, y_conv, y_sgu, lse, w_out, g_post, mod)


def _flash_backward(q, k, v, do, stats, name, comm=None):
    S = q.shape[0]
    t = ATT_TILE
    tk = 2 * t
    nq = S // t
    comm = comm or _Comm([], [])
    nc = comm.n

    def body(*refs):
        q_ref, do_ref, st_ref, k_ref, v_ref = refs[:5]
        c_src = refs[5:5 + nc]
        dq_ref, dk_ref, dv_ref = refs[5 + nc:8 + nc]
        c_out = refs[8 + nc:8 + 2 * nc]
        dk_sc, dv_sc = refs[8 + 2 * nc:10 + 2 * nc]
        c_sems = refs[10 + 2 * nc:]
        h = pl.program_id(0)
        j = pl.program_id(1)
        if nc:
            @pl.when((h == 0) & (j == 0))
            def _():
                comm.start(c_src, c_out, c_sems)

        @pl.when(j == 0)
        def _():
            dq_ref[...] = jnp.zeros(dq_ref.shape, F32)

        dk_sc[...] = jnp.zeros(dk_sc.shape, F32)
        dv_sc[...] = jnp.zeros(dv_sc.shape, F32)
        lane = lax.broadcasted_iota(jnp.int32, (1, 128), 1)

        def chain(hf, qv, dov, lse2, delta, diagonal):
            kt = k_ref[hf * t:(hf + 1) * t, :]
            s = lax.dot_general(qv, kt, NT, preferred_element_type=F32)
            p = jnp.exp2(s * EXP2_SCALE - lse2)
            if diagonal:
                ri = lax.broadcasted_iota(jnp.int32, (t, t), 0)
                ci = lax.broadcasted_iota(jnp.int32, (t, t), 1)
                p = jnp.where(ci <= ri, p, 0.0)
            dv_sc[hf] += lax.dot_general(p.astype(BF16), dov, TN, preferred_element_type=F32)
            dp = lax.dot_general(dov, v_ref[hf * t:(hf + 1) * t, :], NT, preferred_element_type=F32)
            ds = (p * (dp - delta) * ATT_SCALE).astype(BF16)
            dk_sc[hf] += lax.dot_general(ds, qv, TN, preferred_element_type=F32)
            return jnp.dot(ds, kt, preferred_element_type=F32)

        def q_tile(qb, modes):
            rows = pl.ds(pl.multiple_of(qb * t, t), t)
            qv = q_ref[rows, :]
            dov = do_ref[rows, :]
            st = st_ref[rows, :]
            lse2 = jnp.sum(jnp.where(lane == 2 * h, st, 0.0), axis=-1, keepdims=True) * LOG2E
            delta = jnp.sum(jnp.where(lane == 2 * h + 1, st, 0.0), axis=-1, keepdims=True)
            parts = [chain(hf, qv, dov, lse2, delta, modes[hf]) for hf in range(2) if modes[hf] is not None]
            dq_ref[rows, :] += parts[0] if len(parts) == 1 else parts[0] + parts[1]

        q_tile(2 * j, (True, None))
        q_tile(2 * j + 1, (False, True))

        def loop_body(qb, carry):
            q_tile(qb, (False, False))
            return carry

        lax.fori_loop(2 * j + 2, nq, loop_body, 0)
        for hf in range(2):
            dk_ref[hf * t:(hf + 1) * t, :] = dk_sc[hf]
            dv_ref[hf * t:(hf + 1) * t, :] = dv_sc[hf]

        if nc:
            @pl.when((h == HEADS - 1) & (j == S // tk - 1))
            def _():
                comm.finish(c_src, c_out, c_sems)

    outs = pl.pallas_call(
        body, name=name, grid=(HEADS, S // tk),
        in_specs=[pl.BlockSpec((S, HQ), lambda h, j: (0, h)),
                  pl.BlockSpec((S, VDIM), lambda h, j: (0, h)),
                  pl.BlockSpec((S, 128), lambda h, j: (0, 0)),
                  pl.BlockSpec((tk, HQ), lambda h, j: (j, h)),
                  pl.BlockSpec((tk, VDIM), lambda h, j: (j, h))] + comm.specs,
        out_specs=[pl.BlockSpec((S, HQ), lambda h, j: (0, h)),
                   pl.BlockSpec((tk, HQ), lambda h, j: (j, h)),
                   pl.BlockSpec((tk, VDIM), lambda h, j: (j, h))] + comm.specs,
        out_shape=[_sds((S, HEADS * HQ)), _sds((S, HEADS * HQ)), _sds((S, ATT_W))] + comm.out_shape,
        scratch_shapes=[pltpu.VMEM((2, t, HQ), F32), pltpu.VMEM((2, t, VDIM), F32)] + comm.scratch,
        compiler_params=_cparams("arbitrary", "arbitrary"),
    )(q, do, stats, k, v, *comm.srcs)
    return outs[0], outs[1], outs[2], outs[3:]


def _att_prep_backward(z, pos, dq, dk, dv, q_g, kv_g, wq_p, w_ukv, rope_rows, name):
    S = z.shape[0]
    tm = ROW_TILE

    def body(z_ref, pos_ref, dq_ref, dk_ref, dv_ref, qg_ref, kvg_ref, wq_ref, wkv_ref, rope_ref,
             dz_ref, qn_ref, dqp_ref, kvn_ref, dkv_ref, dqg_ref, dkvg_ref):
        @pl.when(pl.program_id(0) == 0)
        def _():
            dqg_ref[...] = jnp.zeros(dqg_ref.shape, F32)
            dkvg_ref[...] = jnp.zeros(dkvg_ref.shape, F32)

        zz = z_ref[...]
        ql, kvl = zz[:, 0:Q_RANK], zz[:, Q_RANK:Q_RANK + KV_RANK]
        q_rstd = lax.rsqrt(jnp.mean(ql * ql, axis=-1, keepdims=True) + EPS)
        kv_rstd = lax.rsqrt(jnp.mean(kvl * kvl, axis=-1, keepdims=True) + EPS)
        qhat, kvhat = ql * q_rstd, kvl * kv_rstd
        qg, kvg = qg_ref[...], kvg_ref[...]
        qn_ref[...] = (qhat * qg).astype(BF16)
        kvn_ref[...] = (kvhat * kvg).astype(BF16)
        ct, st = _rope_tables(pos_ref, rope_ref)

        def unrotate(d):
            return d * ct + _swap_halves(d * st)

        dkrot = jnp.zeros((tm, NOPE), F32)
        for h in range(HEADS):
            b = h * HQ
            dqp_ref[:, b:b + NOPE] = dq_ref[:, b:b + NOPE].astype(BF16)
            dqp_ref[:, b + NOPE:b + HQ] = unrotate(dq_ref[:, b + NOPE:b + HQ]).astype(BF16)
            dkv_ref[:, b:b + NOPE] = dk_ref[:, b:b + NOPE].astype(BF16)
            dkv_ref[:, b + NOPE:b + HQ] = dv_ref[:, h * VDIM:(h + 1) * VDIM].astype(BF16)
            dkrot = dkrot + dk_ref[:, b + NOPE:b + HQ]
        dqn = lax.dot_general(dqp_ref[...], wq_ref[...], NT, preferred_element_type=F32)
        dkvn = lax.dot_general(dkv_ref[...], wkv_ref[...], NT, preferred_element_type=F32)
        dqg_ref[...] += jnp.sum(dqn * qhat, axis=0, keepdims=True)
        dkvg_ref[...] += jnp.sum(dkvn * kvhat, axis=0, keepdims=True)
        dqh, dkvh = dqn * qg, dkvn * kvg
        dql = q_rstd * (dqh - qhat * jnp.mean(dqh * qhat, axis=-1, keepdims=True))
        dkvl = kv_rstd * (dkvh - kvhat * jnp.mean(dkvh * kvhat, axis=-1, keepdims=True))
        dz_ref[:, 0:Q_RANK] = dql.astype(BF16)
        dz_ref[:, Q_RANK:Q_RANK + KV_RANK] = dkvl.astype(BF16)
        dz_ref[:, Q_RANK + KV_RANK:] = unrotate(dkrot).astype(BF16)

    W = HEADS * HQ
    return pl.pallas_call(
        body, name=name, grid=(S // tm,),
        in_specs=[_rows(tm, 512, 0), _rows(tm, 1), _rows(tm, W), _rows(tm, W), _rows(tm, ATT_W),
                  _full((1, Q_RANK)), _full((1, KV_RANK)), _full((Q_RANK, W)), _full((KV_RANK, W)), _full((8, 128))],
        out_specs=(_rows(tm, 512), _rows(tm, Q_RANK), _rows(tm, W), _rows(tm, KV_RANK), _rows(tm, W),
                   _full((1, Q_RANK)), _full((1, KV_RANK))),
        out_shape=(_sds((S, 512), BF16), _sds((S, Q_RANK), BF16), _sds((S, W), BF16), _sds((S, KV_RANK), BF16),
                   _sds((S, W), BF16), _sds((1, Q_RANK)), _sds((1, KV_RANK))),
        compiler_params=_cparams("arbitrary"),
    )(z, pos, dq, dk, dv, q_g, kv_g, wq_p, w_ukv, rope_rows)


def _conv_norm_backward(dyc, cv, ln_g, ln_b, w_pw2, name):
    S = cv.shape[0]
    tm = ROW_TILE

    def body(dy_ref, cv_ref, g_ref, be_ref, pw_ref, dcv_ref, sl_ref, dyb_ref, dg_ref, db_ref, dcb_ref):
        @pl.when(pl.program_id(0) == 0)
        def _():
            dg_ref[...] = jnp.zeros(dg_ref.shape, F32)
            db_ref[...] = jnp.zeros(db_ref.shape, F32)
            dcb_ref[...] = jnp.zeros(dcb_ref.shape, F32)

        cv_v = cv_ref[...]
        mu = jnp.mean(cv_v, axis=-1, keepdims=True)
        cc = cv_v - mu
        rstd = lax.rsqrt(jnp.mean(cc * cc, axis=-1, keepdims=True) + EPS)
        nh = cc * rstd
        g = g_ref[...]
        n = nh * g + be_ref[...]
        sil, dsil = _silu_and_grad(n)
        sl_ref[...] = sil.astype(BF16)
        dyb = dy_ref[...].astype(BF16)
        dyb_ref[...] = dyb
        dn = lax.dot_general(dyb, pw_ref[...], NT, preferred_element_type=F32) * dsil
        db_ref[...] += jnp.sum(dn, axis=0, keepdims=True)
        dg_ref[...] += jnp.sum(dn * nh, axis=0, keepdims=True)
        dnh = dn * g
        dcv = rstd * (dnh - jnp.mean(dnh, axis=-1, keepdims=True) - nh * jnp.mean(dnh * nh, axis=-1, keepdims=True))
        dcv_ref[...] = dcv
        dcb_ref[...] += jnp.sum(dcv, axis=0, keepdims=True)

    vec = _full((1, CONV_W))
    return pl.pallas_call(
        body, name=name, grid=(S // tm,),
        in_specs=[_rows(tm, CONV_W), _rows(tm, CONV_W), vec, vec, _full((CONV_W, CONV_W))],
        out_specs=(_rows(tm, CONV_W), _rows(tm, CONV_W), _rows(tm, CONV_W), vec, vec, vec),
        out_shape=(_sds((S, CONV_W)), _sds((S, CONV_W), BF16), _sds((S, CONV_W), BF16),
                   _sds((1, CONV_W)), _sds((1, CONV_W)), _sds((1, CONV_W))),
        compiler_params=_cparams("arbitrary"),
    )(dyc, cv, ln_g, ln_b, w_pw2)


def _conv_backward(z, dcv, conv_w_p, name):
    S = z.shape[0]
    tm = ROW_TILE
    per = tm // HALO
    last_halo = S // HALO - 1

    def body(a_ref, b_ref, ap_ref, bp_ref, d_ref, dn_ref, w_ref, da_ref, db_ref, dw_ref, win, dwin, dw_acc):
        i = pl.program_id(0)

        @pl.when(i == 0)
        def _():
            dw_acc[...] = jnp.zeros(dw_acc.shape, F32)

        av, bv = a_ref[...], b_ref[...]
        _conv_window(win, ap_ref[...], bp_ref[...], av, bv, i == 0)
        dcur = d_ref[...]
        dwin[0:tm, :] = dcur
        dwin[tm:, :] = jnp.where(i == pl.num_programs(0) - 1, 0.0, dn_ref[...])
        dh = jnp.zeros((tm, CONV_W), F32)
        for kk in range(CONV_K):
            dh = dh + w_ref[kk:kk + 1, :] * dwin[pl.ds(CONV_K - 1 - kk, tm), :]
            prod = dcur * win[pl.ds(HALO - (CONV_K - 1) + kk, tm), :]
            dw_acc[kk] += jnp.sum(prod.reshape(tm // 8, 8, CONV_W), axis=0)
        sb = _sigmoid(bv)
        da_ref[...] = (dh * sb).astype(BF16)
        db_ref[...] = (dh * av * sb * (1.0 - sb)).astype(BF16)

        @pl.when(i == pl.num_programs(0) - 1)
        def _():
            dw_ref[...] = jnp.sum(dw_acc[...], axis=1)

    nxt = pl.BlockSpec((HALO, CONV_W), lambda i: (jnp.minimum((i + 1) * per, last_halo), 0))
    return pl.pallas_call(
        body, name=name, grid=(S // tm,),
        in_specs=_conv_in_specs(tm) + [_rows(tm, CONV_W), nxt, _full((HALO, CONV_W))],
        out_specs=(_rows(tm, CONV_W), _rows(tm, CONV_W), _full((HALO, CONV_W))),
        out_shape=(_sds((S, CONV_W), BF16), _sds((S, CONV_W), BF16), _sds((HALO, CONV_W))),
        scratch_shapes=[pltpu.VMEM((tm + HALO, CONV_W), F32), pltpu.VMEM((tm + HALO, CONV_W), F32),
                        pltpu.VMEM((HALO, 8, CONV_W), F32)],
        compiler_params=_cparams("arbitrary"),
    )(z, z, z, z, dcv, dcv, conv_w_p)


def _sgu_backward(z, dy, ln_g, ln_b, w_s, bias_full, name):
    S = z.shape[0]
    tm = ROW_TILE

    def body(u_ref, v_ref, dy_ref, g_ref, be_ref, ws_ref, bias_ref, du_ref, dv_ref, dws_ref, dbs_ref, dg_ref, db_ref):
        @pl.when(pl.program_id(0) == 0)
        def _():
            dws_ref[...] = jnp.zeros(dws_ref.shape, F32)
            dbs_ref[...] = jnp.zeros(dbs_ref.shape, F32)
            dg_ref[...] = jnp.zeros(dg_ref.shape, F32)
            db_ref[...] = jnp.zeros(db_ref.shape, F32)

        gmask, tril = _sgu_masks()
        lane = lax.broadcasted_iota(jnp.int32, (1, 128), 1)
        wm = [jnp.where(tril, ws_ref[g], 0.0).astype(BF16) for g in range(SGU_G)]
        gain = g_ref[...]
        for ch in range(tm // SGU_T):
            rows = slice(ch * SGU_T, (ch + 1) * SGU_T)
            gu, dgu, dgv, rstd, nh, vn = _sgu_common(u_ref[rows, :], v_ref[rows, :], g_ref, be_ref)
            vb = vn.astype(BF16)
            sv = bias_ref[...]
            for g in range(SGU_G):
                sv = sv + jnp.where(gmask[g], jnp.dot(wm[g], vb, preferred_element_type=F32), 0.0)
            dyv = dy_ref[rows, :]
            du_ref[rows, :] = (dyv * sv * dgu).astype(BF16)
            dsv = dyv * gu
            dsvb = dsv.astype(BF16)
            dvn = jnp.zeros((SGU_T, SGU_W), F32)
            for g in range(SGU_G):
                dsg = jnp.where(gmask[g], dsv, 0.0)
                dwg = lax.dot_general(dsg.astype(BF16), vb, NT, preferred_element_type=F32)
                dws_ref[g] += jnp.where(tril, dwg, 0.0)
                dvn = dvn + jnp.where(gmask[g], lax.dot_general(wm[g], dsvb, TN, preferred_element_type=F32), 0.0)
                dbs_ref[...] += jnp.where(lane == g, jnp.sum(dsg, axis=-1, keepdims=True), 0.0)
            db_ref[...] += jnp.sum(dvn, axis=0, keepdims=True)
            dg_ref[...] += jnp.sum(dvn * nh, axis=0, keepdims=True)
            dnh = dvn * gain
            dgvv = rstd * (dnh - jnp.mean(dnh, axis=-1, keepdims=True) - nh * jnp.mean(dnh * nh, axis=-1, keepdims=True))
            dv_ref[rows, :] = (dgvv * dgv).astype(BF16)

    vec = _full((1, SGU_W))
    return pl.pallas_call(
        body, name=name, grid=(S // tm,),
        in_specs=[_rows(tm, SGU_W, 7), _rows(tm, SGU_W, 8), _rows(tm, SGU_W), vec, vec,
                  _full((SGU_G, SGU_T, SGU_T)), _full((SGU_T, SGU_W))],
        out_specs=(_rows(tm, SGU_W), _rows(tm, SGU_W), _full((SGU_G, SGU_T, SGU_T)), _full((SGU_T, 128)), vec, vec),
        out_shape=(_sds((S, SGU_W), BF16), _sds((S, SGU_W), BF16), _sds((SGU_G, SGU_T, SGU_T)), _sds((SGU_T, 128)),
                   _sds((1, SGU_W)), _sds((1, SGU_W))),
        compiler_params=_cparams("arbitrary"),
    )(z, z, dy, ln_g, ln_b, w_s, bias_full)


def _inproj_backward(x, dxo, dz_att, dga, dca, dcb, dgc, dsu, dsv, dgs, g_pre, mod, w_in_p, name):
    S = x.shape[0]
    tm = ROW_TILE

    def body(x_ref, dxo_ref, p0, p1, p2, p3, p4, p5, p6, p7, g_ref, mod_ref, w_ref,
             dx_ref, hb_ref, dzb_ref, dmod_ref, dg_ref):
        @pl.when(pl.program_id(0) == 0)
        def _():
            dmod_ref[...] = jnp.zeros(dmod_ref.shape, F32)
            dg_ref[...] = jnp.zeros(dg_ref.shape, F32)

        off = 0
        for piece in (p0, p1, p2, p3, p4, p5, p6, p7):
            wdt = piece.shape[1]
            dzb_ref[:, off:off + wdt] = piece[...]
            off += wdt
        dh = lax.dot_general(dzb_ref[...], w_ref[...], NT, preferred_element_type=F32)
        xv = x_ref[...]
        g = g_ref[...]
        one_scale = 1.0 + mod_ref[1:2, :]
        rstd = lax.rsqrt(jnp.mean(xv * xv, axis=-1, keepdims=True) + EPS)
        xhat = xv * rstd
        xg = xhat * g
        hb_ref[...] = (xg * one_scale + mod_ref[0:1, :]).astype(BF16)
        dmod_ref[0:1, :] += jnp.sum(dh, axis=0, keepdims=True)
        dmod_ref[1:2, :] += jnp.sum(dh * xg, axis=0, keepdims=True)
        dhs = dh * one_scale
        dg_ref[...] += jnp.sum(dhs * xhat, axis=0, keepdims=True)
        dxh = dhs * g
        dx_ref[...] = dxo_ref[...] + rstd * (dxh - xhat * jnp.mean(dxh * xhat, axis=-1, keepdims=True))

    widths = (512, 512, 256, 256, 256, 256, 256, 256)
    return pl.pallas_call(
        body, name=name, grid=(S // tm,),
        in_specs=[_rows(tm, D), _rows(tm, D)] + [_rows(tm, w) for w in widths]
                 + [_full((1, D)), _full((3, D)), _full((D, DZ))],
        out_specs=(_rows(tm, D), _rows(tm, D), _rows(tm, DZ), _full((2, D)), _full((1, D))),
        out_shape=(_sds((S, D)), _sds((S, D), BF16), _sds((S, DZ), BF16), _sds((2, D)), _sds((1, D))),
        compiler_params=_cparams("arbitrary"),
    )(x, dxo, dz_att, dga, dca, dcb, dgc, dsu, dsv, dgs, g_pre, mod, w_in_p)


def _adamw(w, gparts, m, v, name):
    shape = w.shape
    cols = shape[-1]
    rows = int(np.prod(shape[:-1]))
    parts = gparts.shape[0]
    w2, m2, v2 = (a.reshape(rows, cols) for a in (w, m, v))
    g3 = gparts.reshape(parts, rows, cols)
    tr = rows
    for cand in (256, 128):
        if rows > cand and rows % cand == 0:
            tr = cand
            break

    def body(w_ref, g_ref, m_ref, v_ref, go_ref, d_ref, mo_ref, vo_ref):
        g = g_ref[0].astype(F32)
        for p in range(1, parts):
            g = g + g_ref[p].astype(F32)
        wv = w_ref[...]
        mn = ADAM_B1 * m_ref[...] + (1.0 - ADAM_B1) * g
        vn = ADAM_B2 * v_ref[...] + (1.0 - ADAM_B2) * (g * g)
        m_hat = mn / (1.0 - ADAM_B1 ** ADAM_STEP)
        v_hat = vn / (1.0 - ADAM_B2 ** ADAM_STEP)
        go_ref[...] = g
        d_ref[...] = -ADAM_LR * (m_hat / (jnp.sqrt(v_hat) + ADAM_EPS) + ADAM_WD * wv)
        mo_ref[...] = mn
        vo_ref[...] = vn

    blk = pl.BlockSpec((tr, cols), lambda i: (i, 0))
    outs = pl.pallas_call(
        body, name=name, grid=(rows // tr,),
        in_specs=[blk, pl.BlockSpec((parts, tr, cols), lambda i: (0, i, 0)), blk, blk],
        out_specs=(blk, blk, blk, blk),
        out_shape=tuple(_sds((rows, cols)) for _ in range(4)),
        compiler_params=_cparams("parallel"),
    )(w2, g3, m2, v2)
    return tuple(o.reshape(shape) for o in outs)


_GATHERED = ("w_in", "w_out", "w_uq", "w_ukv", "w_pw2", "conv_w")
_COL_SHARDED = ("w_in", "w_uq", "w_ukv", "conv_w")

_SMALL = (("dmod", (3 * D,)), ("g_pre", (D,)), ("g_post", (D,)), ("q_norm_g", (Q_RANK,)),
          ("kv_norm_g", (KV_RANK,)), ("conv_b", (CONV_W,)), ("conv_ln_g", (CONV_W,)),
          ("conv_ln_b", (CONV_W,)), ("sgu_ln_g", (SGU_W,)), ("sgu_ln_b", (SGU_W,)),
          ("w_s", (SGU_G, SGU_T, SGU_T)), ("b_s", (SGU_G, SGU_T)))


def _assemble(name, parts):
    if name in _COL_SHARDED:
        p = jnp.moveaxis(parts, 0, 1)
        return p.reshape(p.shape[0], p.shape[1] * p.shape[2])
    return parts.reshape(parts.shape[0] * parts.shape[1], parts.shape[2])


def _scatter_layout(name, full):
    if name in _COL_SHARDED:
        return jnp.moveaxis(full.reshape(full.shape[0], N_DEV, full.shape[1] // N_DEV), 1, 0)
    return full.reshape(N_DEV, full.shape[0] // N_DEV, full.shape[1])


def kernel(x, c, positions, w_ada, b_ada, g_pre, g_post, w_in, q_norm_g, w_uq, kv_norm_g, w_ukv, conv_w, conv_b, conv_ln_g, conv_ln_b, w_pw2, sgu_ln_g, sgu_ln_b, w_s, b_s, w_out, loss_target, m_w_ada, m_b_ada, m_g_pre, m_g_post, m_w_in, m_q_norm_g, m_w_uq, m_kv_norm_g, m_w_ukv, m_conv_w, m_conv_b, m_conv_ln_g, m_conv_ln_b, m_w_pw2, m_sgu_ln_g, m_sgu_ln_b, m_w_s, m_b_s, m_w_out, v_w_ada, v_b_ada, v_g_pre, v_g_post, v_w_in, v_q_norm_g, v_w_uq, v_kv_norm_g, v_w_ukv, v_conv_w, v_conv_b, v_conv_ln_g, v_conv_ln_b, v_w_pw2, v_sgu_ln_g, v_sgu_ln_b, v_w_s, v_b_s, v_w_out):
    weights = dict(w_ada=w_ada, b_ada=b_ada, g_pre=g_pre, g_post=g_post, w_in=w_in, q_norm_g=q_norm_g, w_uq=w_uq,
                   kv_norm_g=kv_norm_g, w_ukv=w_ukv, conv_w=conv_w, conv_b=conv_b, conv_ln_g=conv_ln_g,
                   conv_ln_b=conv_ln_b, w_pw2=w_pw2, sgu_ln_g=sgu_ln_g, sgu_ln_b=sgu_ln_b, w_s=w_s, b_s=b_s, w_out=w_out)
    m_in = dict(w_ada=m_w_ada, b_ada=m_b_ada, g_pre=m_g_pre, g_post=m_g_post, w_in=m_w_in, q_norm_g=m_q_norm_g,
                w_uq=m_w_uq, kv_norm_g=m_kv_norm_g, w_ukv=m_w_ukv, conv_w=m_conv_w, conv_b=m_conv_b,
                conv_ln_g=m_conv_ln_g, conv_ln_b=m_conv_ln_b, w_pw2=m_w_pw2, sgu_ln_g=m_sgu_ln_g,
                sgu_ln_b=m_sgu_ln_b, w_s=m_w_s, b_s=m_b_s, w_out=m_w_out)
    v_in = dict(w_ada=v_w_ada, b_ada=v_b_ada, g_pre=v_g_pre, g_post=v_g_post, w_in=v_w_in, q_norm_g=v_q_norm_g,
                w_uq=v_w_uq, kv_norm_g=v_kv_norm_g, w_ukv=v_w_ukv, conv_w=v_conv_w, conv_b=v_conv_b,
                conv_ln_g=v_conv_ln_g, conv_ln_b=v_conv_ln_b, w_pw2=v_w_pw2, sgu_ln_g=v_sgu_ln_g,
                sgu_ln_b=v_sgu_ln_b, w_s=v_w_s, b_s=v_b_s, w_out=v_w_out)
    order = list(weights)

    S = x.shape[1]
    me = 4 * lax.axis_index("x") + 2 * lax.axis_index("y") + lax.axis_index("c")
    x0 = x.reshape(S, D)
    target = loss_target.reshape(S, D)
    pos = positions.reshape(S, 1)

    def shards(l):
        return [weights[n][l].astype(BF16) for n in _GATHERED]

    def layer_weights(parts):
        full = {n: _assemble(n, p) for n, p in zip(_GATHERED, parts)}
        w_in_f = full["w_in"]
        wq = jnp.pad(full["w_uq"].reshape(Q_RANK, HEADS, QK), ((0, 0), (0, 0), (0, HQ - QK)))
        return dict(
            w_in=jnp.concatenate([w_in_f[:, :ATT_IN], jnp.zeros((D, PAD_IN), BF16), w_in_f[:, ATT_IN:]], axis=1),
            w_uq=wq.reshape(Q_RANK, HEADS * HQ), w_ukv=full["w_ukv"], w_pw2=full["w_pw2"], w_out=full["w_out"],
            conv_w=jnp.pad(full["conv_w"].astype(F32), ((0, HALO - CONV_K), (0, 0))))

    gathered = _exchange(shards(0) + [c.reshape(8, D // 8)], [False] * (len(_GATHERED) + 1), name="gather_weights_0")
    lw = [layer_weights(gathered[:-1])] + [None] * (DEPTH - 1)
    c_all = gathered[-1].reshape(N_DEV, D)

    ada_cols = w_ada.shape[-1]
    b_cols = lax.dynamic_slice_in_dim(b_ada, me * ada_cols, ada_cols, axis=1)
    sc_rows, mod_part = _ada_forward(jnp.pad(c_all, ((0, 8), (0, 0))), w_ada, b_cols)
    mod_recv = _exchange([jnp.moveaxis(mod_part[:, :N_DEV], 1, 0)], [True], name="exchange_mod")[0]
    mod = jnp.moveaxis(mod_recv, 0, 1).reshape(DEPTH, 3, D)

    bias_full = jnp.repeat(jnp.swapaxes(b_s, 1, 2), SGU_GD, axis=2)
    inv_freq = ROPE_THETA ** (-jnp.arange(0, ROPE, 2, dtype=F32) / ROPE)
    zeros32 = jnp.zeros((ROPE // 2,), F32)
    ones32 = jnp.ones((ROPE // 2,), F32)
    rope_rows = jnp.zeros((8, 128), F32)
    rope_rows = rope_rows.at[0].set(jnp.concatenate([inv_freq, inv_freq, zeros32, zeros32]))
    rope_rows = rope_rows.at[1].set(jnp.concatenate([ones32, ones32, zeros32, zeros32]))
    rope_rows = rope_rows.at[2].set(jnp.concatenate([-ones32, ones32, zeros32, zeros32]))

    def vec(a, l):
        return a[l].reshape(1, -1)

    saved = []
    xl = x0
    for l in range(DEPTH):
        w = lw[l]
        z = _prenorm_inproj(xl, vec(g_pre, l), mod[l], w["w_in"], name=f"prenorm_inproj_{l}")
        q, k, v = _att_prep(z, pos, vec(q_norm_g, l), vec(kv_norm_g, l), w["w_uq"], w["w_ukv"], rope_rows,
                            name=f"att_prep_{l}")
        ahead = _Comm(shards(l + 1), [False] * len(_GATHERED)) if l + 1 < DEPTH else None
        y_att, lse, arrived = _flash_forward(q, k, v, name=f"flash_forward_{l}", comm=ahead)
        if ahead is not None:
            lw[l + 1] = layer_weights(arrived)
        cv, y_conv = _conv_forward(z, w["conv_w"], vec(conv_b, l), vec(conv_ln_g, l), vec(conv_ln_b, l),
                                   w["w_pw2"], name=f"conv_forward_{l}")
        y_sgu = _sgu_forward(z, vec(sgu_ln_g, l), vec(sgu_ln_b, l), w_s[l], bias_full[l], name=f"sgu_forward_{l}")
        x_next, y, ycat = _out_proj(xl, z, y_att, y_conv, y_sgu, w["w_out"], vec(g_post, l), mod[l],
                                    name=f"out_proj_{l}")
        saved.append(dict(x=xl, z=z, q=q, k=k, v=v, y_att=y_att, lse=lse, cv=cv, y_conv=y_conv, y_sgu=y_sgu, y=y, ycat=ycat))
        xl = x_next

    loss_part, dx = _loss_head(xl, target)
    loss = lax.psum(loss_part.reshape(()), ("x", "y", "c"))

    spack = _Packer(_SMALL, 8)
    grad_kinds = [True] * len(_GATHERED) + [False]
    received = [None] * DEPTH
    pending = None
    for l in reversed(range(DEPTH)):
        sv = saved[l]
        w = lw[l]
        (dyb, dob, stats, dga, dyc, dgc, dys, dgs, dgate, dgpost) = _out_proj_backward(
            dx, sv["y"], sv["z"], sv["y_att"], sv["y_conv"], sv["y_sgu"], sv["lse"], w["w_out"],
            vec(g_post, l), mod[l], name=f"out_proj_backward_{l}")
        big = dict(w_out=_matmul_tn(sv["ycat"], dyb, name=f"grad_w_out_{l}"))
        riding = _Comm(pending, grad_kinds) if pending is not None else None
        dq, dk, dv, arrived = _flash_backward(sv["q"], sv["k"], sv["v"], dob, stats, name=f"flash_backward_{l}",
                                              comm=riding)
        if riding is not None:
            received[l + 1] = arrived
        dz_att, qn_b, dqp_b, kvn_b, dkv_b, dqg, dkvg = _att_prep_backward(
            sv["z"], pos, dq, dk, dv, vec(q_norm_g, l), vec(kv_norm_g, l), w["w_uq"], w["w_ukv"], rope_rows,
            name=f"att_prep_backward_{l}")
        dwq_p = _matmul_tn(qn_b, dqp_b, name=f"grad_w_uq_{l}")
        big["w_uq"] = dwq_p.reshape(Q_RANK, HEADS, HQ)[:, :, :QK].reshape(Q_RANK, HEADS * QK)
        big["w_ukv"] = _matmul_tn(kvn_b, dkv_b, name=f"grad_w_ukv_{l}")
        dcv, sl_b, dyc_b, dclg, dclb, dcb = _conv_norm_backward(dyc, sv["cv"], vec(conv_ln_g, l), vec(conv_ln_b, l),
                                                              w["w_pw2"], name=f"conv_norm_backward_{l}")
        big["w_pw2"] = _matmul_tn(sl_b, dyc_b, name=f"grad_w_pw2_{l}")
        dca, dcbb, dconvw = _conv_backward(sv["z"], dcv, w["conv_w"], name=f"conv_backward_{l}")
        big["conv_w"] = dconvw[:CONV_K]
        dsu, dsvv, dws, dbs, dslg, dslb = _sgu_backward(sv["z"], dys, vec(sgu_ln_g, l), vec(sgu_ln_b, l), w_s[l],
                                                       bias_full[l], name=f"sgu_backward_{l}")
        dx, h_b, dz_b, dmod2, dgpre = _inproj_backward(
            sv["x"], dx, dz_att, dga, dca, dcbb, dgc, dsu, dsvv, dgs, vec(g_pre, l), mod[l], w["w_in"],
            name=f"inproj_backward_{l}")
        dwin_p = _matmul_tn(h_b, dz_b, name=f"grad_w_in_{l}")
        big["w_in"] = jnp.concatenate([dwin_p[:, :ATT_IN], dwin_p[:, ATT_IN + PAD_IN:]], axis=1)
        small = dict(dmod=jnp.concatenate([dmod2.reshape(-1), dgate.reshape(-1)]), g_pre=dgpre, g_post=dgpost,
                     q_norm_g=dqg, kv_norm_g=dkvg, conv_b=dcb, conv_ln_g=dclg, conv_ln_b=dclb, sgu_ln_g=dslg,
                     sgu_ln_b=dslb, w_s=dws, b_s=jnp.swapaxes(dbs[:, :SGU_G], 0, 1))
        pending = [_scatter_layout(n, big[n]).astype(BF16) for n in _GATHERED] + [spack.pack(small, F32)]
    grad_x = dx.reshape(1, S, D)
    received[0] = _exchange(pending, grad_kinds, name="exchange_grads_0")

    gparts = {n: jnp.stack([received[l][i] for l in range(DEPTH)], axis=1) for i, n in enumerate(_GATHERED)}
    sparts = [spack.unpack(received[l][-1], (N_DEV,)) for l in range(DEPTH)]
    sparts = {n: jnp.stack([sparts[l][n] for l in range(DEPTH)], axis=1) for n, _ in _SMALL}
    dmod_all = sparts["dmod"]
    dmod_cols = lax.dynamic_slice_in_dim(dmod_all, me * ada_cols, ada_cols, axis=2)
    sc_t = jnp.pad(sc_rows[:N_DEV].T, ((0, 0), (0, 128 - N_DEV)))
    dmod_rows = jnp.pad(jnp.moveaxis(dmod_cols, 0, 1), ((0, 0), (0, 128 - N_DEV), (0, 0)))
    gparts["w_ada"] = _ada_backward(sc_t, dmod_rows)[None]
    gparts["b_ada"] = dmod_all
    for n, _ in _SMALL[1:]:
        gparts[n] = sparts[n]

    grads, deltas, new_m, new_v = {}, {}, {}, {}
    for n in order:
        grads[n], deltas[n], new_m[n], new_v[n] = _adamw(weights[n], gparts[n], m_in[n], v_in[n], name=f"adamw_{n}")
    return (loss, grad_x, *[grads[n] for n in order], *[deltas[n] for n in order],
            *[new_m[n] for n in order], *[new_v[n] for n in order])
```

```python
import functools
import math

import numpy as np
import jax
import jax.numpy as jnp
from jax import lax
from jax.experimental import pallas as pl
from jax.experimental.pallas import tpu as pltpu

F32 = jnp.float32
BF16 = jnp.bfloat16

N_DEV = 8
DEPTH = 2
D = 1024
HEADS = 4
NOPE = 128
ROPE = 64
VDIM = 128
QK = NOPE + ROPE
Q_RANK = 256
KV_RANK = 128
ATT_W = HEADS * VDIM
CONV_W = 256
CONV_K = 31
SGU_W = 256
SGU_G = 4
SGU_GD = SGU_W // SGU_G
SGU_T = 128
D_IN = 2496
ATT_IN = Q_RANK + KV_RANK + ROPE
PAD_IN = 64
DZ = D_IN + PAD_IN
HQ = 2 * NOPE
EPS = 1e-6
ROPE_THETA = 10000.0
ATT_SCALE = QK ** -0.5
LOG2E = math.log2(math.e)
EXP2_SCALE = ATT_SCALE * LOG2E
NEG_INF = float("-inf")

ADAM_LR = 0.001
ADAM_B1 = 0.9
ADAM_B2 = 0.999
ADAM_EPS = 1e-08
ADAM_WD = 0.01
ADAM_STEP = 10

VMEM_LIMIT = 56 * 1024 * 1024
ROW_TILE = 512
MATMUL_TN_ROWS = 1024
ATT_TILE = 512
HALO = 32
PACK_LANES = 128

MESH = pl.DeviceIdType.MESH
NT = (((1,), (1,)), ((), ()))
TN = (((0,), (0,)), ((), ()))


def _cparams(*sem):
    return pltpu.CompilerParams(dimension_semantics=sem, vmem_limit_bytes=VMEM_LIMIT)


def _sds(shape, dtype=F32):
    return jax.ShapeDtypeStruct(tuple(shape), dtype)


def _rows(tm, width, col=0):
    return pl.BlockSpec((tm, width), lambda i: (i, col))


def _full(shape):
    nd = len(shape)
    return pl.BlockSpec(tuple(shape), lambda *_: (0,) * nd)


def _sigmoid(x):
    return 1.0 / (1.0 + jnp.exp(-x))


def _silu_and_grad(g):
    s = _sigmoid(g)
    return g * s, s * (1.0 + g * (1.0 - s))


def _gelu_and_grad(x):
    cdf = 0.5 * (1.0 + lax.erf(x * (1.0 / math.sqrt(2.0))))
    pdf = jnp.exp(-0.5 * x * x) * (1.0 / math.sqrt(2.0 * math.pi))
    return x * cdf, cdf + x * pdf


def _swap_halves(a):
    lane = lax.broadcasted_iota(jnp.int32, a.shape, 1)
    up = pltpu.roll(a, 32, 1)
    down = pltpu.roll(a, 96, 1)
    return jnp.where(lane < 32, down, jnp.where(lane < 64, up, 0.0))


def _rope_tables(pos_ref, rope_ref):
    ang = pos_ref[...].astype(F32) * rope_ref[0:1, :]
    return jnp.cos(ang) * rope_ref[1:2, :], jnp.sin(ang) * rope_ref[2:3, :]


class _Comm:
    def __init__(self, srcs, kinds):
        self.srcs = list(srcs)
        self.kinds = list(kinds)
        self.n = len(self.srcs)
        self.out_shape = [_sds((N_DEV,) + tuple(s.shape[1:] if k else s.shape), s.dtype)
                          for s, k in zip(self.srcs, self.kinds)]
        self.specs = [pl.BlockSpec(memory_space=pl.ANY)] * self.n
        self.scratch = [pltpu.SemaphoreType.DMA((self.n, N_DEV - 1)), pltpu.SemaphoreType.DMA((self.n, N_DEV - 1)),
                        pltpu.SemaphoreType.DMA((self.n,))] if self.n else []

    def _copies(self, src_refs, out_refs, sems, with_recvs):
        send_sems, recv_sems, local_sems = sems
        x, y, c = lax.axis_index("x"), lax.axis_index("y"), lax.axis_index("c")
        me = 4 * x + 2 * y + c
        local, sends, recvs = [], [], []
        for a in range(self.n):
            def block_for(dest, src_ref=src_refs[a], a2a=self.kinds[a]):
                return src_ref.at[dest] if a2a else src_ref

            local.append(pltpu.make_async_copy(block_for(me), out_refs[a].at[me], local_sems.at[a]))
            for r in range(1, N_DEV):
                px = 1 - x if (r >> 2) & 1 else x
                py = 1 - y if (r >> 1) & 1 else y
                pc = 1 - c if r & 1 else c
                peer = 4 * px + 2 * py + pc
                sends.append(pltpu.make_async_remote_copy(
                    src_ref=block_for(peer), dst_ref=out_refs[a].at[me],
                    send_sem=send_sems.at[a, r - 1], recv_sem=recv_sems.at[a, r - 1],
                    device_id=(px, py, pc), device_id_type=MESH))
                if with_recvs:
                    recvs.append(pltpu.make_async_remote_copy(
                        src_ref=block_for(me), dst_ref=out_refs[a].at[peer],
                        send_sem=send_sems.at[a, r - 1], recv_sem=recv_sems.at[a, r - 1],
                        device_id=(px, py, pc), device_id_type=MESH))
        return local, sends, recvs

    def start(self, src_refs, out_refs, sems):
        local, sends, _ = self._copies(src_refs, out_refs, sems, False)
        for cp in local + sends:
            cp.start()

    def finish(self, src_refs, out_refs, sems):
        local, sends, recvs = self._copies(src_refs, out_refs, sems, True)
        for cp in recvs:
            cp.wait_recv()
        for cp in sends:
            cp.wait_send()
        for cp in local:
            cp.wait()


def _exchange(srcs, kinds, name):
    comm = _Comm(srcs, kinds)
    n = comm.n

    def body(*refs):
        src_refs, out_refs, sems = refs[:n], refs[n:2 * n], refs[2 * n:]
        comm.start(src_refs, out_refs, sems)
        comm.finish(src_refs, out_refs, sems)

    return pl.pallas_call(
        body, name=name, out_shape=comm.out_shape, in_specs=comm.specs, out_specs=comm.specs,
        scratch_shapes=comm.scratch,
    )(*srcs)


class _Packer:
    def __init__(self, entries, row_multiple):
        self.entries = entries
        self.offsets = {}
        off = 0
        for name, shape in entries:
            self.offsets[name] = off
            off += int(np.prod(shape))
        quantum = PACK_LANES * row_multiple
        self.total = -(-off // quantum) * quantum
        self.used = off
        self.rows = self.total // PACK_LANES

    def pack(self, arrays, dtype, lead=()):
        n = len(lead)
        flat = [arrays[name].astype(dtype).reshape(lead + (-1,)) for name, _ in self.entries]
        flat.append(jnp.zeros(lead + (self.total - self.used,), dtype))
        return jnp.concatenate(flat, axis=n).reshape(lead + (self.rows, PACK_LANES))

    def unpack(self, buf, lead=()):
        flat = buf.reshape(lead + (self.total,))
        out = {}
        for name, shape in self.entries:
            o = self.offsets[name]
            out[name] = lax.slice_in_dim(flat, o, o + int(np.prod(shape)), axis=len(lead)).reshape(lead + tuple(shape))
        return out


def _ada_forward(c_rows, w_ada, b_ada_cols):
    cols = w_ada.shape[-1]
    rows = c_rows.shape[0]

    def body(c_ref, w_ref, b_ref, sc_ref, part_ref):
        cv = c_ref[...]
        sc = cv * _sigmoid(cv)
        sc_ref[...] = sc
        scb = sc.astype(BF16)
        for l in range(DEPTH):
            part_ref[l] = jnp.dot(scb, w_ref[l].astype(BF16), preferred_element_type=F32) + b_ref[l:l + 1, :]

    return pl.pallas_call(
        body, name="ada_forward",
        out_shape=(_sds((rows, D)), _sds((DEPTH, rows, cols))),
        compiler_params=pltpu.CompilerParams(vmem_limit_bytes=VMEM_LIMIT),
    )(c_rows, w_ada, b_ada_cols)


def _ada_backward(sc_t, dmod_cols):
    cols = dmod_cols.shape[-1]

    def body(sc_ref, dm_ref, gw_ref):
        scb = sc_ref[...].astype(BF16)
        for l in range(DEPTH):
            gw_ref[l] = jnp.dot(scb, dm_ref[l].astype(BF16), preferred_element_type=F32)

    return pl.pallas_call(
        body, name="ada_backward",
        out_shape=_sds((DEPTH, D, cols)),
        compiler_params=pltpu.CompilerParams(vmem_limit_bytes=VMEM_LIMIT),
    )(sc_t, dmod_cols)


def _riding(comm, nsteps, c_src, c_out, c_sems, where):
    if not comm.n:
        return
    step = 0 if where == "start" else nsteps - 1

    @pl.when(pl.program_id(0) == step)
    def _():
        (comm.start if where == "start" else comm.finish)(c_src, c_out, c_sems)


def _prenorm_inproj(x, g_pre, mod, w_in_p, name, comm=None):
    S = x.shape[0]
    tm = ROW_TILE
    comm = comm or _Comm([], [])
    nc = comm.n

    def body(*refs):
        x_ref, g_ref, mod_ref, w_ref = refs[:4]
        c_src, z_ref, c_out, c_sems = refs[4:4 + nc], refs[4 + nc], refs[5 + nc:5 + 2 * nc], refs[5 + 2 * nc:]
        _riding(comm, S // tm, c_src, c_out, c_sems, "start")
        xv = x_ref[...]
        rstd = lax.rsqrt(jnp.mean(xv * xv, axis=-1, keepdims=True) + EPS)
        h = (xv * rstd * g_ref[...]) * (1.0 + mod_ref[1:2, :]) + mod_ref[0:1, :]
        z_ref[...] = jnp.dot(h.astype(BF16), w_ref[...], preferred_element_type=F32)
        _riding(comm, S // tm, c_src, c_out, c_sems, "finish")

    outs = pl.pallas_call(
        body, name=name, grid=(S // tm,),
        in_specs=[_rows(tm, D), _full((1, D)), _full((3, D)), _full((D, DZ))] + comm.specs,
        out_specs=[_rows(tm, DZ)] + comm.specs, out_shape=[_sds((S, DZ))] + comm.out_shape,
        scratch_shapes=comm.scratch,
        compiler_params=_cparams("arbitrary"),
    )(x, g_pre, mod, w_in_p, *comm.srcs)
    return outs[0], outs[1:]


def _att_prep(z, pos, q_g, kv_g, wq_p, w_ukv, rope_rows, name):
    S = z.shape[0]
    tm = ROW_TILE

    def body(z_ref, pos_ref, qg_ref, kvg_ref, wq_ref, wkv_ref, rope_ref, q_ref, k_ref, v_ref):
        zz = z_ref[...]
        ql, kvl, ka = zz[:, 0:Q_RANK], zz[:, Q_RANK:Q_RANK + KV_RANK], zz[:, Q_RANK + KV_RANK:]
        qn = ql * lax.rsqrt(jnp.mean(ql * ql, axis=-1, keepdims=True) + EPS) * qg_ref[...]
        kvn = kvl * lax.rsqrt(jnp.mean(kvl * kvl, axis=-1, keepdims=True) + EPS) * kvg_ref[...]
        q = jnp.dot(qn.astype(BF16), wq_ref[...], preferred_element_type=F32)
        kv = jnp.dot(kvn.astype(BF16), wkv_ref[...], preferred_element_type=F32)
        ct, st = _rope_tables(pos_ref, rope_ref)
        krot = (ka * ct + _swap_halves(ka) * st).astype(BF16)
        for h in range(HEADS):
            b = h * HQ
            q_ref[:, b:b + NOPE] = q[:, b:b + NOPE].astype(BF16)
            a = q[:, b + NOPE:b + HQ]
            q_ref[:, b + NOPE:b + HQ] = (a * ct + _swap_halves(a) * st).astype(BF16)
            k_ref[:, b:b + NOPE] = kv[:, b:b + NOPE].astype(BF16)
            k_ref[:, b + NOPE:b + HQ] = krot
            v_ref[:, h * VDIM:(h + 1) * VDIM] = kv[:, b + NOPE:b + HQ].astype(BF16)

    return pl.pallas_call(
        body, name=name, grid=(S // tm,),
        in_specs=[_rows(tm, 512, 0), _rows(tm, 1), _full((1, Q_RANK)), _full((1, KV_RANK)),
                  _full((Q_RANK, HEADS * HQ)), _full((KV_RANK, HEADS * HQ)), _full((8, 128))],
        out_specs=(_rows(tm, HEADS * HQ), _rows(tm, HEADS * HQ), _rows(tm, ATT_W)),
        out_shape=(_sds((S, HEADS * HQ), BF16), _sds((S, HEADS * HQ), BF16), _sds((S, ATT_W), BF16)),
        compiler_params=_cparams("parallel"),
    )(z, pos, q_g, kv_g, wq_p, w_ukv, rope_rows)


def _flash_forward(q, k, v, name, comm=None):
    S = q.shape[0]
    t = ATT_TILE
    tq = 2 * t
    nq = S // tq
    nl = t // 128
    comm = comm or _Comm([], [])
    nc = comm.n

    def body(*refs):
        q_ref, k_ref, v_ref = refs[:3]
        c_src = refs[3:3 + nc]
        o_ref, lse_ref = refs[3 + nc:5 + nc]
        c_out = refs[5 + nc:5 + 2 * nc]
        m_sc, l_sc, acc_sc = refs[5 + 2 * nc:8 + 2 * nc]
        c_sems = refs[8 + 2 * nc:]
        if nc:
            @pl.when((pl.program_id(0) == 0) & (pl.program_id(1) == 0))
            def _():
                comm.start(c_src, c_out, c_sems)

        qb = pl.program_id(1)
        m_sc[...] = jnp.full(m_sc.shape, NEG_INF, F32)
        l_sc[...] = jnp.zeros(l_sc.shape, F32)
        acc_sc[...] = jnp.zeros(acc_sc.shape, F32)

        def scores(hf, rows, diagonal):
            s = lax.dot_general(q_ref[hf * t:(hf + 1) * t, :], k_ref[rows, :], NT, preferred_element_type=F32)
            if diagonal:
                ri = lax.broadcasted_iota(jnp.int32, (t, t), 0)
                ci = lax.broadcasted_iota(jnp.int32, (t, t), 1)
                s = jnp.where(ci <= ri, s, NEG_INF)
            return s

        def max_step(hf, kb, diagonal):
            s = scores(hf, pl.ds(pl.multiple_of(kb * t, t), t), diagonal)
            mp = m_sc[hf]
            for c in range(nl):
                mp = jnp.maximum(mp, s[:, c * 128:(c + 1) * 128])
            m_sc[hf] = mp

        def sum_step(hf, kb, diagonal):
            rows = pl.ds(pl.multiple_of(kb * t, t), t)
            s = scores(hf, rows, diagonal)
            p = jnp.exp2(s * EXP2_SCALE - jnp.tile(m_sc[hf], (1, nl)))
            lp = l_sc[hf]
            for c in range(nl):
                lp = lp + p[:, c * 128:(c + 1) * 128]
            l_sc[hf] = lp
            acc_sc[hf] += jnp.dot(p.astype(BF16), v_ref[rows, :], preferred_element_type=F32)

        def sweep(step):
            def loop_body(kb, carry):
                step(0, kb, False)
                step(1, kb, False)
                return carry

            lax.fori_loop(0, 2 * qb, loop_body, 0)
            step(0, 2 * qb, True)
            step(1, 2 * qb, False)
            step(1, 2 * qb + 1, True)

        sweep(max_step)
        for hf in range(2):
            m = jnp.max(m_sc[hf], axis=-1, keepdims=True)
            m_sc[hf] = jnp.broadcast_to(m * EXP2_SCALE, (t, 128))
        sweep(sum_step)
        for hf in range(2):
            l = jnp.sum(l_sc[hf], axis=-1, keepdims=True)
            o_ref[hf * t:(hf + 1) * t, :] = acc_sc[hf] / l
            m_scaled = jnp.max(m_sc[hf], axis=-1, keepdims=True) * (1.0 / LOG2E)
            lse_ref[0, hf * t:(hf + 1) * t, :] = m_scaled + jnp.log(l)

        if nc:
            @pl.when((pl.program_id(0) == HEADS - 1) & (pl.program_id(1) == nq - 1))
            def _():
                comm.finish(c_src, c_out, c_sems)

    outs = pl.pallas_call(
        body, name=name, grid=(HEADS, nq),
        in_specs=[pl.BlockSpec((tq, HQ), lambda h, i: (i, h)),
                  pl.BlockSpec((S, HQ), lambda h, i: (0, h)),
                  pl.BlockSpec((S, VDIM), lambda h, i: (0, h))] + comm.specs,
        out_specs=[pl.BlockSpec((tq, VDIM), lambda h, i: (i, h)),
                   pl.BlockSpec((1, tq, 1), lambda h, i: (h, i, 0))] + comm.specs,
        out_shape=[_sds((S, ATT_W)), _sds((HEADS, S, 1))] + comm.out_shape,
        scratch_shapes=[pltpu.VMEM((2, t, 128), F32), pltpu.VMEM((2, t, 128), F32), pltpu.VMEM((2, t, VDIM), F32)]
                       + comm.scratch,
        compiler_params=_cparams("arbitrary", "arbitrary"),
    )(q, k, v, *comm.srcs)
    return outs[0], outs[1], outs[2:]


def _conv_window(win_ref, a_prev, b_prev, a_cur, b_cur, first):
    hp = a_prev * _sigmoid(b_prev)
    win_ref[0:HALO, :] = jnp.where(first, 0.0, hp)
    win_ref[HALO:, :] = a_cur * _sigmoid(b_cur)


def _shifted_copies(win_ref, sh_ref, tm):
    for b in range(1, 8):
        sh_ref[b - 1] = win_ref[pl.ds(b, tm + HALO - 8), :]


def _tap(win_ref, sh_ref, offset, tm):
    a, b = divmod(offset, 8)
    if b == 0:
        return win_ref[pl.ds(8 * a, tm), :]
    return sh_ref[b - 1, pl.ds(8 * a, tm), :]


def _conv_in_specs(tm):
    per = tm // HALO
    prev = lambda col: pl.BlockSpec((HALO, CONV_W), lambda i: (jnp.maximum(i * per - 1, 0), col))
    return [_rows(tm, CONV_W, 4), _rows(tm, CONV_W, 5), prev(4), prev(5)]


def _conv_forward(z, conv_w_p, conv_b, ln_g, ln_b, w_pw2, name):
    S = z.shape[0]
    tm = ROW_TILE

    def body(a_ref, b_ref, ap_ref, bp_ref, w_ref, cb_ref, g_ref, be_ref, pw_ref, cv_ref, y_ref, win, sh):
        _conv_window(win, ap_ref[...], bp_ref[...], a_ref[...], b_ref[...], pl.program_id(0) == 0)
        _shifted_copies(win, sh, tm)
        acc = jnp.zeros((tm, CONV_W), F32)
        for kk in range(CONV_K):
            acc = acc + w_ref[kk:kk + 1, :] * _tap(win, sh, HALO - (CONV_K - 1) + kk, tm)
        cv = acc + cb_ref[...]
        cv_ref[...] = cv
        mu = jnp.mean(cv, axis=-1, keepdims=True)
        cc = cv - mu
        rstd = lax.rsqrt(jnp.mean(cc * cc, axis=-1, keepdims=True) + EPS)
        n = cc * rstd * g_ref[...] + be_ref[...]
        sl = n * _sigmoid(n)
        y_ref[...] = jnp.dot(sl.astype(BF16), pw_ref[...], preferred_element_type=F32)

    return pl.pallas_call(
        body, name=name, grid=(S // tm,),
        in_specs=_conv_in_specs(tm) + [_full((HALO, CONV_W)), _full((1, CONV_W)), _full((1, CONV_W)),
                                       _full((1, CONV_W)), _full((CONV_W, CONV_W))],
        out_specs=(_rows(tm, CONV_W), _rows(tm, CONV_W)),
        out_shape=(_sds((S, CONV_W)), _sds((S, CONV_W))),
        scratch_shapes=[pltpu.VMEM((tm + HALO, CONV_W), F32), pltpu.VMEM((7, tm + HALO - 8, CONV_W), F32)],
        compiler_params=_cparams("parallel"),
    )(z, z, z, z, conv_w_p, conv_b, ln_g, ln_b, w_pw2)


def _sgu_common(u, v, g_ref, be_ref):
    gu, dgu = _gelu_and_grad(u)
    gv, dgv = _gelu_and_grad(v)
    mu = jnp.mean(gv, axis=-1, keepdims=True)
    cc = gv - mu
    rstd = lax.rsqrt(jnp.mean(cc * cc, axis=-1, keepdims=True) + EPS)
    nh = cc * rstd
    vn = nh * g_ref[...] + be_ref[...]
    return gu, dgu, dgv, rstd, nh, vn


def _sgu_masks():
    lane_group = lax.broadcasted_iota(jnp.int32, (1, SGU_W), 1) // SGU_GD
    ri = lax.broadcasted_iota(jnp.int32, (SGU_T, SGU_T), 0)
    ci = lax.broadcasted_iota(jnp.int32, (SGU_T, SGU_T), 1)
    return [lane_group == g for g in range(SGU_G)], ci <= ri


def _sgu_forward(z, ln_g, ln_b, w_s, bias_full, name):
    S = z.shape[0]
    tm = ROW_TILE

    def body(u_ref, v_ref, g_ref, be_ref, ws_ref, bias_ref, y_ref):
        gmask, tril = _sgu_masks()
        wm = [jnp.where(tril, ws_ref[g], 0.0).astype(BF16) for g in range(SGU_G)]
        for ch in range(tm // SGU_T):
            rows = slice(ch * SGU_T, (ch + 1) * SGU_T)
            gu, _, _, _, _, vn = _sgu_common(u_ref[rows, :], v_ref[rows, :], g_ref, be_ref)
            vb = vn.astype(BF16)
            sv = bias_ref[...]
            for g in range(SGU_G):
                sv = sv + jnp.where(gmask[g], jnp.dot(wm[g], vb, preferred_element_type=F32), 0.0)
            y_ref[rows, :] = gu * sv

    return pl.pallas_call(
        body, name=name, grid=(S // tm,),
        in_specs=[_rows(tm, SGU_W, 7), _rows(tm, SGU_W, 8), _full((1, SGU_W)), _full((1, SGU_W)),
                  _full((SGU_G, SGU_T, SGU_T)), _full((SGU_T, SGU_W))],
        out_specs=_rows(tm, SGU_W), out_shape=_sds((S, SGU_W)),
        compiler_params=_cparams("parallel"),
    )(z, z, ln_g, ln_b, w_s, bias_full)


def _out_proj(x, z, y_att, y_conv, y_sgu, w_out, g_post, mod, name):
    S = x.shape[0]
    tm = ROW_TILE

    def body(x_ref, ga_ref, gc_ref, gs_ref, ya_ref, yc_ref, ys_ref, w_ref, gp_ref, mod_ref, xn_ref, y_ref, cat_ref):
        ca = (ya_ref[...] * _silu_and_grad(ga_ref[...])[0]).astype(BF16)
        cc = (yc_ref[...] * _silu_and_grad(gc_ref[...])[0]).astype(BF16)
        cs = (ys_ref[...] * _silu_and_grad(gs_ref[...])[0]).astype(BF16)
        cat_ref[:, 0:ATT_W] = ca
        cat_ref[:, ATT_W:ATT_W + CONV_W] = cc
        cat_ref[:, ATT_W + CONV_W:] = cs
        y = (jnp.dot(ca, w_ref[0:ATT_W, :], preferred_element_type=F32)
             + jnp.dot(cc, w_ref[ATT_W:ATT_W + CONV_W, :], preferred_element_type=F32)
             + jnp.dot(cs, w_ref[ATT_W + CONV_W:, :], preferred_element_type=F32))
        y_ref[...] = y
        rstd = lax.rsqrt(jnp.mean(y * y, axis=-1, keepdims=True) + EPS)
        xn_ref[...] = x_ref[...] + mod_ref[2:3, :] * (y * rstd * gp_ref[...])

    return pl.pallas_call(
        body, name=name, grid=(S // tm,),
        in_specs=[_rows(tm, D), _rows(tm, 512, 1), _rows(tm, 256, 6), _rows(tm, 256, 9),
                  _rows(tm, ATT_W), _rows(tm, CONV_W), _rows(tm, SGU_W),
                  _full((D, D)), _full((1, D)), _full((3, D))],
        out_specs=(_rows(tm, D), _rows(tm, D), _rows(tm, D)),
        out_shape=(_sds((S, D)), _sds((S, D)), _sds((S, D), BF16)),
        compiler_params=_cparams("parallel"),
    )(x, z, z, z, y_att, y_conv, y_sgu, w_out, g_post, mod)


def _loss_head(y, target):
    S = y.shape[0]
    tm = ROW_TILE

    def body(y_ref, t_ref, loss_ref, dy_ref):
        @pl.when(pl.program_id(0) == 0)
        def _():
            loss_ref[...] = jnp.zeros(loss_ref.shape, F32)

        err = y_ref[...] - t_ref[...]
        dy_ref[...] = err * (1.0 / D)
        row = jnp.sum(err * err, axis=-1, keepdims=True) * (1.0 / D)
        loss_ref[...] += 0.5 * jnp.sum(row, axis=0, keepdims=True)

    return pl.pallas_call(
        body, name="loss_head", grid=(S // tm,),
        in_specs=[_rows(tm, D), _rows(tm, D)],
        out_specs=(_full((1, 1)), _rows(tm, D)),
        out_shape=(_sds((1, 1)), _sds((S, D))),
        compiler_params=_cparams("arbitrary"),
    )(y, target)


def _matmul_tn(a, b, name, comm=None):
    S, M = a.shape
    N = b.shape[1]
    bk = min(MATMUL_TN_ROWS, S)
    riding = comm is not None
    comm = comm or _Comm([], [])
    nc = comm.n

    def body(*refs):
        a_ref, b_ref = refs[:2]
        c_src, o_ref, c_out, c_sems = refs[2:2 + nc], refs[2 + nc], refs[3 + nc:3 + 2 * nc], refs[3 + 2 * nc:]
        _riding(comm, S // bk, c_src, c_out, c_sems, "start")

        @pl.when(pl.program_id(0) == 0)
        def _():
            o_ref[...] = jnp.zeros(o_ref.shape, F32)

        o_ref[...] += lax.dot_general(a_ref[...], b_ref[...], TN, preferred_element_type=F32)
        _riding(comm, S // bk, c_src, c_out, c_sems, "finish")

    outs = pl.pallas_call(
        body, name=name, grid=(S // bk,),
        in_specs=[pl.BlockSpec((bk, M), lambda k: (k, 0)), pl.BlockSpec((bk, N), lambda k: (k, 0))] + comm.specs,
        out_specs=[_full((M, N))] + comm.specs, out_shape=[_sds((M, N))] + comm.out_shape,
        scratch_shapes=comm.scratch,
        compiler_params=_cparams("arbitrary"),
    )(a, b, *comm.srcs)
    return (outs[0], outs[1:]) if riding else outs[0]


def _out_proj_backward(dxo, y, z, y_att, y_conv, y_sgu, lse, w_out, g_post, mod, name):
    S = dxo.shape[0]
    tm = ROW_TILE

    def body(dxo_ref, y_ref, ga_ref, gc_ref, gs_ref, ya_ref, yc_ref, ys_ref, lse_ref, w_ref, gp_ref, mod_ref,
             dyb_ref, dob_ref, st_ref, dga_ref, dyc_ref, dgc_ref, dys_ref, dgs_ref, dgate_ref, dgp_ref):
        @pl.when(pl.program_id(0) == 0)
        def _():
            dgate_ref[...] = jnp.zeros(dgate_ref.shape, F32)
            dgp_ref[...] = jnp.zeros(dgp_ref.shape, F32)

        dxo_v = dxo_ref[...]
        yv = y_ref[...]
        gp = gp_ref[...]
        rstd = lax.rsqrt(jnp.mean(yv * yv, axis=-1, keepdims=True) + EPS)
        yhat = yv * rstd
        dgate_ref[...] += jnp.sum(dxo_v * (yhat * gp), axis=0, keepdims=True)
        dr = dxo_v * mod_ref[2:3, :]
        dgp_ref[...] += jnp.sum(dr * yhat, axis=0, keepdims=True)
        dyh = dr * gp
        dy = rstd * (dyh - yhat * jnp.mean(dyh * yhat, axis=-1, keepdims=True))
        dyb = dy.astype(BF16)
        dyb_ref[...] = dyb
        dcat = lax.dot_general(dyb, w_ref[...], NT, preferred_element_type=F32)

        ya = ya_ref[...]
        sil, dsil = _silu_and_grad(ga_ref[...])
        da = dcat[:, 0:ATT_W]
        do = da * sil
        dob_ref[...] = do.astype(BF16)
        dga_ref[...] = (da * ya * dsil).astype(BF16)
        lane = lax.broadcasted_iota(jnp.int32, (1, 128), 1)
        stats = jnp.zeros((tm, 128), F32)
        for h in range(HEADS):
            cols = slice(h * VDIM, (h + 1) * VDIM)
            delta = jnp.sum(do[:, cols] * ya[:, cols], axis=-1, keepdims=True)
            stats = stats + jnp.where(lane == 2 * h, lse_ref[h], 0.0) + jnp.where(lane == 2 * h + 1, delta, 0.0)
        st_ref[...] = stats

        sil, dsil = _silu_and_grad(gc_ref[...])
        dc = dcat[:, ATT_W:ATT_W + CONV_W]
        dyc_ref[...] = dc * sil
        dgc_ref[...] = (dc * yc_ref[...] * dsil).astype(BF16)
        sil, dsil = _silu_and_grad(gs_ref[...])
        dsg = dcat[:, ATT_W + CONV_W:]
        dys_ref[...] = dsg * sil
        dgs_ref[...] = (dsg * ys_ref[...] * dsil).astype(BF16)

    return pl.pallas_call(
        body, name=name, grid=(S // tm,),
        in_specs=[_rows(tm, D), _rows(tm, D), _rows(tm, 512, 1), _rows(tm, 256, 6), _rows(tm, 256, 9),
                  _rows(tm, ATT_W), _rows(tm, CONV_W), _rows(tm, SGU_W),
                  pl.BlockSpec((HEADS, tm, 1), lambda i: (0, i, 0)),
                  _full((D, D)), _full((1, D)), _full((3, D))],
        out_specs=(_rows(tm, D), _rows(tm, ATT_W), _rows(tm, 128), _rows(tm, ATT_W),
                   _rows(tm, CONV_W), _rows(tm, CONV_W), _rows(tm, SGU_W), _rows(tm, SGU_W),
                   _full((1, D)), _full((1, D))),
        out_shape=(_sds((S, D), BF16), _sds((S, ATT_W), BF16), _sds((S, 128)), _sds((S, ATT_W), BF16),
                   _sds((S, CONV_W)), _sds((S, CONV_W), BF16), _sds((S, SGU_W)), _sds((S, SGU_W), BF16),
                   _sds((1, D)), _sds((1, D))),
        compiler_params=_cparams("arbitrary"),
    )(dxo, y, z, z, z, y_att, y_conv, y_sgu, lse, w_out, g_post, mod)


def _flash_backward(q, k, v, do, stats, name, comm=None):
    S = q.shape[0]
    t = ATT_TILE
    tk = 2 * t
    nq = S // t
    comm = comm or _Comm([], [])
    nc = comm.n

    def body(*refs):
        q_ref, do_ref, st_ref, k_ref, v_ref = refs[:5]
        c_src = refs[5:5 + nc]
        dq_ref, dk_ref, dv_ref = refs[5 + nc:8 + nc]
        c_out = refs[8 + nc:8 + 2 * nc]
        dk_sc, dv_sc = refs[8 + 2 * nc:10 + 2 * nc]
        c_sems = refs[10 + 2 * nc:]
        h = pl.program_id(0)
        j = pl.program_id(1)
        if nc:
            @pl.when((h == 0) & (j == 0))
            def _():
                comm.start(c_src, c_out, c_sems)

        @pl.when(j == 0)
        def _():
            dq_ref[...] = jnp.zeros(dq_ref.shape, F32)

        dk_sc[...] = jnp.zeros(dk_sc.shape, F32)
        dv_sc[...] = jnp.zeros(dv_sc.shape, F32)
        lane = lax.broadcasted_iota(jnp.int32, (1, 128), 1)

        def chain(hf, qv, dov, lse2, delta, diagonal):
            kt = k_ref[hf * t:(hf + 1) * t, :]
            s = lax.dot_general(qv, kt, NT, preferred_element_type=F32)
            p = jnp.exp2(s * EXP2_SCALE - lse2)
            if diagonal:
                ri = lax.broadcasted_iota(jnp.int32, (t, t), 0)
                ci = lax.broadcasted_iota(jnp.int32, (t, t), 1)
                p = jnp.where(ci <= ri, p, 0.0)
            dv_sc[hf] += lax.dot_general(p.astype(BF16), dov, TN, preferred_element_type=F32)
            dp = lax.dot_general(dov, v_ref[hf * t:(hf + 1) * t, :], NT, preferred_element_type=F32)
            ds = (p * (dp - delta) * ATT_SCALE).astype(BF16)
            dk_sc[hf] += lax.dot_general(ds, qv, TN, preferred_element_type=F32)
            return jnp.dot(ds, kt, preferred_element_type=F32)

        def q_tile(qb, modes):
            rows = pl.ds(pl.multiple_of(qb * t, t), t)
            qv = q_ref[rows, :]
            dov = do_ref[rows, :]
            st = st_ref[rows, :]
            lse2 = jnp.sum(jnp.where(lane == 2 * h, st, 0.0), axis=-1, keepdims=True) * LOG2E
            delta = jnp.sum(jnp.where(lane == 2 * h + 1, st, 0.0), axis=-1, keepdims=True)
            parts = [chain(hf, qv, dov, lse2, delta, modes[hf]) for hf in range(2) if modes[hf] is not None]
            dq_ref[rows, :] += parts[0] if len(parts) == 1 else parts[0] + parts[1]

        q_tile(2 * j, (True, None))
        q_tile(2 * j + 1, (False, True))

        def loop_body(qb, carry):
            q_tile(qb, (False, False))
            return carry

        lax.fori_loop(2 * j + 2, nq, loop_body, 0)
        for hf in range(2):
            dk_ref[hf * t:(hf + 1) * t, :] = dk_sc[hf]
            dv_ref[hf * t:(hf + 1) * t, :] = dv_sc[hf]

        if nc:
            @pl.when((h == HEADS - 1) & (j == S // tk - 1))
            def _():
                comm.finish(c_src, c_out, c_sems)

    outs = pl.pallas_call(
        body, name=name, grid=(HEADS, S // tk),
        in_specs=[pl.BlockSpec((S, HQ), lambda h, j: (0, h)),
                  pl.BlockSpec((S, VDIM), lambda h, j: (0, h)),
                  pl.BlockSpec((S, 128), lambda h, j: (0, 0)),
                  pl.BlockSpec((tk, HQ), lambda h, j: (j, h)),
                  pl.BlockSpec((tk, VDIM), lambda h, j: (j, h))] + comm.specs,
        out_specs=[pl.BlockSpec((S, HQ), lambda h, j: (0, h)),
                   pl.BlockSpec((tk, HQ), lambda h, j: (j, h)),
                   pl.BlockSpec((tk, VDIM), lambda h, j: (j, h))] + comm.specs,
        out_shape=[_sds((S, HEADS * HQ)), _sds((S, HEADS * HQ)), _sds((S, ATT_W))] + comm.out_shape,
        scratch_shapes=[pltpu.VMEM((2, t, HQ), F32), pltpu.VMEM((2, t, VDIM), F32)] + comm.scratch,
        compiler_params=_cparams("arbitrary", "arbitrary"),
    )(q, do, stats, k, v, *comm.srcs)
    return outs[0], outs[1], outs[2], outs[3:]


def _att_prep_backward(z, pos, dq, dk, dv, q_g, kv_g, wq_p, w_ukv, rope_rows, name):
    S = z.shape[0]
    tm = ROW_TILE

    def body(z_ref, pos_ref, dq_ref, dk_ref, dv_ref, qg_ref, kvg_ref, wq_ref, wkv_ref, rope_ref,
             dz_ref, qn_ref, dqp_ref, kvn_ref, dkv_ref, dqg_ref, dkvg_ref):
        @pl.when(pl.program_id(0) == 0)
        def _():
            dqg_ref[...] = jnp.zeros(dqg_ref.shape, F32)
            dkvg_ref[...] = jnp.zeros(dkvg_ref.shape, F32)

        zz = z_ref[...]
        ql, kvl = zz[:, 0:Q_RANK], zz[:, Q_RANK:Q_RANK + KV_RANK]
        q_rstd = lax.rsqrt(jnp.mean(ql * ql, axis=-1, keepdims=True) + EPS)
        kv_rstd = lax.rsqrt(jnp.mean(kvl * kvl, axis=-1, keepdims=True) + EPS)
        qhat, kvhat = ql * q_rstd, kvl * kv_rstd
        qg, kvg = qg_ref[...], kvg_ref[...]
        qn_ref[...] = (qhat * qg).astype(BF16)
        kvn_ref[...] = (kvhat * kvg).astype(BF16)
        ct, st = _rope_tables(pos_ref, rope_ref)

        def unrotate(d):
            return d * ct + _swap_halves(d * st)

        dkrot = jnp.zeros((tm, NOPE), F32)
        for h in range(HEADS):
            b = h * HQ
            dqp_ref[:, b:b + NOPE] = dq_ref[:, b:b + NOPE].astype(BF16)
            dqp_ref[:, b + NOPE:b + HQ] = unrotate(dq_ref[:, b + NOPE:b + HQ]).astype(BF16)
            dkv_ref[:, b:b + NOPE] = dk_ref[:, b:b + NOPE].astype(BF16)
            dkv_ref[:, b + NOPE:b + HQ] = dv_ref[:, h * VDIM:(h + 1) * VDIM].astype(BF16)
            dkrot = dkrot + dk_ref[:, b + NOPE:b + HQ]
        dqn = lax.dot_general(dqp_ref[...], wq_ref[...], NT, preferred_element_type=F32)
        dkvn = lax.dot_general(dkv_ref[...], wkv_ref[...], NT, preferred_element_type=F32)
        dqg_ref[...] += jnp.sum(dqn * qhat, axis=0, keepdims=True)
        dkvg_ref[...] += jnp.sum(dkvn * kvhat, axis=0, keepdims=True)
        dqh, dkvh = dqn * qg, dkvn * kvg
        dql = q_rstd * (dqh - qhat * jnp.mean(dqh * qhat, axis=-1, keepdims=True))
        dkvl = kv_rstd * (dkvh - kvhat * jnp.mean(dkvh * kvhat, axis=-1, keepdims=True))
        dz_ref[:, 0:Q_RANK] = dql.astype(BF16)
        dz_ref[:, Q_RANK:Q_RANK + KV_RANK] = dkvl.astype(BF16)
        dz_ref[:, Q_RANK + KV_RANK:] = unrotate(dkrot).astype(BF16)

    W = HEADS * HQ
    return pl.pallas_call(
        body, name=name, grid=(S // tm,),
        in_specs=[_rows(tm, 512, 0), _rows(tm, 1), _rows(tm, W), _rows(tm, W), _rows(tm, ATT_W),
                  _full((1, Q_RANK)), _full((1, KV_RANK)), _full((Q_RANK, W)), _full((KV_RANK, W)), _full((8, 128))],
        out_specs=(_rows(tm, 512), _rows(tm, Q_RANK), _rows(tm, W), _rows(tm, KV_RANK), _rows(tm, W),
                   _full((1, Q_RANK)), _full((1, KV_RANK))),
        out_shape=(_sds((S, 512), BF16), _sds((S, Q_RANK), BF16), _sds((S, W), BF16), _sds((S, KV_RANK), BF16),
                   _sds((S, W), BF16), _sds((1, Q_RANK)), _sds((1, KV_RANK))),
        compiler_params=_cparams("arbitrary"),
    )(z, pos, dq, dk, dv, q_g, kv_g, wq_p, w_ukv, rope_rows)


def _conv_norm_backward(dyc, cv, ln_g, ln_b, w_pw2, name):
    S = cv.shape[0]
    tm = ROW_TILE

    def body(dy_ref, cv_ref, g_ref, be_ref, pw_ref, dcv_ref, sl_ref, dyb_ref, dg_ref, db_ref, dcb_ref):
        @pl.when(pl.program_id(0) == 0)
        def _():
            dg_ref[...] = jnp.zeros(dg_ref.shape, F32)
            db_ref[...] = jnp.zeros(db_ref.shape, F32)
            dcb_ref[...] = jnp.zeros(dcb_ref.shape, F32)

        cv_v = cv_ref[...]
        mu = jnp.mean(cv_v, axis=-1, keepdims=True)
        cc = cv_v - mu
        rstd = lax.rsqrt(jnp.mean(cc * cc, axis=-1, keepdims=True) + EPS)
        nh = cc * rstd
        g = g_ref[...]
        n = nh * g + be_ref[...]
        sil, dsil = _silu_and_grad(n)
        sl_ref[...] = sil.astype(BF16)
        dyb = dy_ref[...].astype(BF16)
        dyb_ref[...] = dyb
        dn = lax.dot_general(dyb, pw_ref[...], NT, preferred_element_type=F32) * dsil
        db_ref[...] += jnp.sum(dn, axis=0, keepdims=True)
        dg_ref[...] += jnp.sum(dn * nh, axis=0, keepdims=True)
        dnh = dn * g
        dcv = rstd * (dnh - jnp.mean(dnh, axis=-1, keepdims=True) - nh * jnp.mean(dnh * nh, axis=-1, keepdims=True))
        dcv_ref[...] = dcv
        dcb_ref[...] += jnp.sum(dcv, axis=0, keepdims=True)

    vec = _full((1, CONV_W))
    return pl.pallas_call(
        body, name=name, grid=(S // tm,),
        in_specs=[_rows(tm, CONV_W), _rows(tm, CONV_W), vec, vec, _full((CONV_W, CONV_W))],
        out_specs=(_rows(tm, CONV_W), _rows(tm, CONV_W), _rows(tm, CONV_W), vec, vec, vec),
        out_shape=(_sds((S, CONV_W)), _sds((S, CONV_W), BF16), _sds((S, CONV_W), BF16),
                   _sds((1, CONV_W)), _sds((1, CONV_W)), _sds((1, CONV_W))),
        compiler_params=_cparams("arbitrary"),
    )(dyc, cv, ln_g, ln_b, w_pw2)


def _conv_backward(z, dcv, conv_w_p, name):
    S = z.shape[0]
    tm = ROW_TILE
    per = tm // HALO
    last_halo = S // HALO - 1

    def body(a_ref, b_ref, ap_ref, bp_ref, d_ref, dn_ref, w_ref, da_ref, db_ref, dw_ref, win, dwin, dw_acc, sh, dsh):
        i = pl.program_id(0)

        @pl.when(i == 0)
        def _():
            dw_acc[...] = jnp.zeros(dw_acc.shape, F32)

        av, bv = a_ref[...], b_ref[...]
        _conv_window(win, ap_ref[...], bp_ref[...], av, bv, i == 0)
        dcur = d_ref[...]
        dwin[0:tm, :] = dcur
        dwin[tm:, :] = jnp.where(i == pl.num_programs(0) - 1, 0.0, dn_ref[...])
        _shifted_copies(win, sh, tm)
        _shifted_copies(dwin, dsh, tm)
        dh = jnp.zeros((tm, CONV_W), F32)
        for kk in range(CONV_K):
            dh = dh + w_ref[kk:kk + 1, :] * _tap(dwin, dsh, CONV_K - 1 - kk, tm)
            prod = dcur * _tap(win, sh, HALO - (CONV_K - 1) + kk, tm)
            dw_acc[kk] += jnp.sum(prod.reshape(tm // 8, 8, CONV_W), axis=0)
        sb = _sigmoid(bv)
        da_ref[...] = (dh * sb).astype(BF16)
        db_ref[...] = (dh * av * sb * (1.0 - sb)).astype(BF16)

        @pl.when(i == pl.num_programs(0) - 1)
        def _():
            dw_ref[...] = jnp.sum(dw_acc[...], axis=1)

    nxt = pl.BlockSpec((HALO, CONV_W), lambda i: (jnp.minimum((i + 1) * per, last_halo), 0))
    return pl.pallas_call(
        body, name=name, grid=(S // tm,),
        in_specs=_conv_in_specs(tm) + [_rows(tm, CONV_W), nxt, _full((HALO, CONV_W))],
        out_specs=(_rows(tm, CONV_W), _rows(tm, CONV_W), _full((HALO, CONV_W))),
        out_shape=(_sds((S, CONV_W), BF16), _sds((S, CONV_W), BF16), _sds((HALO, CONV_W))),
        scratch_shapes=[pltpu.VMEM((tm + HALO, CONV_W), F32), pltpu.VMEM((tm + HALO, CONV_W), F32),
                        pltpu.VMEM((HALO, 8, CONV_W), F32), pltpu.VMEM((7, tm + HALO - 8, CONV_W), F32),
                        pltpu.VMEM((7, tm + HALO - 8, CONV_W), F32)],
        compiler_params=_cparams("arbitrary"),
    )(z, z, z, z, dcv, dcv, conv_w_p)


def _sgu_backward(z, dy, ln_g, ln_b, w_s, bias_full, name):
    S = z.shape[0]
    tm = ROW_TILE

    def body(u_ref, v_ref, dy_ref, g_ref, be_ref, ws_ref, bias_ref, du_ref, dv_ref, dws_ref, dbs_ref, dg_ref, db_ref):
        @pl.when(pl.program_id(0) == 0)
        def _():
            dws_ref[...] = jnp.zeros(dws_ref.shape, F32)
            dbs_ref[...] = jnp.zeros(dbs_ref.shape, F32)
            dg_ref[...] = jnp.zeros(dg_ref.shape, F32)
            db_ref[...] = jnp.zeros(db_ref.shape, F32)

        gmask, tril = _sgu_masks()
        lane = lax.broadcasted_iota(jnp.int32, (1, 128), 1)
        wm = [jnp.where(tril, ws_ref[g], 0.0).astype(BF16) for g in range(SGU_G)]
        gain = g_ref[...]
        for ch in range(tm // SGU_T):
            rows = slice(ch * SGU_T, (ch + 1) * SGU_T)
            gu, dgu, dgv, rstd, nh, vn = _sgu_common(u_ref[rows, :], v_ref[rows, :], g_ref, be_ref)
            vb = vn.astype(BF16)
            sv = bias_ref[...]
            for g in range(SGU_G):
                sv = sv + jnp.where(gmask[g], jnp.dot(wm[g], vb, preferred_element_type=F32), 0.0)
            dyv = dy_ref[rows, :]
            du_ref[rows, :] = (dyv * sv * dgu).astype(BF16)
            dsv = dyv * gu
            dsvb = dsv.astype(BF16)
            dvn = jnp.zeros((SGU_T, SGU_W), F32)
            for g in range(SGU_G):
                dsg = jnp.where(gmask[g], dsv, 0.0)
                dwg = lax.dot_general(dsg.astype(BF16), vb, NT, preferred_element_type=F32)
                dws_ref[g] += jnp.where(tril, dwg, 0.0)
                dvn = dvn + jnp.where(gmask[g], lax.dot_general(wm[g], dsvb, TN, preferred_element_type=F32), 0.0)
                dbs_ref[...] += jnp.where(lane == g, jnp.sum(dsg, axis=-1, keepdims=True), 0.0)
            db_ref[...] += jnp.sum(dvn, axis=0, keepdims=True)
            dg_ref[...] += jnp.sum(dvn * nh, axis=0, keepdims=True)
            dnh = dvn * gain
            dgvv = rstd * (dnh - jnp.mean(dnh, axis=-1, keepdims=True) - nh * jnp.mean(dnh * nh, axis=-1, keepdims=True))
            dv_ref[rows, :] = (dgvv * dgv).astype(BF16)

    vec = _full((1, SGU_W))
    return pl.pallas_call(
        body, name=name, grid=(S // tm,),
        in_specs=[_rows(tm, SGU_W, 7), _rows(tm, SGU_W, 8), _rows(tm, SGU_W), vec, vec,
                  _full((SGU_G, SGU_T, SGU_T)), _full((SGU_T, SGU_W))],
        out_specs=(_rows(tm, SGU_W), _rows(tm, SGU_W), _full((SGU_G, SGU_T, SGU_T)), _full((SGU_T, 128)), vec, vec),
        out_shape=(_sds((S, SGU_W), BF16), _sds((S, SGU_W), BF16), _sds((SGU_G, SGU_T, SGU_T)), _sds((SGU_T, 128)),
                   _sds((1, SGU_W)), _sds((1, SGU_W))),
        compiler_params=_cparams("arbitrary"),
    )(z, z, dy, ln_g, ln_b, w_s, bias_full)


def _inproj_backward(x, dxo, dz_att, dga, dca, dcb, dgc, dsu, dsv, dgs, g_pre, mod, w_in_p, name):
    S = x.shape[0]
    tm = ROW_TILE

    def body(x_ref, dxo_ref, p0, p1, p2, p3, p4, p5, p6, p7, g_ref, mod_ref, w_ref,
             dx_ref, hb_ref, dzb_ref, dmod_ref, dg_ref):
        @pl.when(pl.program_id(0) == 0)
        def _():
            dmod_ref[...] = jnp.zeros(dmod_ref.shape, F32)
            dg_ref[...] = jnp.zeros(dg_ref.shape, F32)

        off = 0
        for piece in (p0, p1, p2, p3, p4, p5, p6, p7):
            wdt = piece.shape[1]
            dzb_ref[:, off:off + wdt] = piece[...]
            off += wdt
        dh = lax.dot_general(dzb_ref[...], w_ref[...], NT, preferred_element_type=F32)
        xv = x_ref[...]
        g = g_ref[...]
        one_scale = 1.0 + mod_ref[1:2, :]
        rstd = lax.rsqrt(jnp.mean(xv * xv, axis=-1, keepdims=True) + EPS)
        xhat = xv * rstd
        xg = xhat * g
        hb_ref[...] = (xg * one_scale + mod_ref[0:1, :]).astype(BF16)
        dmod_ref[0:1, :] += jnp.sum(dh, axis=0, keepdims=True)
        dmod_ref[1:2, :] += jnp.sum(dh * xg, axis=0, keepdims=True)
        dhs = dh * one_scale
        dg_ref[...] += jnp.sum(dhs * xhat, axis=0, keepdims=True)
        dxh = dhs * g
        dx_ref[...] = dxo_ref[...] + rstd * (dxh - xhat * jnp.mean(dxh * xhat, axis=-1, keepdims=True))

    widths = (512, 512, 256, 256, 256, 256, 256, 256)
    return pl.pallas_call(
        body, name=name, grid=(S // tm,),
        in_specs=[_rows(tm, D), _rows(tm, D)] + [_rows(tm, w) for w in widths]
                 + [_full((1, D)), _full((3, D)), _full((D, DZ))],
        out_specs=(_rows(tm, D), _rows(tm, D), _rows(tm, DZ), _full((2, D)), _full((1, D))),
        out_shape=(_sds((S, D)), _sds((S, D), BF16), _sds((S, DZ), BF16), _sds((2, D)), _sds((1, D))),
        compiler_params=_cparams("arbitrary"),
    )(x, dxo, dz_att, dga, dca, dcb, dgc, dsu, dsv, dgs, g_pre, mod, w_in_p)


def _adamw(w, gparts, m, v, name):
    shape = w.shape
    cols = shape[-1]
    rows = int(np.prod(shape[:-1]))
    parts = gparts.shape[0]
    w2, m2, v2 = (a.reshape(rows, cols) for a in (w, m, v))
    g3 = gparts.reshape(parts, rows, cols)
    tr = rows
    for cand in (256, 128):
        if rows > cand and rows % cand == 0:
            tr = cand
            break

    def body(w_ref, g_ref, m_ref, v_ref, go_ref, d_ref, mo_ref, vo_ref):
        g = g_ref[0].astype(F32)
        for p in range(1, parts):
            g = g + g_ref[p].astype(F32)
        wv = w_ref[...]
        mn = ADAM_B1 * m_ref[...] + (1.0 - ADAM_B1) * g
        vn = ADAM_B2 * v_ref[...] + (1.0 - ADAM_B2) * (g * g)
        m_hat = mn / (1.0 - ADAM_B1 ** ADAM_STEP)
        v_hat = vn / (1.0 - ADAM_B2 ** ADAM_STEP)
        go_ref[...] = g
        d_ref[...] = -ADAM_LR * (m_hat / (jnp.sqrt(v_hat) + ADAM_EPS) + ADAM_WD * wv)
        mo_ref[...] = mn
        vo_ref[...] = vn

    blk = pl.BlockSpec((tr, cols), lambda i: (i, 0))
    outs = pl.pallas_call(
        body, name=name, grid=(rows // tr,),
        in_specs=[blk, pl.BlockSpec((parts, tr, cols), lambda i: (0, i, 0)), blk, blk],
        out_specs=(blk, blk, blk, blk),
        out_shape=tuple(_sds((rows, cols)) for _ in range(4)),
        compiler_params=_cparams("parallel"),
    )(w2, g3, m2, v2)
    return tuple(o.reshape(shape) for o in outs)


_GATHERED = ("w_in", "w_out", "w_uq", "w_ukv", "w_pw2", "conv_w")
_COL_SHARDED = ("w_in", "w_uq", "w_ukv", "conv_w")

_SMALL = (("dmod", (3 * D,)), ("g_pre", (D,)), ("g_post", (D,)), ("q_norm_g", (Q_RANK,)),
          ("kv_norm_g", (KV_RANK,)), ("conv_b", (CONV_W,)), ("conv_ln_g", (CONV_W,)),
          ("conv_ln_b", (CONV_W,)), ("sgu_ln_g", (SGU_W,)), ("sgu_ln_b", (SGU_W,)),
          ("w_s", (SGU_G, SGU_T, SGU_T)), ("b_s", (SGU_G, SGU_T)))


def _assemble(name, parts):
    if name in _COL_SHARDED:
        p = jnp.moveaxis(parts, 0, 1)
        return p.reshape(p.shape[0], p.shape[1] * p.shape[2])
    return parts.reshape(parts.shape[0] * parts.shape[1], parts.shape[2])


def _scatter_layout(name, full):
    if name in _COL_SHARDED:
        return jnp.moveaxis(full.reshape(full.shape[0], N_DEV, full.shape[1] // N_DEV), 1, 0)
    return full.reshape(N_DEV, full.shape[0] // N_DEV, full.shape[1])


def kernel(x, c, positions, w_ada, b_ada, g_pre, g_post, w_in, q_norm_g, w_uq, kv_norm_g, w_ukv, conv_w, conv_b, conv_ln_g, conv_ln_b, w_pw2, sgu_ln_g, sgu_ln_b, w_s, b_s, w_out, loss_target, m_w_ada, m_b_ada, m_g_pre, m_g_post, m_w_in, m_q_norm_g, m_w_uq, m_kv_norm_g, m_w_ukv, m_conv_w, m_conv_b, m_conv_ln_g, m_conv_ln_b, m_w_pw2, m_sgu_ln_g, m_sgu_ln_b, m_w_s, m_b_s, m_w_out, v_w_ada, v_b_ada, v_g_pre, v_g_post, v_w_in, v_q_norm_g, v_w_uq, v_kv_norm_g, v_w_ukv, v_conv_w, v_conv_b, v_conv_ln_g, v_conv_ln_b, v_w_pw2, v_sgu_ln_g, v_sgu_ln_b, v_w_s, v_b_s, v_w_out):
    weights = dict(w_ada=w_ada, b_ada=b_ada, g_pre=g_pre, g_post=g_post, w_in=w_in, q_norm_g=q_norm_g, w_uq=w_uq,
                   kv_norm_g=kv_norm_g, w_ukv=w_ukv, conv_w=conv_w, conv_b=conv_b, conv_ln_g=conv_ln_g,
                   conv_ln_b=conv_ln_b, w_pw2=w_pw2, sgu_ln_g=sgu_ln_g, sgu_ln_b=sgu_ln_b, w_s=w_s, b_s=b_s, w_out=w_out)
    m_in = dict(w_ada=m_w_ada, b_ada=m_b_ada, g_pre=m_g_pre, g_post=m_g_post, w_in=m_w_in, q_norm_g=m_q_norm_g,
                w_uq=m_w_uq, kv_norm_g=m_kv_norm_g, w_ukv=m_w_ukv, conv_w=m_conv_w, conv_b=m_conv_b,
                conv_ln_g=m_conv_ln_g, conv_ln_b=m_conv_ln_b, w_pw2=m_w_pw2, sgu_ln_g=m_sgu_ln_g,
                sgu_ln_b=m_sgu_ln_b, w_s=m_w_s, b_s=m_b_s, w_out=m_w_out)
    v_in = dict(w_ada=v_w_ada, b_ada=v_b_ada, g_pre=v_g_pre, g_post=v_g_post, w_in=v_w_in, q_norm_g=v_q_norm_g,
                w_uq=v_w_uq, kv_norm_g=v_kv_norm_g, w_ukv=v_w_ukv, conv_w=v_conv_w, conv_b=v_conv_b,
                conv_ln_g=v_conv_ln_g, conv_ln_b=v_conv_ln_b, w_pw2=v_w_pw2, sgu_ln_g=v_sgu_ln_g,
                sgu_ln_b=v_sgu_ln_b, w_s=v_w_s, b_s=v_b_s, w_out=v_w_out)
    order = list(weights)

    S = x.shape[1]
    me = 4 * lax.axis_index("x") + 2 * lax.axis_index("y") + lax.axis_index("c")
    x0 = x.reshape(S, D)
    target = loss_target.reshape(S, D)
    pos = positions.reshape(S, 1)

    def shards(l):
        return [weights[n][l].astype(BF16) for n in _GATHERED]

    def w_in_operand(part):
        w_in_f = _assemble("w_in", part)
        return jnp.concatenate([w_in_f[:, :ATT_IN], jnp.zeros((D, PAD_IN), BF16), w_in_f[:, ATT_IN:]], axis=1)

    def other_operands(parts):
        full = {n: _assemble(n, p) for n, p in zip(_GATHERED[1:], parts)}
        wq = jnp.pad(full["w_uq"].reshape(Q_RANK, HEADS, QK), ((0, 0), (0, 0), (0, HQ - QK)))
        return dict(w_uq=wq.reshape(Q_RANK, HEADS * HQ), w_ukv=full["w_ukv"], w_pw2=full["w_pw2"], w_out=full["w_out"],
                    conv_w=jnp.pad(full["conv_w"].astype(F32), ((0, HALO - CONV_K), (0, 0))))

    first_w_in, c_parts = _exchange([shards(0)[0], c.reshape(8, D // 8)], [False, False], name="gather_w_in_0")
    lw = [dict(w_in=w_in_operand(first_w_in))] + [None] * (DEPTH - 1)
    c_all = c_parts.reshape(N_DEV, D)

    ada_cols = w_ada.shape[-1]
    b_cols = lax.dynamic_slice_in_dim(b_ada, me * ada_cols, ada_cols, axis=1)
    sc_rows, mod_part = _ada_forward(jnp.pad(c_all, ((0, 8), (0, 0))), w_ada, b_cols)
    mod_recv = _exchange([jnp.moveaxis(mod_part[:, :N_DEV], 1, 0)], [True], name="exchange_mod")[0]
    mod = jnp.moveaxis(mod_recv, 0, 1).reshape(DEPTH, 3, D)

    bias_full = jnp.repeat(jnp.swapaxes(b_s, 1, 2), SGU_GD, axis=2)
    inv_freq = ROPE_THETA ** (-jnp.arange(0, ROPE, 2, dtype=F32) / ROPE)
    zeros32 = jnp.zeros((ROPE // 2,), F32)
    ones32 = jnp.ones((ROPE // 2,), F32)
    rope_rows = jnp.zeros((8, 128), F32)
    rope_rows = rope_rows.at[0].set(jnp.concatenate([inv_freq, inv_freq, zeros32, zeros32]))
    rope_rows = rope_rows.at[1].set(jnp.concatenate([ones32, ones32, zeros32, zeros32]))
    rope_rows = rope_rows.at[2].set(jnp.concatenate([-ones32, ones32, zeros32, zeros32]))

    def vec(a, l):
        return a[l].reshape(1, -1)

    saved = []
    xl = x0
    for l in range(DEPTH):
        w = lw[l]
        late = _Comm(shards(0)[1:], [False] * (len(_GATHERED) - 1)) if l == 0 else None
        z, arrived = _prenorm_inproj(xl, vec(g_pre, l), mod[l], w["w_in"], name=f"prenorm_inproj_{l}", comm=late)
        if late is not None:
            w.update(other_operands(arrived))
        q, k, v = _att_prep(z, pos, vec(q_norm_g, l), vec(kv_norm_g, l), w["w_uq"], w["w_ukv"], rope_rows,
                            name=f"att_prep_{l}")
        ahead = _Comm(shards(l + 1), [False] * len(_GATHERED)) if l + 1 < DEPTH else None
        y_att, lse, arrived = _flash_forward(q, k, v, name=f"flash_forward_{l}", comm=ahead)
        if ahead is not None:
            lw[l + 1] = dict(w_in=w_in_operand(arrived[0]), **other_operands(arrived[1:]))
        cv, y_conv = _conv_forward(z, w["conv_w"], vec(conv_b, l), vec(conv_ln_g, l), vec(conv_ln_b, l),
                                   w["w_pw2"], name=f"conv_forward_{l}")
        y_sgu = _sgu_forward(z, vec(sgu_ln_g, l), vec(sgu_ln_b, l), w_s[l], bias_full[l], name=f"sgu_forward_{l}")
        x_next, y, ycat = _out_proj(xl, z, y_att, y_conv, y_sgu, w["w_out"], vec(g_post, l), mod[l],
                                    name=f"out_proj_{l}")
        saved.append(dict(x=xl, z=z, q=q, k=k, v=v, y_att=y_att, lse=lse, cv=cv, y_conv=y_conv, y_sgu=y_sgu, y=y, ycat=ycat))
        xl = x_next

    loss_part, dx = _loss_head(xl, target)
    loss = lax.psum(loss_part.reshape(()), ("x", "y", "c"))

    spack = _Packer(_SMALL, 8)
    grad_kinds = [True] * len(_GATHERED) + [False]
    received = [None] * DEPTH
    pending = None
    for l in reversed(range(DEPTH)):
        sv = saved[l]
        w = lw[l]
        (dyb, dob, stats, dga, dyc, dgc, dys, dgs, dgate, dgpost) = _out_proj_backward(
            dx, sv["y"], sv["z"], sv["y_att"], sv["y_conv"], sv["y_sgu"], sv["lse"], w["w_out"],
            vec(g_post, l), mod[l], name=f"out_proj_backward_{l}")
        big = dict(w_out=_matmul_tn(sv["ycat"], dyb, name=f"grad_w_out_{l}"))
        riding = _Comm(pending, grad_kinds) if pending is not None else None
        dq, dk, dv, arrived = _flash_backward(sv["q"], sv["k"], sv["v"], dob, stats, name=f"flash_backward_{l}",
                                              comm=riding)
        if riding is not None:
            received[l + 1] = arrived
        dz_att, qn_b, dqp_b, kvn_b, dkv_b, dqg, dkvg = _att_prep_backward(
            sv["z"], pos, dq, dk, dv, vec(q_norm_g, l), vec(kv_norm_g, l), w["w_uq"], w["w_ukv"], rope_rows,
            name=f"att_prep_backward_{l}")
        dwq_p = _matmul_tn(qn_b, dqp_b, name=f"grad_w_uq_{l}")
        big["w_uq"] = dwq_p.reshape(Q_RANK, HEADS, HQ)[:, :, :QK].reshape(Q_RANK, HEADS * QK)
        big["w_ukv"] = _matmul_tn(kvn_b, dkv_b, name=f"grad_w_ukv_{l}")
        dcv, sl_b, dyc_b, dclg, dclb, dcb = _conv_norm_backward(dyc, sv["cv"], vec(conv_ln_g, l), vec(conv_ln_b, l),
                                                              w["w_pw2"], name=f"conv_norm_backward_{l}")
        big["w_pw2"] = _matmul_tn(sl_b, dyc_b, name=f"grad_w_pw2_{l}")
        dca, dcbb, dconvw = _conv_backward(sv["z"], dcv, w["conv_w"], name=f"conv_backward_{l}")
        big["conv_w"] = dconvw[:CONV_K]
        dsu, dsvv, dws, dbs, dslg, dslb = _sgu_backward(sv["z"], dys, vec(sgu_ln_g, l), vec(sgu_ln_b, l), w_s[l],
                                                       bias_full[l], name=f"sgu_backward_{l}")
        dx, h_b, dz_b, dmod2, dgpre = _inproj_backward(
            sv["x"], dx, dz_att, dga, dca, dcbb, dgc, dsu, dsvv, dgs, vec(g_pre, l), mod[l], w["w_in"],
            name=f"inproj_backward_{l}")
        small = dict(dmod=jnp.concatenate([dmod2.reshape(-1), dgate.reshape(-1)]), g_pre=dgpre, g_post=dgpost,
                     q_norm_g=dqg, kv_norm_g=dkvg, conv_b=dcb, conv_ln_g=dclg, conv_ln_b=dclb, sgu_ln_g=dslg,
                     sgu_ln_b=dslb, w_s=dws, b_s=jnp.swapaxes(dbs[:, :SGU_G], 0, 1))
        rest = [_scatter_layout(n, big[n]).astype(BF16) for n in _GATHERED[1:]] + [spack.pack(small, F32)]
        early = _Comm(rest, grad_kinds[1:]) if l == 0 else None
        dwin_p = _matmul_tn(h_b, dz_b, name=f"grad_w_in_{l}", comm=early)
        if early is not None:
            dwin_p, rest_arrived = dwin_p
        dwin = jnp.concatenate([dwin_p[:, :ATT_IN], dwin_p[:, ATT_IN + PAD_IN:]], axis=1)
        pending = [_scatter_layout("w_in", dwin).astype(BF16)] + rest
    grad_x = dx.reshape(1, S, D)
    received[0] = list(_exchange(pending[:1], grad_kinds[:1], name="exchange_grad_w_in_0")) + list(rest_arrived)

    gparts = {n: jnp.stack([received[l][i] for l in range(DEPTH)], axis=1) for i, n in enumerate(_GATHERED)}
    sparts = [spack.unpack(received[l][-1], (N_DEV,)) for l in range(DEPTH)]
    sparts = {n: jnp.stack([sparts[l][n] for l in range(DEPTH)], axis=1) for n, _ in _SMALL}
    dmod_all = sparts["dmod"]
    dmod_cols = lax.dynamic_slice_in_dim(dmod_all, me * ada_cols, ada_cols, axis=2)
    sc_t = jnp.pad(sc_rows[:N_DEV].T, ((0, 0), (0, 128 - N_DEV)))
    dmod_rows = jnp.pad(jnp.moveaxis(dmod_cols, 0, 1), ((0, 0), (0, 128 - N_DEV), (0, 0)))
    gparts["w_ada"] = _ada_backward(sc_t, dmod_rows)[None]
    gparts["b_ada"] = dmod_all
    for n, _ in _SMALL[1:]:
        gparts[n] = sparts[n]

    grads, deltas, new_m, new_v = {}, {}, {}, {}
    for n in order:
        grads[n], deltas[n], new_m[n], new_v[n] = _adamw(weights[n], gparts[n], m_in[n], v_in[n], name=f"adamw_{n}")
    return (loss, grad_x, *[grads[n] for n in order], *[deltas[n] for n in order],
            *[new_m[n] for n in order], *[new_v[n] for n in order])
```

```python
import functools
import math

import numpy as np
import jax
import jax.numpy as jnp
from jax import lax
from jax.experimental import pallas as pl
from jax.experimental.pallas import tpu as pltpu

F32 = jnp.float32
BF16 = jnp.bfloat16

N_DEV = 8
DEPTH = 2
D = 1024
HEADS = 4
NOPE = 128
ROPE = 64
VDIM = 128
QK = NOPE + ROPE
Q_RANK = 256
KV_RANK = 128
ATT_W = HEADS * VDIM
CONV_W = 256
CONV_K = 31
SGU_W = 256
SGU_G = 4
SGU_GD = SGU_W // SGU_G
SGU_T = 128
D_IN = 2496
ATT_IN = Q_RANK + KV_RANK + ROPE
PAD_IN = 64
DZ = D_IN + PAD_IN
HQ = 2 * NOPE
EPS = 1e-6
ROPE_THETA = 10000.0
ATT_SCALE = QK ** -0.5
LOG2E = math.log2(math.e)
EXP2_SCALE = ATT_SCALE * LOG2E
NEG_INF = float("-inf")

ADAM_LR = 0.001
ADAM_B1 = 0.9
ADAM_B2 = 0.999
ADAM_EPS = 1e-08
ADAM_WD = 0.01
ADAM_STEP = 10

VMEM_LIMIT = 56 * 1024 * 1024
ROW_TILE = 512
MATMUL_TN_ROWS = 1024
ATT_TILE = 512
HALO = 32
PACK_LANES = 128

MESH = pl.DeviceIdType.MESH
NT = (((1,), (1,)), ((), ()))
TN = (((0,), (0,)), ((), ()))


def _cparams(*sem):
    return pltpu.CompilerParams(dimension_semantics=sem, vmem_limit_bytes=VMEM_LIMIT)


def _sds(shape, dtype=F32):
    return jax.ShapeDtypeStruct(tuple(shape), dtype)


def _rows(tm, width, col=0):
    return pl.BlockSpec((tm, width), lambda i: (i, col))


def _full(shape):
    nd = len(shape)
    return pl.BlockSpec(tuple(shape), lambda *_: (0,) * nd)


def _sigmoid(x):
    return 1.0 / (1.0 + jnp.exp(-x))


def _silu_and_grad(g):
    s = _sigmoid(g)
    return g * s, s * (1.0 + g * (1.0 - s))


def _gelu_and_grad(x):
    cdf = 0.5 * (1.0 + lax.erf(x * (1.0 / math.sqrt(2.0))))
    pdf = jnp.exp(-0.5 * x * x) * (1.0 / math.sqrt(2.0 * math.pi))
    return x * cdf, cdf + x * pdf


def _swap_halves(a):
    lane = lax.broadcasted_iota(jnp.int32, a.shape, 1)
    up = pltpu.roll(a, 32, 1)
    down = pltpu.roll(a, 96, 1)
    return jnp.where(lane < 32, down, jnp.where(lane < 64, up, 0.0))


def _rope_tables(pos_ref, rope_ref):
    ang = pos_ref[...].astype(F32) * rope_ref[0:1, :]
    return jnp.cos(ang) * rope_ref[1:2, :], jnp.sin(ang) * rope_ref[2:3, :]


class _Comm:
    def __init__(self, srcs, kinds):
        self.srcs = list(srcs)
        self.kinds = list(kinds)
        self.n = len(self.srcs)
        self.out_shape = [_sds((N_DEV,) + tuple(s.shape[1:] if k else s.shape), s.dtype)
                          for s, k in zip(self.srcs, self.kinds)]
        self.specs = [pl.BlockSpec(memory_space=pl.ANY)] * self.n
        self.scratch = [pltpu.SemaphoreType.DMA((self.n, N_DEV - 1)), pltpu.SemaphoreType.DMA((self.n, N_DEV - 1)),
                        pltpu.SemaphoreType.DMA((self.n,))] if self.n else []

    def _copies(self, src_refs, out_refs, sems, with_recvs):
        send_sems, recv_sems, local_sems = sems
        x, y, c = lax.axis_index("x"), lax.axis_index("y"), lax.axis_index("c")
        me = 4 * x + 2 * y + c
        local, sends, recvs = [], [], []
        for a in range(self.n):
            def block_for(dest, src_ref=src_refs[a], a2a=self.kinds[a]):
                return src_ref.at[dest] if a2a else src_ref

            local.append(pltpu.make_async_copy(block_for(me), out_refs[a].at[me], local_sems.at[a]))
            for r in range(1, N_DEV):
                px = 1 - x if (r >> 2) & 1 else x
                py = 1 - y if (r >> 1) & 1 else y
                pc = 1 - c if r & 1 else c
                peer = 4 * px + 2 * py + pc
                sends.append(pltpu.make_async_remote_copy(
                    src_ref=block_for(peer), dst_ref=out_refs[a].at[me],
                    send_sem=send_sems.at[a, r - 1], recv_sem=recv_sems.at[a, r - 1],
                    device_id=(px, py, pc), device_id_type=MESH))
                if with_recvs:
                    recvs.append(pltpu.make_async_remote_copy(
                        src_ref=block_for(me), dst_ref=out_refs[a].at[peer],
                        send_sem=send_sems.at[a, r - 1], recv_sem=recv_sems.at[a, r - 1],
                        device_id=(px, py, pc), device_id_type=MESH))
        return local, sends, recvs

    def start(self, src_refs, out_refs, sems):
        local, sends, _ = self._copies(src_refs, out_refs, sems, False)
        for cp in local + sends:
            cp.start()

    def finish(self, src_refs, out_refs, sems):
        local, sends, recvs = self._copies(src_refs, out_refs, sems, True)
        for cp in recvs:
            cp.wait_recv()
        for cp in sends:
            cp.wait_send()
        for cp in local:
            cp.wait()


def _exchange(srcs, kinds, name):
    comm = _Comm(srcs, kinds)
    n = comm.n

    def body(*refs):
        src_refs, out_refs, sems = refs[:n], refs[n:2 * n], refs[2 * n:]
        comm.start(src_refs, out_refs, sems)
        comm.finish(src_refs, out_refs, sems)

    return pl.pallas_call(
        body, name=name, out_shape=comm.out_shape, in_specs=comm.specs, out_specs=comm.specs,
        scratch_shapes=comm.scratch,
    )(*srcs)


class _Packer:
    def __init__(self, entries, row_multiple):
        self.entries = entries
        self.offsets = {}
        off = 0
        for name, shape in entries:
            self.offsets[name] = off
            off += int(np.prod(shape))
        quantum = PACK_LANES * row_multiple
        self.total = -(-off // quantum) * quantum
        self.used = off
        self.rows = self.total // PACK_LANES

    def pack(self, arrays, dtype, lead=()):
        n = len(lead)
        flat = [arrays[name].astype(dtype).reshape(lead + (-1,)) for name, _ in self.entries]
        flat.append(jnp.zeros(lead + (self.total - self.used,), dtype))
        return jnp.concatenate(flat, axis=n).reshape(lead + (self.rows, PACK_LANES))

    def unpack(self, buf, lead=()):
        flat = buf.reshape(lead + (self.total,))
        out = {}
        for name, shape in self.entries:
            o = self.offsets[name]
            out[name] = lax.slice_in_dim(flat, o, o + int(np.prod(shape)), axis=len(lead)).reshape(lead + tuple(shape))
        return out


def _ada_forward(c_rows, w_ada, b_ada_cols):
    cols = w_ada.shape[-1]
    rows = c_rows.shape[0]

    def body(c_ref, w_ref, b_ref, sc_ref, part_ref):
        cv = c_ref[...]
        sc = cv * _sigmoid(cv)
        sc_ref[...] = sc
        scb = sc.astype(BF16)
        for l in range(DEPTH):
            part_ref[l] = jnp.dot(scb, w_ref[l].astype(BF16), preferred_element_type=F32) + b_ref[l:l + 1, :]

    return pl.pallas_call(
        body, name="ada_forward",
        out_shape=(_sds((rows, D)), _sds((DEPTH, rows, cols))),
        compiler_params=pltpu.CompilerParams(vmem_limit_bytes=VMEM_LIMIT),
    )(c_rows, w_ada, b_ada_cols)


def _ada_backward(sc_t, dmod_cols):
    cols = dmod_cols.shape[-1]

    def body(sc_ref, dm_ref, gw_ref):
        scb = sc_ref[...].astype(BF16)
        for l in range(DEPTH):
            gw_ref[l] = jnp.dot(scb, dm_ref[l].astype(BF16), preferred_element_type=F32)

    return pl.pallas_call(
        body, name="ada_backward",
        out_shape=_sds((DEPTH, D, cols)),
        compiler_params=pltpu.CompilerParams(vmem_limit_bytes=VMEM_LIMIT),
    )(sc_t, dmod_cols)


def _riding(comm, nsteps, c_src, c_out, c_sems, where):
    if not comm.n:
        return
    step = 0 if where == "start" else nsteps - 1

    @pl.when(pl.program_id(0) == step)
    def _():
        (comm.start if where == "start" else comm.finish)(c_src, c_out, c_sems)


def _prenorm_inproj(x, g_pre, mod, w_in_p, name, comm=None):
    S = x.shape[0]
    tm = ROW_TILE
    comm = comm or _Comm([], [])
    nc = comm.n

    def body(*refs):
        x_ref, g_ref, mod_ref, w_ref = refs[:4]
        c_src, z_ref, c_out, c_sems = refs[4:4 + nc], refs[4 + nc], refs[5 + nc:5 + 2 * nc], refs[5 + 2 * nc:]
        _riding(comm, S // tm, c_src, c_out, c_sems, "start")
        xv = x_ref[...]
        rstd = lax.rsqrt(jnp.mean(xv * xv, axis=-1, keepdims=True) + EPS)
        h = (xv * rstd * g_ref[...]) * (1.0 + mod_ref[1:2, :]) + mod_ref[0:1, :]
        z_ref[...] = jnp.dot(h.astype(BF16), w_ref[...], preferred_element_type=F32)
        _riding(comm, S // tm, c_src, c_out, c_sems, "finish")

    outs = pl.pallas_call(
        body, name=name, grid=(S // tm,),
        in_specs=[_rows(tm, D), _full((1, D)), _full((3, D)), _full((D, DZ))] + comm.specs,
        out_specs=[_rows(tm, DZ)] + comm.specs, out_shape=[_sds((S, DZ))] + comm.out_shape,
        scratch_shapes=comm.scratch,
        compiler_params=_cparams("arbitrary"),
    )(x, g_pre, mod, w_in_p, *comm.srcs)
    return outs[0], outs[1:]


def _att_prep(z, pos, q_g, kv_g, wq_p, w_ukv, rope_rows, name):
    S = z.shape[0]
    tm = ROW_TILE

    def body(z_ref, pos_ref, qg_ref, kvg_ref, wq_ref, wkv_ref, rope_ref, q_ref, k_ref, v_ref):
        zz = z_ref[...]
        ql, kvl, ka = zz[:, 0:Q_RANK], zz[:, Q_RANK:Q_RANK + KV_RANK], zz[:, Q_RANK + KV_RANK:]
        qn = ql * lax.rsqrt(jnp.mean(ql * ql, axis=-1, keepdims=True) + EPS) * qg_ref[...]
        kvn = kvl * lax.rsqrt(jnp.mean(kvl * kvl, axis=-1, keepdims=True) + EPS) * kvg_ref[...]
        q = jnp.dot(qn.astype(BF16), wq_ref[...], preferred_element_type=F32)
        kv = jnp.dot(kvn.astype(BF16), wkv_ref[...], preferred_element_type=F32)
        ct, st = _rope_tables(pos_ref, rope_ref)
        krot = (ka * ct + _swap_halves(ka) * st).astype(BF16)
        for h in range(HEADS):
            b = h * HQ
            q_ref[:, b:b + NOPE] = q[:, b:b + NOPE].astype(BF16)
            a = q[:, b + NOPE:b + HQ]
            q_ref[:, b + NOPE:b + HQ] = (a * ct + _swap_halves(a) * st).astype(BF16)
            k_ref[:, b:b + NOPE] = kv[:, b:b + NOPE].astype(BF16)
            k_ref[:, b + NOPE:b + HQ] = krot
            v_ref[:, h * VDIM:(h + 1) * VDIM] = kv[:, b + NOPE:b + HQ].astype(BF16)

    return pl.pallas_call(
        body, name=name, grid=(S // tm,),
        in_specs=[_rows(tm, 512, 0), _rows(tm, 1), _full((1, Q_RANK)), _full((1, KV_RANK)),
                  _full((Q_RANK, HEADS * HQ)), _full((KV_RANK, HEADS * HQ)), _full((8, 128))],
        out_specs=(_rows(tm, HEADS * HQ), _rows(tm, HEADS * HQ), _rows(tm, ATT_W)),
        out_shape=(_sds((S, HEADS * HQ), BF16), _sds((S, HEADS * HQ), BF16), _sds((S, ATT_W), BF16)),
        compiler_params=_cparams("parallel"),
    )(z, pos, q_g, kv_g, wq_p, w_ukv, rope_rows)


def _flash_forward(q, k, v, name, comm=None):
    S = q.shape[0]
    t = ATT_TILE
    nq = S // t
    nl = t // 128
    comm = comm or _Comm([], [])
    nc = comm.n

    def body(*refs):
        q_ref, k_ref, v_ref = refs[:3]
        c_src = refs[3:3 + nc]
        o_ref, lse_ref = refs[3 + nc:5 + nc]
        c_out = refs[5 + nc:5 + 2 * nc]
        m_sc, l_sc, acc_sc, s_sc, mp_sc = refs[5 + 2 * nc:10 + 2 * nc]
        c_sems = refs[10 + 2 * nc:]
        if nc:
            @pl.when((pl.program_id(0) == 0) & (pl.program_id(1) == 0))
            def _():
                comm.start(c_src, c_out, c_sems)

        qb = pl.program_id(1)
        m_sc[...] = jnp.full(m_sc.shape, NEG_INF, F32)
        l_sc[...] = jnp.zeros(l_sc.shape, F32)
        acc_sc[...] = jnp.zeros(acc_sc.shape, F32)

        def score_phase(kb, slot, diagonal):
            s = lax.dot_general(q_ref[...], k_ref[pl.ds(pl.multiple_of(kb * t, t), t), :], NT,
                                preferred_element_type=F32)
            if diagonal:
                ri = lax.broadcasted_iota(jnp.int32, (t, t), 0)
                ci = lax.broadcasted_iota(jnp.int32, (t, t), 1)
                s = jnp.where(ci <= ri, s, NEG_INF)
            s_sc[slot] = s
            mp = s[:, 0:128]
            for c in range(1, nl):
                mp = jnp.maximum(mp, s[:, c * 128:(c + 1) * 128])
            mp_sc[slot] = mp

        def sum_phase(kb, slot):
            m_prev = m_sc[...]
            m_new = jnp.maximum(m_prev, jnp.max(mp_sc[slot], axis=-1, keepdims=True))
            alpha = jnp.exp2((m_prev - m_new) * EXP2_SCALE)
            p = jnp.exp2(s_sc[slot] * EXP2_SCALE - jnp.tile(m_new * EXP2_SCALE, (1, nl)))
            lp = alpha * l_sc[...]
            for c in range(nl):
                lp = lp + p[:, c * 128:(c + 1) * 128]
            l_sc[...] = lp
            acc_sc[...] = alpha * acc_sc[...] + jnp.dot(p.astype(BF16), v_ref[pl.ds(pl.multiple_of(kb * t, t), t), :],
                                                        preferred_element_type=F32)
            m_sc[...] = m_new

        def tile_at(pos):
            return jnp.where(pos == 0, qb, pos - 1)

        score_phase(qb, 0, True)

        def pair(u, carry):
            p0 = 2 * u
            sum_phase(tile_at(p0), 0)
            score_phase(p0, 1, False)
            sum_phase(p0, 1)
            score_phase(p0 + 1, 0, False)
            return carry

        lax.fori_loop(0, qb // 2, pair, 0)

        @pl.when(qb % 2 == 1)
        def _():
            sum_phase(tile_at(qb - 1), 0)
            score_phase(qb - 1, 1, False)
            sum_phase(qb - 1, 1)

        @pl.when(qb % 2 == 0)
        def _():
            sum_phase(tile_at(qb), 0)

        l = jnp.sum(l_sc[...], axis=-1, keepdims=True)
        o_ref[...] = acc_sc[...] / l
        lse_ref[0] = jnp.max(m_sc[...], axis=-1, keepdims=True) * ATT_SCALE + jnp.log(l)

        if nc:
            @pl.when((pl.program_id(0) == HEADS - 1) & (pl.program_id(1) == nq - 1))
            def _():
                comm.finish(c_src, c_out, c_sems)

    outs = pl.pallas_call(
        body, name=name, grid=(HEADS, nq),
        in_specs=[pl.BlockSpec((t, HQ), lambda h, i: (i, h)),
                  pl.BlockSpec((S, HQ), lambda h, i: (0, h)),
                  pl.BlockSpec((S, VDIM), lambda h, i: (0, h))] + comm.specs,
        out_specs=[pl.BlockSpec((t, VDIM), lambda h, i: (i, h)),
                   pl.BlockSpec((1, t, 1), lambda h, i: (h, i, 0))] + comm.specs,
        out_shape=[_sds((S, ATT_W)), _sds((HEADS, S, 1))] + comm.out_shape,
        scratch_shapes=[pltpu.VMEM((t, 128), F32), pltpu.VMEM((t, 128), F32), pltpu.VMEM((t, VDIM), F32),
                        pltpu.VMEM((2, t, t), F32), pltpu.VMEM((2, t, 128), F32)] + comm.scratch,
        compiler_params=_cparams("arbitrary", "arbitrary"),
    )(q, k, v, *comm.srcs)
    return outs[0], outs[1], outs[2:]


def _conv_window(win_ref, a_prev, b_prev, a_cur, b_cur, first):
    hp = a_prev * _sigmoid(b_prev)
    win_ref[0:HALO, :] = jnp.where(first, 0.0, hp)
    win_ref[HALO:, :] = a_cur * _sigmoid(b_cur)


def _shifted_copies(win_ref, sh_ref, tm):
    for b in range(1, 8):
        sh_ref[b - 1] = win_ref[pl.ds(b, tm + HALO - 8), :]


def _tap(win_ref, sh_ref, offset, tm):
    a, b = divmod(offset, 8)
    if b == 0:
        return win_ref[pl.ds(8 * a, tm), :]
    return sh_ref[b - 1, pl.ds(8 * a, tm), :]


def _conv_in_specs(tm):
    per = tm // HALO
    prev = lambda col: pl.BlockSpec((HALO, CONV_W), lambda i: (jnp.maximum(i * per - 1, 0), col))
    return [_rows(tm, CONV_W, 4), _rows(tm, CONV_W, 5), prev(4), prev(5)]


def _conv_forward(z, conv_w_p, conv_b, ln_g, ln_b, w_pw2, name):
    S = z.shape[0]
    tm = ROW_TILE

    def body(a_ref, b_ref, ap_ref, bp_ref, w_ref, cb_ref, g_ref, be_ref, pw_ref, cv_ref, y_ref, win, sh):
        _conv_window(win, ap_ref[...], bp_ref[...], a_ref[...], b_ref[...], pl.program_id(0) == 0)
        _shifted_copies(win, sh, tm)
        acc = jnp.zeros((tm, CONV_W), F32)
        for kk in range(CONV_K):
            acc = acc + w_ref[kk:kk + 1, :] * _tap(win, sh, HALO - (CONV_K - 1) + kk, tm)
        cv = acc + cb_ref[...]
        cv_ref[...] = cv
        mu = jnp.mean(cv, axis=-1, keepdims=True)
        cc = cv - mu
        rstd = lax.rsqrt(jnp.mean(cc * cc, axis=-1, keepdims=True) + EPS)
        n = cc * rstd * g_ref[...] + be_ref[...]
        sl = n * _sigmoid(n)
        y_ref[...] = jnp.dot(sl.astype(BF16), pw_ref[...], preferred_element_type=F32)

    return pl.pallas_call(
        body, name=name, grid=(S // tm,),
        in_specs=_conv_in_specs(tm) + [_full((HALO, CONV_W)), _full((1, CONV_W)), _full((1, CONV_W)),
                                       _full((1, CONV_W)), _full((CONV_W, CONV_W))],
        out_specs=(_rows(tm, CONV_W), _rows(tm, CONV_W)),
        out_shape=(_sds((S, CONV_W)), _sds((S, CONV_W))),
        scratch_shapes=[pltpu.VMEM((tm + HALO, CONV_W), F32), pltpu.VMEM((7, tm + HALO - 8, CONV_W), F32)],
        compiler_params=_cparams("parallel"),
    )(z, z, z, z, conv_w_p, conv_b, ln_g, ln_b, w_pw2)


def _sgu_common(u, v, g_ref, be_ref):
    gu, dgu = _gelu_and_grad(u)
    gv, dgv = _gelu_and_grad(v)
    mu = jnp.mean(gv, axis=-1, keepdims=True)
    cc = gv - mu
    rstd = lax.rsqrt(jnp.mean(cc * cc, axis=-1, keepdims=True) + EPS)
    nh = cc * rstd
    vn = nh * g_ref[...] + be_ref[...]
    return gu, dgu, dgv, rstd, nh, vn


def _sgu_masks():
    lane_group = lax.broadcasted_iota(jnp.int32, (1, SGU_W), 1) // SGU_GD
    ri = lax.broadcasted_iota(jnp.int32, (SGU_T, SGU_T), 0)
    ci = lax.broadcasted_iota(jnp.int32, (SGU_T, SGU_T), 1)
    return [lane_group == g for g in range(SGU_G)], ci <= ri


def _sgu_forward(z, ln_g, ln_b, w_s, bias_full, name):
    S = z.shape[0]
    tm = ROW_TILE

    def body(u_ref, v_ref, g_ref, be_ref, ws_ref, bias_ref, y_ref):
        gmask, tril = _sgu_masks()
        wm = [jnp.where(tril, ws_ref[g], 0.0).astype(BF16) for g in range(SGU_G)]
        for ch in range(tm // SGU_T):
            rows = slice(ch * SGU_T, (ch + 1) * SGU_T)
            gu, _, _, _, _, vn = _sgu_common(u_ref[rows, :], v_ref[rows, :], g_ref, be_ref)
            vb = vn.astype(BF16)
            sv = bias_ref[...]
            for g in range(SGU_G):
                sv = sv + jnp.where(gmask[g], jnp.dot(wm[g], vb, preferred_element_type=F32), 0.0)
            y_ref[rows, :] = gu * sv

    return pl.pallas_call(
        body, name=name, grid=(S // tm,),
        in_specs=[_rows(tm, SGU_W, 7), _rows(tm, SGU_W, 8), _full((1, SGU_W)), _full((1, SGU_W)),
                  _full((SGU_G, SGU_T, SGU_T)), _full((SGU_T, SGU_W))],
        out_specs=_rows(tm, SGU_W), out_shape=_sds((S, SGU_W)),
        compiler_params=_cparams("parallel"),
    )(z, z, ln_g, ln_b, w_s, bias_full)


def _out_proj(x, z, y_att, y_conv, y_sgu, w_out, g_post, mod, name):
    S = x.shape[0]
    tm = ROW_TILE

    def body(x_ref, ga_ref, gc_ref, gs_ref, ya_ref, yc_ref, ys_ref, w_ref, gp_ref, mod_ref, xn_ref, y_ref, cat_ref):
        ca = (ya_ref[...] * _silu_and_grad(ga_ref[...])[0]).astype(BF16)
        cc = (yc_ref[...] * _silu_and_grad(gc_ref[...])[0]).astype(BF16)
        cs = (ys_ref[...] * _silu_and_grad(gs_ref[...])[0]).astype(BF16)
        cat_ref[:, 0:ATT_W] = ca
        cat_ref[:, ATT_W:ATT_W + CONV_W] = cc
        cat_ref[:, ATT_W + CONV_W:] = cs
        y = (jnp.dot(ca, w_ref[0:ATT_W, :], preferred_element_type=F32)
             + jnp.dot(cc, w_ref[ATT_W:ATT_W + CONV_W, :], preferred_element_type=F32)
             + jnp.dot(cs, w_ref[ATT_W + CONV_W:, :], preferred_element_type=F32))
        y_ref[...] = y
        rstd = lax.rsqrt(jnp.mean(y * y, axis=-1, keepdims=True) + EPS)
        xn_ref[...] = x_ref[...] + mod_ref[2:3, :] * (y * rstd * gp_ref[...])

    return pl.pallas_call(
        body, name=name, grid=(S // tm,),
        in_specs=[_rows(tm, D), _rows(tm, 512, 1), _rows(tm, 256, 6), _rows(tm, 256, 9),
                  _rows(tm, ATT_W), _rows(tm, CONV_W), _rows(tm, SGU_W),
                  _full((D, D)), _full((1, D)), _full((3, D))],
        out_specs=(_rows(tm, D), _rows(tm, D), _rows(tm, D)),
        out_shape=(_sds((S, D)), _sds((S, D)), _sds((S, D), BF16)),
        compiler_params=_cparams("parallel"),
    )(x, z, z, z, y_att, y_conv, y_sgu, w_out, g_post, mod)


def _loss_head(y, target):
    S = y.shape[0]
    tm = ROW_TILE

    def body(y_ref, t_ref, loss_ref, dy_ref):
        @pl.when(pl.program_id(0) == 0)
        def _():
            loss_ref[...] = jnp.zeros(loss_ref.shape, F32)

        err = y_ref[...] - t_ref[...]
        dy_ref[...] = err * (1.0 / D)
        row = jnp.sum(err * err, axis=-1, keepdims=True) * (1.0 / D)
        loss_ref[...] += 0.5 * jnp.sum(row, axis=0, keepdims=True)

    return pl.pallas_call(
        body, name="loss_head", grid=(S // tm,),
        in_specs=[_rows(tm, D), _rows(tm, D)],
        out_specs=(_full((1, 1)), _rows(tm, D)),
        out_shape=(_sds((1, 1)), _sds((S, D))),
        compiler_params=_cparams("arbitrary"),
    )(y, target)


def _matmul_tn(a, b, name, comm=None):
    S, M = a.shape
    N = b.shape[1]
    bk = min(MATMUL_TN_ROWS, S)
    riding = comm is not None
    comm = comm or _Comm([], [])
    nc = comm.n

    def body(*refs):
        a_ref, b_ref = refs[:2]
        c_src, o_ref, c_out, c_sems = refs[2:2 + nc], refs[2 + nc], refs[3 + nc:3 + 2 * nc], refs[3 + 2 * nc:]
        _riding(comm, S // bk, c_src, c_out, c_sems, "start")

        @pl.when(pl.program_id(0) == 0)
        def _():
            o_ref[...] = jnp.zeros(o_ref.shape, F32)

        o_ref[...] += lax.dot_general(a_ref[...], b_ref[...], TN, preferred_element_type=F32)
        _riding(comm, S // bk, c_src, c_out, c_sems, "finish")

    outs = pl.pallas_call(
        body, name=name, grid=(S // bk,),
        in_specs=[pl.BlockSpec((bk, M), lambda k: (k, 0)), pl.BlockSpec((bk, N), lambda k: (k, 0))] + comm.specs,
        out_specs=[_full((M, N))] + comm.specs, out_shape=[_sds((M, N))] + comm.out_shape,
        scratch_shapes=comm.scratch,
        compiler_params=_cparams("arbitrary"),
    )(a, b, *comm.srcs)
    return (outs[0], outs[1:]) if riding else outs[0]


def _out_proj_backward(dxo, y, z, y_att, y_conv, y_sgu, lse, w_out, g_post, mod, name):
    S = dxo.shape[0]
    tm = ROW_TILE

    def body(dxo_ref, y_ref, ga_ref, gc_ref, gs_ref, ya_ref, yc_ref, ys_ref, lse_ref, w_ref, gp_ref, mod_ref,
             dyb_ref, dob_ref, st_ref, dga_ref, dyc_ref, dgc_ref, dys_ref, dgs_ref, dgate_ref, dgp_ref):
        @pl.when(pl.program_id(0) == 0)
        def _():
            dgate_ref[...] = jnp.zeros(dgate_ref.shape, F32)
            dgp_ref[...] = jnp.zeros(dgp_ref.shape, F32)

        dxo_v = dxo_ref[...]
        yv = y_ref[...]
        gp = gp_ref[...]
        rstd = lax.rsqrt(jnp.mean(yv * yv, axis=-1, keepdims=True) + EPS)
        yhat = yv * rstd
        dgate_ref[...] += jnp.sum(dxo_v * (yhat * gp), axis=0, keepdims=True)
        dr = dxo_v * mod_ref[2:3, :]
        dgp_ref[...] += jnp.sum(dr * yhat, axis=0, keepdims=True)
        dyh = dr * gp
        dy = rstd * (dyh - yhat * jnp.mean(dyh * yhat, axis=-1, keepdims=True))
        dyb = dy.astype(BF16)
        dyb_ref[...] = dyb
        dcat = lax.dot_general(dyb, w_ref[...], NT, preferred_element_type=F32)

        ya = ya_ref[...]
        sil, dsil = _silu_and_grad(ga_ref[...])
        da = dcat[:, 0:ATT_W]
        do = da * sil
        dob_ref[...] = do.astype(BF16)
        dga_ref[...] = (da * ya * dsil).astype(BF16)
        lane = lax.broadcasted_iota(jnp.int32, (1, 128), 1)
        stats = jnp.zeros((tm, 128), F32)
        for h in range(HEADS):
            cols = slice(h * VDIM, (h + 1) * VDIM)
            delta = jnp.sum(do[:, cols] * ya[:, cols], axis=-1, keepdims=True)
            stats = stats + jnp.where(lane == 2 * h, lse_ref[h], 0.0) + jnp.where(lane == 2 * h + 1, delta, 0.0)
        st_ref[...] = stats

        sil, dsil = _silu_and_grad(gc_ref[...])
        dc = dcat[:, ATT_W:ATT_W + CONV_W]
        dyc_ref[...] = dc * sil
        dgc_ref[...] = (dc * yc_ref[...] * dsil).astype(BF16)
        sil, dsil = _silu_and_grad(gs_ref[...])
        dsg = dcat[:, ATT_W + CONV_W:]
        dys_ref[...] = dsg * sil
        dgs_ref[...] = (dsg * ys_ref[...] * dsil).astype(BF16)

    return pl.pallas_call(
        body, name=name, grid=(S // tm,),
        in_specs=[_rows(tm, D), _rows(tm, D), _rows(tm, 512, 1), _rows(tm, 256, 6), _rows(tm, 256, 9),
                  _rows(tm, ATT_W), _rows(tm, CONV_W), _rows(tm, SGU_W),
                  pl.BlockSpec((HEADS, tm, 1), lambda i: (0, i, 0)),
                  _full((D, D)), _full((1, D)), _full((3, D))],
        out_specs=(_rows(tm, D), _rows(tm, ATT_W), _rows(tm, 128), _rows(tm, ATT_W),
                   _rows(tm, CONV_W), _rows(tm, CONV_W), _rows(tm, SGU_W), _rows(tm, SGU_W),
                   _full((1, D)), _full((1, D))),
        out_shape=(_sds((S, D), BF16), _sds((S, ATT_W), BF16), _sds((S, 128)), _sds((S, ATT_W), BF16),
                   _sds((S, CONV_W)), _sds((S, CONV_W), BF16), _sds((S, SGU_W)), _sds((S, SGU_W), BF16),
                   _sds((1, D)), _sds((1, D))),
        compiler_params=_cparams("arbitrary"),
    )(dxo, y, z, z, z, y_att, y_conv, y_sgu, lse, w_out, g_post, mod)


def _flash_backward(q, k, v, do, stats, name, comm=None):
    S = q.shape[0]
    t = ATT_TILE
    tk = 2 * t
    nq = S // t
    comm = comm or _Comm([], [])
    nc = comm.n

    def body(*refs):
        q_ref, do_ref, st_ref, k_ref, v_ref = refs[:5]
        c_src = refs[5:5 + nc]
        dq_ref, dk_ref, dv_ref = refs[5 + nc:8 + nc]
        c_out = refs[8 + nc:8 + 2 * nc]
        dk_sc, dv_sc = refs[8 + 2 * nc:10 + 2 * nc]
        c_sems = refs[10 + 2 * nc:]
        h = pl.program_id(0)
        j = pl.program_id(1)
        if nc:
            @pl.when((h == 0) & (j == 0))
            def _():
                comm.start(c_src, c_out, c_sems)

        @pl.when(j == 0)
        def _():
            dq_ref[...] = jnp.zeros(dq_ref.shape, F32)

        dk_sc[...] = jnp.zeros(dk_sc.shape, F32)
        dv_sc[...] = jnp.zeros(dv_sc.shape, F32)
        lane = lax.broadcasted_iota(jnp.int32, (1, 128), 1)

        def chain(hf, qv, dov, lse2, delta, diagonal):
            kt = k_ref[hf * t:(hf + 1) * t, :]
            s = lax.dot_general(qv, kt, NT, preferred_element_type=F32)
            p = jnp.exp2(s * EXP2_SCALE - lse2)
            if diagonal:
                ri = lax.broadcasted_iota(jnp.int32, (t, t), 0)
                ci = lax.broadcasted_iota(jnp.int32, (t, t), 1)
                p = jnp.where(ci <= ri, p, 0.0)
            dv_sc[hf] += lax.dot_general(p.astype(BF16), dov, TN, preferred_element_type=F32)
            dp = lax.dot_general(dov, v_ref[hf * t:(hf + 1) * t, :], NT, preferred_element_type=F32)
            ds = (p * (dp - delta) * ATT_SCALE).astype(BF16)
            dk_sc[hf] += lax.dot_general(ds, qv, TN, preferred_element_type=F32)
            return jnp.dot(ds, kt, preferred_element_type=F32)

        def q_tile(qb, modes):
            rows = pl.ds(pl.multiple_of(qb * t, t), t)
            qv = q_ref[rows, :]
            dov = do_ref[rows, :]
            st = st_ref[rows, :]
            lse2 = jnp.sum(jnp.where(lane == 2 * h, st, 0.0), axis=-1, keepdims=True) * LOG2E
            delta = jnp.sum(jnp.where(lane == 2 * h + 1, st, 0.0), axis=-1, keepdims=True)
            parts = [chain(hf, qv, dov, lse2, delta, modes[hf]) for hf in range(2) if modes[hf] is not None]
            dq_ref[rows, :] += parts[0] if len(parts) == 1 else parts[0] + parts[1]

        q_tile(2 * j, (True, None))
        q_tile(2 * j + 1, (False, True))

        def loop_body(i, carry):
            q_tile(2 * (j + 1 + i), (False, False))
            q_tile(2 * (j + 1 + i) + 1, (False, False))
            return carry

        lax.fori_loop(0, nq // 2 - j - 1, loop_body, 0)
        for hf in range(2):
            dk_ref[hf * t:(hf + 1) * t, :] = dk_sc[hf]
            dv_ref[hf * t:(hf + 1) * t, :] = dv_sc[hf]

        if nc:
            @pl.when((h == HEADS - 1) & (j == S // tk - 1))
            def _():
                comm.finish(c_src, c_out, c_sems)

    outs = pl.pallas_call(
        body, name=name, grid=(HEADS, S // tk),
        in_specs=[pl.BlockSpec((S, HQ), lambda h, j: (0, h)),
                  pl.BlockSpec((S, VDIM), lambda h, j: (0, h)),
                  pl.BlockSpec((S, 128), lambda h, j: (0, 0)),
                  pl.BlockSpec((tk, HQ), lambda h, j: (j, h)),
                  pl.BlockSpec((tk, VDIM), lambda h, j: (j, h))] + comm.specs,
        out_specs=[pl.BlockSpec((S, HQ), lambda h, j: (0, h)),
                   pl.BlockSpec((tk, HQ), lambda h, j: (j, h)),
                   pl.BlockSpec((tk, VDIM), lambda h, j: (j, h))] + comm.specs,
        out_shape=[_sds((S, HEADS * HQ)), _sds((S, HEADS * HQ)), _sds((S, ATT_W))] + comm.out_shape,
        scratch_shapes=[pltpu.VMEM((2, t, HQ), F32), pltpu.VMEM((2, t, VDIM), F32)] + comm.scratch,
        compiler_params=_cparams("arbitrary", "arbitrary"),
    )(q, do, stats, k, v, *comm.srcs)
    return outs[0], outs[1], outs[2], outs[3:]


def _att_prep_backward(z, pos, dq, dk, dv, q_g, kv_g, wq_p, w_ukv, rope_rows, name):
    S = z.shape[0]
    tm = ROW_TILE

    def body(z_ref, pos_ref, dq_ref, dk_ref, dv_ref, qg_ref, kvg_ref, wq_ref, wkv_ref, rope_ref,
             dz_ref, qn_ref, dqp_ref, kvn_ref, dkv_ref, dqg_ref, dkvg_ref):
        @pl.when(pl.program_id(0) == 0)
        def _():
            dqg_ref[...] = jnp.zeros(dqg_ref.shape, F32)
            dkvg_ref[...] = jnp.zeros(dkvg_ref.shape, F32)

        zz = z_ref[...]
        ql, kvl = zz[:, 0:Q_RANK], zz[:, Q_RANK:Q_RANK + KV_RANK]
        q_rstd = lax.rsqrt(jnp.mean(ql * ql, axis=-1, keepdims=True) + EPS)
        kv_rstd = lax.rsqrt(jnp.mean(kvl * kvl, axis=-1, keepdims=True) + EPS)
        qhat, kvhat = ql * q_rstd, kvl * kv_rstd
        qg, kvg = qg_ref[...], kvg_ref[...]
        qn_ref[...] = (qhat * qg).astype(BF16)
        kvn_ref[...] = (kvhat * kvg).astype(BF16)
        ct, st = _rope_tables(pos_ref, rope_ref)

        def unrotate(d):
            return d * ct + _swap_halves(d * st)

        dkrot = jnp.zeros((tm, NOPE), F32)
        for h in range(HEADS):
            b = h * HQ
            dqp_ref[:, b:b + NOPE] = dq_ref[:, b:b + NOPE].astype(BF16)
            dqp_ref[:, b + NOPE:b + HQ] = unrotate(dq_ref[:, b + NOPE:b + HQ]).astype(BF16)
            dkv_ref[:, b:b + NOPE] = dk_ref[:, b:b + NOPE].astype(BF16)
            dkv_ref[:, b + NOPE:b + HQ] = dv_ref[:, h * VDIM:(h + 1) * VDIM].astype(BF16)
            dkrot = dkrot + dk_ref[:, b + NOPE:b + HQ]
        dqn = lax.dot_general(dqp_ref[...], wq_ref[...], NT, preferred_element_type=F32)
        dkvn = lax.dot_general(dkv_ref[...], wkv_ref[...], NT, preferred_element_type=F32)
        dqg_ref[...] += jnp.sum(dqn * qhat, axis=0, keepdims=True)
        dkvg_ref[...] += jnp.sum(dkvn * kvhat, axis=0, keepdims=True)
        dqh, dkvh = dqn * qg, dkvn * kvg
        dql = q_rstd * (dqh - qhat * jnp.mean(dqh * qhat, axis=-1, keepdims=True))
        dkvl = kv_rstd * (dkvh - kvhat * jnp.mean(dkvh * kvhat, axis=-1, keepdims=True))
        dz_ref[:, 0:Q_RANK] = dql.astype(BF16)
        dz_ref[:, Q_RANK:Q_RANK + KV_RANK] = dkvl.astype(BF16)
        dz_ref[:, Q_RANK + KV_RANK:] = unrotate(dkrot).astype(BF16)

    W = HEADS * HQ
    return pl.pallas_call(
        body, name=name, grid=(S // tm,),
        in_specs=[_rows(tm, 512, 0), _rows(tm, 1), _rows(tm, W), _rows(tm, W), _rows(tm, ATT_W),
                  _full((1, Q_RANK)), _full((1, KV_RANK)), _full((Q_RANK, W)), _full((KV_RANK, W)), _full((8, 128))],
        out_specs=(_rows(tm, 512), _rows(tm, Q_RANK), _rows(tm, W), _rows(tm, KV_RANK), _rows(tm, W),
                   _full((1, Q_RANK)), _full((1, KV_RANK))),
        out_shape=(_sds((S, 512), BF16), _sds((S, Q_RANK), BF16), _sds((S, W), BF16), _sds((S, KV_RANK), BF16),
                   _sds((S, W), BF16), _sds((1, Q_RANK)), _sds((1, KV_RANK))),
        compiler_params=_cparams("arbitrary"),
    )(z, pos, dq, dk, dv, q_g, kv_g, wq_p, w_ukv, rope_rows)


def _conv_norm_backward(dyc, cv, ln_g, ln_b, w_pw2, name):
    S = cv.shape[0]
    tm = ROW_TILE

    def body(dy_ref, cv_ref, g_ref, be_ref, pw_ref, dcv_ref, sl_ref, dyb_ref, dg_ref, db_ref, dcb_ref):
        @pl.when(pl.program_id(0) == 0)
        def _():
            dg_ref[...] = jnp.zeros(dg_ref.shape, F32)
            db_ref[...] = jnp.zeros(db_ref.shape, F32)
            dcb_ref[...] = jnp.zeros(dcb_ref.shape, F32)

        cv_v = cv_ref[...]
        mu = jnp.mean(cv_v, axis=-1, keepdims=True)
        cc = cv_v - mu
        rstd = lax.rsqrt(jnp.mean(cc * cc, axis=-1, keepdims=True) + EPS)
        nh = cc * rstd
        g = g_ref[...]
        n = nh * g + be_ref[...]
        sil, dsil = _silu_and_grad(n)
        sl_ref[...] = sil.astype(BF16)
        dyb = dy_ref[...].astype(BF16)
        dyb_ref[...] = dyb
        dn = lax.dot_general(dyb, pw_ref[...], NT, preferred_element_type=F32) * dsil
        db_ref[...] += jnp.sum(dn, axis=0, keepdims=True)
        dg_ref[...] += jnp.sum(dn * nh, axis=0, keepdims=True)
        dnh = dn * g
        dcv = rstd * (dnh - jnp.mean(dnh, axis=-1, keepdims=True) - nh * jnp.mean(dnh * nh, axis=-1, keepdims=True))
        dcv_ref[...] = dcv
        dcb_ref[...] += jnp.sum(dcv, axis=0, keepdims=True)

    vec = _full((1, CONV_W))
    return pl.pallas_call(
        body, name=name, grid=(S // tm,),
        in_specs=[_rows(tm, CONV_W), _rows(tm, CONV_W), vec, vec, _full((CONV_W, CONV_W))],
        out_specs=(_rows(tm, CONV_W), _rows(tm, CONV_W), _rows(tm, CONV_W), vec, vec, vec),
        out_shape=(_sds((S, CONV_W)), _sds((S, CONV_W), BF16), _sds((S, CONV_W), BF16),
                   _sds((1, CONV_W)), _sds((1, CONV_W)), _sds((1, CONV_W))),
        compiler_params=_cparams("arbitrary"),
    )(dyc, cv, ln_g, ln_b, w_pw2)


def _conv_backward(z, dcv, conv_w_p, name):
    S = z.shape[0]
    tm = ROW_TILE
    per = tm // HALO
    last_halo = S // HALO - 1

    def body(a_ref, b_ref, ap_ref, bp_ref, d_ref, dn_ref, w_ref, da_ref, db_ref, dw_ref, win, dwin, dw_acc, sh, dsh):
        i = pl.program_id(0)

        @pl.when(i == 0)
        def _():
            dw_acc[...] = jnp.zeros(dw_acc.shape, F32)

        av, bv = a_ref[...], b_ref[...]
        _conv_window(win, ap_ref[...], bp_ref[...], av, bv, i == 0)
        dcur = d_ref[...]
        dwin[0:tm, :] = dcur
        dwin[tm:, :] = jnp.where(i == pl.num_programs(0) - 1, 0.0, dn_ref[...])
        _shifted_copies(win, sh, tm)
        _shifted_copies(dwin, dsh, tm)
        dh = jnp.zeros((tm, CONV_W), F32)
        for kk in range(CONV_K):
            dh = dh + w_ref[kk:kk + 1, :] * _tap(dwin, dsh, CONV_K - 1 - kk, tm)
            prod = dcur * _tap(win, sh, HALO - (CONV_K - 1) + kk, tm)
            dw_acc[kk] += jnp.sum(prod.reshape(tm // 8, 8, CONV_W), axis=0)
        sb = _sigmoid(bv)
        da_ref[...] = (dh * sb).astype(BF16)
        db_ref[...] = (dh * av * sb * (1.0 - sb)).astype(BF16)

        @pl.when(i == pl.num_programs(0) - 1)
        def _():
            dw_ref[...] = jnp.sum(dw_acc[...], axis=1)

    nxt = pl.BlockSpec((HALO, CONV_W), lambda i: (jnp.minimum((i + 1) * per, last_halo), 0))
    return pl.pallas_call(
        body, name=name, grid=(S // tm,),
        in_specs=_conv_in_specs(tm) + [_rows(tm, CONV_W), nxt, _full((HALO, CONV_W))],
        out_specs=(_rows(tm, CONV_W), _rows(tm, CONV_W), _full((HALO, CONV_W))),
        out_shape=(_sds((S, CONV_W), BF16), _sds((S, CONV_W), BF16), _sds((HALO, CONV_W))),
        scratch_shapes=[pltpu.VMEM((tm + HALO, CONV_W), F32), pltpu.VMEM((tm + HALO, CONV_W), F32),
                        pltpu.VMEM((HALO, 8, CONV_W), F32), pltpu.VMEM((7, tm + HALO - 8, CONV_W), F32),
                        pltpu.VMEM((7, tm + HALO - 8, CONV_W), F32)],
        compiler_params=_cparams("arbitrary"),
    )(z, z, z, z, dcv, dcv, conv_w_p)


def _sgu_backward(z, dy, ln_g, ln_b, w_s, bias_full, name):
    S = z.shape[0]
    tm = ROW_TILE

    def body(u_ref, v_ref, dy_ref, g_ref, be_ref, ws_ref, bias_ref, du_ref, dv_ref, dws_ref, dbs_ref, dg_ref, db_ref):
        @pl.when(pl.program_id(0) == 0)
        def _():
            dws_ref[...] = jnp.zeros(dws_ref.shape, F32)
            dbs_ref[...] = jnp.zeros(dbs_ref.shape, F32)
            dg_ref[...] = jnp.zeros(dg_ref.shape, F32)
            db_ref[...] = jnp.zeros(db_ref.shape, F32)

        gmask, tril = _sgu_masks()
        lane = lax.broadcasted_iota(jnp.int32, (1, 128), 1)
        wm = [jnp.where(tril, ws_ref[g], 0.0).astype(BF16) for g in range(SGU_G)]
        gain = g_ref[...]
        for ch in range(tm // SGU_T):
            rows = slice(ch * SGU_T, (ch + 1) * SGU_T)
            gu, dgu, dgv, rstd, nh, vn = _sgu_common(u_ref[rows, :], v_ref[rows, :], g_ref, be_ref)
            vb = vn.astype(BF16)
            sv = bias_ref[...]
            for g in range(SGU_G):
                sv = sv + jnp.where(gmask[g], jnp.dot(wm[g], vb, preferred_element_type=F32), 0.0)
            dyv = dy_ref[rows, :]
            du_ref[rows, :] = (dyv * sv * dgu).astype(BF16)
            dsv = dyv * gu
            dsvb = dsv.astype(BF16)
            dvn = jnp.zeros((SGU_T, SGU_W), F32)
            for g in range(SGU_G):
                dsg = jnp.where(gmask[g], dsv, 0.0)
                dwg = lax.dot_general(dsg.astype(BF16), vb, NT, preferred_element_type=F32)
                dws_ref[g] += jnp.where(tril, dwg, 0.0)
                dvn = dvn + jnp.where(gmask[g], lax.dot_general(wm[g], dsvb, TN, preferred_element_type=F32), 0.0)
                dbs_ref[...] += jnp.where(lane == g, jnp.sum(dsg, axis=-1, keepdims=True), 0.0)
            db_ref[...] += jnp.sum(dvn, axis=0, keepdims=True)
            dg_ref[...] += jnp.sum(dvn * nh, axis=0, keepdims=True)
            dnh = dvn * gain
            dgvv = rstd * (dnh - jnp.mean(dnh, axis=-1, keepdims=True) - nh * jnp.mean(dnh * nh, axis=-1, keepdims=True))
            dv_ref[rows, :] = (dgvv * dgv).astype(BF16)

    vec = _full((1, SGU_W))
    return pl.pallas_call(
        body, name=name, grid=(S // tm,),
        in_specs=[_rows(tm, SGU_W, 7), _rows(tm, SGU_W, 8), _rows(tm, SGU_W), vec, vec,
                  _full((SGU_G, SGU_T, SGU_T)), _full((SGU_T, SGU_W))],
        out_specs=(_rows(tm, SGU_W), _rows(tm, SGU_W), _full((SGU_G, SGU_T, SGU_T)), _full((SGU_T, 128)), vec, vec),
        out_shape=(_sds((S, SGU_W), BF16), _sds((S, SGU_W), BF16), _sds((SGU_G, SGU_T, SGU_T)), _sds((SGU_T, 128)),
                   _sds((1, SGU_W)), _sds((1, SGU_W))),
        compiler_params=_cparams("arbitrary"),
    )(z, z, dy, ln_g, ln_b, w_s, bias_full)


def _inproj_backward(x, dxo, dz_att, dga, dca, dcb, dgc, dsu, dsv, dgs, g_pre, mod, w_in_p, name):
    S = x.shape[0]
    tm = ROW_TILE

    def body(x_ref, dxo_ref, p0, p1, p2, p3, p4, p5, p6, p7, g_ref, mod_ref, w_ref,
             dx_ref, hb_ref, dzb_ref, dmod_ref, dg_ref):
        @pl.when(pl.program_id(0) == 0)
        def _():
            dmod_ref[...] = jnp.zeros(dmod_ref.shape, F32)
            dg_ref[...] = jnp.zeros(dg_ref.shape, F32)

        off = 0
        for piece in (p0, p1, p2, p3, p4, p5, p6, p7):
            wdt = piece.shape[1]
            dzb_ref[:, off:off + wdt] = piece[...]
            off += wdt
        dh = lax.dot_general(dzb_ref[...], w_ref[...], NT, preferred_element_type=F32)
        xv = x_ref[...]
        g = g_ref[...]
        one_scale = 1.0 + mod_ref[1:2, :]
        rstd = lax.rsqrt(jnp.mean(xv * xv, axis=-1, keepdims=True) + EPS)
        xhat = xv * rstd
        xg = xhat * g
        hb_ref[...] = (xg * one_scale + mod_ref[0:1, :]).astype(BF16)
        dmod_ref[0:1, :] += jnp.sum(dh, axis=0, keepdims=True)
        dmod_ref[1:2, :] += jnp.sum(dh * xg, axis=0, keepdims=True)
        dhs = dh * one_scale
        dg_ref[...] += jnp.sum(dhs * xhat, axis=0, keepdims=True)
        dxh = dhs * g
        dx_ref[...] = dxo_ref[...] + rstd * (dxh - xhat * jnp.mean(dxh * xhat, axis=-1, keepdims=True))

    widths = (512, 512, 256, 256, 256, 256, 256, 256)
    return pl.pallas_call(
        body, name=name, grid=(S // tm,),
        in_specs=[_rows(tm, D), _rows(tm, D)] + [_rows(tm, w) for w in widths]
                 + [_full((1, D)), _full((3, D)), _full((D, DZ))],
        out_specs=(_rows(tm, D), _rows(tm, D), _rows(tm, DZ), _full((2, D)), _full((1, D))),
        out_shape=(_sds((S, D)), _sds((S, D), BF16), _sds((S, DZ), BF16), _sds((2, D)), _sds((1, D))),
        compiler_params=_cparams("arbitrary"),
    )(x, dxo, dz_att, dga, dca, dcb, dgc, dsu, dsv, dgs, g_pre, mod, w_in_p)


def _adamw(w, gparts, m, v, name):
    shape = w.shape
    cols = shape[-1]
    rows = int(np.prod(shape[:-1]))
    parts = gparts.shape[0]
    w2, m2, v2 = (a.reshape(rows, cols) for a in (w, m, v))
    g3 = gparts.reshape(parts, rows, cols)
    tr = rows
    for cand in (256, 128):
        if rows > cand and rows % cand == 0:
            tr = cand
            break

    def body(w_ref, g_ref, m_ref, v_ref, go_ref, d_ref, mo_ref, vo_ref):
        g = g_ref[0].astype(F32)
        for p in range(1, parts):
            g = g + g_ref[p].astype(F32)
        wv = w_ref[...]
        mn = ADAM_B1 * m_ref[...] + (1.0 - ADAM_B1) * g
        vn = ADAM_B2 * v_ref[...] + (1.0 - ADAM_B2) * (g * g)
        m_hat = mn / (1.0 - ADAM_B1 ** ADAM_STEP)
        v_hat = vn / (1.0 - ADAM_B2 ** ADAM_STEP)
        go_ref[...] = g
        d_ref[...] = -ADAM_LR * (m_hat / (jnp.sqrt(v_hat) + ADAM_EPS) + ADAM_WD * wv)
        mo_ref[...] = mn
        vo_ref[...] = vn

    blk = pl.BlockSpec((tr, cols), lambda i: (i, 0))
    outs = pl.pallas_call(
        body, name=name, grid=(rows // tr,),
        in_specs=[blk, pl.BlockSpec((parts, tr, cols), lambda i: (0, i, 0)), blk, blk],
        out_specs=(blk, blk, blk, blk),
        out_shape=tuple(_sds((rows, cols)) for _ in range(4)),
        compiler_params=_cparams("parallel"),
    )(w2, g3, m2, v2)
    return tuple(o.reshape(shape) for o in outs)


_GATHERED = ("w_in", "w_out", "w_uq", "w_ukv", "w_pw2", "conv_w")
_COL_SHARDED = ("w_in", "w_uq", "w_ukv", "conv_w")

_SMALL = (("dmod", (3 * D,)), ("g_pre", (D,)), ("g_post", (D,)), ("q_norm_g", (Q_RANK,)),
          ("kv_norm_g", (KV_RANK,)), ("conv_b", (CONV_W,)), ("conv_ln_g", (CONV_W,)),
          ("conv_ln_b", (CONV_W,)), ("sgu_ln_g", (SGU_W,)), ("sgu_ln_b", (SGU_W,)),
          ("w_s", (SGU_G, SGU_T, SGU_T)), ("b_s", (SGU_G, SGU_T)))


def _assemble(name, parts):
    if name in _COL_SHARDED:
        p = jnp.moveaxis(parts, 0, 1)
        return p.reshape(p.shape[0], p.shape[1] * p.shape[2])
    return parts.reshape(parts.shape[0] * parts.shape[1], parts.shape[2])


def _scatter_layout(name, full):
    if name in _COL_SHARDED:
        return jnp.moveaxis(full.reshape(full.shape[0], N_DEV, full.shape[1] // N_DEV), 1, 0)
    return full.reshape(N_DEV, full.shape[0] // N_DEV, full.shape[1])


def kernel(x, c, positions, w_ada, b_ada, g_pre, g_post, w_in, q_norm_g, w_uq, kv_norm_g, w_ukv, conv_w, conv_b, conv_ln_g, conv_ln_b, w_pw2, sgu_ln_g, sgu_ln_b, w_s, b_s, w_out, loss_target, m_w_ada, m_b_ada, m_g_pre, m_g_post, m_w_in, m_q_norm_g, m_w_uq, m_kv_norm_g, m_w_ukv, m_conv_w, m_conv_b, m_conv_ln_g, m_conv_ln_b, m_w_pw2, m_sgu_ln_g, m_sgu_ln_b, m_w_s, m_b_s, m_w_out, v_w_ada, v_b_ada, v_g_pre, v_g_post, v_w_in, v_q_norm_g, v_w_uq, v_kv_norm_g, v_w_ukv, v_conv_w, v_conv_b, v_conv_ln_g, v_conv_ln_b, v_w_pw2, v_sgu_ln_g, v_sgu_ln_b, v_w_s, v_b_s, v_w_out):
    weights = dict(w_ada=w_ada, b_ada=b_ada, g_pre=g_pre, g_post=g_post, w_in=w_in, q_norm_g=q_norm_g, w_uq=w_uq,
                   kv_norm_g=kv_norm_g, w_ukv=w_ukv, conv_w=conv_w, conv_b=conv_b, conv_ln_g=conv_ln_g,
                   conv_ln_b=conv_ln_b, w_pw2=w_pw2, sgu_ln_g=sgu_ln_g, sgu_ln_b=sgu_ln_b, w_s=w_s, b_s=b_s, w_out=w_out)
    m_in = dict(w_ada=m_w_ada, b_ada=m_b_ada, g_pre=m_g_pre, g_post=m_g_post, w_in=m_w_in, q_norm_g=m_q_norm_g,
                w_uq=m_w_uq, kv_norm_g=m_kv_norm_g, w_ukv=m_w_ukv, conv_w=m_conv_w, conv_b=m_conv_b,
                conv_ln_g=m_conv_ln_g, conv_ln_b=m_conv_ln_b, w_pw2=m_w_pw2, sgu_ln_g=m_sgu_ln_g,
                sgu_ln_b=m_sgu_ln_b, w_s=m_w_s, b_s=m_b_s, w_out=m_w_out)
    v_in = dict(w_ada=v_w_ada, b_ada=v_b_ada, g_pre=v_g_pre, g_post=v_g_post, w_in=v_w_in, q_norm_g=v_q_norm_g,
                w_uq=v_w_uq, kv_norm_g=v_kv_norm_g, w_ukv=v_w_ukv, conv_w=v_conv_w, conv_b=v_conv_b,
                conv_ln_g=v_conv_ln_g, conv_ln_b=v_conv_ln_b, w_pw2=v_w_pw2, sgu_ln_g=v_sgu_ln_g,
                sgu_ln_b=v_sgu_ln_b, w_s=v_w_s, b_s=v_b_s, w_out=v_w_out)
    order = list(weights)

    S = x.shape[1]
    me = 4 * lax.axis_index("x") + 2 * lax.axis_index("y") + lax.axis_index("c")
    x0 = x.reshape(S, D)
    target = loss_target.reshape(S, D)
    pos = positions.reshape(S, 1)

    def shards(l):
        return [weights[n][l].astype(BF16) for n in _GATHERED]

    def w_in_operand(part):
        w_in_f = _assemble("w_in", part)
        return jnp.concatenate([w_in_f[:, :ATT_IN], jnp.zeros((D, PAD_IN), BF16), w_in_f[:, ATT_IN:]], axis=1)

    def other_operands(parts):
        full = {n: _assemble(n, p) for n, p in zip(_GATHERED[1:], parts)}
        wq = jnp.pad(full["w_uq"].reshape(Q_RANK, HEADS, QK), ((0, 0), (0, 0), (0, HQ - QK)))
        return dict(w_uq=wq.reshape(Q_RANK, HEADS * HQ), w_ukv=full["w_ukv"], w_pw2=full["w_pw2"], w_out=full["w_out"],
                    conv_w=jnp.pad(full["conv_w"].astype(F32), ((0, HALO - CONV_K), (0, 0))))

    first_w_in, c_parts = _exchange([shards(0)[0], c.reshape(8, D // 8)], [False, False], name="gather_w_in_0")
    lw = [dict(w_in=w_in_operand(first_w_in))] + [None] * (DEPTH - 1)
    c_all = c_parts.reshape(N_DEV, D)

    ada_cols = w_ada.shape[-1]
    b_cols = lax.dynamic_slice_in_dim(b_ada, me * ada_cols, ada_cols, axis=1)
    sc_rows, mod_part = _ada_forward(jnp.pad(c_all, ((0, 8), (0, 0))), w_ada, b_cols)
    mod_recv = _exchange([jnp.moveaxis(mod_part[:, :N_DEV], 1, 0)], [True], name="exchange_mod")[0]
    mod = jnp.moveaxis(mod_recv, 0, 1).reshape(DEPTH, 3, D)

    bias_full = jnp.repeat(jnp.swapaxes(b_s, 1, 2), SGU_GD, axis=2)
    inv_freq = ROPE_THETA ** (-jnp.arange(0, ROPE, 2, dtype=F32) / ROPE)
    zeros32 = jnp.zeros((ROPE // 2,), F32)
    ones32 = jnp.ones((ROPE // 2,), F32)
    rope_rows = jnp.zeros((8, 128), F32)
    rope_rows = rope_rows.at[0].set(jnp.concatenate([inv_freq, inv_freq, zeros32, zeros32]))
    rope_rows = rope_rows.at[1].set(jnp.concatenate([ones32, ones32, zeros32, zeros32]))
    rope_rows = rope_rows.at[2].set(jnp.concatenate([-ones32, ones32, zeros32, zeros32]))

    def vec(a, l):
        return a[l].reshape(1, -1)

    saved = []
    xl = x0
    for l in range(DEPTH):
        w = lw[l]
        late = _Comm(shards(0)[1:], [False] * (len(_GATHERED) - 1)) if l == 0 else None
        z, arrived = _prenorm_inproj(xl, vec(g_pre, l), mod[l], w["w_in"], name=f"prenorm_inproj_{l}", comm=late)
        if late is not None:
            w.update(other_operands(arrived))
        q, k, v = _att_prep(z, pos, vec(q_norm_g, l), vec(kv_norm_g, l), w["w_uq"], w["w_ukv"], rope_rows,
                            name=f"att_prep_{l}")
        ahead = _Comm(shards(l + 1), [False] * len(_GATHERED)) if l + 1 < DEPTH else None
        y_att, lse, arrived = _flash_forward(q, k, v, name=f"flash_forward_{l}", comm=ahead)
        if ahead is not None:
            lw[l + 1] = dict(w_in=w_in_operand(arrived[0]), **other_operands(arrived[1:]))
        cv, y_conv = _conv_forward(z, w["conv_w"], vec(conv_b, l), vec(conv_ln_g, l), vec(conv_ln_b, l),
                                   w["w_pw2"], name=f"conv_forward_{l}")
        y_sgu = _sgu_forward(z, vec(sgu_ln_g, l), vec(sgu_ln_b, l), w_s[l], bias_full[l], name=f"sgu_forward_{l}")
        x_next, y, ycat = _out_proj(xl, z, y_att, y_conv, y_sgu, w["w_out"], vec(g_post, l), mod[l],
                                    name=f"out_proj_{l}")
        saved.append(dict(x=xl, z=z, q=q, k=k, v=v, y_att=y_att, lse=lse, cv=cv, y_conv=y_conv, y_sgu=y_sgu, y=y, ycat=ycat))
        xl = x_next

    loss_part, dx = _loss_head(xl, target)
    loss = lax.psum(loss_part.reshape(()), ("x", "y", "c"))

    spack = _Packer(_SMALL, 8)
    grad_kinds = [True] * len(_GATHERED) + [False]
    received = [None] * DEPTH
    pending = None
    for l in reversed(range(DEPTH)):
        sv = saved[l]
        w = lw[l]
        (dyb, dob, stats, dga, dyc, dgc, dys, dgs, dgate, dgpost) = _out_proj_backward(
            dx, sv["y"], sv["z"], sv["y_att"], sv["y_conv"], sv["y_sgu"], sv["lse"], w["w_out"],
            vec(g_post, l), mod[l], name=f"out_proj_backward_{l}")
        big = dict(w_out=_matmul_tn(sv["ycat"], dyb, name=f"grad_w_out_{l}"))
        riding = _Comm(pending, grad_kinds) if pending is not None else None
        dq, dk, dv, arrived = _flash_backward(sv["q"], sv["k"], sv["v"], dob, stats, name=f"flash_backward_{l}",
                                              comm=riding)
        if riding is not None:
            received[l + 1] = arrived
        dz_att, qn_b, dqp_b, kvn_b, dkv_b, dqg, dkvg = _att_prep_backward(
            sv["z"], pos, dq, dk, dv, vec(q_norm_g, l), vec(kv_norm_g, l), w["w_uq"], w["w_ukv"], rope_rows,
            name=f"att_prep_backward_{l}")
        dwq_p = _matmul_tn(qn_b, dqp_b, name=f"grad_w_uq_{l}")
        big["w_uq"] = dwq_p.reshape(Q_RANK, HEADS, HQ)[:, :, :QK].reshape(Q_RANK, HEADS * QK)
        big["w_ukv"] = _matmul_tn(kvn_b, dkv_b, name=f"grad_w_ukv_{l}")
        dcv, sl_b, dyc_b, dclg, dclb, dcb = _conv_norm_backward(dyc, sv["cv"], vec(conv_ln_g, l), vec(conv_ln_b, l),
                                                              w["w_pw2"], name=f"conv_norm_backward_{l}")
        big["w_pw2"] = _matmul_tn(sl_b, dyc_b, name=f"grad_w_pw2_{l}")
        dca, dcbb, dconvw = _conv_backward(sv["z"], dcv, w["conv_w"], name=f"conv_backward_{l}")
        big["conv_w"] = dconvw[:CONV_K]
        dsu, dsvv, dws, dbs, dslg, dslb = _sgu_backward(sv["z"], dys, vec(sgu_ln_g, l), vec(sgu_ln_b, l), w_s[l],
                                                       bias_full[l], name=f"sgu_backward_{l}")
        dx, h_b, dz_b, dmod2, dgpre = _inproj_backward(
            sv["x"], dx, dz_att, dga, dca, dcbb, dgc, dsu, dsvv, dgs, vec(g_pre, l), mod[l], w["w_in"],
            name=f"inproj_backward_{l}")
        small = dict(dmod=jnp.concatenate([dmod2.reshape(-1), dgate.reshape(-1)]), g_pre=dgpre, g_post=dgpost,
                     q_norm_g=dqg, kv_norm_g=dkvg, conv_b=dcb, conv_ln_g=dclg, conv_ln_b=dclb, sgu_ln_g=dslg,
                     sgu_ln_b=dslb, w_s=dws, b_s=jnp.swapaxes(dbs[:, :SGU_G], 0, 1))
        rest = [_scatter_layout(n, big[n]).astype(BF16) for n in _GATHERED[1:]] + [spack.pack(small, F32)]
        early = _Comm(rest, grad_kinds[1:]) if l == 0 else None
        dwin_p = _matmul_tn(h_b, dz_b, name=f"grad_w_in_{l}", comm=early)
        if early is not None:
            dwin_p, rest_arrived = dwin_p
        dwin = jnp.concatenate([dwin_p[:, :ATT_IN], dwin_p[:, ATT_IN + PAD_IN:]], axis=1)
        pending = [_scatter_layout("w_in", dwin).astype(BF16)] + rest
    grad_x = dx.reshape(1, S, D)
    received[0] = list(_exchange(pending[:1], grad_kinds[:1], name="exchange_grad_w_in_0")) + list(rest_arrived)

    gparts = {n: jnp.stack([received[l][i] for l in range(DEPTH)], axis=1) for i, n in enumerate(_GATHERED)}
    sparts = [spack.unpack(received[l][-1], (N_DEV,)) for l in range(DEPTH)]
    sparts = {n: jnp.stack([sparts[l][n] for l in range(DEPTH)], axis=1) for n, _ in _SMALL}
    dmod_all = sparts["dmod"]
    dmod_cols = lax.dynamic_slice_in_dim(dmod_all, me * ada_cols, ada_cols, axis=2)
    sc_t = jnp.pad(sc_rows[:N_DEV].T, ((0, 0), (0, 128 - N_DEV)))
    dmod_rows = jnp.pad(jnp.moveaxis(dmod_cols, 0, 1), ((0, 0), (0, 128 - N_DEV), (0, 0)))
    gparts["w_ada"] = _ada_backward(sc_t, dmod_rows)[None]
    gparts["b_ada"] = dmod_all
    for n, _ in _SMALL[1:]:
        gparts[n] = sparts[n]

    grads, deltas, new_m, new_v = {}, {}, {}, {}
    for n in order:
        grads[n], deltas[n], new_m[n], new_v[n] = _adamw(weights[n], gparts[n], m_in[n], v_in[n], name=f"adamw_{n}")
    return (loss, grad_x, *[grads[n] for n in order], *[deltas[n] for n in order],
            *[new_m[n] for n in order], *[new_v[n] for n in order])
```

```python
import functools
import math

import numpy as np
import jax
import jax.numpy as jnp
from jax import lax
from jax.experimental import pallas as pl
from jax.experimental.pallas import tpu as pltpu

F32 = jnp.float32
BF16 = jnp.bfloat16

N_DEV = 8
DEPTH = 2
D = 1024
HEADS = 4
NOPE = 128
ROPE = 64
VDIM = 128
QK = NOPE + ROPE
Q_RANK = 256
KV_RANK = 128
ATT_W = HEADS * VDIM
CONV_W = 256
CONV_K = 31
SGU_W = 256
SGU_G = 4
SGU_GD = SGU_W // SGU_G
SGU_T = 128
D_IN = 2496
ATT_IN = Q_RANK + KV_RANK + ROPE
PAD_IN = 64
DZ = D_IN + PAD_IN
HQ = 2 * NOPE
EPS = 1e-6
ROPE_THETA = 10000.0
ATT_SCALE = QK ** -0.5
LOG2E = math.log2(math.e)
EXP2_SCALE = ATT_SCALE * LOG2E
NEG_INF = float("-inf")

ADAM_LR = 0.001
ADAM_B1 = 0.9
ADAM_B2 = 0.999
ADAM_EPS = 1e-08
ADAM_WD = 0.01
ADAM_STEP = 10

VMEM_LIMIT = 56 * 1024 * 1024
ROW_TILE = 512
MATMUL_TN_ROWS = 1024
ATT_TILE = 512
FLASH_UNROLL = 4
HALO = 32
CONV_ROWS = 64
PACK_LANES = 128

MESH = pl.DeviceIdType.MESH
NT = (((1,), (1,)), ((), ()))
TN = (((0,), (0,)), ((), ()))


def _cparams(*sem):
    return pltpu.CompilerParams(dimension_semantics=sem, vmem_limit_bytes=VMEM_LIMIT)


def _sds(shape, dtype=F32):
    return jax.ShapeDtypeStruct(tuple(shape), dtype)


def _rows(tm, width, col=0):
    return pl.BlockSpec((tm, width), lambda i: (i, col))


def _full(shape):
    nd = len(shape)
    return pl.BlockSpec(tuple(shape), lambda *_: (0,) * nd)


def _sigmoid(x):
    return 1.0 / (1.0 + jnp.exp(-x))


def _silu_and_grad(g):
    s = _sigmoid(g)
    return g * s, s * (1.0 + g * (1.0 - s))


def _gelu_and_grad(x):
    cdf = 0.5 * (1.0 + lax.erf(x * (1.0 / math.sqrt(2.0))))
    pdf = jnp.exp(-0.5 * x * x) * (1.0 / math.sqrt(2.0 * math.pi))
    return x * cdf, cdf + x * pdf


def _swap_halves(a):
    lane = lax.broadcasted_iota(jnp.int32, a.shape, 1)
    up = pltpu.roll(a, 32, 1)
    down = pltpu.roll(a, 96, 1)
    return jnp.where(lane < 32, down, jnp.where(lane < 64, up, 0.0))


def _rope_tables(pos_ref, rope_ref):
    ang = pos_ref[...].astype(F32) * rope_ref[0:1, :]
    return jnp.cos(ang) * rope_ref[1:2, :], jnp.sin(ang) * rope_ref[2:3, :]


class _Comm:
    def __init__(self, srcs, kinds):
        self.srcs = list(srcs)
        self.kinds = list(kinds)
        self.n = len(self.srcs)
        self.out_shape = [_sds((N_DEV,) + tuple(s.shape[1:] if k else s.shape), s.dtype)
                          for s, k in zip(self.srcs, self.kinds)]
        self.specs = [pl.BlockSpec(memory_space=pl.ANY)] * self.n
        self.scratch = [pltpu.SemaphoreType.DMA((self.n, N_DEV - 1)), pltpu.SemaphoreType.DMA((self.n, N_DEV - 1)),
                        pltpu.SemaphoreType.DMA((self.n,))] if self.n else []

    def _copies(self, src_refs, out_refs, sems, with_recvs):
        send_sems, recv_sems, local_sems = sems
        x, y, c = lax.axis_index("x"), lax.axis_index("y"), lax.axis_index("c")
        me = 4 * x + 2 * y + c
        local, sends, recvs = [], [], []
        for a in range(self.n):
            def block_for(dest, src_ref=src_refs[a], a2a=self.kinds[a]):
                return src_ref.at[dest] if a2a else src_ref

            local.append(pltpu.make_async_copy(block_for(me), out_refs[a].at[me], local_sems.at[a]))
            for r in range(1, N_DEV):
                px = 1 - x if (r >> 2) & 1 else x
                py = 1 - y if (r >> 1) & 1 else y
                pc = 1 - c if r & 1 else c
                peer = 4 * px + 2 * py + pc
                sends.append(pltpu.make_async_remote_copy(
                    src_ref=block_for(peer), dst_ref=out_refs[a].at[me],
                    send_sem=send_sems.at[a, r - 1], recv_sem=recv_sems.at[a, r - 1],
                    device_id=(px, py, pc), device_id_type=MESH))
                if with_recvs:
                    recvs.append(pltpu.make_async_remote_copy(
                        src_ref=block_for(me), dst_ref=out_refs[a].at[peer],
                        send_sem=send_sems.at[a, r - 1], recv_sem=recv_sems.at[a, r - 1],
                        device_id=(px, py, pc), device_id_type=MESH))
        return local, sends, recvs

    def start(self, src_refs, out_refs, sems):
        local, sends, _ = self._copies(src_refs, out_refs, sems, False)
        for cp in local + sends:
            cp.start()

    def finish(self, src_refs, out_refs, sems):
        local, sends, recvs = self._copies(src_refs, out_refs, sems, True)
        for cp in recvs:
            cp.wait_recv()
        for cp in sends:
            cp.wait_send()
        for cp in local:
            cp.wait()


def _exchange(srcs, kinds, name):
    comm = _Comm(srcs, kinds)
    n = comm.n

    def body(*refs):
        src_refs, out_refs, sems = refs[:n], refs[n:2 * n], refs[2 * n:]
        comm.start(src_refs, out_refs, sems)
        comm.finish(src_refs, out_refs, sems)

    return pl.pallas_call(
        body, name=name, out_shape=comm.out_shape, in_specs=comm.specs, out_specs=comm.specs,
        scratch_shapes=comm.scratch,
    )(*srcs)


class _Packer:
    def __init__(self, entries, row_multiple):
        self.entries = entries
        self.offsets = {}
        off = 0
        for name, shape in entries:
            self.offsets[name] = off
            off += int(np.prod(shape))
        quantum = PACK_LANES * row_multiple
        self.total = -(-off // quantum) * quantum
        self.used = off
        self.rows = self.total // PACK_LANES

    def pack(self, arrays, dtype, lead=()):
        n = len(lead)
        flat = [arrays[name].astype(dtype).reshape(lead + (-1,)) for name, _ in self.entries]
        flat.append(jnp.zeros(lead + (self.total - self.used,), dtype))
        return jnp.concatenate(flat, axis=n).reshape(lead + (self.rows, PACK_LANES))

    def unpack(self, buf, lead=()):
        flat = buf.reshape(lead + (self.total,))
        out = {}
        for name, shape in self.entries:
            o = self.offsets[name]
            out[name] = lax.slice_in_dim(flat, o, o + int(np.prod(shape)), axis=len(lead)).reshape(lead + tuple(shape))
        return out


def _ada_forward(c_rows, w_ada, b_ada_cols):
    cols = w_ada.shape[-1]
    rows = c_rows.shape[0]

    def body(c_ref, w_ref, b_ref, sc_ref, part_ref):
        cv = c_ref[...]
        sc = cv * _sigmoid(cv)
        sc_ref[...] = sc
        scb = sc.astype(BF16)
        for l in range(DEPTH):
            part_ref[l] = jnp.dot(scb, w_ref[l].astype(BF16), preferred_element_type=F32) + b_ref[l:l + 1, :]

    return pl.pallas_call(
        body, name="ada_forward",
        out_shape=(_sds((rows, D)), _sds((DEPTH, rows, cols))),
        compiler_params=pltpu.CompilerParams(vmem_limit_bytes=VMEM_LIMIT),
    )(c_rows, w_ada, b_ada_cols)


def _ada_backward(sc_t, dmod_cols):
    cols = dmod_cols.shape[-1]

    def body(sc_ref, dm_ref, gw_ref):
        scb = sc_ref[...].astype(BF16)
        for l in range(DEPTH):
            gw_ref[l] = jnp.dot(scb, dm_ref[l].astype(BF16), preferred_element_type=F32)

    return pl.pallas_call(
        body, name="ada_backward",
        out_shape=_sds((DEPTH, D, cols)),
        compiler_params=pltpu.CompilerParams(vmem_limit_bytes=VMEM_LIMIT),
    )(sc_t, dmod_cols)


def _riding(comm, nsteps, c_src, c_out, c_sems, where):
    if not comm.n:
        return
    step = 0 if where == "start" else nsteps - 1

    @pl.when(pl.program_id(0) == step)
    def _():
        (comm.start if where == "start" else comm.finish)(c_src, c_out, c_sems)


def _prenorm_inproj(x, g_pre, mod, w_in_p, name, comm=None):
    S = x.shape[0]
    tm = ROW_TILE
    comm = comm or _Comm([], [])
    nc = comm.n

    def body(*refs):
        x_ref, g_ref, mod_ref, w_ref = refs[:4]
        c_src, z_ref, c_out, c_sems = refs[4:4 + nc], refs[4 + nc], refs[5 + nc:5 + 2 * nc], refs[5 + 2 * nc:]
        _riding(comm, S // tm, c_src, c_out, c_sems, "start")
        xv = x_ref[...]
        rstd = lax.rsqrt(jnp.mean(xv * xv, axis=-1, keepdims=True) + EPS)
        h = (xv * rstd * g_ref[...]) * (1.0 + mod_ref[1:2, :]) + mod_ref[0:1, :]
        z_ref[...] = jnp.dot(h.astype(BF16), w_ref[...], preferred_element_type=F32)
        _riding(comm, S // tm, c_src, c_out, c_sems, "finish")

    outs = pl.pallas_call(
        body, name=name, grid=(S // tm,),
        in_specs=[_rows(tm, D), _full((1, D)), _full((3, D)), _full((D, DZ))] + comm.specs,
        out_specs=[_rows(tm, DZ)] + comm.specs, out_shape=[_sds((S, DZ))] + comm.out_shape,
        scratch_shapes=comm.scratch,
        compiler_params=_cparams("arbitrary"),
    )(x, g_pre, mod, w_in_p, *comm.srcs)
    return outs[0], outs[1:]


def _att_prep(z, pos, q_g, kv_g, wq_p, w_ukv, rope_rows, name):
    S = z.shape[0]
    tm = ROW_TILE

    def body(z_ref, pos_ref, qg_ref, kvg_ref, wq_ref, wkv_ref, rope_ref, q_ref, k_ref, v_ref):
        zz = z_ref[...]
        ql, kvl, ka = zz[:, 0:Q_RANK], zz[:, Q_RANK:Q_RANK + KV_RANK], zz[:, Q_RANK + KV_RANK:]
        qn = ql * lax.rsqrt(jnp.mean(ql * ql, axis=-1, keepdims=True) + EPS) * qg_ref[...]
        kvn = kvl * lax.rsqrt(jnp.mean(kvl * kvl, axis=-1, keepdims=True) + EPS) * kvg_ref[...]
        q = jnp.dot(qn.astype(BF16), wq_ref[...], preferred_element_type=F32)
        kv = jnp.dot(kvn.astype(BF16), wkv_ref[...], preferred_element_type=F32)
        ct, st = _rope_tables(pos_ref, rope_ref)
        krot = (ka * ct + _swap_halves(ka) * st).astype(BF16)
        for h in range(HEADS):
            b = h * HQ
            q_ref[:, b:b + NOPE] = q[:, b:b + NOPE].astype(BF16)
            a = q[:, b + NOPE:b + HQ]
            q_ref[:, b + NOPE:b + HQ] = (a * ct + _swap_halves(a) * st).astype(BF16)
            k_ref[:, b:b + NOPE] = kv[:, b:b + NOPE].astype(BF16)
            k_ref[:, b + NOPE:b + HQ] = krot
            v_ref[:, h * VDIM:(h + 1) * VDIM] = kv[:, b + NOPE:b + HQ].astype(BF16)

    return pl.pallas_call(
        body, name=name, grid=(S // tm,),
        in_specs=[_rows(tm, 512, 0), _rows(tm, 1), _full((1, Q_RANK)), _full((1, KV_RANK)),
                  _full((Q_RANK, HEADS * HQ)), _full((KV_RANK, HEADS * HQ)), _full((8, 128))],
        out_specs=(_rows(tm, HEADS * HQ), _rows(tm, HEADS * HQ), _rows(tm, ATT_W)),
        out_shape=(_sds((S, HEADS * HQ), BF16), _sds((S, HEADS * HQ), BF16), _sds((S, ATT_W), BF16)),
        compiler_params=_cparams("parallel"),
    )(z, pos, q_g, kv_g, wq_p, w_ukv, rope_rows)


def _flash_forward(q, k, v, name, comm=None):
    S = q.shape[0]
    t = ATT_TILE
    nq = S // t
    nl = t // 128
    comm = comm or _Comm([], [])
    nc = comm.n

    def body(*refs):
        q_ref, k_ref, v_ref = refs[:3]
        c_src = refs[3:3 + nc]
        o_ref, lse_ref = refs[3 + nc:5 + nc]
        c_out = refs[5 + nc:5 + 2 * nc]
        m_sc, l_sc, acc_sc, s_sc, mp_sc = refs[5 + 2 * nc:10 + 2 * nc]
        c_sems = refs[10 + 2 * nc:]
        if nc:
            @pl.when((pl.program_id(0) == 0) & (pl.program_id(1) == 0))
            def _():
                comm.start(c_src, c_out, c_sems)

        qb = pl.program_id(1)
        m_sc[...] = jnp.full(m_sc.shape, NEG_INF, F32)
        l_sc[...] = jnp.zeros(l_sc.shape, F32)
        acc_sc[...] = jnp.zeros(acc_sc.shape, F32)

        def score_phase(kb, slot, diagonal):
            s = lax.dot_general(q_ref[...], k_ref[pl.ds(pl.multiple_of(kb * t, t), t), :], NT,
                                preferred_element_type=F32)
            if diagonal:
                ri = lax.broadcasted_iota(jnp.int32, (t, t), 0)
                ci = lax.broadcasted_iota(jnp.int32, (t, t), 1)
                s = jnp.where(ci <= ri, s, NEG_INF)
            s_sc[slot] = s
            mp = s[:, 0:128]
            for c in range(1, nl):
                mp = jnp.maximum(mp, s[:, c * 128:(c + 1) * 128])
            mp_sc[slot] = mp

        def sum_phase(kb, slot):
            m_prev = m_sc[...]
            m_new = jnp.maximum(m_prev, jnp.max(mp_sc[slot], axis=-1, keepdims=True))
            alpha = jnp.exp2((m_prev - m_new) * EXP2_SCALE)
            p = jnp.exp2(s_sc[slot] * EXP2_SCALE - jnp.tile(m_new * EXP2_SCALE, (1, nl)))
            lp = alpha * l_sc[...]
            for c in range(nl):
                lp = lp + p[:, c * 128:(c + 1) * 128]
            l_sc[...] = lp
            acc_sc[...] = alpha * acc_sc[...] + jnp.dot(p.astype(BF16), v_ref[pl.ds(pl.multiple_of(kb * t, t), t), :],
                                                        preferred_element_type=F32)
            m_sc[...] = m_new

        def tile_at(pos):
            return jnp.where(pos == 0, qb, pos - 1)

        def run(p0, count, final):
            for i in range(count):
                sum_phase(tile_at(p0 + i), i % 2)
                if not (final and i == count - 1):
                    score_phase(p0 + i, (i + 1) % 2, False)

        score_phase(qb, 0, True)

        def trip(u, carry):
            run(FLASH_UNROLL * u, FLASH_UNROLL, False)
            return carry

        lax.fori_loop(0, qb // FLASH_UNROLL, trip, 0)
        for left in range(FLASH_UNROLL):
            @pl.when(qb % FLASH_UNROLL == left)
            def _():
                run(qb - left, left + 1, True)

        l = jnp.sum(l_sc[...], axis=-1, keepdims=True)
        o_ref[...] = acc_sc[...] / l
        lse_ref[0] = jnp.max(m_sc[...], axis=-1, keepdims=True) * ATT_SCALE + jnp.log(l)

        if nc:
            @pl.when((pl.program_id(0) == HEADS - 1) & (pl.program_id(1) == nq - 1))
            def _():
                comm.finish(c_src, c_out, c_sems)

    outs = pl.pallas_call(
        body, name=name, grid=(HEADS, nq),
        in_specs=[pl.BlockSpec((t, HQ), lambda h, i: (i, h)),
                  pl.BlockSpec((S, HQ), lambda h, i: (0, h)),
                  pl.BlockSpec((S, VDIM), lambda h, i: (0, h))] + comm.specs,
        out_specs=[pl.BlockSpec((t, VDIM), lambda h, i: (i, h)),
                   pl.BlockSpec((1, t, 1), lambda h, i: (h, i, 0))] + comm.specs,
        out_shape=[_sds((S, ATT_W)), _sds((HEADS, S, 1))] + comm.out_shape,
        scratch_shapes=[pltpu.VMEM((t, 128), F32), pltpu.VMEM((t, 128), F32), pltpu.VMEM((t, VDIM), F32),
                        pltpu.VMEM((2, t, t), F32), pltpu.VMEM((2, t, 128), F32)] + comm.scratch,
        compiler_params=_cparams("arbitrary", "arbitrary"),
    )(q, k, v, *comm.srcs)
    return outs[0], outs[1], outs[2:]


def _conv_window(win_ref, a_prev, b_prev, a_cur, b_cur, first):
    hp = a_prev * _sigmoid(b_prev)
    win_ref[0:HALO, :] = jnp.where(first, 0.0, hp)
    win_ref[HALO:, :] = a_cur * _sigmoid(b_cur)


def _shifted_copies(win_ref, sh_ref, tm):
    for b in range(1, 8):
        sh_ref[b - 1] = win_ref[pl.ds(b, tm + HALO - 8), :]


def _tap(win_ref, sh_ref, offset, tm):
    a, b = divmod(offset, 8)
    if b == 0:
        return win_ref[pl.ds(8 * a, tm), :]
    return sh_ref[b - 1, pl.ds(8 * a, tm), :]


def _conv_in_specs(tm):
    per = tm // HALO
    prev = lambda col: pl.BlockSpec((HALO, CONV_W), lambda i: (jnp.maximum(i * per - 1, 0), col))
    return [_rows(tm, CONV_W, 4), _rows(tm, CONV_W, 5), prev(4), prev(5)]


def _conv_forward(z, conv_w_p, conv_b, ln_g, ln_b, w_pw2, name):
    S = z.shape[0]
    tm = ROW_TILE

    def body(a_ref, b_ref, ap_ref, bp_ref, w_ref, cb_ref, g_ref, be_ref, pw_ref, cv_ref, y_ref, win, sh):
        _conv_window(win, ap_ref[...], bp_ref[...], a_ref[...], b_ref[...], pl.program_id(0) == 0)
        _shifted_copies(win, sh, tm)
        for r0 in range(0, tm, CONV_ROWS):
            acc = jnp.zeros((CONV_ROWS, CONV_W), F32)
            for kk in range(CONV_K):
                acc = acc + w_ref[kk:kk + 1, :] * _tap(win, sh, r0 + HALO - (CONV_K - 1) + kk, CONV_ROWS)
            cv_ref[r0:r0 + CONV_ROWS, :] = acc + cb_ref[...]
        cv = cv_ref[...]
        mu = jnp.mean(cv, axis=-1, keepdims=True)
        cc = cv - mu
        rstd = lax.rsqrt(jnp.mean(cc * cc, axis=-1, keepdims=True) + EPS)
        n = cc * rstd * g_ref[...] + be_ref[...]
        sl = n * _sigmoid(n)
        y_ref[...] = jnp.dot(sl.astype(BF16), pw_ref[...], preferred_element_type=F32)

    return pl.pallas_call(
        body, name=name, grid=(S // tm,),
        in_specs=_conv_in_specs(tm) + [_full((HALO, CONV_W)), _full((1, CONV_W)), _full((1, CONV_W)),
                                       _full((1, CONV_W)), _full((CONV_W, CONV_W))],
        out_specs=(_rows(tm, CONV_W), _rows(tm, CONV_W)),
        out_shape=(_sds((S, CONV_W)), _sds((S, CONV_W))),
        scratch_shapes=[pltpu.VMEM((tm + HALO, CONV_W), F32), pltpu.VMEM((7, tm + HALO - 8, CONV_W), F32)],
        compiler_params=_cparams("parallel"),
    )(z, z, z, z, conv_w_p, conv_b, ln_g, ln_b, w_pw2)


def _sgu_common(u, v, g_ref, be_ref):
    gu, dgu = _gelu_and_grad(u)
    gv, dgv = _gelu_and_grad(v)
    mu = jnp.mean(gv, axis=-1, keepdims=True)
    cc = gv - mu
    rstd = lax.rsqrt(jnp.mean(cc * cc, axis=-1, keepdims=True) + EPS)
    nh = cc * rstd
    vn = nh * g_ref[...] + be_ref[...]
    return gu, dgu, dgv, rstd, nh, vn


def _sgu_masks():
    lane_group = lax.broadcasted_iota(jnp.int32, (1, SGU_W), 1) // SGU_GD
    ri = lax.broadcasted_iota(jnp.int32, (SGU_T, SGU_T), 0)
    ci = lax.broadcasted_iota(jnp.int32, (SGU_T, SGU_T), 1)
    return [lane_group == g for g in range(SGU_G)], ci <= ri


def _sgu_forward(z, ln_g, ln_b, w_s, bias_full, name):
    S = z.shape[0]
    tm = ROW_TILE

    def body(u_ref, v_ref, g_ref, be_ref, ws_ref, bias_ref, y_ref):
        gmask, tril = _sgu_masks()
        wm = [jnp.where(tril, ws_ref[g], 0.0).astype(BF16) for g in range(SGU_G)]
        for ch in range(tm // SGU_T):
            rows = slice(ch * SGU_T, (ch + 1) * SGU_T)
            gu, _, _, _, _, vn = _sgu_common(u_ref[rows, :], v_ref[rows, :], g_ref, be_ref)
            vb = vn.astype(BF16)
            sv = bias_ref[...]
            for g in range(SGU_G):
                sv = sv + jnp.where(gmask[g], jnp.dot(wm[g], vb, preferred_element_type=F32), 0.0)
            y_ref[rows, :] = gu * sv

    return pl.pallas_call(
        body, name=name, grid=(S // tm,),
        in_specs=[_rows(tm, SGU_W, 7), _rows(tm, SGU_W, 8), _full((1, SGU_W)), _full((1, SGU_W)),
                  _full((SGU_G, SGU_T, SGU_T)), _full((SGU_T, SGU_W))],
        out_specs=_rows(tm, SGU_W), out_shape=_sds((S, SGU_W)),
        compiler_params=_cparams("parallel"),
    )(z, z, ln_g, ln_b, w_s, bias_full)


def _out_proj(x, z, y_att, y_conv, y_sgu, w_out, g_post, mod, name, target=None):
    S = x.shape[0]
    tm = ROW_TILE
    head = target is not None

    def body(*refs):
        x_ref, ga_ref, gc_ref, gs_ref, ya_ref, yc_ref, ys_ref, w_ref, gp_ref, mod_ref = refs[:10]
        t_ref = refs[10] if head else None
        xn_ref, y_ref, cat_ref = refs[10 + head:13 + head]
        ca = (ya_ref[...] * _silu_and_grad(ga_ref[...])[0]).astype(BF16)
        cc = (yc_ref[...] * _silu_and_grad(gc_ref[...])[0]).astype(BF16)
        cs = (ys_ref[...] * _silu_and_grad(gs_ref[...])[0]).astype(BF16)
        cat_ref[:, 0:ATT_W] = ca
        cat_ref[:, ATT_W:ATT_W + CONV_W] = cc
        cat_ref[:, ATT_W + CONV_W:] = cs
        y = (jnp.dot(ca, w_ref[0:ATT_W, :], preferred_element_type=F32)
             + jnp.dot(cc, w_ref[ATT_W:ATT_W + CONV_W, :], preferred_element_type=F32)
             + jnp.dot(cs, w_ref[ATT_W + CONV_W:, :], preferred_element_type=F32))
        y_ref[...] = y
        rstd = lax.rsqrt(jnp.mean(y * y, axis=-1, keepdims=True) + EPS)
        xn = x_ref[...] + mod_ref[2:3, :] * (y * rstd * gp_ref[...])
        if not head:
            xn_ref[...] = xn
            return
        loss_ref = refs[14]

        @pl.when(pl.program_id(0) == 0)
        def _():
            loss_ref[...] = jnp.zeros(loss_ref.shape, F32)

        err = xn - t_ref[...]
        xn_ref[...] = err * (1.0 / D)
        row = jnp.sum(err * err, axis=-1, keepdims=True) * (1.0 / D)
        loss_ref[...] += 0.5 * jnp.sum(row, axis=0, keepdims=True)

    return pl.pallas_call(
        body, name=name, grid=(S // tm,),
        in_specs=[_rows(tm, D), _rows(tm, 512, 1), _rows(tm, 256, 6), _rows(tm, 256, 9),
                  _rows(tm, ATT_W), _rows(tm, CONV_W), _rows(tm, SGU_W),
                  _full((D, D)), _full((1, D)), _full((3, D))] + ([_rows(tm, D)] if head else []),
        out_specs=[_rows(tm, D), _rows(tm, D), _rows(tm, D)] + ([_full((1, 1))] if head else []),
        out_shape=[_sds((S, D)), _sds((S, D)), _sds((S, D), BF16)] + ([_sds((1, 1))] if head else []),
        compiler_params=_cparams("arbitrary" if head else "parallel"),
    )(x, z, z, z, y_att, y_conv, y_sgu, w_out, g_post, mod, *([target] if head else []))


def _matmul_tn(a, b, name, comm=None):
    S, M = a.shape
    N = b.shape[1]
    bk = min(MATMUL_TN_ROWS, S)
    riding = comm is not None
    comm = comm or _Comm([], [])
    nc = comm.n

    def body(*refs):
        a_ref, b_ref = refs[:2]
        c_src, o_ref, c_out, c_sems = refs[2:2 + nc], refs[2 + nc], refs[3 + nc:3 + 2 * nc], refs[3 + 2 * nc:]
        _riding(comm, S // bk, c_src, c_out, c_sems, "start")

        @pl.when(pl.program_id(0) == 0)
        def _():
            o_ref[...] = jnp.zeros(o_ref.shape, F32)

        o_ref[...] += lax.dot_general(a_ref[...], b_ref[...], TN, preferred_element_type=F32)
        _riding(comm, S // bk, c_src, c_out, c_sems, "finish")

    outs = pl.pallas_call(
        body, name=name, grid=(S // bk,),
        in_specs=[pl.BlockSpec((bk, M), lambda k: (k, 0)), pl.BlockSpec((bk, N), lambda k: (k, 0))] + comm.specs,
        out_specs=[_full((M, N))] + comm.specs, out_shape=[_sds((M, N))] + comm.out_shape,
        scratch_shapes=comm.scratch,
        compiler_params=_cparams("arbitrary"),
    )(a, b, *comm.srcs)
    return (outs[0], outs[1:]) if riding else outs[0]


def _out_proj_backward(dxo, y, z, y_att, y_conv, y_sgu, lse, w_out, g_post, mod, name):
    S = dxo.shape[0]
    tm = ROW_TILE

    def body(dxo_ref, y_ref, ga_ref, gc_ref, gs_ref, ya_ref, yc_ref, ys_ref, lse_ref, w_ref, gp_ref, mod_ref,
             dyb_ref, dob_ref, st_ref, dga_ref, dyc_ref, dgc_ref, dys_ref, dgs_ref, dgate_ref, dgp_ref):
        @pl.when(pl.program_id(0) == 0)
        def _():
            dgate_ref[...] = jnp.zeros(dgate_ref.shape, F32)
            dgp_ref[...] = jnp.zeros(dgp_ref.shape, F32)

        dxo_v = dxo_ref[...]
        yv = y_ref[...]
        gp = gp_ref[...]
        rstd = lax.rsqrt(jnp.mean(yv * yv, axis=-1, keepdims=True) + EPS)
        yhat = yv * rstd
        dgate_ref[...] += jnp.sum(dxo_v * (yhat * gp), axis=0, keepdims=True)
        dr = dxo_v * mod_ref[2:3, :]
        dgp_ref[...] += jnp.sum(dr * yhat, axis=0, keepdims=True)
        dyh = dr * gp
        dy = rstd * (dyh - yhat * jnp.mean(dyh * yhat, axis=-1, keepdims=True))
        dyb = dy.astype(BF16)
        dyb_ref[...] = dyb
        dcat = lax.dot_general(dyb, w_ref[...], NT, preferred_element_type=F32)

        ya = ya_ref[...]
        sil, dsil = _silu_and_grad(ga_ref[...])
        da = dcat[:, 0:ATT_W]
        do = da * sil
        dob_ref[...] = do.astype(BF16)
        dga_ref[...] = (da * ya * dsil).astype(BF16)
        lane = lax.broadcasted_iota(jnp.int32, (1, 128), 1)
        stats = jnp.zeros((tm, 128), F32)
        for h in range(HEADS):
            cols = slice(h * VDIM, (h + 1) * VDIM)
            delta = jnp.sum(do[:, cols] * ya[:, cols], axis=-1, keepdims=True)
            stats = stats + jnp.where(lane == 2 * h, lse_ref[h], 0.0) + jnp.where(lane == 2 * h + 1, delta, 0.0)
        st_ref[...] = stats

        sil, dsil = _silu_and_grad(gc_ref[...])
        dc = dcat[:, ATT_W:ATT_W + CONV_W]
        dyc_ref[...] = dc * sil
        dgc_ref[...] = (dc * yc_ref[...] * dsil).astype(BF16)
        sil, dsil = _silu_and_grad(gs_ref[...])
        dsg = dcat[:, ATT_W + CONV_W:]
        dys_ref[...] = dsg * sil
        dgs_ref[...] = (dsg * ys_ref[...] * dsil).astype(BF16)

    return pl.pallas_call(
        body, name=name, grid=(S // tm,),
        in_specs=[_rows(tm, D), _rows(tm, D), _rows(tm, 512, 1), _rows(tm, 256, 6), _rows(tm, 256, 9),
                  _rows(tm, ATT_W), _rows(tm, CONV_W), _rows(tm, SGU_W),
                  pl.BlockSpec((HEADS, tm, 1), lambda i: (0, i, 0)),
                  _full((D, D)), _full((1, D)), _full((3, D))],
        out_specs=(_rows(tm, D), _rows(tm, ATT_W), _rows(tm, 128), _rows(tm, ATT_W),
                   _rows(tm, CONV_W), _rows(tm, CONV_W), _rows(tm, SGU_W), _rows(tm, SGU_W),
                   _full((1, D)), _full((1, D))),
        out_shape=(_sds((S, D), BF16), _sds((S, ATT_W), BF16), _sds((S, 128)), _sds((S, ATT_W), BF16),
                   _sds((S, CONV_W)), _sds((S, CONV_W), BF16), _sds((S, SGU_W)), _sds((S, SGU_W), BF16),
                   _sds((1, D)), _sds((1, D))),
        compiler_params=_cparams("arbitrary"),
    )(dxo, y, z, z, z, y_att, y_conv, y_sgu, lse, w_out, g_post, mod)


def _flash_backward(q, k, v, do, stats, name, comm=None):
    S = q.shape[0]
    t = ATT_TILE
    tk = 2 * t
    nq = S // t
    comm = comm or _Comm([], [])
    nc = comm.n

    def body(*refs):
        q_ref, do_ref, st_ref, k_ref, v_ref = refs[:5]
        c_src = refs[5:5 + nc]
        dq_ref, dk_ref, dv_ref = refs[5 + nc:8 + nc]
        c_out = refs[8 + nc:8 + 2 * nc]
        dk_sc, dv_sc = refs[8 + 2 * nc:10 + 2 * nc]
        c_sems = refs[10 + 2 * nc:]
        h = pl.program_id(0)
        j = pl.program_id(1)
        if nc:
            @pl.when((h == 0) & (j == 0))
            def _():
                comm.start(c_src, c_out, c_sems)

        @pl.when(j == 0)
        def _():
            dq_ref[...] = jnp.zeros(dq_ref.shape, F32)

        dk_sc[...] = jnp.zeros(dk_sc.shape, F32)
        dv_sc[...] = jnp.zeros(dv_sc.shape, F32)
        lane = lax.broadcasted_iota(jnp.int32, (1, 128), 1)

        def chain(hf, qv, dov, lse2, delta, diagonal):
            kt = k_ref[hf * t:(hf + 1) * t, :]
            s = lax.dot_general(qv, kt, NT, preferred_element_type=F32)
            p = jnp.exp2(s * EXP2_SCALE - lse2)
            if diagonal:
                ri = lax.broadcasted_iota(jnp.int32, (t, t), 0)
                ci = lax.broadcasted_iota(jnp.int32, (t, t), 1)
                p = jnp.where(ci <= ri, p, 0.0)
            dv_sc[hf] += lax.dot_general(p.astype(BF16), dov, TN, preferred_element_type=F32)
            dp = lax.dot_general(dov, v_ref[hf * t:(hf + 1) * t, :], NT, preferred_element_type=F32)
            ds = (p * (dp - delta) * ATT_SCALE).astype(BF16)
            dk_sc[hf] += lax.dot_general(ds, qv, TN, preferred_element_type=F32)
            return jnp.dot(ds, kt, preferred_element_type=F32)

        def q_tile(qb, modes):
            rows = pl.ds(pl.multiple_of(qb * t, t), t)
            qv = q_ref[rows, :]
            dov = do_ref[rows, :]
            st = st_ref[rows, :]
            lse2 = jnp.sum(jnp.where(lane == 2 * h, st, 0.0), axis=-1, keepdims=True) * LOG2E
            delta = jnp.sum(jnp.where(lane == 2 * h + 1, st, 0.0), axis=-1, keepdims=True)
            parts = [chain(hf, qv, dov, lse2, delta, modes[hf]) for hf in range(2) if modes[hf] is not None]
            dq_ref[rows, :] += parts[0] if len(parts) == 1 else parts[0] + parts[1]

        q_tile(2 * j, (True, None))
        q_tile(2 * j + 1, (False, True))

        def loop_body(i, carry):
            q_tile(2 * (j + 1 + i), (False, False))
            q_tile(2 * (j + 1 + i) + 1, (False, False))
            return carry

        lax.fori_loop(0, nq // 2 - j - 1, loop_body, 0)
        for hf in range(2):
            dk_ref[hf * t:(hf + 1) * t, :] = dk_sc[hf]
            dv_ref[hf * t:(hf + 1) * t, :] = dv_sc[hf]

        if nc:
            @pl.when((h == HEADS - 1) & (j == S // tk - 1))
            def _():
                comm.finish(c_src, c_out, c_sems)

    outs = pl.pallas_call(
        body, name=name, grid=(HEADS, S // tk),
        in_specs=[pl.BlockSpec((S, HQ), lambda h, j: (0, h)),
                  pl.BlockSpec((S, VDIM), lambda h, j: (0, h)),
                  pl.BlockSpec((S, 128), lambda h, j: (0, 0)),
                  pl.BlockSpec((tk, HQ), lambda h, j: (j, h)),
                  pl.BlockSpec((tk, VDIM), lambda h, j: (j, h))] + comm.specs,
        out_specs=[pl.BlockSpec((S, HQ), lambda h, j: (0, h)),
                   pl.BlockSpec((tk, HQ), lambda h, j: (j, h)),
                   pl.BlockSpec((tk, VDIM), lambda h, j: (j, h))] + comm.specs,
        out_shape=[_sds((S, HEADS * HQ)), _sds((S, HEADS * HQ)), _sds((S, ATT_W))] + comm.out_shape,
        scratch_shapes=[pltpu.VMEM((2, t, HQ), F32), pltpu.VMEM((2, t, VDIM), F32)] + comm.scratch,
        compiler_params=_cparams("arbitrary", "arbitrary"),
    )(q, do, stats, k, v, *comm.srcs)
    return outs[0], outs[1], outs[2], outs[3:]


def _att_prep_backward(z, pos, dq, dk, dv, q_g, kv_g, wq_p, w_ukv, rope_rows, name):
    S = z.shape[0]
    tm = ROW_TILE

    def body(z_ref, pos_ref, dq_ref, dk_ref, dv_ref, qg_ref, kvg_ref, wq_ref, wkv_ref, rope_ref,
             dz_ref, qn_ref, dqp_ref, kvn_ref, dkv_ref, dqg_ref, dkvg_ref):
        @pl.when(pl.program_id(0) == 0)
        def _():
            dqg_ref[...] = jnp.zeros(dqg_ref.shape, F32)
            dkvg_ref[...] = jnp.zeros(dkvg_ref.shape, F32)

        zz = z_ref[...]
        ql, kvl = zz[:, 0:Q_RANK], zz[:, Q_RANK:Q_RANK + KV_RANK]
        q_rstd = lax.rsqrt(jnp.mean(ql * ql, axis=-1, keepdims=True) + EPS)
        kv_rstd = lax.rsqrt(jnp.mean(kvl * kvl, axis=-1, keepdims=True) + EPS)
        qhat, kvhat = ql * q_rstd, kvl * kv_rstd
        qg, kvg = qg_ref[...], kvg_ref[...]
        qn_ref[...] = (qhat * qg).astype(BF16)
        kvn_ref[...] = (kvhat * kvg).astype(BF16)
        ct, st = _rope_tables(pos_ref, rope_ref)

        def unrotate(d):
            return d * ct + _swap_halves(d * st)

        dkrot = jnp.zeros((tm, NOPE), F32)
        for h in range(HEADS):
            b = h * HQ
            dqp_ref[:, b:b + NOPE] = dq_ref[:, b:b + NOPE].astype(BF16)
            dqp_ref[:, b + NOPE:b + HQ] = unrotate(dq_ref[:, b + NOPE:b + HQ]).astype(BF16)
            dkv_ref[:, b:b + NOPE] = dk_ref[:, b:b + NOPE].astype(BF16)
            dkv_ref[:, b + NOPE:b + HQ] = dv_ref[:, h * VDIM:(h + 1) * VDIM].astype(BF16)
            dkrot = dkrot + dk_ref[:, b + NOPE:b + HQ]
        dqn = lax.dot_general(dqp_ref[...], wq_ref[...], NT, preferred_element_type=F32)
        dkvn = lax.dot_general(dkv_ref[...], wkv_ref[...], NT, preferred_element_type=F32)
        dqg_ref[...] += jnp.sum(dqn * qhat, axis=0, keepdims=True)
        dkvg_ref[...] += jnp.sum(dkvn * kvhat, axis=0, keepdims=True)
        dqh, dkvh = dqn * qg, dkvn * kvg
        dql = q_rstd * (dqh - qhat * jnp.mean(dqh * qhat, axis=-1, keepdims=True))
        dkvl = kv_rstd * (dkvh - kvhat * jnp.mean(dkvh * kvhat, axis=-1, keepdims=True))
        dz_ref[:, 0:Q_RANK] = dql.astype(BF16)
        dz_ref[:, Q_RANK:Q_RANK + KV_RANK] = dkvl.astype(BF16)
        dz_ref[:, Q_RANK + KV_RANK:] = unrotate(dkrot).astype(BF16)

    W = HEADS * HQ
    return pl.pallas_call(
        body, name=name, grid=(S // tm,),
        in_specs=[_rows(tm, 512, 0), _rows(tm, 1), _rows(tm, W), _rows(tm, W), _rows(tm, ATT_W),
                  _full((1, Q_RANK)), _full((1, KV_RANK)), _full((Q_RANK, W)), _full((KV_RANK, W)), _full((8, 128))],
        out_specs=(_rows(tm, 512), _rows(tm, Q_RANK), _rows(tm, W), _rows(tm, KV_RANK), _rows(tm, W),
                   _full((1, Q_RANK)), _full((1, KV_RANK))),
        out_shape=(_sds((S, 512), BF16), _sds((S, Q_RANK), BF16), _sds((S, W), BF16), _sds((S, KV_RANK), BF16),
                   _sds((S, W), BF16), _sds((1, Q_RANK)), _sds((1, KV_RANK))),
        compiler_params=_cparams("arbitrary"),
    )(z, pos, dq, dk, dv, q_g, kv_g, wq_p, w_ukv, rope_rows)


def _conv_norm_backward(dyc, cv, ln_g, ln_b, w_pw2, name):
    S = cv.shape[0]
    tm = ROW_TILE

    def body(dy_ref, cv_ref, g_ref, be_ref, pw_ref, dcv_ref, sl_ref, dyb_ref, dg_ref, db_ref, dcb_ref):
        @pl.when(pl.program_id(0) == 0)
        def _():
            dg_ref[...] = jnp.zeros(dg_ref.shape, F32)
            db_ref[...] = jnp.zeros(db_ref.shape, F32)
            dcb_ref[...] = jnp.zeros(dcb_ref.shape, F32)

        cv_v = cv_ref[...]
        mu = jnp.mean(cv_v, axis=-1, keepdims=True)
        cc = cv_v - mu
        rstd = lax.rsqrt(jnp.mean(cc * cc, axis=-1, keepdims=True) + EPS)
        nh = cc * rstd
        g = g_ref[...]
        n = nh * g + be_ref[...]
        sil, dsil = _silu_and_grad(n)
        sl_ref[...] = sil.astype(BF16)
        dyb = dy_ref[...].astype(BF16)
        dyb_ref[...] = dyb
        dn = lax.dot_general(dyb, pw_ref[...], NT, preferred_element_type=F32) * dsil
        db_ref[...] += jnp.sum(dn, axis=0, keepdims=True)
        dg_ref[...] += jnp.sum(dn * nh, axis=0, keepdims=True)
        dnh = dn * g
        dcv = rstd * (dnh - jnp.mean(dnh, axis=-1, keepdims=True) - nh * jnp.mean(dnh * nh, axis=-1, keepdims=True))
        dcv_ref[...] = dcv
        dcb_ref[...] += jnp.sum(dcv, axis=0, keepdims=True)

    vec = _full((1, CONV_W))
    return pl.pallas_call(
        body, name=name, grid=(S // tm,),
        in_specs=[_rows(tm, CONV_W), _rows(tm, CONV_W), vec, vec, _full((CONV_W, CONV_W))],
        out_specs=(_rows(tm, CONV_W), _rows(tm, CONV_W), _rows(tm, CONV_W), vec, vec, vec),
        out_shape=(_sds((S, CONV_W)), _sds((S, CONV_W), BF16), _sds((S, CONV_W), BF16),
                   _sds((1, CONV_W)), _sds((1, CONV_W)), _sds((1, CONV_W))),
        compiler_params=_cparams("arbitrary"),
    )(dyc, cv, ln_g, ln_b, w_pw2)


def _conv_backward(z, dcv, conv_w_p, name):
    S = z.shape[0]
    tm = ROW_TILE
    per = tm // HALO
    last_halo = S // HALO - 1

    def body(a_ref, b_ref, ap_ref, bp_ref, d_ref, dn_ref, w_ref, da_ref, db_ref, dw_ref, win, dwin, dw_acc, sh, dsh):
        i = pl.program_id(0)

        @pl.when(i == 0)
        def _():
            dw_acc[...] = jnp.zeros(dw_acc.shape, F32)

        av, bv = a_ref[...], b_ref[...]
        _conv_window(win, ap_ref[...], bp_ref[...], av, bv, i == 0)
        dcur = d_ref[...]
        dwin[0:tm, :] = dcur
        dwin[tm:, :] = jnp.where(i == pl.num_programs(0) - 1, 0.0, dn_ref[...])
        _shifted_copies(win, sh, tm)
        _shifted_copies(dwin, dsh, tm)
        for r0 in range(0, tm, CONV_ROWS):
            dchunk = d_ref[r0:r0 + CONV_ROWS, :]
            dh = jnp.zeros((CONV_ROWS, CONV_W), F32)
            for kk in range(CONV_K):
                dh = dh + w_ref[kk:kk + 1, :] * _tap(dwin, dsh, r0 + CONV_K - 1 - kk, CONV_ROWS)
                prod = dchunk * _tap(win, sh, r0 + HALO - (CONV_K - 1) + kk, CONV_ROWS)
                dw_acc[kk] += jnp.sum(prod.reshape(CONV_ROWS // 8, 8, CONV_W), axis=0)
            sb = _sigmoid(b_ref[r0:r0 + CONV_ROWS, :])
            da_ref[r0:r0 + CONV_ROWS, :] = (dh * sb).astype(BF16)
            db_ref[r0:r0 + CONV_ROWS, :] = (dh * a_ref[r0:r0 + CONV_ROWS, :] * sb * (1.0 - sb)).astype(BF16)

        @pl.when(i == pl.num_programs(0) - 1)
        def _():
            dw_ref[...] = jnp.sum(dw_acc[...], axis=1)

    nxt = pl.BlockSpec((HALO, CONV_W), lambda i: (jnp.minimum((i + 1) * per, last_halo), 0))
    return pl.pallas_call(
        body, name=name, grid=(S // tm,),
        in_specs=_conv_in_specs(tm) + [_rows(tm, CONV_W), nxt, _full((HALO, CONV_W))],
        out_specs=(_rows(tm, CONV_W), _rows(tm, CONV_W), _full((HALO, CONV_W))),
        out_shape=(_sds((S, CONV_W), BF16), _sds((S, CONV_W), BF16), _sds((HALO, CONV_W))),
        scratch_shapes=[pltpu.VMEM((tm + HALO, CONV_W), F32), pltpu.VMEM((tm + HALO, CONV_W), F32),
                        pltpu.VMEM((HALO, 8, CONV_W), F32), pltpu.VMEM((7, tm + HALO - 8, CONV_W), F32),
                        pltpu.VMEM((7, tm + HALO - 8, CONV_W), F32)],
        compiler_params=_cparams("arbitrary"),
    )(z, z, z, z, dcv, dcv, conv_w_p)


def _sgu_backward(z, dy, ln_g, ln_b, w_s, bias_full, name):
    S = z.shape[0]
    tm = ROW_TILE

    def body(u_ref, v_ref, dy_ref, g_ref, be_ref, ws_ref, bias_ref, du_ref, dv_ref, dws_ref, dbs_ref, dg_ref, db_ref):
        @pl.when(pl.program_id(0) == 0)
        def _():
            dws_ref[...] = jnp.zeros(dws_ref.shape, F32)
            dbs_ref[...] = jnp.zeros(dbs_ref.shape, F32)
            dg_ref[...] = jnp.zeros(dg_ref.shape, F32)
            db_ref[...] = jnp.zeros(db_ref.shape, F32)

        gmask, tril = _sgu_masks()
        lane = lax.broadcasted_iota(jnp.int32, (1, 128), 1)
        wm = [jnp.where(tril, ws_ref[g], 0.0).astype(BF16) for g in range(SGU_G)]
        gain = g_ref[...]
        for ch in range(tm // SGU_T):
            rows = slice(ch * SGU_T, (ch + 1) * SGU_T)
            gu, dgu, dgv, rstd, nh, vn = _sgu_common(u_ref[rows, :], v_ref[rows, :], g_ref, be_ref)
            vb = vn.astype(BF16)
            sv = bias_ref[...]
            for g in range(SGU_G):
                sv = sv + jnp.where(gmask[g], jnp.dot(wm[g], vb, preferred_element_type=F32), 0.0)
            dyv = dy_ref[rows, :]
            du_ref[rows, :] = (dyv * sv * dgu).astype(BF16)
            dsv = dyv * gu
            dsvb = dsv.astype(BF16)
            dvn = jnp.zeros((SGU_T, SGU_W), F32)
            for g in range(SGU_G):
                dsg = jnp.where(gmask[g], dsv, 0.0)
                dwg = lax.dot_general(dsg.astype(BF16), vb, NT, preferred_element_type=F32)
                dws_ref[g] += jnp.where(tril, dwg, 0.0)
                dvn = dvn + jnp.where(gmask[g], lax.dot_general(wm[g], dsvb, TN, preferred_element_type=F32), 0.0)
                dbs_ref[...] += jnp.where(lane == g, jnp.sum(dsg, axis=-1, keepdims=True), 0.0)
            db_ref[...] += jnp.sum(dvn, axis=0, keepdims=True)
            dg_ref[...] += jnp.sum(dvn * nh, axis=0, keepdims=True)
            dnh = dvn * gain
            dgvv = rstd * (dnh - jnp.mean(dnh, axis=-1, keepdims=True) - nh * jnp.mean(dnh * nh, axis=-1, keepdims=True))
            dv_ref[rows, :] = (dgvv * dgv).astype(BF16)

    vec = _full((1, SGU_W))
    return pl.pallas_call(
        body, name=name, grid=(S // tm,),
        in_specs=[_rows(tm, SGU_W, 7), _rows(tm, SGU_W, 8), _rows(tm, SGU_W), vec, vec,
                  _full((SGU_G, SGU_T, SGU_T)), _full((SGU_T, SGU_W))],
        out_specs=(_rows(tm, SGU_W), _rows(tm, SGU_W), _full((SGU_G, SGU_T, SGU_T)), _full((SGU_T, 128)), vec, vec),
        out_shape=(_sds((S, SGU_W), BF16), _sds((S, SGU_W), BF16), _sds((SGU_G, SGU_T, SGU_T)), _sds((SGU_T, 128)),
                   _sds((1, SGU_W)), _sds((1, SGU_W))),
        compiler_params=_cparams("arbitrary"),
    )(z, z, dy, ln_g, ln_b, w_s, bias_full)


def _inproj_backward(x, dxo, dz_att, dga, dca, dcb, dgc, dsu, dsv, dgs, g_pre, mod, w_in_p, name):
    S = x.shape[0]
    tm = ROW_TILE

    def body(x_ref, dxo_ref, p0, p1, p2, p3, p4, p5, p6, p7, g_ref, mod_ref, w_ref,
             dx_ref, hb_ref, dzb_ref, dmod_ref, dg_ref):
        @pl.when(pl.program_id(0) == 0)
        def _():
            dmod_ref[...] = jnp.zeros(dmod_ref.shape, F32)
            dg_ref[...] = jnp.zeros(dg_ref.shape, F32)

        off = 0
        for piece in (p0, p1, p2, p3, p4, p5, p6, p7):
            wdt = piece.shape[1]
            dzb_ref[:, off:off + wdt] = piece[...]
            off += wdt
        dh = lax.dot_general(dzb_ref[...], w_ref[...], NT, preferred_element_type=F32)
        xv = x_ref[...]
        g = g_ref[...]
        one_scale = 1.0 + mod_ref[1:2, :]
        rstd = lax.rsqrt(jnp.mean(xv * xv, axis=-1, keepdims=True) + EPS)
        xhat = xv * rstd
        xg = xhat * g
        hb_ref[...] = (xg * one_scale + mod_ref[0:1, :]).astype(BF16)
        dmod_ref[0:1, :] += jnp.sum(dh, axis=0, keepdims=True)
        dmod_ref[1:2, :] += jnp.sum(dh * xg, axis=0, keepdims=True)
        dhs = dh * one_scale
        dg_ref[...] += jnp.sum(dhs * xhat, axis=0, keepdims=True)
        dxh = dhs * g
        dx_ref[...] = dxo_ref[...] + rstd * (dxh - xhat * jnp.mean(dxh * xhat, axis=-1, keepdims=True))

    widths = (512, 512, 256, 256, 256, 256, 256, 256)
    return pl.pallas_call(
        body, name=name, grid=(S // tm,),
        in_specs=[_rows(tm, D), _rows(tm, D)] + [_rows(tm, w) for w in widths]
                 + [_full((1, D)), _full((3, D)), _full((D, DZ))],
        out_specs=(_rows(tm, D), _rows(tm, D), _rows(tm, DZ), _full((2, D)), _full((1, D))),
        out_shape=(_sds((S, D)), _sds((S, D), BF16), _sds((S, DZ), BF16), _sds((2, D)), _sds((1, D))),
        compiler_params=_cparams("arbitrary"),
    )(x, dxo, dz_att, dga, dca, dcb, dgc, dsu, dsv, dgs, g_pre, mod, w_in_p)


def _adamw(w, gparts, m, v, name):
    shape = w.shape
    cols = shape[-1]
    rows = int(np.prod(shape[:-1]))
    parts = gparts.shape[0]
    w2, m2, v2 = (a.reshape(rows, cols) for a in (w, m, v))
    g3 = gparts.reshape(parts, rows, cols)
    tr = rows
    for cand in (256, 128):
        if rows > cand and rows % cand == 0:
            tr = cand
            break

    def body(w_ref, g_ref, m_ref, v_ref, go_ref, d_ref, mo_ref, vo_ref):
        g = g_ref[0].astype(F32)
        for p in range(1, parts):
            g = g + g_ref[p].astype(F32)
        wv = w_ref[...]
        mn = ADAM_B1 * m_ref[...] + (1.0 - ADAM_B1) * g
        vn = ADAM_B2 * v_ref[...] + (1.0 - ADAM_B2) * (g * g)
        m_hat = mn / (1.0 - ADAM_B1 ** ADAM_STEP)
        v_hat = vn / (1.0 - ADAM_B2 ** ADAM_STEP)
        go_ref[...] = g
        d_ref[...] = -ADAM_LR * (m_hat / (jnp.sqrt(v_hat) + ADAM_EPS) + ADAM_WD * wv)
        mo_ref[...] = mn
        vo_ref[...] = vn

    blk = pl.BlockSpec((tr, cols), lambda i: (i, 0))
    outs = pl.pallas_call(
        body, name=name, grid=(rows // tr,),
        in_specs=[blk, pl.BlockSpec((parts, tr, cols), lambda i: (0, i, 0)), blk, blk],
        out_specs=(blk, blk, blk, blk),
        out_shape=tuple(_sds((rows, cols)) for _ in range(4)),
        compiler_params=_cparams("parallel"),
    )(w2, g3, m2, v2)
    return tuple(o.reshape(shape) for o in outs)


_GATHERED = ("w_in", "w_out", "w_uq", "w_ukv", "w_pw2", "conv_w")
_COL_SHARDED = ("w_in", "w_uq", "w_ukv", "conv_w")

_SMALL = (("dmod", (3 * D,)), ("g_pre", (D,)), ("g_post", (D,)), ("q_norm_g", (Q_RANK,)),
          ("kv_norm_g", (KV_RANK,)), ("conv_b", (CONV_W,)), ("conv_ln_g", (CONV_W,)),
          ("conv_ln_b", (CONV_W,)), ("sgu_ln_g", (SGU_W,)), ("sgu_ln_b", (SGU_W,)),
          ("w_s", (SGU_G, SGU_T, SGU_T)), ("b_s", (SGU_G, SGU_T)))


def _assemble(name, parts):
    if name in _COL_SHARDED:
        p = jnp.moveaxis(parts, 0, 1)
        return p.reshape(p.shape[0], p.shape[1] * p.shape[2])
    return parts.reshape(parts.shape[0] * parts.shape[1], parts.shape[2])


def _scatter_layout(name, full):
    if name in _COL_SHARDED:
        return jnp.moveaxis(full.reshape(full.shape[0], N_DEV, full.shape[1] // N_DEV), 1, 0)
    return full.reshape(N_DEV, full.shape[0] // N_DEV, full.shape[1])


def kernel(x, c, positions, w_ada, b_ada, g_pre, g_post, w_in, q_norm_g, w_uq, kv_norm_g, w_ukv, conv_w, conv_b, conv_ln_g, conv_ln_b, w_pw2, sgu_ln_g, sgu_ln_b, w_s, b_s, w_out, loss_target, m_w_ada, m_b_ada, m_g_pre, m_g_post, m_w_in, m_q_norm_g, m_w_uq, m_kv_norm_g, m_w_ukv, m_conv_w, m_conv_b, m_conv_ln_g, m_conv_ln_b, m_w_pw2, m_sgu_ln_g, m_sgu_ln_b, m_w_s, m_b_s, m_w_out, v_w_ada, v_b_ada, v_g_pre, v_g_post, v_w_in, v_q_norm_g, v_w_uq, v_kv_norm_g, v_w_ukv, v_conv_w, v_conv_b, v_conv_ln_g, v_conv_ln_b, v_w_pw2, v_sgu_ln_g, v_sgu_ln_b, v_w_s, v_b_s, v_w_out):
    weights = dict(w_ada=w_ada, b_ada=b_ada, g_pre=g_pre, g_post=g_post, w_in=w_in, q_norm_g=q_norm_g, w_uq=w_uq,
                   kv_norm_g=kv_norm_g, w_ukv=w_ukv, conv_w=conv_w, conv_b=conv_b, conv_ln_g=conv_ln_g,
                   conv_ln_b=conv_ln_b, w_pw2=w_pw2, sgu_ln_g=sgu_ln_g, sgu_ln_b=sgu_ln_b, w_s=w_s, b_s=b_s, w_out=w_out)
    m_in = dict(w_ada=m_w_ada, b_ada=m_b_ada, g_pre=m_g_pre, g_post=m_g_post, w_in=m_w_in, q_norm_g=m_q_norm_g,
                w_uq=m_w_uq, kv_norm_g=m_kv_norm_g, w_ukv=m_w_ukv, conv_w=m_conv_w, conv_b=m_conv_b,
                conv_ln_g=m_conv_ln_g, conv_ln_b=m_conv_ln_b, w_pw2=m_w_pw2, sgu_ln_g=m_sgu_ln_g,
                sgu_ln_b=m_sgu_ln_b, w_s=m_w_s, b_s=m_b_s, w_out=m_w_out)
    v_in = dict(w_ada=v_w_ada, b_ada=v_b_ada, g_pre=v_g_pre, g_post=v_g_post, w_in=v_w_in, q_norm_g=v_q_norm_g,
                w_uq=v_w_uq, kv_norm_g=v_kv_norm_g, w_ukv=v_w_ukv, conv_w=v_conv_w, conv_b=v_conv_b,
                conv_ln_g=v_conv_ln_g, conv_ln_b=v_conv_ln_b, w_pw2=v_w_pw2, sgu_ln_g=v_sgu_ln_g,
                sgu_ln_b=v_sgu_ln_b, w_s=v_w_s, b_s=v_b_s, w_out=v_w_out)
    order = list(weights)

    S = x.shape[1]
    me = 4 * lax.axis_index("x") + 2 * lax.axis_index("y") + lax.axis_index("c")
    x0 = x.reshape(S, D)
    target = loss_target.reshape(S, D)
    pos = positions.reshape(S, 1)

    def shards(l):
        return [weights[n][l].astype(BF16) for n in _GATHERED]

    def w_in_operand(part):
        w_in_f = _assemble("w_in", part)
        return jnp.concatenate([w_in_f[:, :ATT_IN], jnp.zeros((D, PAD_IN), BF16), w_in_f[:, ATT_IN:]], axis=1)

    def other_operands(parts):
        full = {n: _assemble(n, p) for n, p in zip(_GATHERED[1:], parts)}
        wq = jnp.pad(full["w_uq"].reshape(Q_RANK, HEADS, QK), ((0, 0), (0, 0), (0, HQ - QK)))
        return dict(w_uq=wq.reshape(Q_RANK, HEADS * HQ), w_ukv=full["w_ukv"], w_pw2=full["w_pw2"], w_out=full["w_out"],
                    conv_w=jnp.pad(full["conv_w"].astype(F32), ((0, HALO - CONV_K), (0, 0))))

    first_w_in, c_parts = _exchange([shards(0)[0], c.reshape(8, D // 8)], [False, False], name="gather_w_in_0")
    lw = [dict(w_in=w_in_operand(first_w_in))] + [None] * (DEPTH - 1)
    c_all = c_parts.reshape(N_DEV, D)

    ada_cols = w_ada.shape[-1]
    b_cols = lax.dynamic_slice_in_dim(b_ada, me * ada_cols, ada_cols, axis=1)
    sc_rows, mod_part = _ada_forward(jnp.pad(c_all, ((0, 8), (0, 0))), w_ada, b_cols)
    mod_recv = _exchange([jnp.moveaxis(mod_part[:, :N_DEV], 1, 0)], [True], name="exchange_mod")[0]
    mod = jnp.moveaxis(mod_recv, 0, 1).reshape(DEPTH, 3, D)

    bias_full = jnp.repeat(jnp.swapaxes(b_s, 1, 2), SGU_GD, axis=2)
    inv_freq = ROPE_THETA ** (-jnp.arange(0, ROPE, 2, dtype=F32) / ROPE)
    zeros32 = jnp.zeros((ROPE // 2,), F32)
    ones32 = jnp.ones((ROPE // 2,), F32)
    rope_rows = jnp.zeros((8, 128), F32)
    rope_rows = rope_rows.at[0].set(jnp.concatenate([inv_freq, inv_freq, zeros32, zeros32]))
    rope_rows = rope_rows.at[1].set(jnp.concatenate([ones32, ones32, zeros32, zeros32]))
    rope_rows = rope_rows.at[2].set(jnp.concatenate([-ones32, ones32, zeros32, zeros32]))

    def vec(a, l):
        return a[l].reshape(1, -1)

    saved = []
    xl = x0
    for l in range(DEPTH):
        w = lw[l]
        late = _Comm(shards(0)[1:], [False] * (len(_GATHERED) - 1)) if l == 0 else None
        z, arrived = _prenorm_inproj(xl, vec(g_pre, l), mod[l], w["w_in"], name=f"prenorm_inproj_{l}", comm=late)
        if late is not None:
            w.update(other_operands(arrived))
        q, k, v = _att_prep(z, pos, vec(q_norm_g, l), vec(kv_norm_g, l), w["w_uq"], w["w_ukv"], rope_rows,
                            name=f"att_prep_{l}")
        ahead = _Comm(shards(l + 1), [False] * len(_GATHERED)) if l + 1 < DEPTH else None
        y_att, lse, arrived = _flash_forward(q, k, v, name=f"flash_forward_{l}", comm=ahead)
        if ahead is not None:
            lw[l + 1] = dict(w_in=w_in_operand(arrived[0]), **other_operands(arrived[1:]))
        cv, y_conv = _conv_forward(z, w["conv_w"], vec(conv_b, l), vec(conv_ln_g, l), vec(conv_ln_b, l),
                                   w["w_pw2"], name=f"conv_forward_{l}")
        y_sgu = _sgu_forward(z, vec(sgu_ln_g, l), vec(sgu_ln_b, l), w_s[l], bias_full[l], name=f"sgu_forward_{l}")
        outs = _out_proj(xl, z, y_att, y_conv, y_sgu, w["w_out"], vec(g_post, l), mod[l], name=f"out_proj_{l}",
                         target=target if l == DEPTH - 1 else None)
        saved.append(dict(x=xl, z=z, q=q, k=k, v=v, y_att=y_att, lse=lse, cv=cv, y_conv=y_conv, y_sgu=y_sgu,
                          y=outs[1], ycat=outs[2]))
        xl = outs[0]

    dx = xl
    loss = lax.psum(outs[3].reshape(()), ("x", "y", "c"))

    spack = _Packer(_SMALL, 8)
    grad_kinds = [True] * len(_GATHERED) + [False]
    received = [None] * DEPTH
    pending = None
    for l in reversed(range(DEPTH)):
        sv = saved[l]
        w = lw[l]
        (dyb, dob, stats, dga, dyc, dgc, dys, dgs, dgate, dgpost) = _out_proj_backward(
            dx, sv["y"], sv["z"], sv["y_att"], sv["y_conv"], sv["y_sgu"], sv["lse"], w["w_out"],
            vec(g_post, l), mod[l], name=f"out_proj_backward_{l}")
        big = dict(w_out=_matmul_tn(sv["ycat"], dyb, name=f"grad_w_out_{l}"))
        riding = _Comm(pending, grad_kinds) if pending is not None else None
        dq, dk, dv, arrived = _flash_backward(sv["q"], sv["k"], sv["v"], dob, stats, name=f"flash_backward_{l}",
                                              comm=riding)
        if riding is not None:
            received[l + 1] = arrived
        dz_att, qn_b, dqp_b, kvn_b, dkv_b, dqg, dkvg = _att_prep_backward(
            sv["z"], pos, dq, dk, dv, vec(q_norm_g, l), vec(kv_norm_g, l), w["w_uq"], w["w_ukv"], rope_rows,
            name=f"att_prep_backward_{l}")
        dwq_p = _matmul_tn(qn_b, dqp_b, name=f"grad_w_uq_{l}")
        big["w_uq"] = dwq_p.reshape(Q_RANK, HEADS, HQ)[:, :, :QK].reshape(Q_RANK, HEADS * QK)
        big["w_ukv"] = _matmul_tn(kvn_b, dkv_b, name=f"grad_w_ukv_{l}")
        dcv, sl_b, dyc_b, dclg, dclb, dcb = _conv_norm_backward(dyc, sv["cv"], vec(conv_ln_g, l), vec(conv_ln_b, l),
                                                              w["w_pw2"], name=f"conv_norm_backward_{l}")
        big["w_pw2"] = _matmul_tn(sl_b, dyc_b, name=f"grad_w_pw2_{l}")
        dca, dcbb, dconvw = _conv_backward(sv["z"], dcv, w["conv_w"], name=f"conv_backward_{l}")
        big["conv_w"] = dconvw[:CONV_K]
        dsu, dsvv, dws, dbs, dslg, dslb = _sgu_backward(sv["z"], dys, vec(sgu_ln_g, l), vec(sgu_ln_b, l), w_s[l],
                                                       bias_full[l], name=f"sgu_backward_{l}")
        dx, h_b, dz_b, dmod2, dgpre = _inproj_backward(
            sv["x"], dx, dz_att, dga, dca, dcbb, dgc, dsu, dsvv, dgs, vec(g_pre, l), mod[l], w["w_in"],
            name=f"inproj_backward_{l}")
        small = dict(dmod=jnp.concatenate([dmod2.reshape(-1), dgate.reshape(-1)]), g_pre=dgpre, g_post=dgpost,
                     q_norm_g=dqg, kv_norm_g=dkvg, conv_b=dcb, conv_ln_g=dclg, conv_ln_b=dclb, sgu_ln_g=dslg,
                     sgu_ln_b=dslb, w_s=dws, b_s=jnp.swapaxes(dbs[:, :SGU_G], 0, 1))
        rest = [_scatter_layout(n, big[n]).astype(BF16) for n in _GATHERED[1:]] + [spack.pack(small, F32)]
        early = _Comm(rest, grad_kinds[1:]) if l == 0 else None
        dwin_p = _matmul_tn(h_b, dz_b, name=f"grad_w_in_{l}", comm=early)
        if early is not None:
            dwin_p, rest_arrived = dwin_p
        dwin = jnp.concatenate([dwin_p[:, :ATT_IN], dwin_p[:, ATT_IN + PAD_IN:]], axis=1)
        pending = [_scatter_layout("w_in", dwin).astype(BF16)] + rest
    grad_x = dx.reshape(1, S, D)
    received[0] = list(_exchange(pending[:1], grad_kinds[:1], name="exchange_grad_w_in_0")) + list(rest_arrived)

    gparts = {n: jnp.stack([received[l][i] for l in range(DEPTH)], axis=1) for i, n in enumerate(_GATHERED)}
    sparts = [spack.unpack(received[l][-1], (N_DEV,)) for l in range(DEPTH)]
    sparts = {n: jnp.stack([sparts[l][n] for l in range(DEPTH)], axis=1) for n, _ in _SMALL}
    dmod_all = sparts["dmod"]
    dmod_cols = lax.dynamic_slice_in_dim(dmod_all, me * ada_cols, ada_cols, axis=2)
    sc_t = jnp.pad(sc_rows[:N_DEV].T, ((0, 0), (0, 128 - N_DEV)))
    dmod_rows = jnp.pad(jnp.moveaxis(dmod_cols, 0, 1), ((0, 0), (0, 128 - N_DEV), (0, 0)))
    gparts["w_ada"] = _ada_backward(sc_t, dmod_rows)[None]
    gparts["b_ada"] = dmod_all
    for n, _ in _SMALL[1:]:
        gparts[n] = sparts[n]

    grads, deltas, new_m, new_v = {}, {}, {}, {}
    for n in order:
        grads[n], deltas[n], new_m[n], new_v[n] = _adamw(weights[n], gparts[n], m_in[n], v_in[n], name=f"adamw_{n}")
    return (loss, grad_x, *[grads[n] for n in order], *[deltas[n] for n in order],
            *[new_m[n] for n in order], *[new_v[n] for n in order])
```

```python
import functools
import math

import numpy as np
import jax
import jax.numpy as jnp
from jax import lax
from jax.experimental import pallas as pl
from jax.experimental.pallas import tpu as pltpu

F32 = jnp.float32
BF16 = jnp.bfloat16

N_DEV = 8
DEPTH = 2
D = 1024
HEADS = 4
NOPE = 128
ROPE = 64
VDIM = 128
QK = NOPE + ROPE
Q_RANK = 256
KV_RANK = 128
ATT_W = HEADS * VDIM
CONV_W = 256
CONV_K = 31
SGU_W = 256
SGU_G = 4
SGU_GD = SGU_W // SGU_G
SGU_T = 128
D_IN = 2496
ATT_IN = Q_RANK + KV_RANK + ROPE
PAD_IN = 64
DZ = D_IN + PAD_IN
HQ = 2 * NOPE
EPS = 1e-6
ROPE_THETA = 10000.0
ATT_SCALE = QK ** -0.5
LOG2E = math.log2(math.e)
EXP2_SCALE = ATT_SCALE * LOG2E
NEG_INF = float("-inf")

ADAM_LR = 0.001
ADAM_B1 = 0.9
ADAM_B2 = 0.999
ADAM_EPS = 1e-08
ADAM_WD = 0.01
ADAM_STEP = 10

VMEM_LIMIT = 56 * 1024 * 1024
ROW_TILE = 512
MATMUL_TN_ROWS = 1024
ATT_TILE = 512
FLASH_UNROLL = 4
HALO = 32
CONV_ROWS = 64
PACK_LANES = 128

MESH = pl.DeviceIdType.MESH
NT = (((1,), (1,)), ((), ()))
TN = (((0,), (0,)), ((), ()))


def _cparams(*sem):
    return pltpu.CompilerParams(dimension_semantics=sem, vmem_limit_bytes=VMEM_LIMIT)


def _sds(shape, dtype=F32):
    return jax.ShapeDtypeStruct(tuple(shape), dtype)


def _rows(tm, width, col=0):
    return pl.BlockSpec((tm, width), lambda i: (i, col))


def _full(shape):
    nd = len(shape)
    return pl.BlockSpec(tuple(shape), lambda *_: (0,) * nd)


def _sigmoid(x):
    return 1.0 / (1.0 + jnp.exp(-x))


def _silu_and_grad(g):
    s = _sigmoid(g)
    return g * s, s * (1.0 + g * (1.0 - s))


def _gelu_and_grad(x):
    cdf = 0.5 * (1.0 + lax.erf(x * (1.0 / math.sqrt(2.0))))
    pdf = jnp.exp(-0.5 * x * x) * (1.0 / math.sqrt(2.0 * math.pi))
    return x * cdf, cdf + x * pdf


def _swap_halves(a):
    lane = lax.broadcasted_iota(jnp.int32, a.shape, 1)
    up = pltpu.roll(a, 32, 1)
    down = pltpu.roll(a, 96, 1)
    return jnp.where(lane < 32, down, jnp.where(lane < 64, up, 0.0))


def _rope_tables(pos_ref, rope_ref):
    ang = pos_ref[...].astype(F32) * rope_ref[0:1, :]
    return jnp.cos(ang) * rope_ref[1:2, :], jnp.sin(ang) * rope_ref[2:3, :]


class _Comm:
    def __init__(self, srcs, kinds):
        self.srcs = list(srcs)
        self.kinds = list(kinds)
        self.n = len(self.srcs)
        self.out_shape = [_sds((N_DEV,) + tuple(s.shape[1:] if k else s.shape), s.dtype)
                          for s, k in zip(self.srcs, self.kinds)]
        self.specs = [pl.BlockSpec(memory_space=pl.ANY)] * self.n
        self.scratch = [pltpu.SemaphoreType.DMA((self.n, N_DEV - 1)), pltpu.SemaphoreType.DMA((self.n, N_DEV - 1)),
                        pltpu.SemaphoreType.DMA((self.n,))] if self.n else []

    def _copies(self, src_refs, out_refs, sems, with_recvs):
        send_sems, recv_sems, local_sems = sems
        x, y, c = lax.axis_index("x"), lax.axis_index("y"), lax.axis_index("c")
        me = 4 * x + 2 * y + c
        local, sends, recvs = [], [], []
        for a in range(self.n):
            def block_for(dest, src_ref=src_refs[a], a2a=self.kinds[a]):
                return src_ref.at[dest] if a2a else src_ref

            local.append(pltpu.make_async_copy(block_for(me), out_refs[a].at[me], local_sems.at[a]))
            for r in range(1, N_DEV):
                px = 1 - x if (r >> 2) & 1 else x
                py = 1 - y if (r >> 1) & 1 else y
                pc = 1 - c if r & 1 else c
                peer = 4 * px + 2 * py + pc
                sends.append(pltpu.make_async_remote_copy(
                    src_ref=block_for(peer), dst_ref=out_refs[a].at[me],
                    send_sem=send_sems.at[a, r - 1], recv_sem=recv_sems.at[a, r - 1],
                    device_id=(px, py, pc), device_id_type=MESH))
                if with_recvs:
                    recvs.append(pltpu.make_async_remote_copy(
                        src_ref=block_for(me), dst_ref=out_refs[a].at[peer],
                        send_sem=send_sems.at[a, r - 1], recv_sem=recv_sems.at[a, r - 1],
                        device_id=(px, py, pc), device_id_type=MESH))
        return local, sends, recvs

    def start(self, src_refs, out_refs, sems):
        local, sends, _ = self._copies(src_refs, out_refs, sems, False)
        for cp in local + sends:
            cp.start()

    def finish(self, src_refs, out_refs, sems):
        local, sends, recvs = self._copies(src_refs, out_refs, sems, True)
        for cp in recvs:
            cp.wait_recv()
        for cp in sends:
            cp.wait_send()
        for cp in local:
            cp.wait()


def _exchange(srcs, kinds, name):
    comm = _Comm(srcs, kinds)
    n = comm.n

    def body(*refs):
        src_refs, out_refs, sems = refs[:n], refs[n:2 * n], refs[2 * n:]
        comm.start(src_refs, out_refs, sems)
        comm.finish(src_refs, out_refs, sems)

    return pl.pallas_call(
        body, name=name, out_shape=comm.out_shape, in_specs=comm.specs, out_specs=comm.specs,
        scratch_shapes=comm.scratch,
    )(*srcs)


def _gather_two_level(srcs, name):
    n = len(srcs)

    def body(*refs):
        src_refs, out_refs = refs[:n], refs[n:2 * n]
        send_sems, recv_sems, local_sems = refs[2 * n:]
        x, y, c = lax.axis_index("x"), lax.axis_index("y"), lax.axis_index("c")
        me, sibling = (x, y, c), (x, y, 1 - c)
        chips = [(1 - x, y), (x, 1 - y), (1 - x, 1 - y)]

        def block(a, dev):
            return out_refs[a].at[4 * dev[0] + 2 * dev[1] + dev[2]]

        def copy(a, k, owner, to, src=None):
            return pltpu.make_async_remote_copy(
                src_ref=block(a, owner) if src is None else src, dst_ref=block(a, owner),
                send_sem=send_sems.at[a, k], recv_sem=recv_sems.at[a, k], device_id=to, device_id_type=MESH)

        mine, first, passed = [], [], []
        for a in range(n):
            mine.append(pltpu.make_async_copy(src_refs[a], block(a, me), local_sems.at[a]))
            first.append(copy(a, 0, me, sibling, src=src_refs[a]))
            first += [copy(a, 1 + j, me, (*chip, c), src=src_refs[a]) for j, chip in enumerate(chips)]
        for cp in mine + first:
            cp.start()
        for a in range(n):
            for j, chip in enumerate(chips):
                copy(a, 1 + j, (*chip, c), me).wait_recv()
                fwd = copy(a, 4 + j, (*chip, c), sibling)
                fwd.start()
                passed.append(fwd)
        for a in range(n):
            copy(a, 0, sibling, me).wait_recv()
            for j, chip in enumerate(chips):
                copy(a, 4 + j, (*chip, 1 - c), me).wait_recv()
        for cp in first + passed:
            cp.wait_send()
        for cp in mine:
            cp.wait()

    return pl.pallas_call(
        body, name=name,
        out_shape=[_sds((N_DEV,) + tuple(s.shape), s.dtype) for s in srcs],
        in_specs=[pl.BlockSpec(memory_space=pl.ANY)] * n, out_specs=[pl.BlockSpec(memory_space=pl.ANY)] * n,
        scratch_shapes=[pltpu.SemaphoreType.DMA((n, N_DEV - 1)), pltpu.SemaphoreType.DMA((n, N_DEV - 1)),
                        pltpu.SemaphoreType.DMA((n,))],
    )(*srcs)


class _Packer:
    def __init__(self, entries, row_multiple):
        self.entries = entries
        self.offsets = {}
        off = 0
        for name, shape in entries:
            self.offsets[name] = off
            off += int(np.prod(shape))
        quantum = PACK_LANES * row_multiple
        self.total = -(-off // quantum) * quantum
        self.used = off
        self.rows = self.total // PACK_LANES

    def pack(self, arrays, dtype, lead=()):
        n = len(lead)
        flat = [arrays[name].astype(dtype).reshape(lead + (-1,)) for name, _ in self.entries]
        flat.append(jnp.zeros(lead + (self.total - self.used,), dtype))
        return jnp.concatenate(flat, axis=n).reshape(lead + (self.rows, PACK_LANES))

    def unpack(self, buf, lead=()):
        flat = buf.reshape(lead + (self.total,))
        out = {}
        for name, shape in self.entries:
            o = self.offsets[name]
            out[name] = lax.slice_in_dim(flat, o, o + int(np.prod(shape)), axis=len(lead)).reshape(lead + tuple(shape))
        return out


def _ada_forward(c_rows, w_ada, b_ada_cols):
    cols = w_ada.shape[-1]
    rows = c_rows.shape[0]

    def body(c_ref, w_ref, b_ref, sc_ref, part_ref):
        cv = c_ref[...]
        sc = cv * _sigmoid(cv)
        sc_ref[...] = sc
        scb = sc.astype(BF16)
        for l in range(DEPTH):
            part_ref[l] = jnp.dot(scb, w_ref[l].astype(BF16), preferred_element_type=F32) + b_ref[l:l + 1, :]

    return pl.pallas_call(
        body, name="ada_forward",
        out_shape=(_sds((rows, D)), _sds((DEPTH, rows, cols))),
        compiler_params=pltpu.CompilerParams(vmem_limit_bytes=VMEM_LIMIT),
    )(c_rows, w_ada, b_ada_cols)


def _ada_backward(sc_t, dmod_cols):
    cols = dmod_cols.shape[-1]

    def body(sc_ref, dm_ref, gw_ref):
        scb = sc_ref[...].astype(BF16)
        for l in range(DEPTH):
            gw_ref[l] = jnp.dot(scb, dm_ref[l].astype(BF16), preferred_element_type=F32)

    return pl.pallas_call(
        body, name="ada_backward",
        out_shape=_sds((DEPTH, D, cols)),
        compiler_params=pltpu.CompilerParams(vmem_limit_bytes=VMEM_LIMIT),
    )(sc_t, dmod_cols)


def _riding(comm, nsteps, c_src, c_out, c_sems, where):
    if not comm.n:
        return
    step = 0 if where == "start" else nsteps - 1

    @pl.when(pl.program_id(0) == step)
    def _():
        (comm.start if where == "start" else comm.finish)(c_src, c_out, c_sems)


def _prenorm_inproj(x, g_pre, mod, w_in_p, name, comm=None):
    S = x.shape[0]
    tm = ROW_TILE
    comm = comm or _Comm([], [])
    nc = comm.n

    def body(*refs):
        x_ref, g_ref, mod_ref, w_ref = refs[:4]
        c_src, z_ref, c_out, c_sems = refs[4:4 + nc], refs[4 + nc], refs[5 + nc:5 + 2 * nc], refs[5 + 2 * nc:]
        _riding(comm, S // tm, c_src, c_out, c_sems, "start")
        xv = x_ref[...]
        rstd = lax.rsqrt(jnp.mean(xv * xv, axis=-1, keepdims=True) + EPS)
        h = (xv * rstd * g_ref[...]) * (1.0 + mod_ref[1:2, :]) + mod_ref[0:1, :]
        z_ref[...] = jnp.dot(h.astype(BF16), w_ref[...], preferred_element_type=F32)
        _riding(comm, S // tm, c_src, c_out, c_sems, "finish")

    outs = pl.pallas_call(
        body, name=name, grid=(S // tm,),
        in_specs=[_rows(tm, D), _full((1, D)), _full((3, D)), _full((D, DZ))] + comm.specs,
        out_specs=[_rows(tm, DZ)] + comm.specs, out_shape=[_sds((S, DZ))] + comm.out_shape,
        scratch_shapes=comm.scratch,
        compiler_params=_cparams("arbitrary"),
    )(x, g_pre, mod, w_in_p, *comm.srcs)
    return outs[0], outs[1:]


def _att_prep(z, pos, q_g, kv_g, wq_p, w_ukv, rope_rows, name):
    S = z.shape[0]
    tm = ROW_TILE

    def body(z_ref, pos_ref, qg_ref, kvg_ref, wq_ref, wkv_ref, rope_ref, q_ref, k_ref, v_ref):
        zz = z_ref[...]
        ql, kvl, ka = zz[:, 0:Q_RANK], zz[:, Q_RANK:Q_RANK + KV_RANK], zz[:, Q_RANK + KV_RANK:]
        qn = ql * lax.rsqrt(jnp.mean(ql * ql, axis=-1, keepdims=True) + EPS) * qg_ref[...]
        kvn = kvl * lax.rsqrt(jnp.mean(kvl * kvl, axis=-1, keepdims=True) + EPS) * kvg_ref[...]
        q = jnp.dot(qn.astype(BF16), wq_ref[...], preferred_element_type=F32)
        kv = jnp.dot(kvn.astype(BF16), wkv_ref[...], preferred_element_type=F32)
        ct, st = _rope_tables(pos_ref, rope_ref)
        krot = (ka * ct + _swap_halves(ka) * st).astype(BF16)
        for h in range(HEADS):
            b = h * HQ
            q_ref[:, b:b + NOPE] = q[:, b:b + NOPE].astype(BF16)
            a = q[:, b + NOPE:b + HQ]
            q_ref[:, b + NOPE:b + HQ] = (a * ct + _swap_halves(a) * st).astype(BF16)
            k_ref[:, b:b + NOPE] = kv[:, b:b + NOPE].astype(BF16)
            k_ref[:, b + NOPE:b + HQ] = krot
            v_ref[:, h * VDIM:(h + 1) * VDIM] = kv[:, b + NOPE:b + HQ].astype(BF16)

    return pl.pallas_call(
        body, name=name, grid=(S // tm,),
        in_specs=[_rows(tm, 512, 0), _rows(tm, 1), _full((1, Q_RANK)), _full((1, KV_RANK)),
                  _full((Q_RANK, HEADS * HQ)), _full((KV_RANK, HEADS * HQ)), _full((8, 128))],
        out_specs=(_rows(tm, HEADS * HQ), _rows(tm, HEADS * HQ), _rows(tm, ATT_W)),
        out_shape=(_sds((S, HEADS * HQ), BF16), _sds((S, HEADS * HQ), BF16), _sds((S, ATT_W), BF16)),
        compiler_params=_cparams("parallel"),
    )(z, pos, q_g, kv_g, wq_p, w_ukv, rope_rows)


def _flash_forward(q, k, v, name, comm=None):
    S = q.shape[0]
    t = ATT_TILE
    nq = S // t
    nl = t // 128
    comm = comm or _Comm([], [])
    nc = comm.n

    def body(*refs):
        q_ref, k_ref, v_ref = refs[:3]
        c_src = refs[3:3 + nc]
        o_ref, lse_ref = refs[3 + nc:5 + nc]
        c_out = refs[5 + nc:5 + 2 * nc]
        m_sc, l_sc, acc_sc, s_sc, mp_sc = refs[5 + 2 * nc:10 + 2 * nc]
        c_sems = refs[10 + 2 * nc:]
        if nc:
            @pl.when((pl.program_id(0) == 0) & (pl.program_id(1) == 0))
            def _():
                comm.start(c_src, c_out, c_sems)

        qb = pl.program_id(1)
        m_sc[...] = jnp.full(m_sc.shape, NEG_INF, F32)
        l_sc[...] = jnp.zeros(l_sc.shape, F32)
        acc_sc[...] = jnp.zeros(acc_sc.shape, F32)

        def score_phase(kb, slot, diagonal):
            s = lax.dot_general(q_ref[...], k_ref[pl.ds(pl.multiple_of(kb * t, t), t), :], NT,
                                preferred_element_type=F32)
            if diagonal:
                ri = lax.broadcasted_iota(jnp.int32, (t, t), 0)
                ci = lax.broadcasted_iota(jnp.int32, (t, t), 1)
                s = jnp.where(ci <= ri, s, NEG_INF)
            s_sc[slot] = s
            mp = s[:, 0:128]
            for c in range(1, nl):
                mp = jnp.maximum(mp, s[:, c * 128:(c + 1) * 128])
            mp_sc[slot] = mp

        def sum_phase(kb, slot):
            m_prev = m_sc[...]
            m_new = jnp.maximum(m_prev, jnp.max(mp_sc[slot], axis=-1, keepdims=True))
            alpha = jnp.exp2((m_prev - m_new) * EXP2_SCALE)
            p = jnp.exp2(s_sc[slot] * EXP2_SCALE - jnp.tile(m_new * EXP2_SCALE, (1, nl)))
            lp = alpha * l_sc[...]
            for c in range(nl):
                lp = lp + p[:, c * 128:(c + 1) * 128]
            l_sc[...] = lp
            acc_sc[...] = alpha * acc_sc[...] + jnp.dot(p.astype(BF16), v_ref[pl.ds(pl.multiple_of(kb * t, t), t), :],
                                                        preferred_element_type=F32)
            m_sc[...] = m_new

        def tile_at(pos):
            return jnp.where(pos == 0, qb, pos - 1)

        def run(p0, count, final):
            for i in range(count):
                sum_phase(tile_at(p0 + i), i % 2)
                if not (final and i == count - 1):
                    score_phase(p0 + i, (i + 1) % 2, False)

        score_phase(qb, 0, True)

        def trip(u, carry):
            run(FLASH_UNROLL * u, FLASH_UNROLL, False)
            return carry

        lax.fori_loop(0, qb // FLASH_UNROLL, trip, 0)
        for left in range(FLASH_UNROLL):
            @pl.when(qb % FLASH_UNROLL == left)
            def _():
                run(qb - left, left + 1, True)

        l = jnp.sum(l_sc[...], axis=-1, keepdims=True)
        o_ref[...] = acc_sc[...] / l
        lse_ref[0] = jnp.max(m_sc[...], axis=-1, keepdims=True) * ATT_SCALE + jnp.log(l)

        if nc:
            @pl.when((pl.program_id(0) == HEADS - 1) & (pl.program_id(1) == nq - 1))
            def _():
                comm.finish(c_src, c_out, c_sems)

    outs = pl.pallas_call(
        body, name=name, grid=(HEADS, nq),
        in_specs=[pl.BlockSpec((t, HQ), lambda h, i: (i, h)),
                  pl.BlockSpec((S, HQ), lambda h, i: (0, h)),
                  pl.BlockSpec((S, VDIM), lambda h, i: (0, h))] + comm.specs,
        out_specs=[pl.BlockSpec((t, VDIM), lambda h, i: (i, h)),
                   pl.BlockSpec((1, t, 1), lambda h, i: (h, i, 0))] + comm.specs,
        out_shape=[_sds((S, ATT_W)), _sds((HEADS, S, 1))] + comm.out_shape,
        scratch_shapes=[pltpu.VMEM((t, 128), F32), pltpu.VMEM((t, 128), F32), pltpu.VMEM((t, VDIM), F32),
                        pltpu.VMEM((2, t, t), F32), pltpu.VMEM((2, t, 128), F32)] + comm.scratch,
        compiler_params=_cparams("arbitrary", "arbitrary"),
    )(q, k, v, *comm.srcs)
    return outs[0], outs[1], outs[2:]


def _conv_window(win_ref, a_prev, b_prev, a_cur, b_cur, first):
    hp = a_prev * _sigmoid(b_prev)
    win_ref[0:HALO, :] = jnp.where(first, 0.0, hp)
    win_ref[HALO:, :] = a_cur * _sigmoid(b_cur)


def _shifted_copies(win_ref, sh_ref, tm):
    for b in range(1, 8):
        sh_ref[b - 1] = win_ref[pl.ds(b, tm + HALO - 8), :]


def _tap(win_ref, sh_ref, offset, tm):
    a, b = divmod(offset, 8)
    if b == 0:
        return win_ref[pl.ds(8 * a, tm), :]
    return sh_ref[b - 1, pl.ds(8 * a, tm), :]


def _conv_in_specs(tm):
    per = tm // HALO
    prev = lambda col: pl.BlockSpec((HALO, CONV_W), lambda i: (jnp.maximum(i * per - 1, 0), col))
    return [_rows(tm, CONV_W, 4), _rows(tm, CONV_W, 5), prev(4), prev(5)]


def _conv_forward(z, conv_w_p, conv_b, ln_g, ln_b, w_pw2, name):
    S = z.shape[0]
    tm = ROW_TILE

    def body(a_ref, b_ref, ap_ref, bp_ref, w_ref, cb_ref, g_ref, be_ref, pw_ref, cv_ref, y_ref, win, sh):
        _conv_window(win, ap_ref[...], bp_ref[...], a_ref[...], b_ref[...], pl.program_id(0) == 0)
        _shifted_copies(win, sh, tm)
        for r0 in range(0, tm, CONV_ROWS):
            acc = jnp.zeros((CONV_ROWS, CONV_W), F32)
            for kk in range(CONV_K):
                acc = acc + w_ref[kk:kk + 1, :] * _tap(win, sh, r0 + HALO - (CONV_K - 1) + kk, CONV_ROWS)
            cv_ref[r0:r0 + CONV_ROWS, :] = acc + cb_ref[...]
        cv = cv_ref[...]
        mu = jnp.mean(cv, axis=-1, keepdims=True)
        cc = cv - mu
        rstd = lax.rsqrt(jnp.mean(cc * cc, axis=-1, keepdims=True) + EPS)
        n = cc * rstd * g_ref[...] + be_ref[...]
        sl = n * _sigmoid(n)
        y_ref[...] = jnp.dot(sl.astype(BF16), pw_ref[...], preferred_element_type=F32)

    return pl.pallas_call(
        body, name=name, grid=(S // tm,),
        in_specs=_conv_in_specs(tm) + [_full((HALO, CONV_W)), _full((1, CONV_W)), _full((1, CONV_W)),
                                       _full((1, CONV_W)), _full((CONV_W, CONV_W))],
        out_specs=(_rows(tm, CONV_W), _rows(tm, CONV_W)),
        out_shape=(_sds((S, CONV_W)), _sds((S, CONV_W))),
        scratch_shapes=[pltpu.VMEM((tm + HALO, CONV_W), F32), pltpu.VMEM((7, tm + HALO - 8, CONV_W), F32)],
        compiler_params=_cparams("parallel"),
    )(z, z, z, z, conv_w_p, conv_b, ln_g, ln_b, w_pw2)


def _sgu_common(u, v, g_ref, be_ref):
    gu, dgu = _gelu_and_grad(u)
    gv, dgv = _gelu_and_grad(v)
    mu = jnp.mean(gv, axis=-1, keepdims=True)
    cc = gv - mu
    rstd = lax.rsqrt(jnp.mean(cc * cc, axis=-1, keepdims=True) + EPS)
    nh = cc * rstd
    vn = nh * g_ref[...] + be_ref[...]
    return gu, dgu, dgv, rstd, nh, vn


def _sgu_masks():
    lane_group = lax.broadcasted_iota(jnp.int32, (1, SGU_W), 1) // SGU_GD
    ri = lax.broadcasted_iota(jnp.int32, (SGU_T, SGU_T), 0)
    ci = lax.broadcasted_iota(jnp.int32, (SGU_T, SGU_T), 1)
    return [lane_group == g for g in range(SGU_G)], ci <= ri


def _sgu_forward(z, ln_g, ln_b, w_s, bias_full, name):
    S = z.shape[0]
    tm = ROW_TILE

    def body(u_ref, v_ref, g_ref, be_ref, ws_ref, bias_ref, y_ref):
        gmask, tril = _sgu_masks()
        wm = [jnp.where(tril, ws_ref[g], 0.0).astype(BF16) for g in range(SGU_G)]
        for ch in range(tm // SGU_T):
            rows = slice(ch * SGU_T, (ch + 1) * SGU_T)
            gu, _, _, _, _, vn = _sgu_common(u_ref[rows, :], v_ref[rows, :], g_ref, be_ref)
            vb = vn.astype(BF16)
            sv = bias_ref[...]
            for g in range(SGU_G):
                sv = sv + jnp.where(gmask[g], jnp.dot(wm[g], vb, preferred_element_type=F32), 0.0)
            y_ref[rows, :] = gu * sv

    return pl.pallas_call(
        body, name=name, grid=(S // tm,),
        in_specs=[_rows(tm, SGU_W, 7), _rows(tm, SGU_W, 8), _full((1, SGU_W)), _full((1, SGU_W)),
                  _full((SGU_G, SGU_T, SGU_T)), _full((SGU_T, SGU_W))],
        out_specs=_rows(tm, SGU_W), out_shape=_sds((S, SGU_W)),
        compiler_params=_cparams("parallel"),
    )(z, z, ln_g, ln_b, w_s, bias_full)


def _out_proj(x, z, y_att, y_conv, y_sgu, w_out, g_post, mod, name, target=None):
    S = x.shape[0]
    tm = ROW_TILE
    head = target is not None

    def body(*refs):
        x_ref, ga_ref, gc_ref, gs_ref, ya_ref, yc_ref, ys_ref, w_ref, gp_ref, mod_ref = refs[:10]
        t_ref = refs[10] if head else None
        xn_ref, y_ref, cat_ref = refs[10 + head:13 + head]
        ca = (ya_ref[...] * _silu_and_grad(ga_ref[...])[0]).astype(BF16)
        cc = (yc_ref[...] * _silu_and_grad(gc_ref[...])[0]).astype(BF16)
        cs = (ys_ref[...] * _silu_and_grad(gs_ref[...])[0]).astype(BF16)
        cat_ref[:, 0:ATT_W] = ca
        cat_ref[:, ATT_W:ATT_W + CONV_W] = cc
        cat_ref[:, ATT_W + CONV_W:] = cs
        y = (jnp.dot(ca, w_ref[0:ATT_W, :], preferred_element_type=F32)
             + jnp.dot(cc, w_ref[ATT_W:ATT_W + CONV_W, :], preferred_element_type=F32)
             + jnp.dot(cs, w_ref[ATT_W + CONV_W:, :], preferred_element_type=F32))
        y_ref[...] = y
        rstd = lax.rsqrt(jnp.mean(y * y, axis=-1, keepdims=True) + EPS)
        xn = x_ref[...] + mod_ref[2:3, :] * (y * rstd * gp_ref[...])
        if not head:
            xn_ref[...] = xn
            return
        loss_ref = refs[14]

        @pl.when(pl.program_id(0) == 0)
        def _():
            loss_ref[...] = jnp.zeros(loss_ref.shape, F32)

        err = xn - t_ref[...]
        xn_ref[...] = err * (1.0 / D)
        row = jnp.sum(err * err, axis=-1, keepdims=True) * (1.0 / D)
        loss_ref[...] += 0.5 * jnp.sum(row, axis=0, keepdims=True)

    return pl.pallas_call(
        body, name=name, grid=(S // tm,),
        in_specs=[_rows(tm, D), _rows(tm, 512, 1), _rows(tm, 256, 6), _rows(tm, 256, 9),
                  _rows(tm, ATT_W), _rows(tm, CONV_W), _rows(tm, SGU_W),
                  _full((D, D)), _full((1, D)), _full((3, D))] + ([_rows(tm, D)] if head else []),
        out_specs=[_rows(tm, D), _rows(tm, D), _rows(tm, D)] + ([_full((1, 1))] if head else []),
        out_shape=[_sds((S, D)), _sds((S, D)), _sds((S, D), BF16)] + ([_sds((1, 1))] if head else []),
        compiler_params=_cparams("arbitrary" if head else "parallel"),
    )(x, z, z, z, y_att, y_conv, y_sgu, w_out, g_post, mod, *([target] if head else []))


def _matmul_tn(a, b, name, comm=None):
    S, M = a.shape
    pieces = list(b) if isinstance(b, (list, tuple)) else [b]
    nb = len(pieces)
    N = sum(p.shape[1] for p in pieces)
    bk = min(MATMUL_TN_ROWS, S)
    riding = comm is not None
    comm = comm or _Comm([], [])
    nc = comm.n

    def body(*refs):
        a_ref, b_refs = refs[0], refs[1:1 + nb]
        c_src, o_ref = refs[1 + nb:1 + nb + nc], refs[1 + nb + nc]
        c_out = refs[2 + nb + nc:2 + nb + 2 * nc]
        rest = refs[2 + nb + 2 * nc:]
        b_sc, c_sems = (rest[0], rest[1:]) if nb > 1 else (None, rest)
        _riding(comm, S // bk, c_src, c_out, c_sems, "start")

        @pl.when(pl.program_id(0) == 0)
        def _():
            o_ref[...] = jnp.zeros(o_ref.shape, F32)

        if nb > 1:
            _assemble_columns(b_sc, b_refs)
            bv = b_sc[...]
        else:
            bv = b_refs[0][...]
        o_ref[...] += lax.dot_general(a_ref[...], bv, TN, preferred_element_type=F32)
        _riding(comm, S // bk, c_src, c_out, c_sems, "finish")

    outs = pl.pallas_call(
        body, name=name, grid=(S // bk,),
        in_specs=[pl.BlockSpec((bk, M), lambda k: (k, 0))]
                 + [pl.BlockSpec((bk, p.shape[1]), lambda k: (k, 0)) for p in pieces] + comm.specs,
        out_specs=[_full((M, N))] + comm.specs, out_shape=[_sds((M, N))] + comm.out_shape,
        scratch_shapes=([pltpu.VMEM((bk, N), BF16)] if nb > 1 else []) + comm.scratch,
        compiler_params=_cparams("arbitrary"),
    )(a, *pieces, *comm.srcs)
    return (outs[0], outs[1:]) if riding else outs[0]


def _out_proj_backward(dxo, y, z, y_att, y_conv, y_sgu, lse, w_out, g_post, mod, name):
    S = dxo.shape[0]
    tm = ROW_TILE

    def body(dxo_ref, y_ref, ga_ref, gc_ref, gs_ref, ya_ref, yc_ref, ys_ref, lse_ref, w_ref, gp_ref, mod_ref,
             dyb_ref, dob_ref, st_ref, dga_ref, dyc_ref, dgc_ref, dys_ref, dgs_ref, dgate_ref, dgp_ref):
        @pl.when(pl.program_id(0) == 0)
        def _():
            dgate_ref[...] = jnp.zeros(dgate_ref.shape, F32)
            dgp_ref[...] = jnp.zeros(dgp_ref.shape, F32)

        dxo_v = dxo_ref[...]
        yv = y_ref[...]
        gp = gp_ref[...]
        rstd = lax.rsqrt(jnp.mean(yv * yv, axis=-1, keepdims=True) + EPS)
        yhat = yv * rstd
        dgate_ref[...] += jnp.sum(dxo_v * (yhat * gp), axis=0, keepdims=True)
        dr = dxo_v * mod_ref[2:3, :]
        dgp_ref[...] += jnp.sum(dr * yhat, axis=0, keepdims=True)
        dyh = dr * gp
        dy = rstd * (dyh - yhat * jnp.mean(dyh * yhat, axis=-1, keepdims=True))
        dyb = dy.astype(BF16)
        dyb_ref[...] = dyb
        dcat = lax.dot_general(dyb, w_ref[...], NT, preferred_element_type=F32)

        ya = ya_ref[...]
        sil, dsil = _silu_and_grad(ga_ref[...])
        da = dcat[:, 0:ATT_W]
        do = da * sil
        dob_ref[...] = do.astype(BF16)
        dga_ref[...] = (da * ya * dsil).astype(BF16)
        lane = lax.broadcasted_iota(jnp.int32, (1, 128), 1)
        stats = jnp.zeros((tm, 128), F32)
        for h in range(HEADS):
            cols = slice(h * VDIM, (h + 1) * VDIM)
            delta = jnp.sum(do[:, cols] * ya[:, cols], axis=-1, keepdims=True)
            stats = stats + jnp.where(lane == 2 * h, lse_ref[h], 0.0) + jnp.where(lane == 2 * h + 1, delta, 0.0)
        st_ref[...] = stats

        sil, dsil = _silu_and_grad(gc_ref[...])
        dc = dcat[:, ATT_W:ATT_W + CONV_W]
        dyc_ref[...] = dc * sil
        dgc_ref[...] = (dc * yc_ref[...] * dsil).astype(BF16)
        sil, dsil = _silu_and_grad(gs_ref[...])
        dsg = dcat[:, ATT_W + CONV_W:]
        dys_ref[...] = dsg * sil
        dgs_ref[...] = (dsg * ys_ref[...] * dsil).astype(BF16)

    return pl.pallas_call(
        body, name=name, grid=(S // tm,),
        in_specs=[_rows(tm, D), _rows(tm, D), _rows(tm, 512, 1), _rows(tm, 256, 6), _rows(tm, 256, 9),
                  _rows(tm, ATT_W), _rows(tm, CONV_W), _rows(tm, SGU_W),
                  pl.BlockSpec((HEADS, tm, 1), lambda i: (0, i, 0)),
                  _full((D, D)), _full((1, D)), _full((3, D))],
        out_specs=(_rows(tm, D), _rows(tm, ATT_W), _rows(tm, 128), _rows(tm, ATT_W),
                   _rows(tm, CONV_W), _rows(tm, CONV_W), _rows(tm, SGU_W), _rows(tm, SGU_W),
                   _full((1, D)), _full((1, D))),
        out_shape=(_sds((S, D), BF16), _sds((S, ATT_W), BF16), _sds((S, 128)), _sds((S, ATT_W), BF16),
                   _sds((S, CONV_W)), _sds((S, CONV_W), BF16), _sds((S, SGU_W)), _sds((S, SGU_W), BF16),
                   _sds((1, D)), _sds((1, D))),
        compiler_params=_cparams("arbitrary"),
    )(dxo, y, z, z, z, y_att, y_conv, y_sgu, lse, w_out, g_post, mod)


def _flash_backward(q, k, v, do, stats, name, comm=None):
    S = q.shape[0]
    t = ATT_TILE
    tk = 2 * t
    nq = S // t
    comm = comm or _Comm([], [])
    nc = comm.n

    def body(*refs):
        q_ref, do_ref, st_ref, k_ref, v_ref = refs[:5]
        c_src = refs[5:5 + nc]
        dq_ref, dk_ref, dv_ref = refs[5 + nc:8 + nc]
        c_out = refs[8 + nc:8 + 2 * nc]
        dk_sc, dv_sc = refs[8 + 2 * nc:10 + 2 * nc]
        c_sems = refs[10 + 2 * nc:]
        h = pl.program_id(0)
        j = pl.program_id(1)
        if nc:
            @pl.when((h == 0) & (j == 0))
            def _():
                comm.start(c_src, c_out, c_sems)

        @pl.when(j == 0)
        def _():
            dq_ref[...] = jnp.zeros(dq_ref.shape, F32)

        dk_sc[...] = jnp.zeros(dk_sc.shape, F32)
        dv_sc[...] = jnp.zeros(dv_sc.shape, F32)
        lane = lax.broadcasted_iota(jnp.int32, (1, 128), 1)

        def chain(hf, qv, dov, lse2, delta, diagonal):
            kt = k_ref[hf * t:(hf + 1) * t, :]
            s = lax.dot_general(qv, kt, NT, preferred_element_type=F32)
            p = jnp.exp2(s * EXP2_SCALE - lse2)
            if diagonal:
                ri = lax.broadcasted_iota(jnp.int32, (t, t), 0)
                ci = lax.broadcasted_iota(jnp.int32, (t, t), 1)
                p = jnp.where(ci <= ri, p, 0.0)
            dv_sc[hf] += lax.dot_general(p.astype(BF16), dov, TN, preferred_element_type=F32)
            dp = lax.dot_general(dov, v_ref[hf * t:(hf + 1) * t, :], NT, preferred_element_type=F32)
            ds = (p * (dp - delta) * ATT_SCALE).astype(BF16)
            dk_sc[hf] += lax.dot_general(ds, qv, TN, preferred_element_type=F32)
            return jnp.dot(ds, kt, preferred_element_type=F32)

        def q_tile(qb, modes):
            rows = pl.ds(pl.multiple_of(qb * t, t), t)
            qv = q_ref[rows, :]
            dov = do_ref[rows, :]
            st = st_ref[rows, :]
            lse2 = jnp.sum(jnp.where(lane == 2 * h, st, 0.0), axis=-1, keepdims=True) * LOG2E
            delta = jnp.sum(jnp.where(lane == 2 * h + 1, st, 0.0), axis=-1, keepdims=True)
            parts = [chain(hf, qv, dov, lse2, delta, modes[hf]) for hf in range(2) if modes[hf] is not None]
            dq_ref[rows, :] += parts[0] if len(parts) == 1 else parts[0] + parts[1]

        q_tile(2 * j, (True, None))
        q_tile(2 * j + 1, (False, True))

        def loop_body(i, carry):
            q_tile(2 * (j + 1 + i), (False, False))
            q_tile(2 * (j + 1 + i) + 1, (False, False))
            return carry

        lax.fori_loop(0, nq // 2 - j - 1, loop_body, 0)
        for hf in range(2):
            dk_ref[hf * t:(hf + 1) * t, :] = dk_sc[hf]
            dv_ref[hf * t:(hf + 1) * t, :] = dv_sc[hf]

        if nc:
            @pl.when((h == HEADS - 1) & (j == S // tk - 1))
            def _():
                comm.finish(c_src, c_out, c_sems)

    outs = pl.pallas_call(
        body, name=name, grid=(HEADS, S // tk),
        in_specs=[pl.BlockSpec((S, HQ), lambda h, j: (0, h)),
                  pl.BlockSpec((S, VDIM), lambda h, j: (0, h)),
                  pl.BlockSpec((S, 128), lambda h, j: (0, 0)),
                  pl.BlockSpec((tk, HQ), lambda h, j: (j, h)),
                  pl.BlockSpec((tk, VDIM), lambda h, j: (j, h))] + comm.specs,
        out_specs=[pl.BlockSpec((S, HQ), lambda h, j: (0, h)),
                   pl.BlockSpec((tk, HQ), lambda h, j: (j, h)),
                   pl.BlockSpec((tk, VDIM), lambda h, j: (j, h))] + comm.specs,
        out_shape=[_sds((S, HEADS * HQ)), _sds((S, HEADS * HQ)), _sds((S, ATT_W))] + comm.out_shape,
        scratch_shapes=[pltpu.VMEM((2, t, HQ), F32), pltpu.VMEM((2, t, VDIM), F32)] + comm.scratch,
        compiler_params=_cparams("arbitrary", "arbitrary"),
    )(q, do, stats, k, v, *comm.srcs)
    return outs[0], outs[1], outs[2], outs[3:]


def _att_prep_backward(z, pos, dq, dk, dv, q_g, kv_g, wq_p, w_ukv, rope_rows, name):
    S = z.shape[0]
    tm = ROW_TILE

    def body(z_ref, pos_ref, dq_ref, dk_ref, dv_ref, qg_ref, kvg_ref, wq_ref, wkv_ref, rope_ref,
             dz_ref, qn_ref, dqp_ref, kvn_ref, dkv_ref, dqg_ref, dkvg_ref):
        @pl.when(pl.program_id(0) == 0)
        def _():
            dqg_ref[...] = jnp.zeros(dqg_ref.shape, F32)
            dkvg_ref[...] = jnp.zeros(dkvg_ref.shape, F32)

        zz = z_ref[...]
        ql, kvl = zz[:, 0:Q_RANK], zz[:, Q_RANK:Q_RANK + KV_RANK]
        q_rstd = lax.rsqrt(jnp.mean(ql * ql, axis=-1, keepdims=True) + EPS)
        kv_rstd = lax.rsqrt(jnp.mean(kvl * kvl, axis=-1, keepdims=True) + EPS)
        qhat, kvhat = ql * q_rstd, kvl * kv_rstd
        qg, kvg = qg_ref[...], kvg_ref[...]
        qn_ref[...] = (qhat * qg).astype(BF16)
        kvn_ref[...] = (kvhat * kvg).astype(BF16)
        ct, st = _rope_tables(pos_ref, rope_ref)

        def unrotate(d):
            return d * ct + _swap_halves(d * st)

        dkrot = jnp.zeros((tm, NOPE), F32)
        for h in range(HEADS):
            b = h * HQ
            dqp_ref[:, b:b + NOPE] = dq_ref[:, b:b + NOPE].astype(BF16)
            dqp_ref[:, b + NOPE:b + HQ] = unrotate(dq_ref[:, b + NOPE:b + HQ]).astype(BF16)
            dkv_ref[:, b:b + NOPE] = dk_ref[:, b:b + NOPE].astype(BF16)
            dkv_ref[:, b + NOPE:b + HQ] = dv_ref[:, h * VDIM:(h + 1) * VDIM].astype(BF16)
            dkrot = dkrot + dk_ref[:, b + NOPE:b + HQ]
        dqn = lax.dot_general(dqp_ref[...], wq_ref[...], NT, preferred_element_type=F32)
        dkvn = lax.dot_general(dkv_ref[...], wkv_ref[...], NT, preferred_element_type=F32)
        dqg_ref[...] += jnp.sum(dqn * qhat, axis=0, keepdims=True)
        dkvg_ref[...] += jnp.sum(dkvn * kvhat, axis=0, keepdims=True)
        dqh, dkvh = dqn * qg, dkvn * kvg
        dql = q_rstd * (dqh - qhat * jnp.mean(dqh * qhat, axis=-1, keepdims=True))
        dkvl = kv_rstd * (dkvh - kvhat * jnp.mean(dkvh * kvhat, axis=-1, keepdims=True))
        dz_ref[:, 0:Q_RANK] = dql.astype(BF16)
        dz_ref[:, Q_RANK:Q_RANK + KV_RANK] = dkvl.astype(BF16)
        dz_ref[:, Q_RANK + KV_RANK:] = unrotate(dkrot).astype(BF16)

    W = HEADS * HQ
    return pl.pallas_call(
        body, name=name, grid=(S // tm,),
        in_specs=[_rows(tm, 512, 0), _rows(tm, 1), _rows(tm, W), _rows(tm, W), _rows(tm, ATT_W),
                  _full((1, Q_RANK)), _full((1, KV_RANK)), _full((Q_RANK, W)), _full((KV_RANK, W)), _full((8, 128))],
        out_specs=(_rows(tm, 512), _rows(tm, Q_RANK), _rows(tm, W), _rows(tm, KV_RANK), _rows(tm, W),
                   _full((1, Q_RANK)), _full((1, KV_RANK))),
        out_shape=(_sds((S, 512), BF16), _sds((S, Q_RANK), BF16), _sds((S, W), BF16), _sds((S, KV_RANK), BF16),
                   _sds((S, W), BF16), _sds((1, Q_RANK)), _sds((1, KV_RANK))),
        compiler_params=_cparams("arbitrary"),
    )(z, pos, dq, dk, dv, q_g, kv_g, wq_p, w_ukv, rope_rows)


def _conv_norm_backward(dyc, cv, ln_g, ln_b, w_pw2, name):
    S = cv.shape[0]
    tm = ROW_TILE

    def body(dy_ref, cv_ref, g_ref, be_ref, pw_ref, dcv_ref, sl_ref, dyb_ref, dg_ref, db_ref, dcb_ref):
        @pl.when(pl.program_id(0) == 0)
        def _():
            dg_ref[...] = jnp.zeros(dg_ref.shape, F32)
            db_ref[...] = jnp.zeros(db_ref.shape, F32)
            dcb_ref[...] = jnp.zeros(dcb_ref.shape, F32)

        cv_v = cv_ref[...]
        mu = jnp.mean(cv_v, axis=-1, keepdims=True)
        cc = cv_v - mu
        rstd = lax.rsqrt(jnp.mean(cc * cc, axis=-1, keepdims=True) + EPS)
        nh = cc * rstd
        g = g_ref[...]
        n = nh * g + be_ref[...]
        sil, dsil = _silu_and_grad(n)
        sl_ref[...] = sil.astype(BF16)
        dyb = dy_ref[...].astype(BF16)
        dyb_ref[...] = dyb
        dn = lax.dot_general(dyb, pw_ref[...], NT, preferred_element_type=F32) * dsil
        db_ref[...] += jnp.sum(dn, axis=0, keepdims=True)
        dg_ref[...] += jnp.sum(dn * nh, axis=0, keepdims=True)
        dnh = dn * g
        dcv = rstd * (dnh - jnp.mean(dnh, axis=-1, keepdims=True) - nh * jnp.mean(dnh * nh, axis=-1, keepdims=True))
        dcv_ref[...] = dcv
        dcb_ref[...] += jnp.sum(dcv, axis=0, keepdims=True)

    vec = _full((1, CONV_W))
    return pl.pallas_call(
        body, name=name, grid=(S // tm,),
        in_specs=[_rows(tm, CONV_W), _rows(tm, CONV_W), vec, vec, _full((CONV_W, CONV_W))],
        out_specs=(_rows(tm, CONV_W), _rows(tm, CONV_W), _rows(tm, CONV_W), vec, vec, vec),
        out_shape=(_sds((S, CONV_W)), _sds((S, CONV_W), BF16), _sds((S, CONV_W), BF16),
                   _sds((1, CONV_W)), _sds((1, CONV_W)), _sds((1, CONV_W))),
        compiler_params=_cparams("arbitrary"),
    )(dyc, cv, ln_g, ln_b, w_pw2)


def _conv_backward(z, dcv, conv_w_p, name):
    S = z.shape[0]
    tm = ROW_TILE
    per = tm // HALO
    last_halo = S // HALO - 1

    def body(a_ref, b_ref, ap_ref, bp_ref, d_ref, dn_ref, w_ref, da_ref, db_ref, dw_ref, win, dwin, dw_acc, sh, dsh):
        i = pl.program_id(0)

        @pl.when(i == 0)
        def _():
            dw_acc[...] = jnp.zeros(dw_acc.shape, F32)

        av, bv = a_ref[...], b_ref[...]
        _conv_window(win, ap_ref[...], bp_ref[...], av, bv, i == 0)
        dcur = d_ref[...]
        dwin[0:tm, :] = dcur
        dwin[tm:, :] = jnp.where(i == pl.num_programs(0) - 1, 0.0, dn_ref[...])
        _shifted_copies(win, sh, tm)
        _shifted_copies(dwin, dsh, tm)
        for r0 in range(0, tm, CONV_ROWS):
            dchunk = d_ref[r0:r0 + CONV_ROWS, :]
            dh = jnp.zeros((CONV_ROWS, CONV_W), F32)
            for kk in range(CONV_K):
                dh = dh + w_ref[kk:kk + 1, :] * _tap(dwin, dsh, r0 + CONV_K - 1 - kk, CONV_ROWS)
                prod = dchunk * _tap(win, sh, r0 + HALO - (CONV_K - 1) + kk, CONV_ROWS)
                dw_acc[kk] += jnp.sum(prod.reshape(CONV_ROWS // 8, 8, CONV_W), axis=0)
            sb = _sigmoid(b_ref[r0:r0 + CONV_ROWS, :])
            da_ref[r0:r0 + CONV_ROWS, :] = (dh * sb).astype(BF16)
            db_ref[r0:r0 + CONV_ROWS, :] = (dh * a_ref[r0:r0 + CONV_ROWS, :] * sb * (1.0 - sb)).astype(BF16)

        @pl.when(i == pl.num_programs(0) - 1)
        def _():
            dw_ref[...] = jnp.sum(dw_acc[...], axis=1)

    nxt = pl.BlockSpec((HALO, CONV_W), lambda i: (jnp.minimum((i + 1) * per, last_halo), 0))
    return pl.pallas_call(
        body, name=name, grid=(S // tm,),
        in_specs=_conv_in_specs(tm) + [_rows(tm, CONV_W), nxt, _full((HALO, CONV_W))],
        out_specs=(_rows(tm, CONV_W), _rows(tm, CONV_W), _full((HALO, CONV_W))),
        out_shape=(_sds((S, CONV_W), BF16), _sds((S, CONV_W), BF16), _sds((HALO, CONV_W))),
        scratch_shapes=[pltpu.VMEM((tm + HALO, CONV_W), F32), pltpu.VMEM((tm + HALO, CONV_W), F32),
                        pltpu.VMEM((HALO, 8, CONV_W), F32), pltpu.VMEM((7, tm + HALO - 8, CONV_W), F32),
                        pltpu.VMEM((7, tm + HALO - 8, CONV_W), F32)],
        compiler_params=_cparams("arbitrary"),
    )(z, z, z, z, dcv, dcv, conv_w_p)


def _sgu_backward(z, dy, ln_g, ln_b, w_s, bias_full, name):
    S = z.shape[0]
    tm = ROW_TILE

    def body(u_ref, v_ref, dy_ref, g_ref, be_ref, ws_ref, bias_ref, du_ref, dv_ref, dws_ref, dbs_ref, dg_ref, db_ref):
        @pl.when(pl.program_id(0) == 0)
        def _():
            dws_ref[...] = jnp.zeros(dws_ref.shape, F32)
            dbs_ref[...] = jnp.zeros(dbs_ref.shape, F32)
            dg_ref[...] = jnp.zeros(dg_ref.shape, F32)
            db_ref[...] = jnp.zeros(db_ref.shape, F32)

        gmask, tril = _sgu_masks()
        lane = lax.broadcasted_iota(jnp.int32, (1, 128), 1)
        wm = [jnp.where(tril, ws_ref[g], 0.0).astype(BF16) for g in range(SGU_G)]
        gain = g_ref[...]
        for ch in range(tm // SGU_T):
            rows = slice(ch * SGU_T, (ch + 1) * SGU_T)
            gu, dgu, dgv, rstd, nh, vn = _sgu_common(u_ref[rows, :], v_ref[rows, :], g_ref, be_ref)
            vb = vn.astype(BF16)
            sv = bias_ref[...]
            for g in range(SGU_G):
                sv = sv + jnp.where(gmask[g], jnp.dot(wm[g], vb, preferred_element_type=F32), 0.0)
            dyv = dy_ref[rows, :]
            du_ref[rows, :] = (dyv * sv * dgu).astype(BF16)
            dsv = dyv * gu
            dsvb = dsv.astype(BF16)
            dvn = jnp.zeros((SGU_T, SGU_W), F32)
            for g in range(SGU_G):
                dsg = jnp.where(gmask[g], dsv, 0.0)
                dwg = lax.dot_general(dsg.astype(BF16), vb, NT, preferred_element_type=F32)
                dws_ref[g] += jnp.where(tril, dwg, 0.0)
                dvn = dvn + jnp.where(gmask[g], lax.dot_general(wm[g], dsvb, TN, preferred_element_type=F32), 0.0)
                dbs_ref[...] += jnp.where(lane == g, jnp.sum(dsg, axis=-1, keepdims=True), 0.0)
            db_ref[...] += jnp.sum(dvn, axis=0, keepdims=True)
            dg_ref[...] += jnp.sum(dvn * nh, axis=0, keepdims=True)
            dnh = dvn * gain
            dgvv = rstd * (dnh - jnp.mean(dnh, axis=-1, keepdims=True) - nh * jnp.mean(dnh * nh, axis=-1, keepdims=True))
            dv_ref[rows, :] = (dgvv * dgv).astype(BF16)

    vec = _full((1, SGU_W))
    return pl.pallas_call(
        body, name=name, grid=(S // tm,),
        in_specs=[_rows(tm, SGU_W, 7), _rows(tm, SGU_W, 8), _rows(tm, SGU_W), vec, vec,
                  _full((SGU_G, SGU_T, SGU_T)), _full((SGU_T, SGU_W))],
        out_specs=(_rows(tm, SGU_W), _rows(tm, SGU_W), _full((SGU_G, SGU_T, SGU_T)), _full((SGU_T, 128)), vec, vec),
        out_shape=(_sds((S, SGU_W), BF16), _sds((S, SGU_W), BF16), _sds((SGU_G, SGU_T, SGU_T)), _sds((SGU_T, 128)),
                   _sds((1, SGU_W)), _sds((1, SGU_W))),
        compiler_params=_cparams("arbitrary"),
    )(z, z, dy, ln_g, ln_b, w_s, bias_full)


DZ_PIECES = (512, 512, 256, 256, 256, 256, 256, 256)


def _assemble_columns(dst_ref, pieces):
    off = 0
    for piece in pieces:
        wdt = piece.shape[1]
        dst_ref[:, off:off + wdt] = piece[...]
        off += wdt


def _inproj_backward(x, dxo, dz_att, dga, dca, dcb, dgc, dsu, dsv, dgs, g_pre, mod, w_in_p, name):
    S = x.shape[0]
    tm = ROW_TILE

    def body(x_ref, dxo_ref, p0, p1, p2, p3, p4, p5, p6, p7, g_ref, mod_ref, w_ref,
             dx_ref, hb_ref, dmod_ref, dg_ref, dz_sc):
        @pl.when(pl.program_id(0) == 0)
        def _():
            dmod_ref[...] = jnp.zeros(dmod_ref.shape, F32)
            dg_ref[...] = jnp.zeros(dg_ref.shape, F32)

        _assemble_columns(dz_sc, (p0, p1, p2, p3, p4, p5, p6, p7))
        dh = lax.dot_general(dz_sc[...], w_ref[...], NT, preferred_element_type=F32)
        xv = x_ref[...]
        g = g_ref[...]
        one_scale = 1.0 + mod_ref[1:2, :]
        rstd = lax.rsqrt(jnp.mean(xv * xv, axis=-1, keepdims=True) + EPS)
        xhat = xv * rstd
        xg = xhat * g
        hb_ref[...] = (xg * one_scale + mod_ref[0:1, :]).astype(BF16)
        dmod_ref[0:1, :] += jnp.sum(dh, axis=0, keepdims=True)
        dmod_ref[1:2, :] += jnp.sum(dh * xg, axis=0, keepdims=True)
        dhs = dh * one_scale
        dg_ref[...] += jnp.sum(dhs * xhat, axis=0, keepdims=True)
        dxh = dhs * g
        dx_ref[...] = dxo_ref[...] + rstd * (dxh - xhat * jnp.mean(dxh * xhat, axis=-1, keepdims=True))

    return pl.pallas_call(
        body, name=name, grid=(S // tm,),
        in_specs=[_rows(tm, D), _rows(tm, D)] + [_rows(tm, w) for w in DZ_PIECES]
                 + [_full((1, D)), _full((3, D)), _full((D, DZ))],
        out_specs=(_rows(tm, D), _rows(tm, D), _full((2, D)), _full((1, D))),
        out_shape=(_sds((S, D)), _sds((S, D), BF16), _sds((2, D)), _sds((1, D))),
        scratch_shapes=[pltpu.VMEM((tm, DZ), BF16)],
        compiler_params=_cparams("arbitrary"),
    )(x, dxo, dz_att, dga, dca, dcb, dgc, dsu, dsv, dgs, g_pre, mod, w_in_p)


def _adamw(w, gparts, m, v, name):
    shape = w.shape
    cols = shape[-1]
    rows = int(np.prod(shape[:-1]))
    parts = gparts.shape[0]
    w2, m2, v2 = (a.reshape(rows, cols) for a in (w, m, v))
    g3 = gparts.reshape(parts, rows, cols)
    tr = rows
    for cand in (256, 128):
        if rows > cand and rows % cand == 0:
            tr = cand
            break

    def body(w_ref, g_ref, m_ref, v_ref, go_ref, d_ref, mo_ref, vo_ref):
        g = g_ref[0].astype(F32)
        for p in range(1, parts):
            g = g + g_ref[p].astype(F32)
        wv = w_ref[...]
        mn = ADAM_B1 * m_ref[...] + (1.0 - ADAM_B1) * g
        vn = ADAM_B2 * v_ref[...] + (1.0 - ADAM_B2) * (g * g)
        m_hat = mn / (1.0 - ADAM_B1 ** ADAM_STEP)
        v_hat = vn / (1.0 - ADAM_B2 ** ADAM_STEP)
        go_ref[...] = g
        d_ref[...] = -ADAM_LR * (m_hat / (jnp.sqrt(v_hat) + ADAM_EPS) + ADAM_WD * wv)
        mo_ref[...] = mn
        vo_ref[...] = vn

    blk = pl.BlockSpec((tr, cols), lambda i: (i, 0))
    outs = pl.pallas_call(
        body, name=name, grid=(rows // tr,),
        in_specs=[blk, pl.BlockSpec((parts, tr, cols), lambda i: (0, i, 0)), blk, blk],
        out_specs=(blk, blk, blk, blk),
        out_shape=tuple(_sds((rows, cols)) for _ in range(4)),
        compiler_params=_cparams("parallel"),
    )(w2, g3, m2, v2)
    return tuple(o.reshape(shape) for o in outs)


_GATHERED = ("w_in", "w_out", "w_uq", "w_ukv", "w_pw2", "conv_w")
_COL_SHARDED = ("w_in", "w_uq", "w_ukv", "conv_w")

_SMALL = (("dmod", (3 * D,)), ("g_pre", (D,)), ("g_post", (D,)), ("q_norm_g", (Q_RANK,)),
          ("kv_norm_g", (KV_RANK,)), ("conv_b", (CONV_W,)), ("conv_ln_g", (CONV_W,)),
          ("conv_ln_b", (CONV_W,)), ("sgu_ln_g", (SGU_W,)), ("sgu_ln_b", (SGU_W,)),
          ("w_s", (SGU_G, SGU_T, SGU_T)), ("b_s", (SGU_G, SGU_T)))


def _assemble(name, parts):
    if name in _COL_SHARDED:
        p = jnp.moveaxis(parts, 0, 1)
        return p.reshape(p.shape[0], p.shape[1] * p.shape[2])
    return parts.reshape(parts.shape[0] * parts.shape[1], parts.shape[2])


def _scatter_layout(name, full):
    if name in _COL_SHARDED:
        return jnp.moveaxis(full.reshape(full.shape[0], N_DEV, full.shape[1] // N_DEV), 1, 0)
    return full.reshape(N_DEV, full.shape[0] // N_DEV, full.shape[1])


def kernel(x, c, positions, w_ada, b_ada, g_pre, g_post, w_in, q_norm_g, w_uq, kv_norm_g, w_ukv, conv_w, conv_b, conv_ln_g, conv_ln_b, w_pw2, sgu_ln_g, sgu_ln_b, w_s, b_s, w_out, loss_target, m_w_ada, m_b_ada, m_g_pre, m_g_post, m_w_in, m_q_norm_g, m_w_uq, m_kv_norm_g, m_w_ukv, m_conv_w, m_conv_b, m_conv_ln_g, m_conv_ln_b, m_w_pw2, m_sgu_ln_g, m_sgu_ln_b, m_w_s, m_b_s, m_w_out, v_w_ada, v_b_ada, v_g_pre, v_g_post, v_w_in, v_q_norm_g, v_w_uq, v_kv_norm_g, v_w_ukv, v_conv_w, v_conv_b, v_conv_ln_g, v_conv_ln_b, v_w_pw2, v_sgu_ln_g, v_sgu_ln_b, v_w_s, v_b_s, v_w_out):
    weights = dict(w_ada=w_ada, b_ada=b_ada, g_pre=g_pre, g_post=g_post, w_in=w_in, q_norm_g=q_norm_g, w_uq=w_uq,
                   kv_norm_g=kv_norm_g, w_ukv=w_ukv, conv_w=conv_w, conv_b=conv_b, conv_ln_g=conv_ln_g,
                   conv_ln_b=conv_ln_b, w_pw2=w_pw2, sgu_ln_g=sgu_ln_g, sgu_ln_b=sgu_ln_b, w_s=w_s, b_s=b_s, w_out=w_out)
    m_in = dict(w_ada=m_w_ada, b_ada=m_b_ada, g_pre=m_g_pre, g_post=m_g_post, w_in=m_w_in, q_norm_g=m_q_norm_g,
                w_uq=m_w_uq, kv_norm_g=m_kv_norm_g, w_ukv=m_w_ukv, conv_w=m_conv_w, conv_b=m_conv_b,
                conv_ln_g=m_conv_ln_g, conv_ln_b=m_conv_ln_b, w_pw2=m_w_pw2, sgu_ln_g=m_sgu_ln_g,
                sgu_ln_b=m_sgu_ln_b, w_s=m_w_s, b_s=m_b_s, w_out=m_w_out)
    v_in = dict(w_ada=v_w_ada, b_ada=v_b_ada, g_pre=v_g_pre, g_post=v_g_post, w_in=v_w_in, q_norm_g=v_q_norm_g,
                w_uq=v_w_uq, kv_norm_g=v_kv_norm_g, w_ukv=v_w_ukv, conv_w=v_conv_w, conv_b=v_conv_b,
                conv_ln_g=v_conv_ln_g, conv_ln_b=v_conv_ln_b, w_pw2=v_w_pw2, sgu_ln_g=v_sgu_ln_g,
                sgu_ln_b=v_sgu_ln_b, w_s=v_w_s, b_s=v_b_s, w_out=v_w_out)
    order = list(weights)

    S = x.shape[1]
    me = 4 * lax.axis_index("x") + 2 * lax.axis_index("y") + lax.axis_index("c")
    x0 = x.reshape(S, D)
    target = loss_target.reshape(S, D)
    pos = positions.reshape(S, 1)

    def shards(l):
        return [weights[n][l].astype(BF16) for n in _GATHERED]

    def w_in_operand(part):
        w_in_f = _assemble("w_in", part)
        return jnp.concatenate([w_in_f[:, :ATT_IN], jnp.zeros((D, PAD_IN), BF16), w_in_f[:, ATT_IN:]], axis=1)

    def other_operands(parts):
        full = {n: _assemble(n, p) for n, p in zip(_GATHERED[1:], parts)}
        wq = jnp.pad(full["w_uq"].reshape(Q_RANK, HEADS, QK), ((0, 0), (0, 0), (0, HQ - QK)))
        return dict(w_uq=wq.reshape(Q_RANK, HEADS * HQ), w_ukv=full["w_ukv"], w_pw2=full["w_pw2"], w_out=full["w_out"],
                    conv_w=jnp.pad(full["conv_w"].astype(F32), ((0, HALO - CONV_K), (0, 0))))

    first_w_in, c_parts = _gather_two_level([shards(0)[0], c.reshape(8, D // 8)], name="gather_w_in_0")
    lw = [dict(w_in=w_in_operand(first_w_in))] + [None] * (DEPTH - 1)
    c_all = c_parts.reshape(N_DEV, D)

    ada_cols = w_ada.shape[-1]
    b_cols = lax.dynamic_slice_in_dim(b_ada, me * ada_cols, ada_cols, axis=1)
    sc_rows, mod_part = _ada_forward(jnp.pad(c_all, ((0, 8), (0, 0))), w_ada, b_cols)
    mod_recv = _exchange([jnp.moveaxis(mod_part[:, :N_DEV], 1, 0)], [True], name="exchange_mod")[0]
    mod = jnp.moveaxis(mod_recv, 0, 1).reshape(DEPTH, 3, D)

    bias_full = jnp.repeat(jnp.swapaxes(b_s, 1, 2), SGU_GD, axis=2)
    inv_freq = ROPE_THETA ** (-jnp.arange(0, ROPE, 2, dtype=F32) / ROPE)
    zeros32 = jnp.zeros((ROPE // 2,), F32)
    ones32 = jnp.ones((ROPE // 2,), F32)
    rope_rows = jnp.zeros((8, 128), F32)
    rope_rows = rope_rows.at[0].set(jnp.concatenate([inv_freq, inv_freq, zeros32, zeros32]))
    rope_rows = rope_rows.at[1].set(jnp.concatenate([ones32, ones32, zeros32, zeros32]))
    rope_rows = rope_rows.at[2].set(jnp.concatenate([-ones32, ones32, zeros32, zeros32]))

    def vec(a, l):
        return a[l].reshape(1, -1)

    saved = []
    xl = x0
    for l in range(DEPTH):
        w = lw[l]
        late = _Comm(shards(0)[1:], [False] * (len(_GATHERED) - 1)) if l == 0 else None
        z, arrived = _prenorm_inproj(xl, vec(g_pre, l), mod[l], w["w_in"], name=f"prenorm_inproj_{l}", comm=late)
        if late is not None:
            w.update(other_operands(arrived))
        q, k, v = _att_prep(z, pos, vec(q_norm_g, l), vec(kv_norm_g, l), w["w_uq"], w["w_ukv"], rope_rows,
                            name=f"att_prep_{l}")
        ahead = _Comm(shards(l + 1), [False] * len(_GATHERED)) if l + 1 < DEPTH else None
        y_att, lse, arrived = _flash_forward(q, k, v, name=f"flash_forward_{l}", comm=ahead)
        if ahead is not None:
            lw[l + 1] = dict(w_in=w_in_operand(arrived[0]), **other_operands(arrived[1:]))
        cv, y_conv = _conv_forward(z, w["conv_w"], vec(conv_b, l), vec(conv_ln_g, l), vec(conv_ln_b, l),
                                   w["w_pw2"], name=f"conv_forward_{l}")
        y_sgu = _sgu_forward(z, vec(sgu_ln_g, l), vec(sgu_ln_b, l), w_s[l], bias_full[l], name=f"sgu_forward_{l}")
        outs = _out_proj(xl, z, y_att, y_conv, y_sgu, w["w_out"], vec(g_post, l), mod[l], name=f"out_proj_{l}",
                         target=target if l == DEPTH - 1 else None)
        saved.append(dict(x=xl, z=z, q=q, k=k, v=v, y_att=y_att, lse=lse, cv=cv, y_conv=y_conv, y_sgu=y_sgu,
                          y=outs[1], ycat=outs[2]))
        xl = outs[0]

    dx = xl
    loss = lax.psum(outs[3].reshape(()), ("x", "y", "c"))

    spack = _Packer(_SMALL, 8)
    grad_kinds = [True] * len(_GATHERED) + [False]
    received = [None] * DEPTH
    pending = None
    for l in reversed(range(DEPTH)):
        sv = saved[l]
        w = lw[l]
        (dyb, dob, stats, dga, dyc, dgc, dys, dgs, dgate, dgpost) = _out_proj_backward(
            dx, sv["y"], sv["z"], sv["y_att"], sv["y_conv"], sv["y_sgu"], sv["lse"], w["w_out"],
            vec(g_post, l), mod[l], name=f"out_proj_backward_{l}")
        big = dict(w_out=_matmul_tn(sv["ycat"], dyb, name=f"grad_w_out_{l}"))
        riding = _Comm(pending, grad_kinds) if pending is not None else None
        dq, dk, dv, arrived = _flash_backward(sv["q"], sv["k"], sv["v"], dob, stats, name=f"flash_backward_{l}",
                                              comm=riding)
        if riding is not None:
            received[l + 1] = arrived
        dz_att, qn_b, dqp_b, kvn_b, dkv_b, dqg, dkvg = _att_prep_backward(
            sv["z"], pos, dq, dk, dv, vec(q_norm_g, l), vec(kv_norm_g, l), w["w_uq"], w["w_ukv"], rope_rows,
            name=f"att_prep_backward_{l}")
        dwq_p = _matmul_tn(qn_b, dqp_b, name=f"grad_w_uq_{l}")
        big["w_uq"] = dwq_p.reshape(Q_RANK, HEADS, HQ)[:, :, :QK].reshape(Q_RANK, HEADS * QK)
        big["w_ukv"] = _matmul_tn(kvn_b, dkv_b, name=f"grad_w_ukv_{l}")
        dcv, sl_b, dyc_b, dclg, dclb, dcb = _conv_norm_backward(dyc, sv["cv"], vec(conv_ln_g, l), vec(conv_ln_b, l),
                                                              w["w_pw2"], name=f"conv_norm_backward_{l}")
        big["w_pw2"] = _matmul_tn(sl_b, dyc_b, name=f"grad_w_pw2_{l}")
        dca, dcbb, dconvw = _conv_backward(sv["z"], dcv, w["conv_w"], name=f"conv_backward_{l}")
        big["conv_w"] = dconvw[:CONV_K]
        dsu, dsvv, dws, dbs, dslg, dslb = _sgu_backward(sv["z"], dys, vec(sgu_ln_g, l), vec(sgu_ln_b, l), w_s[l],
                                                       bias_full[l], name=f"sgu_backward_{l}")
        dz_pieces = [dz_att, dga, dca, dcbb, dgc, dsu, dsvv, dgs]
        dx, h_b, dmod2, dgpre = _inproj_backward(sv["x"], dx, *dz_pieces, vec(g_pre, l), mod[l], w["w_in"],
                                                 name=f"inproj_backward_{l}")
        small = dict(dmod=jnp.concatenate([dmod2.reshape(-1), dgate.reshape(-1)]), g_pre=dgpre, g_post=dgpost,
                     q_norm_g=dqg, kv_norm_g=dkvg, conv_b=dcb, conv_ln_g=dclg, conv_ln_b=dclb, sgu_ln_g=dslg,
                     sgu_ln_b=dslb, w_s=dws, b_s=jnp.swapaxes(dbs[:, :SGU_G], 0, 1))
        rest = [_scatter_layout(n, big[n]).astype(BF16) for n in _GATHERED[1:]] + [spack.pack(small, F32)]
        early = _Comm(rest, grad_kinds[1:]) if l == 0 else None
        dwin_p = _matmul_tn(h_b, dz_pieces, name=f"grad_w_in_{l}", comm=early)
        if early is not None:
            dwin_p, rest_arrived = dwin_p
        dwin = jnp.concatenate([dwin_p[:, :ATT_IN], dwin_p[:, ATT_IN + PAD_IN:]], axis=1)
        pending = [_scatter_layout("w_in", dwin).astype(BF16)] + rest
    grad_x = dx.reshape(1, S, D)
    received[0] = list(_exchange(pending[:1], grad_kinds[:1], name="exchange_grad_w_in_0")) + list(rest_arrived)

    gparts = {n: jnp.stack([received[l][i] for l in range(DEPTH)], axis=1) for i, n in enumerate(_GATHERED)}
    sparts = [spack.unpack(received[l][-1], (N_DEV,)) for l in range(DEPTH)]
    sparts = {n: jnp.stack([sparts[l][n] for l in range(DEPTH)], axis=1) for n, _ in _SMALL}
    dmod_all = sparts["dmod"]
    dmod_cols = lax.dynamic_slice_in_dim(dmod_all, me * ada_cols, ada_cols, axis=2)
    sc_t = jnp.pad(sc_rows[:N_DEV].T, ((0, 0), (0, 128 - N_DEV)))
    dmod_rows = jnp.pad(jnp.moveaxis(dmod_cols, 0, 1), ((0, 0), (0, 128 - N_DEV), (0, 0)))
    gparts["w_ada"] = _ada_backward(sc_t, dmod_rows)[None]
    gparts["b_ada"] = dmod_all
    for n, _ in _SMALL[1:]:
        gparts[n] = sparts[n]

    grads, deltas, new_m, new_v = {}, {}, {}, {}
    for n in order:
        grads[n], deltas[n], new_m[n], new_v[n] = _adamw(weights[n], gparts[n], m_in[n], v_in[n], name=f"adamw_{n}")
    return (loss, grad_x, *[grads[n] for n in order], *[deltas[n] for n in order],
            *[new_m[n] for n in order], *[new_v[n] for n in order])
```

```python
import functools
import math

import numpy as np
import jax
import jax.numpy as jnp
from jax import lax
from jax.experimental import pallas as pl
from jax.experimental.pallas import tpu as pltpu

F32 = jnp.float32
BF16 = jnp.bfloat16

N_DEV = 8
DEPTH = 2
D = 1024
HEADS = 4
NOPE = 128
ROPE = 64
VDIM = 128
QK = NOPE + ROPE
Q_RANK = 256
KV_RANK = 128
ATT_W = HEADS * VDIM
CONV_W = 256
CONV_K = 31
SGU_W = 256
SGU_G = 4
SGU_GD = SGU_W // SGU_G
SGU_T = 128
D_IN = 2496
ATT_IN = Q_RANK + KV_RANK + ROPE
PAD_IN = 64
DZ = D_IN + PAD_IN
HQ = 2 * NOPE
EPS = 1e-6
ROPE_THETA = 10000.0
ATT_SCALE = QK ** -0.5
LOG2E = math.log2(math.e)
EXP2_SCALE = ATT_SCALE * LOG2E
NEG_INF = float("-inf")

ADAM_LR = 0.001
ADAM_B1 = 0.9
ADAM_B2 = 0.999
ADAM_EPS = 1e-08
ADAM_WD = 0.01
ADAM_STEP = 10

VMEM_LIMIT = 56 * 1024 * 1024
ROW_TILE = 512
MATMUL_TN_ROWS = 1024
ATT_TILE = 512
FLASH_UNROLL = 4
HALO = 32
CONV_ROWS = 64
PACK_LANES = 128

MESH = pl.DeviceIdType.MESH
NT = (((1,), (1,)), ((), ()))
TN = (((0,), (0,)), ((), ()))


def _cparams(*sem):
    return pltpu.CompilerParams(dimension_semantics=sem, vmem_limit_bytes=VMEM_LIMIT)


def _sds(shape, dtype=F32):
    return jax.ShapeDtypeStruct(tuple(shape), dtype)


def _rows(tm, width, col=0):
    return pl.BlockSpec((tm, width), lambda i: (i, col))


def _full(shape):
    nd = len(shape)
    return pl.BlockSpec(tuple(shape), lambda *_: (0,) * nd)


def _sigmoid(x):
    return 1.0 / (1.0 + jnp.exp(-x))


def _silu_and_grad(g):
    s = _sigmoid(g)
    return g * s, s * (1.0 + g * (1.0 - s))


def _gelu_and_grad(x):
    cdf = 0.5 * (1.0 + lax.erf(x * (1.0 / math.sqrt(2.0))))
    pdf = jnp.exp(-0.5 * x * x) * (1.0 / math.sqrt(2.0 * math.pi))
    return x * cdf, cdf + x * pdf


def _swap_halves(a):
    lane = lax.broadcasted_iota(jnp.int32, a.shape, 1)
    up = pltpu.roll(a, 32, 1)
    down = pltpu.roll(a, 96, 1)
    return jnp.where(lane < 32, down, jnp.where(lane < 64, up, 0.0))


def _rope_tables(pos_ref, rope_ref):
    ang = pos_ref[...].astype(F32) * rope_ref[0:1, :]
    return jnp.cos(ang) * rope_ref[1:2, :], jnp.sin(ang) * rope_ref[2:3, :]


class _Comm:
    def __init__(self, srcs, kinds):
        self.srcs = list(srcs)
        self.kinds = list(kinds)
        self.n = len(self.srcs)
        self.out_shape = [_sds((N_DEV,) + tuple(s.shape[1:] if k else s.shape), s.dtype)
                          for s, k in zip(self.srcs, self.kinds)]
        self.specs = [pl.BlockSpec(memory_space=pl.ANY)] * self.n
        self.scratch = [pltpu.SemaphoreType.DMA((self.n, N_DEV - 1)), pltpu.SemaphoreType.DMA((self.n, N_DEV - 1)),
                        pltpu.SemaphoreType.DMA((self.n,))] if self.n else []

    def _copies(self, src_refs, out_refs, sems, with_recvs):
        send_sems, recv_sems, local_sems = sems
        x, y, c = lax.axis_index("x"), lax.axis_index("y"), lax.axis_index("c")
        me = 4 * x + 2 * y + c
        local, sends, recvs = [], [], []
        for a in range(self.n):
            def block_for(dest, src_ref=src_refs[a], a2a=self.kinds[a]):
                return src_ref.at[dest] if a2a else src_ref

            local.append(pltpu.make_async_copy(block_for(me), out_refs[a].at[me], local_sems.at[a]))
            for r in range(1, N_DEV):
                px = 1 - x if (r >> 2) & 1 else x
                py = 1 - y if (r >> 1) & 1 else y
                pc = 1 - c if r & 1 else c
                peer = 4 * px + 2 * py + pc
                sends.append(pltpu.make_async_remote_copy(
                    src_ref=block_for(peer), dst_ref=out_refs[a].at[me],
                    send_sem=send_sems.at[a, r - 1], recv_sem=recv_sems.at[a, r - 1],
                    device_id=(px, py, pc), device_id_type=MESH))
                if with_recvs:
                    recvs.append(pltpu.make_async_remote_copy(
                        src_ref=block_for(me), dst_ref=out_refs[a].at[peer],
                        send_sem=send_sems.at[a, r - 1], recv_sem=recv_sems.at[a, r - 1],
                        device_id=(px, py, pc), device_id_type=MESH))
        return local, sends, recvs

    def start(self, src_refs, out_refs, sems):
        local, sends, _ = self._copies(src_refs, out_refs, sems, False)
        for cp in local + sends:
            cp.start()

    def finish(self, src_refs, out_refs, sems):
        local, sends, recvs = self._copies(src_refs, out_refs, sems, True)
        for cp in recvs:
            cp.wait_recv()
        for cp in sends:
            cp.wait_send()
        for cp in local:
            cp.wait()


def _exchange(srcs, kinds, name):
    comm = _Comm(srcs, kinds)
    n = comm.n

    def body(*refs):
        src_refs, out_refs, sems = refs[:n], refs[n:2 * n], refs[2 * n:]
        comm.start(src_refs, out_refs, sems)
        comm.finish(src_refs, out_refs, sems)

    return pl.pallas_call(
        body, name=name, out_shape=comm.out_shape, in_specs=comm.specs, out_specs=comm.specs,
        scratch_shapes=comm.scratch,
    )(*srcs)


def _gather_two_level(srcs, name):
    n = len(srcs)

    def body(*refs):
        src_refs, out_refs = refs[:n], refs[n:2 * n]
        send_sems, recv_sems, local_sems = refs[2 * n:]
        x, y, c = lax.axis_index("x"), lax.axis_index("y"), lax.axis_index("c")
        me, sibling = (x, y, c), (x, y, 1 - c)
        chips = [(1 - x, y), (x, 1 - y), (1 - x, 1 - y)]

        def block(a, dev):
            return out_refs[a].at[4 * dev[0] + 2 * dev[1] + dev[2]]

        def copy(a, k, owner, to, src=None):
            return pltpu.make_async_remote_copy(
                src_ref=block(a, owner) if src is None else src, dst_ref=block(a, owner),
                send_sem=send_sems.at[a, k], recv_sem=recv_sems.at[a, k], device_id=to, device_id_type=MESH)

        mine, first, passed = [], [], []
        for a in range(n):
            mine.append(pltpu.make_async_copy(src_refs[a], block(a, me), local_sems.at[a]))
            first.append(copy(a, 0, me, sibling, src=src_refs[a]))
            first += [copy(a, 1 + j, me, (*chip, c), src=src_refs[a]) for j, chip in enumerate(chips)]
        for cp in mine + first:
            cp.start()
        for a in range(n):
            for j, chip in enumerate(chips):
                copy(a, 1 + j, (*chip, c), me).wait_recv()
                fwd = copy(a, 4 + j, (*chip, c), sibling)
                fwd.start()
                passed.append(fwd)
        for a in range(n):
            copy(a, 0, sibling, me).wait_recv()
            for j, chip in enumerate(chips):
                copy(a, 4 + j, (*chip, 1 - c), me).wait_recv()
        for cp in first + passed:
            cp.wait_send()
        for cp in mine:
            cp.wait()

    return pl.pallas_call(
        body, name=name,
        out_shape=[_sds((N_DEV,) + tuple(s.shape), s.dtype) for s in srcs],
        in_specs=[pl.BlockSpec(memory_space=pl.ANY)] * n, out_specs=[pl.BlockSpec(memory_space=pl.ANY)] * n,
        scratch_shapes=[pltpu.SemaphoreType.DMA((n, N_DEV - 1)), pltpu.SemaphoreType.DMA((n, N_DEV - 1)),
                        pltpu.SemaphoreType.DMA((n,))],
    )(*srcs)


class _Packer:
    def __init__(self, entries, row_multiple):
        self.entries = entries
        self.offsets = {}
        off = 0
        for name, shape in entries:
            self.offsets[name] = off
            off += int(np.prod(shape))
        quantum = PACK_LANES * row_multiple
        self.total = -(-off // quantum) * quantum
        self.used = off
        self.rows = self.total // PACK_LANES

    def pack(self, arrays, dtype, lead=()):
        n = len(lead)
        flat = [arrays[name].astype(dtype).reshape(lead + (-1,)) for name, _ in self.entries]
        flat.append(jnp.zeros(lead + (self.total - self.used,), dtype))
        return jnp.concatenate(flat, axis=n).reshape(lead + (self.rows, PACK_LANES))

    def unpack(self, buf, lead=()):
        flat = buf.reshape(lead + (self.total,))
        out = {}
        for name, shape in self.entries:
            o = self.offsets[name]
            out[name] = lax.slice_in_dim(flat, o, o + int(np.prod(shape)), axis=len(lead)).reshape(lead + tuple(shape))
        return out


def _ada_forward(c_rows, w_ada, b_ada_cols):
    cols = w_ada.shape[-1]
    rows = c_rows.shape[0]

    def body(c_ref, w_ref, b_ref, sc_ref, part_ref):
        cv = c_ref[...]
        sc = cv * _sigmoid(cv)
        sc_ref[...] = sc
        scb = sc.astype(BF16)
        for l in range(DEPTH):
            part_ref[l] = jnp.dot(scb, w_ref[l].astype(BF16), preferred_element_type=F32) + b_ref[l:l + 1, :]

    return pl.pallas_call(
        body, name="ada_forward",
        out_shape=(_sds((rows, D)), _sds((DEPTH, rows, cols))),
        compiler_params=pltpu.CompilerParams(vmem_limit_bytes=VMEM_LIMIT),
    )(c_rows, w_ada, b_ada_cols)


def _ada_backward(sc_t, dmod_cols):
    cols = dmod_cols.shape[-1]

    def body(sc_ref, dm_ref, gw_ref):
        scb = sc_ref[...].astype(BF16)
        for l in range(DEPTH):
            gw_ref[l] = jnp.dot(scb, dm_ref[l].astype(BF16), preferred_element_type=F32)

    return pl.pallas_call(
        body, name="ada_backward",
        out_shape=_sds((DEPTH, D, cols)),
        compiler_params=pltpu.CompilerParams(vmem_limit_bytes=VMEM_LIMIT),
    )(sc_t, dmod_cols)


def _riding(comm, nsteps, c_src, c_out, c_sems, where):
    if not comm.n:
        return
    step = 0 if where == "start" else nsteps - 1

    @pl.when(pl.program_id(0) == step)
    def _():
        (comm.start if where == "start" else comm.finish)(c_src, c_out, c_sems)


def _prenorm_inproj(x, g_pre, mod, w_in_p, name, comm=None):
    S = x.shape[0]
    tm = ROW_TILE
    comm = comm or _Comm([], [])
    nc = comm.n

    def body(*refs):
        x_ref, g_ref, mod_ref, w_ref = refs[:4]
        c_src, z_ref, c_out, c_sems = refs[4:4 + nc], refs[4 + nc], refs[5 + nc:5 + 2 * nc], refs[5 + 2 * nc:]
        _riding(comm, S // tm, c_src, c_out, c_sems, "start")
        xv = x_ref[...]
        rstd = lax.rsqrt(jnp.mean(xv * xv, axis=-1, keepdims=True) + EPS)
        h = (xv * rstd * g_ref[...]) * (1.0 + mod_ref[1:2, :]) + mod_ref[0:1, :]
        z_ref[...] = jnp.dot(h.astype(BF16), w_ref[...], preferred_element_type=F32).astype(BF16)
        _riding(comm, S // tm, c_src, c_out, c_sems, "finish")

    outs = pl.pallas_call(
        body, name=name, grid=(S // tm,),
        in_specs=[_rows(tm, D), _full((1, D)), _full((3, D)), _full((D, DZ))] + comm.specs,
        out_specs=[_rows(tm, DZ)] + comm.specs, out_shape=[_sds((S, DZ), BF16)] + comm.out_shape,
        scratch_shapes=comm.scratch,
        compiler_params=_cparams("arbitrary"),
    )(x, g_pre, mod, w_in_p, *comm.srcs)
    return outs[0], outs[1:]


def _att_prep(z, pos, q_g, kv_g, wq_p, w_ukv, rope_rows, name):
    S = z.shape[0]
    tm = ROW_TILE

    def body(z_ref, pos_ref, qg_ref, kvg_ref, wq_ref, wkv_ref, rope_ref, q_ref, k_ref, v_ref):
        zz = z_ref[...].astype(F32)
        ql, kvl, ka = zz[:, 0:Q_RANK], zz[:, Q_RANK:Q_RANK + KV_RANK], zz[:, Q_RANK + KV_RANK:]
        qn = ql * lax.rsqrt(jnp.mean(ql * ql, axis=-1, keepdims=True) + EPS) * qg_ref[...]
        kvn = kvl * lax.rsqrt(jnp.mean(kvl * kvl, axis=-1, keepdims=True) + EPS) * kvg_ref[...]
        q = jnp.dot(qn.astype(BF16), wq_ref[...], preferred_element_type=F32)
        kv = jnp.dot(kvn.astype(BF16), wkv_ref[...], preferred_element_type=F32)
        ct, st = _rope_tables(pos_ref, rope_ref)
        krot = (ka * ct + _swap_halves(ka) * st).astype(BF16)
        for h in range(HEADS):
            b = h * HQ
            q_ref[:, b:b + NOPE] = q[:, b:b + NOPE].astype(BF16)
            a = q[:, b + NOPE:b + HQ]
            q_ref[:, b + NOPE:b + HQ] = (a * ct + _swap_halves(a) * st).astype(BF16)
            k_ref[:, b:b + NOPE] = kv[:, b:b + NOPE].astype(BF16)
            k_ref[:, b + NOPE:b + HQ] = krot
            v_ref[:, h * VDIM:(h + 1) * VDIM] = kv[:, b + NOPE:b + HQ].astype(BF16)

    return pl.pallas_call(
        body, name=name, grid=(S // tm,),
        in_specs=[_rows(tm, 512, 0), _rows(tm, 1), _full((1, Q_RANK)), _full((1, KV_RANK)),
                  _full((Q_RANK, HEADS * HQ)), _full((KV_RANK, HEADS * HQ)), _full((8, 128))],
        out_specs=(_rows(tm, HEADS * HQ), _rows(tm, HEADS * HQ), _rows(tm, ATT_W)),
        out_shape=(_sds((S, HEADS * HQ), BF16), _sds((S, HEADS * HQ), BF16), _sds((S, ATT_W), BF16)),
        compiler_params=_cparams("parallel"),
    )(z, pos, q_g, kv_g, wq_p, w_ukv, rope_rows)


def _flash_forward(q, k, v, name, comm=None):
    S = q.shape[0]
    t = ATT_TILE
    nq = S // t
    nl = t // 128
    comm = comm or _Comm([], [])
    nc = comm.n

    def body(*refs):
        q_ref, k_ref, v_ref = refs[:3]
        c_src = refs[3:3 + nc]
        o_ref, lse_ref = refs[3 + nc:5 + nc]
        c_out = refs[5 + nc:5 + 2 * nc]
        m_sc, l_sc, acc_sc, s_sc, mp_sc = refs[5 + 2 * nc:10 + 2 * nc]
        c_sems = refs[10 + 2 * nc:]
        if nc:
            @pl.when((pl.program_id(0) == 0) & (pl.program_id(1) == 0))
            def _():
                comm.start(c_src, c_out, c_sems)

        qb = pl.program_id(1)
        m_sc[...] = jnp.full(m_sc.shape, NEG_INF, F32)
        l_sc[...] = jnp.zeros(l_sc.shape, F32)
        acc_sc[...] = jnp.zeros(acc_sc.shape, F32)

        def score_phase(kb, slot, diagonal):
            s = lax.dot_general(q_ref[...], k_ref[pl.ds(pl.multiple_of(kb * t, t), t), :], NT,
                                preferred_element_type=F32)
            if diagonal:
                ri = lax.broadcasted_iota(jnp.int32, (t, t), 0)
                ci = lax.broadcasted_iota(jnp.int32, (t, t), 1)
                s = jnp.where(ci <= ri, s, NEG_INF)
            s_sc[slot] = s
            mp = s[:, 0:128]
            for c in range(1, nl):
                mp = jnp.maximum(mp, s[:, c * 128:(c + 1) * 128])
            mp_sc[slot] = mp

        def sum_phase(kb, slot):
            m_prev = m_sc[...]
            m_new = jnp.maximum(m_prev, jnp.max(mp_sc[slot], axis=-1, keepdims=True))
            alpha = jnp.exp2((m_prev - m_new) * EXP2_SCALE)
            p = jnp.exp2(s_sc[slot] * EXP2_SCALE - jnp.tile(m_new * EXP2_SCALE, (1, nl)))
            lp = alpha * l_sc[...]
            for c in range(nl):
                lp = lp + p[:, c * 128:(c + 1) * 128]
            l_sc[...] = lp
            acc_sc[...] = alpha * acc_sc[...] + jnp.dot(p.astype(BF16), v_ref[pl.ds(pl.multiple_of(kb * t, t), t), :],
                                                        preferred_element_type=F32)
            m_sc[...] = m_new

        def tile_at(pos):
            return jnp.where(pos == 0, qb, pos - 1)

        def run(p0, count, final):
            for i in range(count):
                sum_phase(tile_at(p0 + i), i % 2)
                if not (final and i == count - 1):
                    score_phase(p0 + i, (i + 1) % 2, False)

        score_phase(qb, 0, True)

        def trip(u, carry):
            run(FLASH_UNROLL * u, FLASH_UNROLL, False)
            return carry

        lax.fori_loop(0, qb // FLASH_UNROLL, trip, 0)
        for left in range(FLASH_UNROLL):
            @pl.when(qb % FLASH_UNROLL == left)
            def _():
                run(qb - left, left + 1, True)

        l = jnp.sum(l_sc[...], axis=-1, keepdims=True)
        o_ref[...] = acc_sc[...] / l
        lse_ref[0] = jnp.max(m_sc[...], axis=-1, keepdims=True) * ATT_SCALE + jnp.log(l)

        if nc:
            @pl.when((pl.program_id(0) == HEADS - 1) & (pl.program_id(1) == nq - 1))
            def _():
                comm.finish(c_src, c_out, c_sems)

    outs = pl.pallas_call(
        body, name=name, grid=(HEADS, nq),
        in_specs=[pl.BlockSpec((t, HQ), lambda h, i: (i, h)),
                  pl.BlockSpec((S, HQ), lambda h, i: (0, h)),
                  pl.BlockSpec((S, VDIM), lambda h, i: (0, h))] + comm.specs,
        out_specs=[pl.BlockSpec((t, VDIM), lambda h, i: (i, h)),
                   pl.BlockSpec((1, t, 1), lambda h, i: (h, i, 0))] + comm.specs,
        out_shape=[_sds((S, ATT_W)), _sds((HEADS, S, 1))] + comm.out_shape,
        scratch_shapes=[pltpu.VMEM((t, 128), F32), pltpu.VMEM((t, 128), F32), pltpu.VMEM((t, VDIM), F32),
                        pltpu.VMEM((2, t, t), F32), pltpu.VMEM((2, t, 128), F32)] + comm.scratch,
        compiler_params=_cparams("arbitrary", "arbitrary"),
    )(q, k, v, *comm.srcs)
    return outs[0], outs[1], outs[2:]


def _conv_window(win_ref, a_prev, b_prev, a_cur, b_cur, first):
    hp = a_prev * _sigmoid(b_prev)
    win_ref[0:HALO, :] = jnp.where(first, 0.0, hp)
    win_ref[HALO:, :] = a_cur * _sigmoid(b_cur)


def _shifted_copies(win_ref, sh_ref, tm):
    for b in range(1, 8):
        sh_ref[b - 1] = win_ref[pl.ds(b, tm + HALO - 8), :]


def _tap(win_ref, sh_ref, offset, tm):
    a, b = divmod(offset, 8)
    if b == 0:
        return win_ref[pl.ds(8 * a, tm), :]
    return sh_ref[b - 1, pl.ds(8 * a, tm), :]


def _conv_in_specs(tm):
    per = tm // HALO
    prev = lambda col: pl.BlockSpec((HALO, CONV_W), lambda i: (jnp.maximum(i * per - 1, 0), col))
    return [_rows(tm, CONV_W, 4), _rows(tm, CONV_W, 5), prev(4), prev(5)]


def _conv_forward(z, conv_w_p, conv_b, ln_g, ln_b, w_pw2, name):
    S = z.shape[0]
    tm = ROW_TILE

    def body(a_ref, b_ref, ap_ref, bp_ref, w_ref, cb_ref, g_ref, be_ref, pw_ref, cv_ref, y_ref, win, sh):
        _conv_window(win, ap_ref[...].astype(F32), bp_ref[...].astype(F32), a_ref[...].astype(F32),
                     b_ref[...].astype(F32), pl.program_id(0) == 0)
        _shifted_copies(win, sh, tm)
        for r0 in range(0, tm, CONV_ROWS):
            acc = jnp.zeros((CONV_ROWS, CONV_W), F32)
            for kk in range(CONV_K):
                acc = acc + w_ref[kk:kk + 1, :] * _tap(win, sh, r0 + HALO - (CONV_K - 1) + kk, CONV_ROWS)
            cv_ref[r0:r0 + CONV_ROWS, :] = acc + cb_ref[...]
        cv = cv_ref[...]
        mu = jnp.mean(cv, axis=-1, keepdims=True)
        cc = cv - mu
        rstd = lax.rsqrt(jnp.mean(cc * cc, axis=-1, keepdims=True) + EPS)
        n = cc * rstd * g_ref[...] + be_ref[...]
        sl = n * _sigmoid(n)
        y_ref[...] = jnp.dot(sl.astype(BF16), pw_ref[...], preferred_element_type=F32)

    return pl.pallas_call(
        body, name=name, grid=(S // tm,),
        in_specs=_conv_in_specs(tm) + [_full((HALO, CONV_W)), _full((1, CONV_W)), _full((1, CONV_W)),
                                       _full((1, CONV_W)), _full((CONV_W, CONV_W))],
        out_specs=(_rows(tm, CONV_W), _rows(tm, CONV_W)),
        out_shape=(_sds((S, CONV_W)), _sds((S, CONV_W))),
        scratch_shapes=[pltpu.VMEM((tm + HALO, CONV_W), F32), pltpu.VMEM((7, tm + HALO - 8, CONV_W), F32)],
        compiler_params=_cparams("parallel"),
    )(z, z, z, z, conv_w_p, conv_b, ln_g, ln_b, w_pw2)


def _sgu_common(u, v, g_ref, be_ref):
    gu, dgu = _gelu_and_grad(u)
    gv, dgv = _gelu_and_grad(v)
    mu = jnp.mean(gv, axis=-1, keepdims=True)
    cc = gv - mu
    rstd = lax.rsqrt(jnp.mean(cc * cc, axis=-1, keepdims=True) + EPS)
    nh = cc * rstd
    vn = nh * g_ref[...] + be_ref[...]
    return gu, dgu, dgv, rstd, nh, vn


def _sgu_masks():
    lane_group = lax.broadcasted_iota(jnp.int32, (1, SGU_W), 1) // SGU_GD
    ri = lax.broadcasted_iota(jnp.int32, (SGU_T, SGU_T), 0)
    ci = lax.broadcasted_iota(jnp.int32, (SGU_T, SGU_T), 1)
    return [lane_group == g for g in range(SGU_G)], ci <= ri


def _sgu_forward(z, ln_g, ln_b, w_s, bias_full, name):
    S = z.shape[0]
    tm = ROW_TILE

    def body(u_ref, v_ref, g_ref, be_ref, ws_ref, bias_ref, y_ref):
        gmask, tril = _sgu_masks()
        wm = [jnp.where(tril, ws_ref[g], 0.0).astype(BF16) for g in range(SGU_G)]
        for ch in range(tm // SGU_T):
            rows = slice(ch * SGU_T, (ch + 1) * SGU_T)
            gu, _, _, _, _, vn = _sgu_common(u_ref[rows, :].astype(F32), v_ref[rows, :].astype(F32), g_ref, be_ref)
            vb = vn.astype(BF16)
            sv = bias_ref[...]
            for g in range(SGU_G):
                sv = sv + jnp.where(gmask[g], jnp.dot(wm[g], vb, preferred_element_type=F32), 0.0)
            y_ref[rows, :] = gu * sv

    return pl.pallas_call(
        body, name=name, grid=(S // tm,),
        in_specs=[_rows(tm, SGU_W, 7), _rows(tm, SGU_W, 8), _full((1, SGU_W)), _full((1, SGU_W)),
                  _full((SGU_G, SGU_T, SGU_T)), _full((SGU_T, SGU_W))],
        out_specs=_rows(tm, SGU_W), out_shape=_sds((S, SGU_W)),
        compiler_params=_cparams("parallel"),
    )(z, z, ln_g, ln_b, w_s, bias_full)


def _out_proj(x, z, y_att, y_conv, y_sgu, w_out, g_post, mod, name, target=None):
    S = x.shape[0]
    tm = ROW_TILE
    head = target is not None

    def body(*refs):
        x_ref, ga_ref, gc_ref, gs_ref, ya_ref, yc_ref, ys_ref, w_ref, gp_ref, mod_ref = refs[:10]
        t_ref = refs[10] if head else None
        xn_ref, y_ref, cat_ref = refs[10 + head:13 + head]
        ca = (ya_ref[...] * _silu_and_grad(ga_ref[...].astype(F32))[0]).astype(BF16)
        cc = (yc_ref[...] * _silu_and_grad(gc_ref[...].astype(F32))[0]).astype(BF16)
        cs = (ys_ref[...] * _silu_and_grad(gs_ref[...].astype(F32))[0]).astype(BF16)
        cat_ref[:, 0:ATT_W] = ca
        cat_ref[:, ATT_W:ATT_W + CONV_W] = cc
        cat_ref[:, ATT_W + CONV_W:] = cs
        y = (jnp.dot(ca, w_ref[0:ATT_W, :], preferred_element_type=F32)
             + jnp.dot(cc, w_ref[ATT_W:ATT_W + CONV_W, :], preferred_element_type=F32)
             + jnp.dot(cs, w_ref[ATT_W + CONV_W:, :], preferred_element_type=F32))
        y_ref[...] = y
        rstd = lax.rsqrt(jnp.mean(y * y, axis=-1, keepdims=True) + EPS)
        xn = x_ref[...] + mod_ref[2:3, :] * (y * rstd * gp_ref[...])
        if not head:
            xn_ref[...] = xn
            return
        loss_ref = refs[14]

        @pl.when(pl.program_id(0) == 0)
        def _():
            loss_ref[...] = jnp.zeros(loss_ref.shape, F32)

        err = xn - t_ref[...]
        xn_ref[...] = err * (1.0 / D)
        row = jnp.sum(err * err, axis=-1, keepdims=True) * (1.0 / D)
        loss_ref[...] += 0.5 * jnp.sum(row, axis=0, keepdims=True)

    return pl.pallas_call(
        body, name=name, grid=(S // tm,),
        in_specs=[_rows(tm, D), _rows(tm, 512, 1), _rows(tm, 256, 6), _rows(tm, 256, 9),
                  _rows(tm, ATT_W), _rows(tm, CONV_W), _rows(tm, SGU_W),
                  _full((D, D)), _full((1, D)), _full((3, D))] + ([_rows(tm, D)] if head else []),
        out_specs=[_rows(tm, D), _rows(tm, D), _rows(tm, D)] + ([_full((1, 1))] if head else []),
        out_shape=[_sds((S, D)), _sds((S, D)), _sds((S, D), BF16)] + ([_sds((1, 1))] if head else []),
        compiler_params=_cparams("arbitrary" if head else "parallel"),
    )(x, z, z, z, y_att, y_conv, y_sgu, w_out, g_post, mod, *([target] if head else []))


def _matmul_tn(a, b, name, comm=None):
    S, M = a.shape
    pieces = list(b) if isinstance(b, (list, tuple)) else [b]
    nb = len(pieces)
    N = sum(p.shape[1] for p in pieces)
    bk = min(MATMUL_TN_ROWS, S)
    riding = comm is not None
    comm = comm or _Comm([], [])
    nc = comm.n

    def body(*refs):
        a_ref, b_refs = refs[0], refs[1:1 + nb]
        c_src, o_ref = refs[1 + nb:1 + nb + nc], refs[1 + nb + nc]
        c_out = refs[2 + nb + nc:2 + nb + 2 * nc]
        rest = refs[2 + nb + 2 * nc:]
        b_sc, c_sems = (rest[0], rest[1:]) if nb > 1 else (None, rest)
        _riding(comm, S // bk, c_src, c_out, c_sems, "start")

        @pl.when(pl.program_id(0) == 0)
        def _():
            o_ref[...] = jnp.zeros(o_ref.shape, F32)

        if nb > 1:
            _assemble_columns(b_sc, b_refs)
            bv = b_sc[...]
        else:
            bv = b_refs[0][...]
        o_ref[...] += lax.dot_general(a_ref[...], bv, TN, preferred_element_type=F32)
        _riding(comm, S // bk, c_src, c_out, c_sems, "finish")

    outs = pl.pallas_call(
        body, name=name, grid=(S // bk,),
        in_specs=[pl.BlockSpec((bk, M), lambda k: (k, 0))]
                 + [pl.BlockSpec((bk, p.shape[1]), lambda k: (k, 0)) for p in pieces] + comm.specs,
        out_specs=[_full((M, N))] + comm.specs, out_shape=[_sds((M, N))] + comm.out_shape,
        scratch_shapes=([pltpu.VMEM((bk, N), BF16)] if nb > 1 else []) + comm.scratch,
        compiler_params=_cparams("arbitrary"),
    )(a, *pieces, *comm.srcs)
    return (outs[0], outs[1:]) if riding else outs[0]


def _out_proj_backward(dxo, y, z, y_att, y_conv, y_sgu, lse, w_out, g_post, mod, name):
    S = dxo.shape[0]
    tm = ROW_TILE

    def body(dxo_ref, y_ref, ga_ref, gc_ref, gs_ref, ya_ref, yc_ref, ys_ref, lse_ref, w_ref, gp_ref, mod_ref,
             dyb_ref, dob_ref, st_ref, dga_ref, dyc_ref, dgc_ref, dys_ref, dgs_ref, dgate_ref, dgp_ref):
        @pl.when(pl.program_id(0) == 0)
        def _():
            dgate_ref[...] = jnp.zeros(dgate_ref.shape, F32)
            dgp_ref[...] = jnp.zeros(dgp_ref.shape, F32)

        dxo_v = dxo_ref[...]
        yv = y_ref[...]
        gp = gp_ref[...]
        rstd = lax.rsqrt(jnp.mean(yv * yv, axis=-1, keepdims=True) + EPS)
        yhat = yv * rstd
        dgate_ref[...] += jnp.sum(dxo_v * (yhat * gp), axis=0, keepdims=True)
        dr = dxo_v * mod_ref[2:3, :]
        dgp_ref[...] += jnp.sum(dr * yhat, axis=0, keepdims=True)
        dyh = dr * gp
        dy = rstd * (dyh - yhat * jnp.mean(dyh * yhat, axis=-1, keepdims=True))
        dyb = dy.astype(BF16)
        dyb_ref[...] = dyb
        dcat = lax.dot_general(dyb, w_ref[...], NT, preferred_element_type=F32)

        ya = ya_ref[...]
        sil, dsil = _silu_and_grad(ga_ref[...].astype(F32))
        da = dcat[:, 0:ATT_W]
        do = da * sil
        dob_ref[...] = do.astype(BF16)
        dga_ref[...] = (da * ya * dsil).astype(BF16)
        lane = lax.broadcasted_iota(jnp.int32, (1, 128), 1)
        stats = jnp.zeros((tm, 128), F32)
        for h in range(HEADS):
            cols = slice(h * VDIM, (h + 1) * VDIM)
            delta = jnp.sum(do[:, cols] * ya[:, cols], axis=-1, keepdims=True)
            stats = stats + jnp.where(lane == 2 * h, lse_ref[h], 0.0) + jnp.where(lane == 2 * h + 1, delta, 0.0)
        st_ref[...] = stats

        sil, dsil = _silu_and_grad(gc_ref[...].astype(F32))
        dc = dcat[:, ATT_W:ATT_W + CONV_W]
        dyc_ref[...] = dc * sil
        dgc_ref[...] = (dc * yc_ref[...] * dsil).astype(BF16)
        sil, dsil = _silu_and_grad(gs_ref[...].astype(F32))
        dsg = dcat[:, ATT_W + CONV_W:]
        dys_ref[...] = dsg * sil
        dgs_ref[...] = (dsg * ys_ref[...] * dsil).astype(BF16)

    return pl.pallas_call(
        body, name=name, grid=(S // tm,),
        in_specs=[_rows(tm, D), _rows(tm, D), _rows(tm, 512, 1), _rows(tm, 256, 6), _rows(tm, 256, 9),
                  _rows(tm, ATT_W), _rows(tm, CONV_W), _rows(tm, SGU_W),
                  pl.BlockSpec((HEADS, tm, 1), lambda i: (0, i, 0)),
                  _full((D, D)), _full((1, D)), _full((3, D))],
        out_specs=(_rows(tm, D), _rows(tm, ATT_W), _rows(tm, 128), _rows(tm, ATT_W),
                   _rows(tm, CONV_W), _rows(tm, CONV_W), _rows(tm, SGU_W), _rows(tm, SGU_W),
                   _full((1, D)), _full((1, D))),
        out_shape=(_sds((S, D), BF16), _sds((S, ATT_W), BF16), _sds((S, 128)), _sds((S, ATT_W), BF16),
                   _sds((S, CONV_W)), _sds((S, CONV_W), BF16), _sds((S, SGU_W)), _sds((S, SGU_W), BF16),
                   _sds((1, D)), _sds((1, D))),
        compiler_params=_cparams("arbitrary"),
    )(dxo, y, z, z, z, y_att, y_conv, y_sgu, lse, w_out, g_post, mod)


def _flash_backward(q, k, v, do, stats, name, comm=None):
    S = q.shape[0]
    t = ATT_TILE
    tk = 2 * t
    nq = S // t
    comm = comm or _Comm([], [])
    nc = comm.n

    def body(*refs):
        q_ref, do_ref, st_ref, k_ref, v_ref = refs[:5]
        c_src = refs[5:5 + nc]
        dq_ref, dk_ref, dv_ref = refs[5 + nc:8 + nc]
        c_out = refs[8 + nc:8 + 2 * nc]
        dk_sc, dv_sc = refs[8 + 2 * nc:10 + 2 * nc]
        c_sems = refs[10 + 2 * nc:]
        h = pl.program_id(0)
        j = pl.program_id(1)
        if nc:
            @pl.when((h == 0) & (j == 0))
            def _():
                comm.start(c_src, c_out, c_sems)

        @pl.when(j == 0)
        def _():
            dq_ref[...] = jnp.zeros(dq_ref.shape, F32)

        dk_sc[...] = jnp.zeros(dk_sc.shape, F32)
        dv_sc[...] = jnp.zeros(dv_sc.shape, F32)
        lane = lax.broadcasted_iota(jnp.int32, (1, 128), 1)

        def chain(hf, qv, dov, lse2, delta, diagonal):
            kt = k_ref[hf * t:(hf + 1) * t, :]
            s = lax.dot_general(qv, kt, NT, preferred_element_type=F32)
            p = jnp.exp2(s * EXP2_SCALE - lse2)
            if diagonal:
                ri = lax.broadcasted_iota(jnp.int32, (t, t), 0)
                ci = lax.broadcasted_iota(jnp.int32, (t, t), 1)
                p = jnp.where(ci <= ri, p, 0.0)
            dv_sc[hf] += lax.dot_general(p.astype(BF16), dov, TN, preferred_element_type=F32)
            dp = lax.dot_general(dov, v_ref[hf * t:(hf + 1) * t, :], NT, preferred_element_type=F32)
            ds = (p * (dp - delta) * ATT_SCALE).astype(BF16)
            dk_sc[hf] += lax.dot_general(ds, qv, TN, preferred_element_type=F32)
            return jnp.dot(ds, kt, preferred_element_type=F32)

        def q_tile(qb, modes):
            rows = pl.ds(pl.multiple_of(qb * t, t), t)
            qv = q_ref[rows, :]
            dov = do_ref[rows, :]
            st = st_ref[rows, :]
            lse2 = jnp.sum(jnp.where(lane == 2 * h, st, 0.0), axis=-1, keepdims=True) * LOG2E
            delta = jnp.sum(jnp.where(lane == 2 * h + 1, st, 0.0), axis=-1, keepdims=True)
            parts = [chain(hf, qv, dov, lse2, delta, modes[hf]) for hf in range(2) if modes[hf] is not None]
            dq_ref[rows, :] += parts[0] if len(parts) == 1 else parts[0] + parts[1]

        q_tile(2 * j, (True, None))
        q_tile(2 * j + 1, (False, True))

        def loop_body(i, carry):
            q_tile(2 * (j + 1 + i), (False, False))
            q_tile(2 * (j + 1 + i) + 1, (False, False))
            return carry

        lax.fori_loop(0, nq // 2 - j - 1, loop_body, 0)
        for hf in range(2):
            dk_ref[hf * t:(hf + 1) * t, :] = dk_sc[hf]
            dv_ref[hf * t:(hf + 1) * t, :] = dv_sc[hf]

        if nc:
            @pl.when((h == HEADS - 1) & (j == S // tk - 1))
            def _():
                comm.finish(c_src, c_out, c_sems)

    outs = pl.pallas_call(
        body, name=name, grid=(HEADS, S // tk),
        in_specs=[pl.BlockSpec((S, HQ), lambda h, j: (0, h)),
                  pl.BlockSpec((S, VDIM), lambda h, j: (0, h)),
                  pl.BlockSpec((S, 128), lambda h, j: (0, 0)),
                  pl.BlockSpec((tk, HQ), lambda h, j: (j, h)),
                  pl.BlockSpec((tk, VDIM), lambda h, j: (j, h))] + comm.specs,
        out_specs=[pl.BlockSpec((S, HQ), lambda h, j: (0, h)),
                   pl.BlockSpec((tk, HQ), lambda h, j: (j, h)),
                   pl.BlockSpec((tk, VDIM), lambda h, j: (j, h))] + comm.specs,
        out_shape=[_sds((S, HEADS * HQ)), _sds((S, HEADS * HQ)), _sds((S, ATT_W))] + comm.out_shape,
        scratch_shapes=[pltpu.VMEM((2, t, HQ), F32), pltpu.VMEM((2, t, VDIM), F32)] + comm.scratch,
        compiler_params=_cparams("arbitrary", "arbitrary"),
    )(q, do, stats, k, v, *comm.srcs)
    return outs[0], outs[1], outs[2], outs[3:]


def _att_prep_backward(z, pos, dq, dk, dv, q_g, kv_g, wq_p, w_ukv, rope_rows, name):
    S = z.shape[0]
    tm = ROW_TILE

    def body(z_ref, pos_ref, dq_ref, dk_ref, dv_ref, qg_ref, kvg_ref, wq_ref, wkv_ref, rope_ref,
             dz_ref, qn_ref, dqp_ref, kvn_ref, dkv_ref, dqg_ref, dkvg_ref):
        @pl.when(pl.program_id(0) == 0)
        def _():
            dqg_ref[...] = jnp.zeros(dqg_ref.shape, F32)
            dkvg_ref[...] = jnp.zeros(dkvg_ref.shape, F32)

        zz = z_ref[...].astype(F32)
        ql, kvl = zz[:, 0:Q_RANK], zz[:, Q_RANK:Q_RANK + KV_RANK]
        q_rstd = lax.rsqrt(jnp.mean(ql * ql, axis=-1, keepdims=True) + EPS)
        kv_rstd = lax.rsqrt(jnp.mean(kvl * kvl, axis=-1, keepdims=True) + EPS)
        qhat, kvhat = ql * q_rstd, kvl * kv_rstd
        qg, kvg = qg_ref[...], kvg_ref[...]
        qn_ref[...] = (qhat * qg).astype(BF16)
        kvn_ref[...] = (kvhat * kvg).astype(BF16)
        ct, st = _rope_tables(pos_ref, rope_ref)

        def unrotate(d):
            return d * ct + _swap_halves(d * st)

        dkrot = jnp.zeros((tm, NOPE), F32)
        for h in range(HEADS):
            b = h * HQ
            dqp_ref[:, b:b + NOPE] = dq_ref[:, b:b + NOPE].astype(BF16)
            dqp_ref[:, b + NOPE:b + HQ] = unrotate(dq_ref[:, b + NOPE:b + HQ]).astype(BF16)
            dkv_ref[:, b:b + NOPE] = dk_ref[:, b:b + NOPE].astype(BF16)
            dkv_ref[:, b + NOPE:b + HQ] = dv_ref[:, h * VDIM:(h + 1) * VDIM].astype(BF16)
            dkrot = dkrot + dk_ref[:, b + NOPE:b + HQ]
        dqn = lax.dot_general(dqp_ref[...], wq_ref[...], NT, preferred_element_type=F32)
        dkvn = lax.dot_general(dkv_ref[...], wkv_ref[...], NT, preferred_element_type=F32)
        dqg_ref[...] += jnp.sum(dqn * qhat, axis=0, keepdims=True)
        dkvg_ref[...] += jnp.sum(dkvn * kvhat, axis=0, keepdims=True)
        dqh, dkvh = dqn * qg, dkvn * kvg
        dql = q_rstd * (dqh - qhat * jnp.mean(dqh * qhat, axis=-1, keepdims=True))
        dkvl = kv_rstd * (dkvh - kvhat * jnp.mean(dkvh * kvhat, axis=-1, keepdims=True))
        dz_ref[:, 0:Q_RANK] = dql.astype(BF16)
        dz_ref[:, Q_RANK:Q_RANK + KV_RANK] = dkvl.astype(BF16)
        dz_ref[:, Q_RANK + KV_RANK:] = unrotate(dkrot).astype(BF16)

    W = HEADS * HQ
    return pl.pallas_call(
        body, name=name, grid=(S // tm,),
        in_specs=[_rows(tm, 512, 0), _rows(tm, 1), _rows(tm, W), _rows(tm, W), _rows(tm, ATT_W),
                  _full((1, Q_RANK)), _full((1, KV_RANK)), _full((Q_RANK, W)), _full((KV_RANK, W)), _full((8, 128))],
        out_specs=(_rows(tm, 512), _rows(tm, Q_RANK), _rows(tm, W), _rows(tm, KV_RANK), _rows(tm, W),
                   _full((1, Q_RANK)), _full((1, KV_RANK))),
        out_shape=(_sds((S, 512), BF16), _sds((S, Q_RANK), BF16), _sds((S, W), BF16), _sds((S, KV_RANK), BF16),
                   _sds((S, W), BF16), _sds((1, Q_RANK)), _sds((1, KV_RANK))),
        compiler_params=_cparams("arbitrary"),
    )(z, pos, dq, dk, dv, q_g, kv_g, wq_p, w_ukv, rope_rows)


def _conv_norm_backward(dyc, cv, ln_g, ln_b, w_pw2, name):
    S = cv.shape[0]
    tm = ROW_TILE

    def body(dy_ref, cv_ref, g_ref, be_ref, pw_ref, dcv_ref, sl_ref, dyb_ref, dg_ref, db_ref, dcb_ref):
        @pl.when(pl.program_id(0) == 0)
        def _():
            dg_ref[...] = jnp.zeros(dg_ref.shape, F32)
            db_ref[...] = jnp.zeros(db_ref.shape, F32)
            dcb_ref[...] = jnp.zeros(dcb_ref.shape, F32)

        cv_v = cv_ref[...]
        mu = jnp.mean(cv_v, axis=-1, keepdims=True)
        cc = cv_v - mu
        rstd = lax.rsqrt(jnp.mean(cc * cc, axis=-1, keepdims=True) + EPS)
        nh = cc * rstd
        g = g_ref[...]
        n = nh * g + be_ref[...]
        sil, dsil = _silu_and_grad(n)
        sl_ref[...] = sil.astype(BF16)
        dyb = dy_ref[...].astype(BF16)
        dyb_ref[...] = dyb
        dn = lax.dot_general(dyb, pw_ref[...], NT, preferred_element_type=F32) * dsil
        db_ref[...] += jnp.sum(dn, axis=0, keepdims=True)
        dg_ref[...] += jnp.sum(dn * nh, axis=0, keepdims=True)
        dnh = dn * g
        dcv = rstd * (dnh - jnp.mean(dnh, axis=-1, keepdims=True) - nh * jnp.mean(dnh * nh, axis=-1, keepdims=True))
        dcv_ref[...] = dcv
        dcb_ref[...] += jnp.sum(dcv, axis=0, keepdims=True)

    vec = _full((1, CONV_W))
    return pl.pallas_call(
        body, name=name, grid=(S // tm,),
        in_specs=[_rows(tm, CONV_W), _rows(tm, CONV_W), vec, vec, _full((CONV_W, CONV_W))],
        out_specs=(_rows(tm, CONV_W), _rows(tm, CONV_W), _rows(tm, CONV_W), vec, vec, vec),
        out_shape=(_sds((S, CONV_W)), _sds((S, CONV_W), BF16), _sds((S, CONV_W), BF16),
                   _sds((1, CONV_W)), _sds((1, CONV_W)), _sds((1, CONV_W))),
        compiler_params=_cparams("arbitrary"),
    )(dyc, cv, ln_g, ln_b, w_pw2)


def _conv_backward(z, dcv, conv_w_p, name):
    S = z.shape[0]
    tm = ROW_TILE
    per = tm // HALO
    last_halo = S // HALO - 1

    def body(a_ref, b_ref, ap_ref, bp_ref, d_ref, dn_ref, w_ref, da_ref, db_ref, dw_ref, win, dwin, dw_acc, sh, dsh):
        i = pl.program_id(0)

        @pl.when(i == 0)
        def _():
            dw_acc[...] = jnp.zeros(dw_acc.shape, F32)

        av, bv = a_ref[...].astype(F32), b_ref[...].astype(F32)
        _conv_window(win, ap_ref[...].astype(F32), bp_ref[...].astype(F32), av, bv, i == 0)
        dcur = d_ref[...]
        dwin[0:tm, :] = dcur
        dwin[tm:, :] = jnp.where(i == pl.num_programs(0) - 1, 0.0, dn_ref[...])
        _shifted_copies(win, sh, tm)
        _shifted_copies(dwin, dsh, tm)
        for r0 in range(0, tm, CONV_ROWS):
            dchunk = d_ref[r0:r0 + CONV_ROWS, :]
            dh = jnp.zeros((CONV_ROWS, CONV_W), F32)
            for kk in range(CONV_K):
                dh = dh + w_ref[kk:kk + 1, :] * _tap(dwin, dsh, r0 + CONV_K - 1 - kk, CONV_ROWS)
                prod = dchunk * _tap(win, sh, r0 + HALO - (CONV_K - 1) + kk, CONV_ROWS)
                dw_acc[kk] += jnp.sum(prod.reshape(CONV_ROWS // 8, 8, CONV_W), axis=0)
            sb = _sigmoid(b_ref[r0:r0 + CONV_ROWS, :].astype(F32))
            da_ref[r0:r0 + CONV_ROWS, :] = (dh * sb).astype(BF16)
            db_ref[r0:r0 + CONV_ROWS, :] = (dh * a_ref[r0:r0 + CONV_ROWS, :].astype(F32) * sb * (1.0 - sb)).astype(BF16)

        @pl.when(i == pl.num_programs(0) - 1)
        def _():
            dw_ref[...] = jnp.sum(dw_acc[...], axis=1)

    nxt = pl.BlockSpec((HALO, CONV_W), lambda i: (jnp.minimum((i + 1) * per, last_halo), 0))
    return pl.pallas_call(
        body, name=name, grid=(S // tm,),
        in_specs=_conv_in_specs(tm) + [_rows(tm, CONV_W), nxt, _full((HALO, CONV_W))],
        out_specs=(_rows(tm, CONV_W), _rows(tm, CONV_W), _full((HALO, CONV_W))),
        out_shape=(_sds((S, CONV_W), BF16), _sds((S, CONV_W), BF16), _sds((HALO, CONV_W))),
        scratch_shapes=[pltpu.VMEM((tm + HALO, CONV_W), F32), pltpu.VMEM((tm + HALO, CONV_W), F32),
                        pltpu.VMEM((HALO, 8, CONV_W), F32), pltpu.VMEM((7, tm + HALO - 8, CONV_W), F32),
                        pltpu.VMEM((7, tm + HALO - 8, CONV_W), F32)],
        compiler_params=_cparams("arbitrary"),
    )(z, z, z, z, dcv, dcv, conv_w_p)


def _sgu_backward(z, dy, ln_g, ln_b, w_s, bias_full, name):
    S = z.shape[0]
    tm = ROW_TILE

    def body(u_ref, v_ref, dy_ref, g_ref, be_ref, ws_ref, bias_ref, du_ref, dv_ref, dws_ref, dbs_ref, dg_ref, db_ref):
        @pl.when(pl.program_id(0) == 0)
        def _():
            dws_ref[...] = jnp.zeros(dws_ref.shape, F32)
            dbs_ref[...] = jnp.zeros(dbs_ref.shape, F32)
            dg_ref[...] = jnp.zeros(dg_ref.shape, F32)
            db_ref[...] = jnp.zeros(db_ref.shape, F32)

        gmask, tril = _sgu_masks()
        lane = lax.broadcasted_iota(jnp.int32, (1, 128), 1)
        wm = [jnp.where(tril, ws_ref[g], 0.0).astype(BF16) for g in range(SGU_G)]
        gain = g_ref[...]
        for ch in range(tm // SGU_T):
            rows = slice(ch * SGU_T, (ch + 1) * SGU_T)
            gu, dgu, dgv, rstd, nh, vn = _sgu_common(u_ref[rows, :].astype(F32), v_ref[rows, :].astype(F32), g_ref, be_ref)
            vb = vn.astype(BF16)
            sv = bias_ref[...]
            for g in range(SGU_G):
                sv = sv + jnp.where(gmask[g], jnp.dot(wm[g], vb, preferred_element_type=F32), 0.0)
            dyv = dy_ref[rows, :]
            du_ref[rows, :] = (dyv * sv * dgu).astype(BF16)
            dsv = dyv * gu
            dsvb = dsv.astype(BF16)
            dvn = jnp.zeros((SGU_T, SGU_W), F32)
            for g in range(SGU_G):
                dsg = jnp.where(gmask[g], dsv, 0.0)
                dwg = lax.dot_general(dsg.astype(BF16), vb, NT, preferred_element_type=F32)
                dws_ref[g] += jnp.where(tril, dwg, 0.0)
                dvn = dvn + jnp.where(gmask[g], lax.dot_general(wm[g], dsvb, TN, preferred_element_type=F32), 0.0)
                dbs_ref[...] += jnp.where(lane == g, jnp.sum(dsg, axis=-1, keepdims=True), 0.0)
            db_ref[...] += jnp.sum(dvn, axis=0, keepdims=True)
            dg_ref[...] += jnp.sum(dvn * nh, axis=0, keepdims=True)
            dnh = dvn * gain
            dgvv = rstd * (dnh - jnp.mean(dnh, axis=-1, keepdims=True) - nh * jnp.mean(dnh * nh, axis=-1, keepdims=True))
            dv_ref[rows, :] = (dgvv * dgv).astype(BF16)

    vec = _full((1, SGU_W))
    return pl.pallas_call(
        body, name=name, grid=(S // tm,),
        in_specs=[_rows(tm, SGU_W, 7), _rows(tm, SGU_W, 8), _rows(tm, SGU_W), vec, vec,
                  _full((SGU_G, SGU_T, SGU_T)), _full((SGU_T, SGU_W))],
        out_specs=(_rows(tm, SGU_W), _rows(tm, SGU_W), _full((SGU_G, SGU_T, SGU_T)), _full((SGU_T, 128)), vec, vec),
        out_shape=(_sds((S, SGU_W), BF16), _sds((S, SGU_W), BF16), _sds((SGU_G, SGU_T, SGU_T)), _sds((SGU_T, 128)),
                   _sds((1, SGU_W)), _sds((1, SGU_W))),
        compiler_params=_cparams("arbitrary"),
    )(z, z, dy, ln_g, ln_b, w_s, bias_full)


DZ_PIECES = (512, 512, 256, 256, 256, 256, 256, 256)


def _assemble_columns(dst_ref, pieces):
    off = 0
    for piece in pieces:
        wdt = piece.shape[1]
        dst_ref[:, off:off + wdt] = piece[...]
        off += wdt


def _inproj_backward(x, dxo, dz_att, dga, dca, dcb, dgc, dsu, dsv, dgs, g_pre, mod, w_in_p, name):
    S = x.shape[0]
    tm = ROW_TILE

    def body(x_ref, dxo_ref, p0, p1, p2, p3, p4, p5, p6, p7, g_ref, mod_ref, w_ref,
             dx_ref, hb_ref, dmod_ref, dg_ref, dz_sc):
        @pl.when(pl.program_id(0) == 0)
        def _():
            dmod_ref[...] = jnp.zeros(dmod_ref.shape, F32)
            dg_ref[...] = jnp.zeros(dg_ref.shape, F32)

        _assemble_columns(dz_sc, (p0, p1, p2, p3, p4, p5, p6, p7))
        dh = lax.dot_general(dz_sc[...], w_ref[...], NT, preferred_element_type=F32)
        xv = x_ref[...]
        g = g_ref[...]
        one_scale = 1.0 + mod_ref[1:2, :]
        rstd = lax.rsqrt(jnp.mean(xv * xv, axis=-1, keepdims=True) + EPS)
        xhat = xv * rstd
        xg = xhat * g
        hb_ref[...] = (xg * one_scale + mod_ref[0:1, :]).astype(BF16)
        dmod_ref[0:1, :] += jnp.sum(dh, axis=0, keepdims=True)
        dmod_ref[1:2, :] += jnp.sum(dh * xg, axis=0, keepdims=True)
        dhs = dh * one_scale
        dg_ref[...] += jnp.sum(dhs * xhat, axis=0, keepdims=True)
        dxh = dhs * g
        dx_ref[...] = dxo_ref[...] + rstd * (dxh - xhat * jnp.mean(dxh * xhat, axis=-1, keepdims=True))

    return pl.pallas_call(
        body, name=name, grid=(S // tm,),
        in_specs=[_rows(tm, D), _rows(tm, D)] + [_rows(tm, w) for w in DZ_PIECES]
                 + [_full((1, D)), _full((3, D)), _full((D, DZ))],
        out_specs=(_rows(tm, D), _rows(tm, D), _full((2, D)), _full((1, D))),
        out_shape=(_sds((S, D)), _sds((S, D), BF16), _sds((2, D)), _sds((1, D))),
        scratch_shapes=[pltpu.VMEM((tm, DZ), BF16)],
        compiler_params=_cparams("arbitrary"),
    )(x, dxo, dz_att, dga, dca, dcb, dgc, dsu, dsv, dgs, g_pre, mod, w_in_p)


def _adamw(w, gparts, m, v, name):
    shape = w.shape
    cols = shape[-1]
    rows = int(np.prod(shape[:-1]))
    parts = gparts.shape[0]
    w2, m2, v2 = (a.reshape(rows, cols) for a in (w, m, v))
    g3 = gparts.reshape(parts, rows, cols)
    tr = rows
    for cand in (256, 128):
        if rows > cand and rows % cand == 0:
            tr = cand
            break

    def body(w_ref, g_ref, m_ref, v_ref, go_ref, d_ref, mo_ref, vo_ref):
        g = g_ref[0].astype(F32)
        for p in range(1, parts):
            g = g + g_ref[p].astype(F32)
        wv = w_ref[...]
        mn = ADAM_B1 * m_ref[...] + (1.0 - ADAM_B1) * g
        vn = ADAM_B2 * v_ref[...] + (1.0 - ADAM_B2) * (g * g)
        m_hat = mn / (1.0 - ADAM_B1 ** ADAM_STEP)
        v_hat = vn / (1.0 - ADAM_B2 ** ADAM_STEP)
        go_ref[...] = g
        d_ref[...] = -ADAM_LR * (m_hat / (jnp.sqrt(v_hat) + ADAM_EPS) + ADAM_WD * wv)
        mo_ref[...] = mn
        vo_ref[...] = vn

    blk = pl.BlockSpec((tr, cols), lambda i: (i, 0))
    outs = pl.pallas_call(
        body, name=name, grid=(rows // tr,),
        in_specs=[blk, pl.BlockSpec((parts, tr, cols), lambda i: (0, i, 0)), blk, blk],
        out_specs=(blk, blk, blk, blk),
        out_shape=tuple(_sds((rows, cols)) for _ in range(4)),
        compiler_params=_cparams("parallel"),
    )(w2, g3, m2, v2)
    return tuple(o.reshape(shape) for o in outs)


_GATHERED = ("w_in", "w_out", "w_uq", "w_ukv", "w_pw2", "conv_w")
_COL_SHARDED = ("w_in", "w_uq", "w_ukv", "conv_w")

_SMALL = (("dmod", (3 * D,)), ("g_pre", (D,)), ("g_post", (D,)), ("q_norm_g", (Q_RANK,)),
          ("kv_norm_g", (KV_RANK,)), ("conv_b", (CONV_W,)), ("conv_ln_g", (CONV_W,)),
          ("conv_ln_b", (CONV_W,)), ("sgu_ln_g", (SGU_W,)), ("sgu_ln_b", (SGU_W,)),
          ("w_s", (SGU_G, SGU_T, SGU_T)), ("b_s", (SGU_G, SGU_T)))


def _assemble(name, parts):
    if name in _COL_SHARDED:
        p = jnp.moveaxis(parts, 0, 1)
        return p.reshape(p.shape[0], p.shape[1] * p.shape[2])
    return parts.reshape(parts.shape[0] * parts.shape[1], parts.shape[2])


def _scatter_layout(name, full):
    if name in _COL_SHARDED:
        return jnp.moveaxis(full.reshape(full.shape[0], N_DEV, full.shape[1] // N_DEV), 1, 0)
    return full.reshape(N_DEV, full.shape[0] // N_DEV, full.shape[1])


def kernel(x, c, positions, w_ada, b_ada, g_pre, g_post, w_in, q_norm_g, w_uq, kv_norm_g, w_ukv, conv_w, conv_b, conv_ln_g, conv_ln_b, w_pw2, sgu_ln_g, sgu_ln_b, w_s, b_s, w_out, loss_target, m_w_ada, m_b_ada, m_g_pre, m_g_post, m_w_in, m_q_norm_g, m_w_uq, m_kv_norm_g, m_w_ukv, m_conv_w, m_conv_b, m_conv_ln_g, m_conv_ln_b, m_w_pw2, m_sgu_ln_g, m_sgu_ln_b, m_w_s, m_b_s, m_w_out, v_w_ada, v_b_ada, v_g_pre, v_g_post, v_w_in, v_q_norm_g, v_w_uq, v_kv_norm_g, v_w_ukv, v_conv_w, v_conv_b, v_conv_ln_g, v_conv_ln_b, v_w_pw2, v_sgu_ln_g, v_sgu_ln_b, v_w_s, v_b_s, v_w_out):
    weights = dict(w_ada=w_ada, b_ada=b_ada, g_pre=g_pre, g_post=g_post, w_in=w_in, q_norm_g=q_norm_g, w_uq=w_uq,
                   kv_norm_g=kv_norm_g, w_ukv=w_ukv, conv_w=conv_w, conv_b=conv_b, conv_ln_g=conv_ln_g,
                   conv_ln_b=conv_ln_b, w_pw2=w_pw2, sgu_ln_g=sgu_ln_g, sgu_ln_b=sgu_ln_b, w_s=w_s, b_s=b_s, w_out=w_out)
    m_in = dict(w_ada=m_w_ada, b_ada=m_b_ada, g_pre=m_g_pre, g_post=m_g_post, w_in=m_w_in, q_norm_g=m_q_norm_g,
                w_uq=m_w_uq, kv_norm_g=m_kv_norm_g, w_ukv=m_w_ukv, conv_w=m_conv_w, conv_b=m_conv_b,
                conv_ln_g=m_conv_ln_g, conv_ln_b=m_conv_ln_b, w_pw2=m_w_pw2, sgu_ln_g=m_sgu_ln_g,
                sgu_ln_b=m_sgu_ln_b, w_s=m_w_s, b_s=m_b_s, w_out=m_w_out)
    v_in = dict(w_ada=v_w_ada, b_ada=v_b_ada, g_pre=v_g_pre, g_post=v_g_post, w_in=v_w_in, q_norm_g=v_q_norm_g,
                w_uq=v_w_uq, kv_norm_g=v_kv_norm_g, w_ukv=v_w_ukv, conv_w=v_conv_w, conv_b=v_conv_b,
                conv_ln_g=v_conv_ln_g, conv_ln_b=v_conv_ln_b, w_pw2=v_w_pw2, sgu_ln_g=v_sgu_ln_g,
                sgu_ln_b=v_sgu_ln_b, w_s=v_w_s, b_s=v_b_s, w_out=v_w_out)
    order = list(weights)

    S = x.shape[1]
    me = 4 * lax.axis_index("x") + 2 * lax.axis_index("y") + lax.axis_index("c")
    x0 = x.reshape(S, D)
    target = loss_target.reshape(S, D)
    pos = positions.reshape(S, 1)

    def shards(l):
        return [weights[n][l].astype(BF16) for n in _GATHERED]

    def w_in_operand(part):
        w_in_f = _assemble("w_in", part)
        return jnp.concatenate([w_in_f[:, :ATT_IN], jnp.zeros((D, PAD_IN), BF16), w_in_f[:, ATT_IN:]], axis=1)

    def other_operands(parts):
        full = {n: _assemble(n, p) for n, p in zip(_GATHERED[1:], parts)}
        wq = jnp.pad(full["w_uq"].reshape(Q_RANK, HEADS, QK), ((0, 0), (0, 0), (0, HQ - QK)))
        return dict(w_uq=wq.reshape(Q_RANK, HEADS * HQ), w_ukv=full["w_ukv"], w_pw2=full["w_pw2"], w_out=full["w_out"],
                    conv_w=jnp.pad(full["conv_w"].astype(F32), ((0, HALO - CONV_K), (0, 0))))

    first_w_in, c_parts = _gather_two_level([shards(0)[0], c.reshape(8, D // 8)], name="gather_w_in_0")
    lw = [dict(w_in=w_in_operand(first_w_in))] + [None] * (DEPTH - 1)
    c_all = c_parts.reshape(N_DEV, D)

    ada_cols = w_ada.shape[-1]
    b_cols = lax.dynamic_slice_in_dim(b_ada, me * ada_cols, ada_cols, axis=1)
    sc_rows, mod_part = _ada_forward(jnp.pad(c_all, ((0, 8), (0, 0))), w_ada, b_cols)
    mod_recv = _exchange([jnp.moveaxis(mod_part[:, :N_DEV], 1, 0)], [True], name="exchange_mod")[0]
    mod = jnp.moveaxis(mod_recv, 0, 1).reshape(DEPTH, 3, D)

    bias_full = jnp.repeat(jnp.swapaxes(b_s, 1, 2), SGU_GD, axis=2)
    inv_freq = ROPE_THETA ** (-jnp.arange(0, ROPE, 2, dtype=F32) / ROPE)
    zeros32 = jnp.zeros((ROPE // 2,), F32)
    ones32 = jnp.ones((ROPE // 2,), F32)
    rope_rows = jnp.zeros((8, 128), F32)
    rope_rows = rope_rows.at[0].set(jnp.concatenate([inv_freq, inv_freq, zeros32, zeros32]))
    rope_rows = rope_rows.at[1].set(jnp.concatenate([ones32, ones32, zeros32, zeros32]))
    rope_rows = rope_rows.at[2].set(jnp.concatenate([-ones32, ones32, zeros32, zeros32]))

    def vec(a, l):
        return a[l].reshape(1, -1)

    saved = []
    xl = x0
    for l in range(DEPTH):
        w = lw[l]
        late = _Comm(shards(0)[1:], [False] * (len(_GATHERED) - 1)) if l == 0 else None
        z, arrived = _prenorm_inproj(xl, vec(g_pre, l), mod[l], w["w_in"], name=f"prenorm_inproj_{l}", comm=late)
        if late is not None:
            w.update(other_operands(arrived))
        q, k, v = _att_prep(z, pos, vec(q_norm_g, l), vec(kv_norm_g, l), w["w_uq"], w["w_ukv"], rope_rows,
                            name=f"att_prep_{l}")
        ahead = _Comm(shards(l + 1), [False] * len(_GATHERED)) if l + 1 < DEPTH else None
        y_att, lse, arrived = _flash_forward(q, k, v, name=f"flash_forward_{l}", comm=ahead)
        if ahead is not None:
            lw[l + 1] = dict(w_in=w_in_operand(arrived[0]), **other_operands(arrived[1:]))
        cv, y_conv = _conv_forward(z, w["conv_w"], vec(conv_b, l), vec(conv_ln_g, l), vec(conv_ln_b, l),
                                   w["w_pw2"], name=f"conv_forward_{l}")
        y_sgu = _sgu_forward(z, vec(sgu_ln_g, l), vec(sgu_ln_b, l), w_s[l], bias_full[l], name=f"sgu_forward_{l}")
        outs = _out_proj(xl, z, y_att, y_conv, y_sgu, w["w_out"], vec(g_post, l), mod[l], name=f"out_proj_{l}",
                         target=target if l == DEPTH - 1 else None)
        saved.append(dict(x=xl, z=z, q=q, k=k, v=v, y_att=y_att, lse=lse, cv=cv, y_conv=y_conv, y_sgu=y_sgu,
                          y=outs[1], ycat=outs[2]))
        xl = outs[0]

    dx = xl
    loss = lax.psum(outs[3].reshape(()), ("x", "y", "c"))

    spack = _Packer(_SMALL, 8)
    grad_kinds = [True] * len(_GATHERED) + [False]
    received = [None] * DEPTH
    pending = None
    for l in reversed(range(DEPTH)):
        sv = saved[l]
        w = lw[l]
        (dyb, dob, stats, dga, dyc, dgc, dys, dgs, dgate, dgpost) = _out_proj_backward(
            dx, sv["y"], sv["z"], sv["y_att"], sv["y_conv"], sv["y_sgu"], sv["lse"], w["w_out"],
            vec(g_post, l), mod[l], name=f"out_proj_backward_{l}")
        big = dict(w_out=_matmul_tn(sv["ycat"], dyb, name=f"grad_w_out_{l}"))
        riding = _Comm(pending, grad_kinds) if pending is not None else None
        dq, dk, dv, arrived = _flash_backward(sv["q"], sv["k"], sv["v"], dob, stats, name=f"flash_backward_{l}",
                                              comm=riding)
        if riding is not None:
            received[l + 1] = arrived
        dz_att, qn_b, dqp_b, kvn_b, dkv_b, dqg, dkvg = _att_prep_backward(
            sv["z"], pos, dq, dk, dv, vec(q_norm_g, l), vec(kv_norm_g, l), w["w_uq"], w["w_ukv"], rope_rows,
            name=f"att_prep_backward_{l}")
        dwq_p = _matmul_tn(qn_b, dqp_b, name=f"grad_w_uq_{l}")
        big["w_uq"] = dwq_p.reshape(Q_RANK, HEADS, HQ)[:, :, :QK].reshape(Q_RANK, HEADS * QK)
        big["w_ukv"] = _matmul_tn(kvn_b, dkv_b, name=f"grad_w_ukv_{l}")
        dcv, sl_b, dyc_b, dclg, dclb, dcb = _conv_norm_backward(dyc, sv["cv"], vec(conv_ln_g, l), vec(conv_ln_b, l),
                                                              w["w_pw2"], name=f"conv_norm_backward_{l}")
        big["w_pw2"] = _matmul_tn(sl_b, dyc_b, name=f"grad_w_pw2_{l}")
        dca, dcbb, dconvw = _conv_backward(sv["z"], dcv, w["conv_w"], name=f"conv_backward_{l}")
        big["conv_w"] = dconvw[:CONV_K]
        dsu, dsvv, dws, dbs, dslg, dslb = _sgu_backward(sv["z"], dys, vec(sgu_ln_g, l), vec(sgu_ln_b, l), w_s[l],
                                                       bias_full[l], name=f"sgu_backward_{l}")
        dz_pieces = [dz_att, dga, dca, dcbb, dgc, dsu, dsvv, dgs]
        dx, h_b, dmod2, dgpre = _inproj_backward(sv["x"], dx, *dz_pieces, vec(g_pre, l), mod[l], w["w_in"],
                                                 name=f"inproj_backward_{l}")
        small = dict(dmod=jnp.concatenate([dmod2.reshape(-1), dgate.reshape(-1)]), g_pre=dgpre, g_post=dgpost,
                     q_norm_g=dqg, kv_norm_g=dkvg, conv_b=dcb, conv_ln_g=dclg, conv_ln_b=dclb, sgu_ln_g=dslg,
                     sgu_ln_b=dslb, w_s=dws, b_s=jnp.swapaxes(dbs[:, :SGU_G], 0, 1))
        rest = [_scatter_layout(n, big[n]).astype(BF16) for n in _GATHERED[1:]] + [spack.pack(small, F32)]
        early = _Comm(rest, grad_kinds[1:]) if l == 0 else None
        dwin_p = _matmul_tn(h_b, dz_pieces, name=f"grad_w_in_{l}", comm=early)
        if early is not None:
            dwin_p, rest_arrived = dwin_p
        dwin = jnp.concatenate([dwin_p[:, :ATT_IN], dwin_p[:, ATT_IN + PAD_IN:]], axis=1)
        pending = [_scatter_layout("w_in", dwin).astype(BF16)] + rest
    grad_x = dx.reshape(1, S, D)
    received[0] = list(_exchange(pending[:1], grad_kinds[:1], name="exchange_grad_w_in_0")) + list(rest_arrived)

    gparts = {n: jnp.stack([received[l][i] for l in range(DEPTH)], axis=1) for i, n in enumerate(_GATHERED)}
    sparts = [spack.unpack(received[l][-1], (N_DEV,)) for l in range(DEPTH)]
    sparts = {n: jnp.stack([sparts[l][n] for l in range(DEPTH)], axis=1) for n, _ in _SMALL}
    dmod_all = sparts["dmod"]
    dmod_cols = lax.dynamic_slice_in_dim(dmod_all, me * ada_cols, ada_cols, axis=2)
    sc_t = jnp.pad(sc_rows[:N_DEV].T, ((0, 0), (0, 128 - N_DEV)))
    dmod_rows = jnp.pad(jnp.moveaxis(dmod_cols, 0, 1), ((0, 0), (0, 128 - N_DEV), (0, 0)))
    gparts["w_ada"] = _ada_backward(sc_t, dmod_rows)[None]
    gparts["b_ada"] = dmod_all
    for n, _ in _SMALL[1:]:
        gparts[n] = sparts[n]

    grads, deltas, new_m, new_v = {}, {}, {}, {}
    for n in order:
        grads[n], deltas[n], new_m[n], new_v[n] = _adamw(weights[n], gparts[n], m_in[n], v_in[n], name=f"adamw_{n}")
    return (loss, grad_x, *[grads[n] for n in order], *[deltas[n] for n in order],
            *[new_m[n] for n in order], *[new_v[n] for n in order])
```

```python
import functools
import math

import numpy as np
import jax
import jax.numpy as jnp
from jax import lax
from jax.experimental import pallas as pl
from jax.experimental.pallas import tpu as pltpu

F32 = jnp.float32
BF16 = jnp.bfloat16

N_DEV = 8
DEPTH = 2
D = 1024
HEADS = 4
NOPE = 128
ROPE = 64
VDIM = 128
QK = NOPE + ROPE
Q_RANK = 256
KV_RANK = 128
ATT_W = HEADS * VDIM
CONV_W = 256
CONV_K = 31
SGU_W = 256
SGU_G = 4
SGU_GD = SGU_W // SGU_G
SGU_T = 128
D_IN = 2496
ATT_IN = Q_RANK + KV_RANK + ROPE
PAD_IN = 64
DZ = D_IN + PAD_IN
HQ = 2 * NOPE
EPS = 1e-6
ROPE_THETA = 10000.0
ATT_SCALE = QK ** -0.5
LOG2E = math.log2(math.e)
EXP2_SCALE = ATT_SCALE * LOG2E
NEG_INF = float("-inf")

ADAM_LR = 0.001
ADAM_B1 = 0.9
ADAM_B2 = 0.999
ADAM_EPS = 1e-08
ADAM_WD = 0.01
ADAM_STEP = 10

VMEM_LIMIT = 56 * 1024 * 1024
ROW_TILE = 512
MATMUL_TN_ROWS = 1024
ATT_TILE = 512
FLASH_UNROLL = 4
HALO = 32
CONV_ROWS = 64
PACK_LANES = 128

MESH = pl.DeviceIdType.MESH
NT = (((1,), (1,)), ((), ()))
TN = (((0,), (0,)), ((), ()))


def _cparams(*sem):
    return pltpu.CompilerParams(dimension_semantics=sem, vmem_limit_bytes=VMEM_LIMIT)


def _sds(shape, dtype=F32):
    return jax.ShapeDtypeStruct(tuple(shape), dtype)


def _rows(tm, width, col=0):
    return pl.BlockSpec((tm, width), lambda i: (i, col))


def _full(shape):
    nd = len(shape)
    return pl.BlockSpec(tuple(shape), lambda *_: (0,) * nd)


def _sigmoid(x):
    return 1.0 / (1.0 + jnp.exp(-x))


def _silu_and_grad(g):
    s = _sigmoid(g)
    return g * s, s * (1.0 + g * (1.0 - s))


def _gelu_and_grad(x):
    cdf = 0.5 * (1.0 + lax.erf(x * (1.0 / math.sqrt(2.0))))
    pdf = jnp.exp(-0.5 * x * x) * (1.0 / math.sqrt(2.0 * math.pi))
    return x * cdf, cdf + x * pdf


def _swap_halves(a):
    lane = lax.broadcasted_iota(jnp.int32, a.shape, 1)
    up = pltpu.roll(a, 32, 1)
    down = pltpu.roll(a, 96, 1)
    return jnp.where(lane < 32, down, jnp.where(lane < 64, up, 0.0))


def _rope_tables(pos_ref, rope_ref):
    ang = pos_ref[...].astype(F32) * rope_ref[0:1, :]
    return jnp.cos(ang) * rope_ref[1:2, :], jnp.sin(ang) * rope_ref[2:3, :]


class _Comm:
    def __init__(self, srcs, kinds):
        self.srcs = list(srcs)
        self.kinds = list(kinds)
        self.n = len(self.srcs)
        self.out_shape = [_sds((N_DEV,) + tuple(s.shape[1:] if k else s.shape), s.dtype)
                          for s, k in zip(self.srcs, self.kinds)]
        self.specs = [pl.BlockSpec(memory_space=pl.ANY)] * self.n
        self.scratch = [pltpu.SemaphoreType.DMA((self.n, N_DEV - 1)), pltpu.SemaphoreType.DMA((self.n, N_DEV - 1)),
                        pltpu.SemaphoreType.DMA((self.n,))] if self.n else []

    def _copies(self, src_refs, out_refs, sems, with_recvs):
        send_sems, recv_sems, local_sems = sems
        x, y, c = lax.axis_index("x"), lax.axis_index("y"), lax.axis_index("c")
        me = 4 * x + 2 * y + c
        local, sends, recvs = [], [], []
        for a in range(self.n):
            def block_for(dest, src_ref=src_refs[a], a2a=self.kinds[a]):
                return src_ref.at[dest] if a2a else src_ref

            local.append(pltpu.make_async_copy(block_for(me), out_refs[a].at[me], local_sems.at[a]))
            for r in range(1, N_DEV):
                px = 1 - x if (r >> 2) & 1 else x
                py = 1 - y if (r >> 1) & 1 else y
                pc = 1 - c if r & 1 else c
                peer = 4 * px + 2 * py + pc
                sends.append(pltpu.make_async_remote_copy(
                    src_ref=block_for(peer), dst_ref=out_refs[a].at[me],
                    send_sem=send_sems.at[a, r - 1], recv_sem=recv_sems.at[a, r - 1],
                    device_id=(px, py, pc), device_id_type=MESH))
                if with_recvs:
                    recvs.append(pltpu.make_async_remote_copy(
                        src_ref=block_for(me), dst_ref=out_refs[a].at[peer],
                        send_sem=send_sems.at[a, r - 1], recv_sem=recv_sems.at[a, r - 1],
                        device_id=(px, py, pc), device_id_type=MESH))
        return local, sends, recvs

    def start(self, src_refs, out_refs, sems):
        local, sends, _ = self._copies(src_refs, out_refs, sems, False)
        for cp in local + sends:
            cp.start()

    def finish(self, src_refs, out_refs, sems):
        local, sends, recvs = self._copies(src_refs, out_refs, sems, True)
        for cp in recvs:
            cp.wait_recv()
        for cp in sends:
            cp.wait_send()
        for cp in local:
            cp.wait()


def _exchange(srcs, kinds, name):
    comm = _Comm(srcs, kinds)
    n = comm.n

    def body(*refs):
        src_refs, out_refs, sems = refs[:n], refs[n:2 * n], refs[2 * n:]
        comm.start(src_refs, out_refs, sems)
        comm.finish(src_refs, out_refs, sems)

    return pl.pallas_call(
        body, name=name, out_shape=comm.out_shape, in_specs=comm.specs, out_specs=comm.specs,
        scratch_shapes=comm.scratch,
    )(*srcs)


def _gather_two_level(srcs, name):
    n = len(srcs)

    def body(*refs):
        src_refs, out_refs = refs[:n], refs[n:2 * n]
        send_sems, recv_sems, local_sems = refs[2 * n:]
        x, y, c = lax.axis_index("x"), lax.axis_index("y"), lax.axis_index("c")
        me, sibling = (x, y, c), (x, y, 1 - c)
        chips = [(1 - x, y), (x, 1 - y), (1 - x, 1 - y)]

        def block(a, dev):
            return out_refs[a].at[4 * dev[0] + 2 * dev[1] + dev[2]]

        def copy(a, k, owner, to, src=None):
            return pltpu.make_async_remote_copy(
                src_ref=block(a, owner) if src is None else src, dst_ref=block(a, owner),
                send_sem=send_sems.at[a, k], recv_sem=recv_sems.at[a, k], device_id=to, device_id_type=MESH)

        mine, first, passed = [], [], []
        for a in range(n):
            mine.append(pltpu.make_async_copy(src_refs[a], block(a, me), local_sems.at[a]))
            first.append(copy(a, 0, me, sibling, src=src_refs[a]))
            first += [copy(a, 1 + j, me, (*chip, c), src=src_refs[a]) for j, chip in enumerate(chips)]
        for cp in mine + first:
            cp.start()
        for a in range(n):
            for j, chip in enumerate(chips):
                copy(a, 1 + j, (*chip, c), me).wait_recv()
                fwd = copy(a, 4 + j, (*chip, c), sibling)
                fwd.start()
                passed.append(fwd)
        for a in range(n):
            copy(a, 0, sibling, me).wait_recv()
            for j, chip in enumerate(chips):
                copy(a, 4 + j, (*chip, 1 - c), me).wait_recv()
        for cp in first + passed:
            cp.wait_send()
        for cp in mine:
            cp.wait()

    return pl.pallas_call(
        body, name=name,
        out_shape=[_sds((N_DEV,) + tuple(s.shape), s.dtype) for s in srcs],
        in_specs=[pl.BlockSpec(memory_space=pl.ANY)] * n, out_specs=[pl.BlockSpec(memory_space=pl.ANY)] * n,
        scratch_shapes=[pltpu.SemaphoreType.DMA((n, N_DEV - 1)), pltpu.SemaphoreType.DMA((n, N_DEV - 1)),
                        pltpu.SemaphoreType.DMA((n,))],
    )(*srcs)


class _Packer:
    def __init__(self, entries, row_multiple):
        self.entries = entries
        self.offsets = {}
        off = 0
        for name, shape in entries:
            self.offsets[name] = off
            off += int(np.prod(shape))
        quantum = PACK_LANES * row_multiple
        self.total = -(-off // quantum) * quantum
        self.used = off
        self.rows = self.total // PACK_LANES

    def pack(self, arrays, dtype, lead=()):
        n = len(lead)
        flat = [arrays[name].astype(dtype).reshape(lead + (-1,)) for name, _ in self.entries]
        flat.append(jnp.zeros(lead + (self.total - self.used,), dtype))
        return jnp.concatenate(flat, axis=n).reshape(lead + (self.rows, PACK_LANES))

    def unpack(self, buf, lead=()):
        flat = buf.reshape(lead + (self.total,))
        out = {}
        for name, shape in self.entries:
            o = self.offsets[name]
            out[name] = lax.slice_in_dim(flat, o, o + int(np.prod(shape)), axis=len(lead)).reshape(lead + tuple(shape))
        return out


def _ada_forward(c_rows, w_ada, b_ada_cols):
    cols = w_ada.shape[-1]
    rows = c_rows.shape[0]

    def body(c_ref, w_ref, b_ref, sc_ref, part_ref):
        cv = c_ref[...]
        sc = cv * _sigmoid(cv)
        sc_ref[...] = sc
        scb = sc.astype(BF16)
        for l in range(DEPTH):
            part_ref[l] = jnp.dot(scb, w_ref[l].astype(BF16), preferred_element_type=F32) + b_ref[l:l + 1, :]

    return pl.pallas_call(
        body, name="ada_forward",
        out_shape=(_sds((rows, D)), _sds((DEPTH, rows, cols))),
        compiler_params=pltpu.CompilerParams(vmem_limit_bytes=VMEM_LIMIT),
    )(c_rows, w_ada, b_ada_cols)


def _ada_backward(sc_t, dmod_cols):
    cols = dmod_cols.shape[-1]

    def body(sc_ref, dm_ref, gw_ref):
        scb = sc_ref[...].astype(BF16)
        for l in range(DEPTH):
            gw_ref[l] = jnp.dot(scb, dm_ref[l].astype(BF16), preferred_element_type=F32)

    return pl.pallas_call(
        body, name="ada_backward",
        out_shape=_sds((DEPTH, D, cols)),
        compiler_params=pltpu.CompilerParams(vmem_limit_bytes=VMEM_LIMIT),
    )(sc_t, dmod_cols)


def _riding(comm, nsteps, c_src, c_out, c_sems, where):
    if not comm.n:
        return
    step = 0 if where == "start" else nsteps - 1

    @pl.when(pl.program_id(0) == step)
    def _():
        (comm.start if where == "start" else comm.finish)(c_src, c_out, c_sems)


def _prenorm_inproj(x, g_pre, mod, w_in_p, name, comm=None):
    S = x.shape[0]
    tm = ROW_TILE
    comm = comm or _Comm([], [])
    nc = comm.n

    def body(*refs):
        x_ref, g_ref, mod_ref, w_ref = refs[:4]
        c_src, z_ref, hb_ref = refs[4:4 + nc], refs[4 + nc], refs[5 + nc]
        c_out, c_sems = refs[6 + nc:6 + 2 * nc], refs[6 + 2 * nc:]
        _riding(comm, S // tm, c_src, c_out, c_sems, "start")
        xv = x_ref[...]
        rstd = lax.rsqrt(jnp.mean(xv * xv, axis=-1, keepdims=True) + EPS)
        hb = ((xv * rstd * g_ref[...]) * (1.0 + mod_ref[1:2, :]) + mod_ref[0:1, :]).astype(BF16)
        hb_ref[...] = hb
        z_ref[...] = jnp.dot(hb, w_ref[...], preferred_element_type=F32).astype(BF16)
        _riding(comm, S // tm, c_src, c_out, c_sems, "finish")

    outs = pl.pallas_call(
        body, name=name, grid=(S // tm,),
        in_specs=[_rows(tm, D), _full((1, D)), _full((3, D)), _full((D, DZ))] + comm.specs,
        out_specs=[_rows(tm, DZ), _rows(tm, D)] + comm.specs,
        out_shape=[_sds((S, DZ), BF16), _sds((S, D), BF16)] + comm.out_shape,
        scratch_shapes=comm.scratch,
        compiler_params=_cparams("arbitrary"),
    )(x, g_pre, mod, w_in_p, *comm.srcs)
    return outs[0], outs[1], outs[2:]


def _att_prep(z, pos, q_g, kv_g, wq_p, w_ukv, rope_rows, name):
    S = z.shape[0]
    tm = ROW_TILE

    def body(z_ref, pos_ref, qg_ref, kvg_ref, wq_ref, wkv_ref, rope_ref, q_ref, k_ref, v_ref):
        zz = z_ref[...].astype(F32)
        ql, kvl, ka = zz[:, 0:Q_RANK], zz[:, Q_RANK:Q_RANK + KV_RANK], zz[:, Q_RANK + KV_RANK:]
        qn = ql * lax.rsqrt(jnp.mean(ql * ql, axis=-1, keepdims=True) + EPS) * qg_ref[...]
        kvn = kvl * lax.rsqrt(jnp.mean(kvl * kvl, axis=-1, keepdims=True) + EPS) * kvg_ref[...]
        q = jnp.dot(qn.astype(BF16), wq_ref[...], preferred_element_type=F32)
        kv = jnp.dot(kvn.astype(BF16), wkv_ref[...], preferred_element_type=F32)
        ct, st = _rope_tables(pos_ref, rope_ref)
        krot = (ka * ct + _swap_halves(ka) * st).astype(BF16)
        for h in range(HEADS):
            b = h * HQ
            q_ref[:, b:b + NOPE] = q[:, b:b + NOPE].astype(BF16)
            a = q[:, b + NOPE:b + HQ]
            q_ref[:, b + NOPE:b + HQ] = (a * ct + _swap_halves(a) * st).astype(BF16)
            k_ref[:, b:b + NOPE] = kv[:, b:b + NOPE].astype(BF16)
            k_ref[:, b + NOPE:b + HQ] = krot
            v_ref[:, h * VDIM:(h + 1) * VDIM] = kv[:, b + NOPE:b + HQ].astype(BF16)

    return pl.pallas_call(
        body, name=name, grid=(S // tm,),
        in_specs=[_rows(tm, 512, 0), _rows(tm, 1), _full((1, Q_RANK)), _full((1, KV_RANK)),
                  _full((Q_RANK, HEADS * HQ)), _full((KV_RANK, HEADS * HQ)), _full((8, 128))],
        out_specs=(_rows(tm, HEADS * HQ), _rows(tm, HEADS * HQ), _rows(tm, ATT_W)),
        out_shape=(_sds((S, HEADS * HQ), BF16), _sds((S, HEADS * HQ), BF16), _sds((S, ATT_W), BF16)),
        compiler_params=_cparams("parallel"),
    )(z, pos, q_g, kv_g, wq_p, w_ukv, rope_rows)


def _flash_forward(q, k, v, name, comm=None):
    S = q.shape[0]
    t = ATT_TILE
    nq = S // t
    nl = t // 128
    comm = comm or _Comm([], [])
    nc = comm.n

    def body(*refs):
        q_ref, k_ref, v_ref = refs[:3]
        c_src = refs[3:3 + nc]
        o_ref, lse_ref = refs[3 + nc:5 + nc]
        c_out = refs[5 + nc:5 + 2 * nc]
        m_sc, l_sc, acc_sc, s_sc, mp_sc = refs[5 + 2 * nc:10 + 2 * nc]
        c_sems = refs[10 + 2 * nc:]
        if nc:
            @pl.when((pl.program_id(0) == 0) & (pl.program_id(1) == 0))
            def _():
                comm.start(c_src, c_out, c_sems)

        qb = pl.program_id(1)
        m_sc[...] = jnp.full(m_sc.shape, NEG_INF, F32)
        l_sc[...] = jnp.zeros(l_sc.shape, F32)
        acc_sc[...] = jnp.zeros(acc_sc.shape, F32)

        def score_phase(kb, slot, diagonal):
            s = lax.dot_general(q_ref[...], k_ref[pl.ds(pl.multiple_of(kb * t, t), t), :], NT,
                                preferred_element_type=F32)
            if diagonal:
                ri = lax.broadcasted_iota(jnp.int32, (t, t), 0)
                ci = lax.broadcasted_iota(jnp.int32, (t, t), 1)
                s = jnp.where(ci <= ri, s, NEG_INF)
            s_sc[slot] = s
            mp = s[:, 0:128]
            for c in range(1, nl):
                mp = jnp.maximum(mp, s[:, c * 128:(c + 1) * 128])
            mp_sc[slot] = mp

        def sum_phase(kb, slot):
            m_prev = m_sc[...]
            m_new = jnp.maximum(m_prev, jnp.max(mp_sc[slot], axis=-1, keepdims=True))
            alpha = jnp.exp2((m_prev - m_new) * EXP2_SCALE)
            p = jnp.exp2(s_sc[slot] * EXP2_SCALE - jnp.tile(m_new * EXP2_SCALE, (1, nl)))
            lp = alpha * l_sc[...]
            for c in range(nl):
                lp = lp + p[:, c * 128:(c + 1) * 128]
            l_sc[...] = lp
            acc_sc[...] = alpha * acc_sc[...] + jnp.dot(p.astype(BF16), v_ref[pl.ds(pl.multiple_of(kb * t, t), t), :],
                                                        preferred_element_type=F32)
            m_sc[...] = m_new

        def tile_at(pos):
            return jnp.where(pos == 0, qb, pos - 1)

        def run(p0, count, final):
            for i in range(count):
                sum_phase(tile_at(p0 + i), i % 2)
                if not (final and i == count - 1):
                    score_phase(p0 + i, (i + 1) % 2, False)

        score_phase(qb, 0, True)

        def trip(u, carry):
            run(FLASH_UNROLL * u, FLASH_UNROLL, False)
            return carry

        lax.fori_loop(0, qb // FLASH_UNROLL, trip, 0)
        for left in range(FLASH_UNROLL):
            @pl.when(qb % FLASH_UNROLL == left)
            def _():
                run(qb - left, left + 1, True)

        l = jnp.sum(l_sc[...], axis=-1, keepdims=True)
        o_ref[...] = acc_sc[...] / l
        lse_ref[0] = jnp.max(m_sc[...], axis=-1, keepdims=True) * ATT_SCALE + jnp.log(l)

        if nc:
            @pl.when((pl.program_id(0) == HEADS - 1) & (pl.program_id(1) == nq - 1))
            def _():
                comm.finish(c_src, c_out, c_sems)

    outs = pl.pallas_call(
        body, name=name, grid=(HEADS, nq),
        in_specs=[pl.BlockSpec((t, HQ), lambda h, i: (i, h)),
                  pl.BlockSpec((S, HQ), lambda h, i: (0, h)),
                  pl.BlockSpec((S, VDIM), lambda h, i: (0, h))] + comm.specs,
        out_specs=[pl.BlockSpec((t, VDIM), lambda h, i: (i, h)),
                   pl.BlockSpec((1, t, 1), lambda h, i: (h, i, 0))] + comm.specs,
        out_shape=[_sds((S, ATT_W)), _sds((HEADS, S, 1))] + comm.out_shape,
        scratch_shapes=[pltpu.VMEM((t, 128), F32), pltpu.VMEM((t, 128), F32), pltpu.VMEM((t, VDIM), F32),
                        pltpu.VMEM((2, t, t), F32), pltpu.VMEM((2, t, 128), F32)] + comm.scratch,
        compiler_params=_cparams("arbitrary", "arbitrary"),
    )(q, k, v, *comm.srcs)
    return outs[0], outs[1], outs[2:]


def _conv_window(win_ref, a_prev, b_prev, a_cur, b_cur, first):
    hp = a_prev * _sigmoid(b_prev)
    win_ref[0:HALO, :] = jnp.where(first, 0.0, hp)
    win_ref[HALO:, :] = a_cur * _sigmoid(b_cur)


def _shifted_copies(win_ref, sh_ref, tm):
    for b in range(1, 8):
        sh_ref[b - 1] = win_ref[pl.ds(b, tm + HALO - 8), :]


def _tap(win_ref, sh_ref, offset, tm):
    a, b = divmod(offset, 8)
    if b == 0:
        return win_ref[pl.ds(8 * a, tm), :]
    return sh_ref[b - 1, pl.ds(8 * a, tm), :]


def _conv_in_specs(tm):
    per = tm // HALO
    prev = lambda col: pl.BlockSpec((HALO, CONV_W), lambda i: (jnp.maximum(i * per - 1, 0), col))
    return [_rows(tm, CONV_W, 4), _rows(tm, CONV_W, 5), prev(4), prev(5)]


def _conv_forward(z, conv_w_p, conv_b, ln_g, ln_b, w_pw2, name):
    S = z.shape[0]
    tm = ROW_TILE

    def body(a_ref, b_ref, ap_ref, bp_ref, w_ref, cb_ref, g_ref, be_ref, pw_ref, cv_ref, y_ref, win, sh):
        _conv_window(win, ap_ref[...].astype(F32), bp_ref[...].astype(F32), a_ref[...].astype(F32),
                     b_ref[...].astype(F32), pl.program_id(0) == 0)
        _shifted_copies(win, sh, tm)
        for r0 in range(0, tm, CONV_ROWS):
            acc = jnp.zeros((CONV_ROWS, CONV_W), F32)
            for kk in range(CONV_K):
                acc = acc + w_ref[kk:kk + 1, :] * _tap(win, sh, r0 + HALO - (CONV_K - 1) + kk, CONV_ROWS)
            cv_ref[r0:r0 + CONV_ROWS, :] = acc + cb_ref[...]
        cv = cv_ref[...]
        mu = jnp.mean(cv, axis=-1, keepdims=True)
        cc = cv - mu
        rstd = lax.rsqrt(jnp.mean(cc * cc, axis=-1, keepdims=True) + EPS)
        n = cc * rstd * g_ref[...] + be_ref[...]
        sl = n * _sigmoid(n)
        y_ref[...] = jnp.dot(sl.astype(BF16), pw_ref[...], preferred_element_type=F32)

    return pl.pallas_call(
        body, name=name, grid=(S // tm,),
        in_specs=_conv_in_specs(tm) + [_full((HALO, CONV_W)), _full((1, CONV_W)), _full((1, CONV_W)),
                                       _full((1, CONV_W)), _full((CONV_W, CONV_W))],
        out_specs=(_rows(tm, CONV_W), _rows(tm, CONV_W)),
        out_shape=(_sds((S, CONV_W)), _sds((S, CONV_W))),
        scratch_shapes=[pltpu.VMEM((tm + HALO, CONV_W), F32), pltpu.VMEM((7, tm + HALO - 8, CONV_W), F32)],
        compiler_params=_cparams("parallel"),
    )(z, z, z, z, conv_w_p, conv_b, ln_g, ln_b, w_pw2)


def _sgu_common(u, v, g_ref, be_ref):
    gu, dgu = _gelu_and_grad(u)
    gv, dgv = _gelu_and_grad(v)
    mu = jnp.mean(gv, axis=-1, keepdims=True)
    cc = gv - mu
    rstd = lax.rsqrt(jnp.mean(cc * cc, axis=-1, keepdims=True) + EPS)
    nh = cc * rstd
    vn = nh * g_ref[...] + be_ref[...]
    return gu, dgu, dgv, rstd, nh, vn


def _sgu_masks():
    lane_group = lax.broadcasted_iota(jnp.int32, (1, SGU_W), 1) // SGU_GD
    ri = lax.broadcasted_iota(jnp.int32, (SGU_T, SGU_T), 0)
    ci = lax.broadcasted_iota(jnp.int32, (SGU_T, SGU_T), 1)
    return [lane_group == g for g in range(SGU_G)], ci <= ri


def _sgu_forward(z, ln_g, ln_b, w_s, bias_full, name):
    S = z.shape[0]
    tm = ROW_TILE

    def body(u_ref, v_ref, g_ref, be_ref, ws_ref, bias_ref, y_ref):
        gmask, tril = _sgu_masks()
        wm = [jnp.where(tril, ws_ref[g], 0.0).astype(BF16) for g in range(SGU_G)]
        for ch in range(tm // SGU_T):
            rows = slice(ch * SGU_T, (ch + 1) * SGU_T)
            gu, _, _, _, _, vn = _sgu_common(u_ref[rows, :].astype(F32), v_ref[rows, :].astype(F32), g_ref, be_ref)
            vb = vn.astype(BF16)
            sv = bias_ref[...]
            for g in range(SGU_G):
                sv = sv + jnp.where(gmask[g], jnp.dot(wm[g], vb, preferred_element_type=F32), 0.0)
            y_ref[rows, :] = gu * sv

    return pl.pallas_call(
        body, name=name, grid=(S // tm,),
        in_specs=[_rows(tm, SGU_W, 7), _rows(tm, SGU_W, 8), _full((1, SGU_W)), _full((1, SGU_W)),
                  _full((SGU_G, SGU_T, SGU_T)), _full((SGU_T, SGU_W))],
        out_specs=_rows(tm, SGU_W), out_shape=_sds((S, SGU_W)),
        compiler_params=_cparams("parallel"),
    )(z, z, ln_g, ln_b, w_s, bias_full)


def _out_proj(x, z, y_att, y_conv, y_sgu, w_out, g_post, mod, name, target=None):
    S = x.shape[0]
    tm = ROW_TILE
    head = target is not None

    def body(*refs):
        x_ref, ga_ref, gc_ref, gs_ref, ya_ref, yc_ref, ys_ref, w_ref, gp_ref, mod_ref = refs[:10]
        t_ref = refs[10] if head else None
        xn_ref, y_ref, cat_ref = refs[10 + head:13 + head]
        ca = (ya_ref[...] * _silu_and_grad(ga_ref[...].astype(F32))[0]).astype(BF16)
        cc = (yc_ref[...] * _silu_and_grad(gc_ref[...].astype(F32))[0]).astype(BF16)
        cs = (ys_ref[...] * _silu_and_grad(gs_ref[...].astype(F32))[0]).astype(BF16)
        cat_ref[:, 0:ATT_W] = ca
        cat_ref[:, ATT_W:ATT_W + CONV_W] = cc
        cat_ref[:, ATT_W + CONV_W:] = cs
        y = (jnp.dot(ca, w_ref[0:ATT_W, :], preferred_element_type=F32)
             + jnp.dot(cc, w_ref[ATT_W:ATT_W + CONV_W, :], preferred_element_type=F32)
             + jnp.dot(cs, w_ref[ATT_W + CONV_W:, :], preferred_element_type=F32))
        y_ref[...] = y
        rstd = lax.rsqrt(jnp.mean(y * y, axis=-1, keepdims=True) + EPS)
        xn = x_ref[...] + mod_ref[2:3, :] * (y * rstd * gp_ref[...])
        if not head:
            xn_ref[...] = xn
            return
        loss_ref = refs[14]

        @pl.when(pl.program_id(0) == 0)
        def _():
            loss_ref[...] = jnp.zeros(loss_ref.shape, F32)

        err = xn - t_ref[...]
        xn_ref[...] = err * (1.0 / D)
        row = jnp.sum(err * err, axis=-1, keepdims=True) * (1.0 / D)
        loss_ref[...] += 0.5 * jnp.sum(row, axis=0, keepdims=True)

    return pl.pallas_call(
        body, name=name, grid=(S // tm,),
        in_specs=[_rows(tm, D), _rows(tm, 512, 1), _rows(tm, 256, 6), _rows(tm, 256, 9),
                  _rows(tm, ATT_W), _rows(tm, CONV_W), _rows(tm, SGU_W),
                  _full((D, D)), _full((1, D)), _full((3, D))] + ([_rows(tm, D)] if head else []),
        out_specs=[_rows(tm, D), _rows(tm, D), _rows(tm, D)] + ([_full((1, 1))] if head else []),
        out_shape=[_sds((S, D)), _sds((S, D)), _sds((S, D), BF16)] + ([_sds((1, 1))] if head else []),
        compiler_params=_cparams("arbitrary" if head else "parallel"),
    )(x, z, z, z, y_att, y_conv, y_sgu, w_out, g_post, mod, *([target] if head else []))


def _matmul_tn(a, b, name, comm=None):
    S, M = a.shape
    pieces = list(b) if isinstance(b, (list, tuple)) else [b]
    nb = len(pieces)
    N = sum(p.shape[1] for p in pieces)
    bk = min(MATMUL_TN_ROWS, S)
    riding = comm is not None
    comm = comm or _Comm([], [])
    nc = comm.n

    def body(*refs):
        a_ref, b_refs = refs[0], refs[1:1 + nb]
        c_src, o_ref = refs[1 + nb:1 + nb + nc], refs[1 + nb + nc]
        c_out = refs[2 + nb + nc:2 + nb + 2 * nc]
        rest = refs[2 + nb + 2 * nc:]
        b_sc, c_sems = (rest[0], rest[1:]) if nb > 1 else (None, rest)
        _riding(comm, S // bk, c_src, c_out, c_sems, "start")

        @pl.when(pl.program_id(0) == 0)
        def _():
            o_ref[...] = jnp.zeros(o_ref.shape, F32)

        if nb > 1:
            _assemble_columns(b_sc, b_refs)
            bv = b_sc[...]
        else:
            bv = b_refs[0][...]
        o_ref[...] += lax.dot_general(a_ref[...], bv, TN, preferred_element_type=F32)
        _riding(comm, S // bk, c_src, c_out, c_sems, "finish")

    outs = pl.pallas_call(
        body, name=name, grid=(S // bk,),
        in_specs=[pl.BlockSpec((bk, M), lambda k: (k, 0))]
                 + [pl.BlockSpec((bk, p.shape[1]), lambda k: (k, 0)) for p in pieces] + comm.specs,
        out_specs=[_full((M, N))] + comm.specs, out_shape=[_sds((M, N))] + comm.out_shape,
        scratch_shapes=([pltpu.VMEM((bk, N), BF16)] if nb > 1 else []) + comm.scratch,
        compiler_params=_cparams("arbitrary"),
    )(a, *pieces, *comm.srcs)
    return (outs[0], outs[1:]) if riding else outs[0]


def _out_proj_backward(dxo, y, z, y_att, y_conv, y_sgu, lse, w_out, g_post, mod, name):
    S = dxo.shape[0]
    tm = ROW_TILE

    def body(dxo_ref, y_ref, ga_ref, gc_ref, gs_ref, ya_ref, yc_ref, ys_ref, lse_ref, w_ref, gp_ref, mod_ref,
             dyb_ref, dob_ref, st_ref, dga_ref, dyc_ref, dgc_ref, dys_ref, dgs_ref, dgate_ref, dgp_ref):
        @pl.when(pl.program_id(0) == 0)
        def _():
            dgate_ref[...] = jnp.zeros(dgate_ref.shape, F32)
            dgp_ref[...] = jnp.zeros(dgp_ref.shape, F32)

        dxo_v = dxo_ref[...]
        yv = y_ref[...]
        gp = gp_ref[...]
        rstd = lax.rsqrt(jnp.mean(yv * yv, axis=-1, keepdims=True) + EPS)
        yhat = yv * rstd
        dgate_ref[...] += jnp.sum(dxo_v * (yhat * gp), axis=0, keepdims=True)
        dr = dxo_v * mod_ref[2:3, :]
        dgp_ref[...] += jnp.sum(dr * yhat, axis=0, keepdims=True)
        dyh = dr * gp
        dy = rstd * (dyh - yhat * jnp.mean(dyh * yhat, axis=-1, keepdims=True))
        dyb = dy.astype(BF16)
        dyb_ref[...] = dyb
        dcat = lax.dot_general(dyb, w_ref[...], NT, preferred_element_type=F32)

        ya = ya_ref[...]
        sil, dsil = _silu_and_grad(ga_ref[...].astype(F32))
        da = dcat[:, 0:ATT_W]
        do = da * sil
        dob_ref[...] = do.astype(BF16)
        dga_ref[...] = (da * ya * dsil).astype(BF16)
        lane = lax.broadcasted_iota(jnp.int32, (1, 128), 1)
        stats = jnp.zeros((tm, 128), F32)
        for h in range(HEADS):
            cols = slice(h * VDIM, (h + 1) * VDIM)
            delta = jnp.sum(do[:, cols] * ya[:, cols], axis=-1, keepdims=True)
            stats = stats + jnp.where(lane == 2 * h, lse_ref[h], 0.0) + jnp.where(lane == 2 * h + 1, delta, 0.0)
        st_ref[...] = stats

        sil, dsil = _silu_and_grad(gc_ref[...].astype(F32))
        dc = dcat[:, ATT_W:ATT_W + CONV_W]
        dyc_ref[...] = dc * sil
        dgc_ref[...] = (dc * yc_ref[...] * dsil).astype(BF16)
        sil, dsil = _silu_and_grad(gs_ref[...].astype(F32))
        dsg = dcat[:, ATT_W + CONV_W:]
        dys_ref[...] = dsg * sil
        dgs_ref[...] = (dsg * ys_ref[...] * dsil).astype(BF16)

    return pl.pallas_call(
        body, name=name, grid=(S // tm,),
        in_specs=[_rows(tm, D), _rows(tm, D), _rows(tm, 512, 1), _rows(tm, 256, 6), _rows(tm, 256, 9),
                  _rows(tm, ATT_W), _rows(tm, CONV_W), _rows(tm, SGU_W),
                  pl.BlockSpec((HEADS, tm, 1), lambda i: (0, i, 0)),
                  _full((D, D)), _full((1, D)), _full((3, D))],
        out_specs=(_rows(tm, D), _rows(tm, ATT_W), _rows(tm, 128), _rows(tm, ATT_W),
                   _rows(tm, CONV_W), _rows(tm, CONV_W), _rows(tm, SGU_W), _rows(tm, SGU_W),
                   _full((1, D)), _full((1, D))),
        out_shape=(_sds((S, D), BF16), _sds((S, ATT_W), BF16), _sds((S, 128)), _sds((S, ATT_W), BF16),
                   _sds((S, CONV_W)), _sds((S, CONV_W), BF16), _sds((S, SGU_W)), _sds((S, SGU_W), BF16),
                   _sds((1, D)), _sds((1, D))),
        compiler_params=_cparams("arbitrary"),
    )(dxo, y, z, z, z, y_att, y_conv, y_sgu, lse, w_out, g_post, mod)


def _flash_backward(q, k, v, do, stats, name, comm=None):
    S = q.shape[0]
    t = ATT_TILE
    tk = 2 * t
    nq = S // t
    comm = comm or _Comm([], [])
    nc = comm.n

    def body(*refs):
        q_ref, do_ref, st_ref, k_ref, v_ref = refs[:5]
        c_src = refs[5:5 + nc]
        dq_ref, dk_ref, dv_ref = refs[5 + nc:8 + nc]
        c_out = refs[8 + nc:8 + 2 * nc]
        dk_sc, dv_sc = refs[8 + 2 * nc:10 + 2 * nc]
        c_sems = refs[10 + 2 * nc:]
        h = pl.program_id(0)
        j = pl.program_id(1)
        if nc:
            @pl.when((h == 0) & (j == 0))
            def _():
                comm.start(c_src, c_out, c_sems)

        @pl.when(j == 0)
        def _():
            dq_ref[...] = jnp.zeros(dq_ref.shape, F32)

        dk_sc[...] = jnp.zeros(dk_sc.shape, F32)
        dv_sc[...] = jnp.zeros(dv_sc.shape, F32)
        lane = lax.broadcasted_iota(jnp.int32, (1, 128), 1)

        def chain(hf, qv, dov, lse2, delta, diagonal):
            kt = k_ref[hf * t:(hf + 1) * t, :]
            s = lax.dot_general(qv, kt, NT, preferred_element_type=F32)
            p = jnp.exp2(s * EXP2_SCALE - lse2)
            if diagonal:
                ri = lax.broadcasted_iota(jnp.int32, (t, t), 0)
                ci = lax.broadcasted_iota(jnp.int32, (t, t), 1)
                p = jnp.where(ci <= ri, p, 0.0)
            dv_sc[hf] += lax.dot_general(p.astype(BF16), dov, TN, preferred_element_type=F32)
            dp = lax.dot_general(dov, v_ref[hf * t:(hf + 1) * t, :], NT, preferred_element_type=F32)
            ds = (p * (dp - delta) * ATT_SCALE).astype(BF16)
            dk_sc[hf] += lax.dot_general(ds, qv, TN, preferred_element_type=F32)
            return jnp.dot(ds, kt, preferred_element_type=F32)

        def q_tile(qb, modes):
            rows = pl.ds(pl.multiple_of(qb * t, t), t)
            qv = q_ref[rows, :]
            dov = do_ref[rows, :]
            st = st_ref[rows, :]
            lse2 = jnp.sum(jnp.where(lane == 2 * h, st, 0.0), axis=-1, keepdims=True) * LOG2E
            delta = jnp.sum(jnp.where(lane == 2 * h + 1, st, 0.0), axis=-1, keepdims=True)
            parts = [chain(hf, qv, dov, lse2, delta, modes[hf]) for hf in range(2) if modes[hf] is not None]
            dq_ref[rows, :] += parts[0] if len(parts) == 1 else parts[0] + parts[1]

        q_tile(2 * j, (True, None))
        q_tile(2 * j + 1, (False, True))

        def loop_body(i, carry):
            q_tile(2 * (j + 1 + i), (False, False))
            q_tile(2 * (j + 1 + i) + 1, (False, False))
            return carry

        lax.fori_loop(0, nq // 2 - j - 1, loop_body, 0)
        for hf in range(2):
            dk_ref[hf * t:(hf + 1) * t, :] = dk_sc[hf]
            dv_ref[hf * t:(hf + 1) * t, :] = dv_sc[hf]

        if nc:
            @pl.when((h == HEADS - 1) & (j == S // tk - 1))
            def _():
                comm.finish(c_src, c_out, c_sems)

    outs = pl.pallas_call(
        body, name=name, grid=(HEADS, S // tk),
        in_specs=[pl.BlockSpec((S, HQ), lambda h, j: (0, h)),
                  pl.BlockSpec((S, VDIM), lambda h, j: (0, h)),
                  pl.BlockSpec((S, 128), lambda h, j: (0, 0)),
                  pl.BlockSpec((tk, HQ), lambda h, j: (j, h)),
                  pl.BlockSpec((tk, VDIM), lambda h, j: (j, h))] + comm.specs,
        out_specs=[pl.BlockSpec((S, HQ), lambda h, j: (0, h)),
                   pl.BlockSpec((tk, HQ), lambda h, j: (j, h)),
                   pl.BlockSpec((tk, VDIM), lambda h, j: (j, h))] + comm.specs,
        out_shape=[_sds((S, HEADS * HQ)), _sds((S, HEADS * HQ)), _sds((S, ATT_W))] + comm.out_shape,
        scratch_shapes=[pltpu.VMEM((2, t, HQ), F32), pltpu.VMEM((2, t, VDIM), F32)] + comm.scratch,
        compiler_params=_cparams("arbitrary", "arbitrary"),
    )(q, do, stats, k, v, *comm.srcs)
    return outs[0], outs[1], outs[2], outs[3:]


def _att_prep_backward(z, pos, dq, dk, dv, q_g, kv_g, wq_p, w_ukv, rope_rows, name):
    S = z.shape[0]
    tm = ROW_TILE

    def body(z_ref, pos_ref, dq_ref, dk_ref, dv_ref, qg_ref, kvg_ref, wq_ref, wkv_ref, rope_ref,
             dz_ref, qn_ref, dqp_ref, kvn_ref, dkv_ref, dqg_ref, dkvg_ref):
        @pl.when(pl.program_id(0) == 0)
        def _():
            dqg_ref[...] = jnp.zeros(dqg_ref.shape, F32)
            dkvg_ref[...] = jnp.zeros(dkvg_ref.shape, F32)

        zz = z_ref[...].astype(F32)
        ql, kvl = zz[:, 0:Q_RANK], zz[:, Q_RANK:Q_RANK + KV_RANK]
        q_rstd = lax.rsqrt(jnp.mean(ql * ql, axis=-1, keepdims=True) + EPS)
        kv_rstd = lax.rsqrt(jnp.mean(kvl * kvl, axis=-1, keepdims=True) + EPS)
        qhat, kvhat = ql * q_rstd, kvl * kv_rstd
        qg, kvg = qg_ref[...], kvg_ref[...]
        qn_ref[...] = (qhat * qg).astype(BF16)
        kvn_ref[...] = (kvhat * kvg).astype(BF16)
        ct, st = _rope_tables(pos_ref, rope_ref)

        def unrotate(d):
            return d * ct + _swap_halves(d * st)

        dkrot = jnp.zeros((tm, NOPE), F32)
        for h in range(HEADS):
            b = h * HQ
            dqp_ref[:, b:b + NOPE] = dq_ref[:, b:b + NOPE].astype(BF16)
            dqp_ref[:, b + NOPE:b + HQ] = unrotate(dq_ref[:, b + NOPE:b + HQ]).astype(BF16)
            dkv_ref[:, b:b + NOPE] = dk_ref[:, b:b + NOPE].astype(BF16)
            dkv_ref[:, b + NOPE:b + HQ] = dv_ref[:, h * VDIM:(h + 1) * VDIM].astype(BF16)
            dkrot = dkrot + dk_ref[:, b + NOPE:b + HQ]
        dqn = lax.dot_general(dqp_ref[...], wq_ref[...], NT, preferred_element_type=F32)
        dkvn = lax.dot_general(dkv_ref[...], wkv_ref[...], NT, preferred_element_type=F32)
        dqg_ref[...] += jnp.sum(dqn * qhat, axis=0, keepdims=True)
        dkvg_ref[...] += jnp.sum(dkvn * kvhat, axis=0, keepdims=True)
        dqh, dkvh = dqn * qg, dkvn * kvg
        dql = q_rstd * (dqh - qhat * jnp.mean(dqh * qhat, axis=-1, keepdims=True))
        dkvl = kv_rstd * (dkvh - kvhat * jnp.mean(dkvh * kvhat, axis=-1, keepdims=True))
        dz_ref[:, 0:Q_RANK] = dql.astype(BF16)
        dz_ref[:, Q_RANK:Q_RANK + KV_RANK] = dkvl.astype(BF16)
        dz_ref[:, Q_RANK + KV_RANK:] = unrotate(dkrot).astype(BF16)

    W = HEADS * HQ
    return pl.pallas_call(
        body, name=name, grid=(S // tm,),
        in_specs=[_rows(tm, 512, 0), _rows(tm, 1), _rows(tm, W), _rows(tm, W), _rows(tm, ATT_W),
                  _full((1, Q_RANK)), _full((1, KV_RANK)), _full((Q_RANK, W)), _full((KV_RANK, W)), _full((8, 128))],
        out_specs=(_rows(tm, 512), _rows(tm, Q_RANK), _rows(tm, W), _rows(tm, KV_RANK), _rows(tm, W),
                   _full((1, Q_RANK)), _full((1, KV_RANK))),
        out_shape=(_sds((S, 512), BF16), _sds((S, Q_RANK), BF16), _sds((S, W), BF16), _sds((S, KV_RANK), BF16),
                   _sds((S, W), BF16), _sds((1, Q_RANK)), _sds((1, KV_RANK))),
        compiler_params=_cparams("arbitrary"),
    )(z, pos, dq, dk, dv, q_g, kv_g, wq_p, w_ukv, rope_rows)


def _conv_norm_backward(dyc, cv, ln_g, ln_b, w_pw2, name):
    S = cv.shape[0]
    tm = ROW_TILE

    def body(dy_ref, cv_ref, g_ref, be_ref, pw_ref, dcv_ref, sl_ref, dyb_ref, dg_ref, db_ref, dcb_ref):
        @pl.when(pl.program_id(0) == 0)
        def _():
            dg_ref[...] = jnp.zeros(dg_ref.shape, F32)
            db_ref[...] = jnp.zeros(db_ref.shape, F32)
            dcb_ref[...] = jnp.zeros(dcb_ref.shape, F32)

        cv_v = cv_ref[...]
        mu = jnp.mean(cv_v, axis=-1, keepdims=True)
        cc = cv_v - mu
        rstd = lax.rsqrt(jnp.mean(cc * cc, axis=-1, keepdims=True) + EPS)
        nh = cc * rstd
        g = g_ref[...]
        n = nh * g + be_ref[...]
        sil, dsil = _silu_and_grad(n)
        sl_ref[...] = sil.astype(BF16)
        dyb = dy_ref[...].astype(BF16)
        dyb_ref[...] = dyb
        dn = lax.dot_general(dyb, pw_ref[...], NT, preferred_element_type=F32) * dsil
        db_ref[...] += jnp.sum(dn, axis=0, keepdims=True)
        dg_ref[...] += jnp.sum(dn * nh, axis=0, keepdims=True)
        dnh = dn * g
        dcv = rstd * (dnh - jnp.mean(dnh, axis=-1, keepdims=True) - nh * jnp.mean(dnh * nh, axis=-1, keepdims=True))
        dcv_ref[...] = dcv
        dcb_ref[...] += jnp.sum(dcv, axis=0, keepdims=True)

    vec = _full((1, CONV_W))
    return pl.pallas_call(
        body, name=name, grid=(S // tm,),
        in_specs=[_rows(tm, CONV_W), _rows(tm, CONV_W), vec, vec, _full((CONV_W, CONV_W))],
        out_specs=(_rows(tm, CONV_W), _rows(tm, CONV_W), _rows(tm, CONV_W), vec, vec, vec),
        out_shape=(_sds((S, CONV_W)), _sds((S, CONV_W), BF16), _sds((S, CONV_W), BF16),
                   _sds((1, CONV_W)), _sds((1, CONV_W)), _sds((1, CONV_W))),
        compiler_params=_cparams("arbitrary"),
    )(dyc, cv, ln_g, ln_b, w_pw2)


def _conv_backward(z, dcv, conv_w_p, name):
    S = z.shape[0]
    tm = ROW_TILE
    per = tm // HALO
    last_halo = S // HALO - 1

    def body(a_ref, b_ref, ap_ref, bp_ref, d_ref, dn_ref, w_ref, da_ref, db_ref, dw_ref, win, dwin, dw_acc, sh, dsh):
        i = pl.program_id(0)

        @pl.when(i == 0)
        def _():
            dw_acc[...] = jnp.zeros(dw_acc.shape, F32)

        av, bv = a_ref[...].astype(F32), b_ref[...].astype(F32)
        _conv_window(win, ap_ref[...].astype(F32), bp_ref[...].astype(F32), av, bv, i == 0)
        dcur = d_ref[...]
        dwin[0:tm, :] = dcur
        dwin[tm:, :] = jnp.where(i == pl.num_programs(0) - 1, 0.0, dn_ref[...])
        _shifted_copies(win, sh, tm)
        _shifted_copies(dwin, dsh, tm)
        for r0 in range(0, tm, CONV_ROWS):
            dchunk = d_ref[r0:r0 + CONV_ROWS, :]
            dh = jnp.zeros((CONV_ROWS, CONV_W), F32)
            for kk in range(CONV_K):
                dh = dh + w_ref[kk:kk + 1, :] * _tap(dwin, dsh, r0 + CONV_K - 1 - kk, CONV_ROWS)
                prod = dchunk * _tap(win, sh, r0 + HALO - (CONV_K - 1) + kk, CONV_ROWS)
                dw_acc[kk] += jnp.sum(prod.reshape(CONV_ROWS // 8, 8, CONV_W), axis=0)
            sb = _sigmoid(b_ref[r0:r0 + CONV_ROWS, :].astype(F32))
            da_ref[r0:r0 + CONV_ROWS, :] = (dh * sb).astype(BF16)
            db_ref[r0:r0 + CONV_ROWS, :] = (dh * a_ref[r0:r0 + CONV_ROWS, :].astype(F32) * sb * (1.0 - sb)).astype(BF16)

        @pl.when(i == pl.num_programs(0) - 1)
        def _():
            dw_ref[...] = jnp.sum(dw_acc[...], axis=1)

    nxt = pl.BlockSpec((HALO, CONV_W), lambda i: (jnp.minimum((i + 1) * per, last_halo), 0))
    return pl.pallas_call(
        body, name=name, grid=(S // tm,),
        in_specs=_conv_in_specs(tm) + [_rows(tm, CONV_W), nxt, _full((HALO, CONV_W))],
        out_specs=(_rows(tm, CONV_W), _rows(tm, CONV_W), _full((HALO, CONV_W))),
        out_shape=(_sds((S, CONV_W), BF16), _sds((S, CONV_W), BF16), _sds((HALO, CONV_W))),
        scratch_shapes=[pltpu.VMEM((tm + HALO, CONV_W), F32), pltpu.VMEM((tm + HALO, CONV_W), F32),
                        pltpu.VMEM((HALO, 8, CONV_W), F32), pltpu.VMEM((7, tm + HALO - 8, CONV_W), F32),
                        pltpu.VMEM((7, tm + HALO - 8, CONV_W), F32)],
        compiler_params=_cparams("arbitrary"),
    )(z, z, z, z, dcv, dcv, conv_w_p)


def _sgu_backward(z, dy, ln_g, ln_b, w_s, bias_full, name):
    S = z.shape[0]
    tm = ROW_TILE

    def body(u_ref, v_ref, dy_ref, g_ref, be_ref, ws_ref, bias_ref, du_ref, dv_ref, dws_ref, dbs_ref, dg_ref, db_ref):
        @pl.when(pl.program_id(0) == 0)
        def _():
            dws_ref[...] = jnp.zeros(dws_ref.shape, F32)
            dbs_ref[...] = jnp.zeros(dbs_ref.shape, F32)
            dg_ref[...] = jnp.zeros(dg_ref.shape, F32)
            db_ref[...] = jnp.zeros(db_ref.shape, F32)

        gmask, tril = _sgu_masks()
        lane = lax.broadcasted_iota(jnp.int32, (1, 128), 1)
        wm = [jnp.where(tril, ws_ref[g], 0.0).astype(BF16) for g in range(SGU_G)]
        gain = g_ref[...]
        for ch in range(tm // SGU_T):
            rows = slice(ch * SGU_T, (ch + 1) * SGU_T)
            gu, dgu, dgv, rstd, nh, vn = _sgu_common(u_ref[rows, :].astype(F32), v_ref[rows, :].astype(F32), g_ref, be_ref)
            vb = vn.astype(BF16)
            sv = bias_ref[...]
            for g in range(SGU_G):
                sv = sv + jnp.where(gmask[g], jnp.dot(wm[g], vb, preferred_element_type=F32), 0.0)
            dyv = dy_ref[rows, :]
            du_ref[rows, :] = (dyv * sv * dgu).astype(BF16)
            dsv = dyv * gu
            dsvb = dsv.astype(BF16)
            dvn = jnp.zeros((SGU_T, SGU_W), F32)
            for g in range(SGU_G):
                dsg = jnp.where(gmask[g], dsv, 0.0)
                dwg = lax.dot_general(dsg.astype(BF16), vb, NT, preferred_element_type=F32)
                dws_ref[g] += jnp.where(tril, dwg, 0.0)
                dvn = dvn + jnp.where(gmask[g], lax.dot_general(wm[g], dsvb, TN, preferred_element_type=F32), 0.0)
                dbs_ref[...] += jnp.where(lane == g, jnp.sum(dsg, axis=-1, keepdims=True), 0.0)
            db_ref[...] += jnp.sum(dvn, axis=0, keepdims=True)
            dg_ref[...] += jnp.sum(dvn * nh, axis=0, keepdims=True)
            dnh = dvn * gain
            dgvv = rstd * (dnh - jnp.mean(dnh, axis=-1, keepdims=True) - nh * jnp.mean(dnh * nh, axis=-1, keepdims=True))
            dv_ref[rows, :] = (dgvv * dgv).astype(BF16)

    vec = _full((1, SGU_W))
    return pl.pallas_call(
        body, name=name, grid=(S // tm,),
        in_specs=[_rows(tm, SGU_W, 7), _rows(tm, SGU_W, 8), _rows(tm, SGU_W), vec, vec,
                  _full((SGU_G, SGU_T, SGU_T)), _full((SGU_T, SGU_W))],
        out_specs=(_rows(tm, SGU_W), _rows(tm, SGU_W), _full((SGU_G, SGU_T, SGU_T)), _full((SGU_T, 128)), vec, vec),
        out_shape=(_sds((S, SGU_W), BF16), _sds((S, SGU_W), BF16), _sds((SGU_G, SGU_T, SGU_T)), _sds((SGU_T, 128)),
                   _sds((1, SGU_W)), _sds((1, SGU_W))),
        compiler_params=_cparams("arbitrary"),
    )(z, z, dy, ln_g, ln_b, w_s, bias_full)


DZ_PIECES = (512, 512, 256, 256, 256, 256, 256, 256)


def _assemble_columns(dst_ref, pieces):
    off = 0
    for piece in pieces:
        wdt = piece.shape[1]
        dst_ref[:, off:off + wdt] = piece[...]
        off += wdt


def _inproj_backward(x, dxo, dz_pieces, g_pre, mod, w_in_p, name, comm=None):
    S = x.shape[0]
    tm = ROW_TILE
    comm = comm or _Comm([], [])
    nc = comm.n
    npc = len(DZ_PIECES)

    def body(*refs):
        x_ref, dxo_ref = refs[:2]
        pieces = refs[2:2 + npc]
        g_ref, mod_ref, w_ref = refs[2 + npc:5 + npc]
        c_src = refs[5 + npc:5 + npc + nc]
        dx_ref, dmod_ref, dg_ref = refs[5 + npc + nc:8 + npc + nc]
        c_out = refs[8 + npc + nc:8 + npc + 2 * nc]
        dz_sc = refs[8 + npc + 2 * nc]
        c_sems = refs[9 + npc + 2 * nc:]
        _riding(comm, S // tm, c_src, c_out, c_sems, "start")

        @pl.when(pl.program_id(0) == 0)
        def _():
            dmod_ref[...] = jnp.zeros(dmod_ref.shape, F32)
            dg_ref[...] = jnp.zeros(dg_ref.shape, F32)

        _assemble_columns(dz_sc, pieces)
        dh = lax.dot_general(dz_sc[...], w_ref[...], NT, preferred_element_type=F32)
        xv = x_ref[...]
        g = g_ref[...]
        one_scale = 1.0 + mod_ref[1:2, :]
        rstd = lax.rsqrt(jnp.mean(xv * xv, axis=-1, keepdims=True) + EPS)
        xhat = xv * rstd
        xg = xhat * g
        dmod_ref[0:1, :] += jnp.sum(dh, axis=0, keepdims=True)
        dmod_ref[1:2, :] += jnp.sum(dh * xg, axis=0, keepdims=True)
        dhs = dh * one_scale
        dg_ref[...] += jnp.sum(dhs * xhat, axis=0, keepdims=True)
        dxh = dhs * g
        dx_ref[...] = dxo_ref[...] + rstd * (dxh - xhat * jnp.mean(dxh * xhat, axis=-1, keepdims=True))
        _riding(comm, S // tm, c_src, c_out, c_sems, "finish")

    outs = pl.pallas_call(
        body, name=name, grid=(S // tm,),
        in_specs=[_rows(tm, D), _rows(tm, D)] + [_rows(tm, w) for w in DZ_PIECES]
                 + [_full((1, D)), _full((3, D)), _full((D, DZ))] + comm.specs,
        out_specs=[_rows(tm, D), _full((2, D)), _full((1, D))] + comm.specs,
        out_shape=[_sds((S, D)), _sds((2, D)), _sds((1, D))] + comm.out_shape,
        scratch_shapes=[pltpu.VMEM((tm, DZ), BF16)] + comm.scratch,
        compiler_params=_cparams("arbitrary"),
    )(x, dxo, *dz_pieces, g_pre, mod, w_in_p, *comm.srcs)
    return outs[0], outs[1], outs[2], outs[3:]


def _adamw(w, gparts, m, v, name):
    shape = w.shape
    cols = shape[-1]
    rows = int(np.prod(shape[:-1]))
    parts = gparts.shape[0]
    w2, m2, v2 = (a.reshape(rows, cols) for a in (w, m, v))
    g3 = gparts.reshape(parts, rows, cols)
    tr = rows
    for cand in (256, 128):
        if rows > cand and rows % cand == 0:
            tr = cand
            break

    def body(w_ref, g_ref, m_ref, v_ref, go_ref, d_ref, mo_ref, vo_ref):
        g = g_ref[0].astype(F32)
        for p in range(1, parts):
            g = g + g_ref[p].astype(F32)
        wv = w_ref[...]
        mn = ADAM_B1 * m_ref[...] + (1.0 - ADAM_B1) * g
        vn = ADAM_B2 * v_ref[...] + (1.0 - ADAM_B2) * (g * g)
        m_hat = mn / (1.0 - ADAM_B1 ** ADAM_STEP)
        v_hat = vn / (1.0 - ADAM_B2 ** ADAM_STEP)
        go_ref[...] = g
        d_ref[...] = -ADAM_LR * (m_hat / (jnp.sqrt(v_hat) + ADAM_EPS) + ADAM_WD * wv)
        mo_ref[...] = mn
        vo_ref[...] = vn

    blk = pl.BlockSpec((tr, cols), lambda i: (i, 0))
    outs = pl.pallas_call(
        body, name=name, grid=(rows // tr,),
        in_specs=[blk, pl.BlockSpec((parts, tr, cols), lambda i: (0, i, 0)), blk, blk],
        out_specs=(blk, blk, blk, blk),
        out_shape=tuple(_sds((rows, cols)) for _ in range(4)),
        compiler_params=_cparams("parallel"),
    )(w2, g3, m2, v2)
    return tuple(o.reshape(shape) for o in outs)


_GATHERED = ("w_in", "w_out", "w_uq", "w_ukv", "w_pw2", "conv_w")
_COL_SHARDED = ("w_in", "w_uq", "w_ukv", "conv_w")

_SMALL = (("dmod", (3 * D,)), ("g_pre", (D,)), ("g_post", (D,)), ("q_norm_g", (Q_RANK,)),
          ("kv_norm_g", (KV_RANK,)), ("conv_b", (CONV_W,)), ("conv_ln_g", (CONV_W,)),
          ("conv_ln_b", (CONV_W,)), ("sgu_ln_g", (SGU_W,)), ("sgu_ln_b", (SGU_W,)),
          ("w_s", (SGU_G, SGU_T, SGU_T)), ("b_s", (SGU_G, SGU_T)))


def _assemble(name, parts):
    if name in _COL_SHARDED:
        p = jnp.moveaxis(parts, 0, 1)
        return p.reshape(p.shape[0], p.shape[1] * p.shape[2])
    return parts.reshape(parts.shape[0] * parts.shape[1], parts.shape[2])


def _scatter_layout(name, full):
    if name in _COL_SHARDED:
        return jnp.moveaxis(full.reshape(full.shape[0], N_DEV, full.shape[1] // N_DEV), 1, 0)
    return full.reshape(N_DEV, full.shape[0] // N_DEV, full.shape[1])


def kernel(x, c, positions, w_ada, b_ada, g_pre, g_post, w_in, q_norm_g, w_uq, kv_norm_g, w_ukv, conv_w, conv_b, conv_ln_g, conv_ln_b, w_pw2, sgu_ln_g, sgu_ln_b, w_s, b_s, w_out, loss_target, m_w_ada, m_b_ada, m_g_pre, m_g_post, m_w_in, m_q_norm_g, m_w_uq, m_kv_norm_g, m_w_ukv, m_conv_w, m_conv_b, m_conv_ln_g, m_conv_ln_b, m_w_pw2, m_sgu_ln_g, m_sgu_ln_b, m_w_s, m_b_s, m_w_out, v_w_ada, v_b_ada, v_g_pre, v_g_post, v_w_in, v_q_norm_g, v_w_uq, v_kv_norm_g, v_w_ukv, v_conv_w, v_conv_b, v_conv_ln_g, v_conv_ln_b, v_w_pw2, v_sgu_ln_g, v_sgu_ln_b, v_w_s, v_b_s, v_w_out):
    weights = dict(w_ada=w_ada, b_ada=b_ada, g_pre=g_pre, g_post=g_post, w_in=w_in, q_norm_g=q_norm_g, w_uq=w_uq,
                   kv_norm_g=kv_norm_g, w_ukv=w_ukv, conv_w=conv_w, conv_b=conv_b, conv_ln_g=conv_ln_g,
                   conv_ln_b=conv_ln_b, w_pw2=w_pw2, sgu_ln_g=sgu_ln_g, sgu_ln_b=sgu_ln_b, w_s=w_s, b_s=b_s, w_out=w_out)
    m_in = dict(w_ada=m_w_ada, b_ada=m_b_ada, g_pre=m_g_pre, g_post=m_g_post, w_in=m_w_in, q_norm_g=m_q_norm_g,
                w_uq=m_w_uq, kv_norm_g=m_kv_norm_g, w_ukv=m_w_ukv, conv_w=m_conv_w, conv_b=m_conv_b,
                conv_ln_g=m_conv_ln_g, conv_ln_b=m_conv_ln_b, w_pw2=m_w_pw2, sgu_ln_g=m_sgu_ln_g,
                sgu_ln_b=m_sgu_ln_b, w_s=m_w_s, b_s=m_b_s, w_out=m_w_out)
    v_in = dict(w_ada=v_w_ada, b_ada=v_b_ada, g_pre=v_g_pre, g_post=v_g_post, w_in=v_w_in, q_norm_g=v_q_norm_g,
                w_uq=v_w_uq, kv_norm_g=v_kv_norm_g, w_ukv=v_w_ukv, conv_w=v_conv_w, conv_b=v_conv_b,
                conv_ln_g=v_conv_ln_g, conv_ln_b=v_conv_ln_b, w_pw2=v_w_pw2, sgu_ln_g=v_sgu_ln_g,
                sgu_ln_b=v_sgu_ln_b, w_s=v_w_s, b_s=v_b_s, w_out=v_w_out)
    order = list(weights)

    S = x.shape[1]
    me = 4 * lax.axis_index("x") + 2 * lax.axis_index("y") + lax.axis_index("c")
    x0 = x.reshape(S, D)
    target = loss_target.reshape(S, D)
    pos = positions.reshape(S, 1)

    def shards(l):
        return [weights[n][l].astype(BF16) for n in _GATHERED]

    def w_in_operand(part):
        w_in_f = _assemble("w_in", part)
        return jnp.concatenate([w_in_f[:, :ATT_IN], jnp.zeros((D, PAD_IN), BF16), w_in_f[:, ATT_IN:]], axis=1)

    def other_operands(parts):
        full = {n: _assemble(n, p) for n, p in zip(_GATHERED[1:], parts)}
        wq = jnp.pad(full["w_uq"].reshape(Q_RANK, HEADS, QK), ((0, 0), (0, 0), (0, HQ - QK)))
        return dict(w_uq=wq.reshape(Q_RANK, HEADS * HQ), w_ukv=full["w_ukv"], w_pw2=full["w_pw2"], w_out=full["w_out"],
                    conv_w=jnp.pad(full["conv_w"].astype(F32), ((0, HALO - CONV_K), (0, 0))))

    first_w_in, c_parts = _gather_two_level([shards(0)[0], c.reshape(8, D // 8)], name="gather_w_in_0")
    lw = [dict(w_in=w_in_operand(first_w_in))] + [None] * (DEPTH - 1)
    c_all = c_parts.reshape(N_DEV, D)

    ada_cols = w_ada.shape[-1]
    b_cols = lax.dynamic_slice_in_dim(b_ada, me * ada_cols, ada_cols, axis=1)
    sc_rows, mod_part = _ada_forward(jnp.pad(c_all, ((0, 8), (0, 0))), w_ada, b_cols)
    mod_recv = _exchange([jnp.moveaxis(mod_part[:, :N_DEV], 1, 0)], [True], name="exchange_mod")[0]
    mod = jnp.moveaxis(mod_recv, 0, 1).reshape(DEPTH, 3, D)

    bias_full = jnp.repeat(jnp.swapaxes(b_s, 1, 2), SGU_GD, axis=2)
    inv_freq = ROPE_THETA ** (-jnp.arange(0, ROPE, 2, dtype=F32) / ROPE)
    zeros32 = jnp.zeros((ROPE // 2,), F32)
    ones32 = jnp.ones((ROPE // 2,), F32)
    rope_rows = jnp.zeros((8, 128), F32)
    rope_rows = rope_rows.at[0].set(jnp.concatenate([inv_freq, inv_freq, zeros32, zeros32]))
    rope_rows = rope_rows.at[1].set(jnp.concatenate([ones32, ones32, zeros32, zeros32]))
    rope_rows = rope_rows.at[2].set(jnp.concatenate([-ones32, ones32, zeros32, zeros32]))

    def vec(a, l):
        return a[l].reshape(1, -1)

    saved = []
    xl = x0
    for l in range(DEPTH):
        w = lw[l]
        late = _Comm(shards(0)[1:], [False] * (len(_GATHERED) - 1)) if l == 0 else None
        z, h_b, arrived = _prenorm_inproj(xl, vec(g_pre, l), mod[l], w["w_in"], name=f"prenorm_inproj_{l}", comm=late)
        if late is not None:
            w.update(other_operands(arrived))
        q, k, v = _att_prep(z, pos, vec(q_norm_g, l), vec(kv_norm_g, l), w["w_uq"], w["w_ukv"], rope_rows,
                            name=f"att_prep_{l}")
        ahead = _Comm(shards(l + 1), [False] * len(_GATHERED)) if l + 1 < DEPTH else None
        y_att, lse, arrived = _flash_forward(q, k, v, name=f"flash_forward_{l}", comm=ahead)
        if ahead is not None:
            lw[l + 1] = dict(w_in=w_in_operand(arrived[0]), **other_operands(arrived[1:]))
        cv, y_conv = _conv_forward(z, w["conv_w"], vec(conv_b, l), vec(conv_ln_g, l), vec(conv_ln_b, l),
                                   w["w_pw2"], name=f"conv_forward_{l}")
        y_sgu = _sgu_forward(z, vec(sgu_ln_g, l), vec(sgu_ln_b, l), w_s[l], bias_full[l], name=f"sgu_forward_{l}")
        outs = _out_proj(xl, z, y_att, y_conv, y_sgu, w["w_out"], vec(g_post, l), mod[l], name=f"out_proj_{l}",
                         target=target if l == DEPTH - 1 else None)
        saved.append(dict(x=xl, h=h_b, z=z, q=q, k=k, v=v, y_att=y_att, lse=lse, cv=cv, y_conv=y_conv, y_sgu=y_sgu,
                          y=outs[1], ycat=outs[2]))
        xl = outs[0]

    dx = xl
    loss = lax.psum(outs[3].reshape(()), ("x", "y", "c"))

    spack = _Packer(_SMALL, 8)
    grad_kinds = [True] * len(_GATHERED) + [False]
    received = [None] * DEPTH
    pending = None
    for l in reversed(range(DEPTH)):
        sv = saved[l]
        w = lw[l]
        (dyb, dob, stats, dga, dyc, dgc, dys, dgs, dgate, dgpost) = _out_proj_backward(
            dx, sv["y"], sv["z"], sv["y_att"], sv["y_conv"], sv["y_sgu"], sv["lse"], w["w_out"],
            vec(g_post, l), mod[l], name=f"out_proj_backward_{l}")
        big = dict(w_out=_matmul_tn(sv["ycat"], dyb, name=f"grad_w_out_{l}"))
        riding = _Comm(pending, grad_kinds) if pending is not None else None
        dq, dk, dv, arrived = _flash_backward(sv["q"], sv["k"], sv["v"], dob, stats, name=f"flash_backward_{l}",
                                              comm=riding)
        if riding is not None:
            received[l + 1] = arrived
        dz_att, qn_b, dqp_b, kvn_b, dkv_b, dqg, dkvg = _att_prep_backward(
            sv["z"], pos, dq, dk, dv, vec(q_norm_g, l), vec(kv_norm_g, l), w["w_uq"], w["w_ukv"], rope_rows,
            name=f"att_prep_backward_{l}")
        dwq_p = _matmul_tn(qn_b, dqp_b, name=f"grad_w_uq_{l}")
        big["w_uq"] = dwq_p.reshape(Q_RANK, HEADS, HQ)[:, :, :QK].reshape(Q_RANK, HEADS * QK)
        big["w_ukv"] = _matmul_tn(kvn_b, dkv_b, name=f"grad_w_ukv_{l}")
        dcv, sl_b, dyc_b, dclg, dclb, dcb = _conv_norm_backward(dyc, sv["cv"], vec(conv_ln_g, l), vec(conv_ln_b, l),
                                                              w["w_pw2"], name=f"conv_norm_backward_{l}")
        big["w_pw2"] = _matmul_tn(sl_b, dyc_b, name=f"grad_w_pw2_{l}")
        dca, dcbb, dconvw = _conv_backward(sv["z"], dcv, w["conv_w"], name=f"conv_backward_{l}")
        big["conv_w"] = dconvw[:CONV_K]
        dsu, dsvv, dws, dbs, dslg, dslb = _sgu_backward(sv["z"], dys, vec(sgu_ln_g, l), vec(sgu_ln_b, l), w_s[l],
                                                       bias_full[l], name=f"sgu_backward_{l}")
        dz_pieces = [dz_att, dga, dca, dcbb, dgc, dsu, dsvv, dgs]
        rest = [_scatter_layout(n, big[n]).astype(BF16) for n in _GATHERED[1:]]
        early = _Comm(rest, grad_kinds[1:-1]) if l == 0 else None
        dwin_p = _matmul_tn(sv["h"], dz_pieces, name=f"grad_w_in_{l}", comm=early)
        if early is not None:
            dwin_p, rest_arrived = dwin_p
        dwin = jnp.concatenate([dwin_p[:, :ATT_IN], dwin_p[:, ATT_IN + PAD_IN:]], axis=1)
        dwin_send = _scatter_layout("w_in", dwin).astype(BF16)
        last = _Comm([dwin_send], grad_kinds[:1]) if l == 0 else None
        dx, dmod2, dgpre, dwin_arrived = _inproj_backward(sv["x"], dx, dz_pieces, vec(g_pre, l), mod[l], w["w_in"],
                                                          name=f"inproj_backward_{l}", comm=last)
        small = dict(dmod=jnp.concatenate([dmod2.reshape(-1), dgate.reshape(-1)]), g_pre=dgpre, g_post=dgpost,
                     q_norm_g=dqg, kv_norm_g=dkvg, conv_b=dcb, conv_ln_g=dclg, conv_ln_b=dclb, sgu_ln_g=dslg,
                     sgu_ln_b=dslb, w_s=dws, b_s=jnp.swapaxes(dbs[:, :SGU_G], 0, 1))
        pending = [dwin_send] + rest + [spack.pack(small, F32)]
    grad_x = dx.reshape(1, S, D)
    small_arrived = _exchange(pending[-1:], grad_kinds[-1:], name="gather_small_grads_0")
    received[0] = list(dwin_arrived) + list(rest_arrived) + list(small_arrived)

    gparts = {n: jnp.stack([received[l][i] for l in range(DEPTH)], axis=1) for i, n in enumerate(_GATHERED)}
    sparts = [spack.unpack(received[l][-1], (N_DEV,)) for l in range(DEPTH)]
    sparts = {n: jnp.stack([sparts[l][n] for l in range(DEPTH)], axis=1) for n, _ in _SMALL}
    dmod_all = sparts["dmod"]
    dmod_cols = lax.dynamic_slice_in_dim(dmod_all, me * ada_cols, ada_cols, axis=2)
    sc_t = jnp.pad(sc_rows[:N_DEV].T, ((0, 0), (0, 128 - N_DEV)))
    dmod_rows = jnp.pad(jnp.moveaxis(dmod_cols, 0, 1), ((0, 0), (0, 128 - N_DEV), (0, 0)))
    gparts["w_ada"] = _ada_backward(sc_t, dmod_rows)[None]
    gparts["b_ada"] = dmod_all
    for n, _ in _SMALL[1:]:
        gparts[n] = sparts[n]

    grads, deltas, new_m, new_v = {}, {}, {}, {}
    for n in order:
        grads[n], deltas[n], new_m[n], new_v[n] = _adamw(weights[n], gparts[n], m_in[n], v_in[n], name=f"adamw_{n}")
    return (loss, grad_x, *[grads[n] for n in order], *[deltas[n] for n in order],
            *[new_m[n] for n in order], *[new_v[n] for n in order])
```

```python
import functools
import math

import numpy as np
import jax
import jax.numpy as jnp
from jax import lax
from jax.experimental import pallas as pl
from jax.experimental.pallas import tpu as pltpu

F32 = jnp.float32
BF16 = jnp.bfloat16

N_DEV = 8
DEPTH = 2
D = 1024
HEADS = 4
NOPE = 128
ROPE = 64
VDIM = 128
QK = NOPE + ROPE
Q_RANK = 256
KV_RANK = 128
ATT_W = HEADS * VDIM
CONV_W = 256
CONV_K = 31
SGU_W = 256
SGU_G = 4
SGU_GD = SGU_W // SGU_G
SGU_T = 128
D_IN = 2496
ATT_IN = Q_RANK + KV_RANK + ROPE
PAD_IN = 64
DZ = D_IN + PAD_IN
HQ = 2 * NOPE
EPS = 1e-6
ROPE_THETA = 10000.0
ATT_SCALE = QK ** -0.5
LOG2E = math.log2(math.e)
EXP2_SCALE = ATT_SCALE * LOG2E
NEG_INF = float("-inf")

ADAM_LR = 0.001
ADAM_B1 = 0.9
ADAM_B2 = 0.999
ADAM_EPS = 1e-08
ADAM_WD = 0.01
ADAM_STEP = 10

VMEM_LIMIT = 56 * 1024 * 1024
ROW_TILE = 512
MATMUL_TN_ROWS = 1024
ATT_TILE = 512
FLASH_UNROLL = 4
HALO = 32
CONV_ROWS = 64
PACK_LANES = 128

MESH = pl.DeviceIdType.MESH
NT = (((1,), (1,)), ((), ()))
TN = (((0,), (0,)), ((), ()))


def _cparams(*sem):
    return pltpu.CompilerParams(dimension_semantics=sem, vmem_limit_bytes=VMEM_LIMIT)


def _sds(shape, dtype=F32):
    return jax.ShapeDtypeStruct(tuple(shape), dtype)


def _rows(tm, width, col=0):
    return pl.BlockSpec((tm, width), lambda i: (i, col))


def _full(shape):
    nd = len(shape)
    return pl.BlockSpec(tuple(shape), lambda *_: (0,) * nd)


def _sigmoid(x):
    return 1.0 / (1.0 + jnp.exp(-x))


def _silu_and_grad(g):
    s = _sigmoid(g)
    return g * s, s * (1.0 + g * (1.0 - s))


def _gelu_and_grad(x):
    cdf = 0.5 * (1.0 + lax.erf(x * (1.0 / math.sqrt(2.0))))
    pdf = jnp.exp(-0.5 * x * x) * (1.0 / math.sqrt(2.0 * math.pi))
    return x * cdf, cdf + x * pdf


def _swap_halves(a):
    lane = lax.broadcasted_iota(jnp.int32, a.shape, 1)
    up = pltpu.roll(a, 32, 1)
    down = pltpu.roll(a, 96, 1)
    return jnp.where(lane < 32, down, jnp.where(lane < 64, up, 0.0))


def _rope_tables(pos_ref, rope_ref):
    ang = pos_ref[...].astype(F32) * rope_ref[0:1, :]
    return jnp.cos(ang) * rope_ref[1:2, :], jnp.sin(ang) * rope_ref[2:3, :]


class _Comm:
    def __init__(self, srcs, kinds):
        self.srcs = list(srcs)
        self.kinds = list(kinds)
        self.n = len(self.srcs)
        self.out_shape = [_sds((N_DEV,) + tuple(s.shape[1:] if k else s.shape), s.dtype)
                          for s, k in zip(self.srcs, self.kinds)]
        self.specs = [pl.BlockSpec(memory_space=pl.ANY)] * self.n
        self.scratch = [pltpu.SemaphoreType.DMA((self.n, N_DEV - 1)), pltpu.SemaphoreType.DMA((self.n, N_DEV - 1)),
                        pltpu.SemaphoreType.DMA((self.n,))] if self.n else []

    def _copies(self, src_refs, out_refs, sems, with_recvs):
        send_sems, recv_sems, local_sems = sems
        x, y, c = lax.axis_index("x"), lax.axis_index("y"), lax.axis_index("c")
        me = 4 * x + 2 * y + c
        local, sends, recvs = [], [], []
        for a in range(self.n):
            def block_for(dest, src_ref=src_refs[a], a2a=self.kinds[a]):
                return src_ref.at[dest] if a2a else src_ref

            local.append(pltpu.make_async_copy(block_for(me), out_refs[a].at[me], local_sems.at[a]))
            for r in range(1, N_DEV):
                px = 1 - x if (r >> 2) & 1 else x
                py = 1 - y if (r >> 1) & 1 else y
                pc = 1 - c if r & 1 else c
                peer = 4 * px + 2 * py + pc
                sends.append(pltpu.make_async_remote_copy(
                    src_ref=block_for(peer), dst_ref=out_refs[a].at[me],
                    send_sem=send_sems.at[a, r - 1], recv_sem=recv_sems.at[a, r - 1],
                    device_id=(px, py, pc), device_id_type=MESH))
                if with_recvs:
                    recvs.append(pltpu.make_async_remote_copy(
                        src_ref=block_for(me), dst_ref=out_refs[a].at[peer],
                        send_sem=send_sems.at[a, r - 1], recv_sem=recv_sems.at[a, r - 1],
                        device_id=(px, py, pc), device_id_type=MESH))
        return local, sends, recvs

    def start(self, src_refs, out_refs, sems):
        local, sends, _ = self._copies(src_refs, out_refs, sems, False)
        for cp in local + sends:
            cp.start()

    def finish(self, src_refs, out_refs, sems):
        local, sends, recvs = self._copies(src_refs, out_refs, sems, True)
        for cp in recvs:
            cp.wait_recv()
        for cp in sends:
            cp.wait_send()
        for cp in local:
            cp.wait()


def _exchange(srcs, kinds, name):
    comm = _Comm(srcs, kinds)
    n = comm.n

    def body(*refs):
        src_refs, out_refs, sems = refs[:n], refs[n:2 * n], refs[2 * n:]
        comm.start(src_refs, out_refs, sems)
        comm.finish(src_refs, out_refs, sems)

    return pl.pallas_call(
        body, name=name, out_shape=comm.out_shape, in_specs=comm.specs, out_specs=comm.specs,
        scratch_shapes=comm.scratch,
    )(*srcs)


def _gather_two_level(srcs, name):
    n = len(srcs)

    def body(*refs):
        src_refs, out_refs = refs[:n], refs[n:2 * n]
        send_sems, recv_sems, local_sems = refs[2 * n:]
        x, y, c = lax.axis_index("x"), lax.axis_index("y"), lax.axis_index("c")
        me, sibling = (x, y, c), (x, y, 1 - c)
        chips = [(1 - x, y), (x, 1 - y), (1 - x, 1 - y)]

        def block(a, dev):
            return out_refs[a].at[4 * dev[0] + 2 * dev[1] + dev[2]]

        def copy(a, k, owner, to, src=None):
            return pltpu.make_async_remote_copy(
                src_ref=block(a, owner) if src is None else src, dst_ref=block(a, owner),
                send_sem=send_sems.at[a, k], recv_sem=recv_sems.at[a, k], device_id=to, device_id_type=MESH)

        mine, first, passed = [], [], []
        for a in range(n):
            mine.append(pltpu.make_async_copy(src_refs[a], block(a, me), local_sems.at[a]))
            first.append(copy(a, 0, me, sibling, src=src_refs[a]))
            first += [copy(a, 1 + j, me, (*chip, c), src=src_refs[a]) for j, chip in enumerate(chips)]
        for cp in mine + first:
            cp.start()
        for a in range(n):
            for j, chip in enumerate(chips):
                copy(a, 1 + j, (*chip, c), me).wait_recv()
                fwd = copy(a, 4 + j, (*chip, c), sibling)
                fwd.start()
                passed.append(fwd)
        for a in range(n):
            copy(a, 0, sibling, me).wait_recv()
            for j, chip in enumerate(chips):
                copy(a, 4 + j, (*chip, 1 - c), me).wait_recv()
        for cp in first + passed:
            cp.wait_send()
        for cp in mine:
            cp.wait()

    return pl.pallas_call(
        body, name=name,
        out_shape=[_sds((N_DEV,) + tuple(s.shape), s.dtype) for s in srcs],
        in_specs=[pl.BlockSpec(memory_space=pl.ANY)] * n, out_specs=[pl.BlockSpec(memory_space=pl.ANY)] * n,
        scratch_shapes=[pltpu.SemaphoreType.DMA((n, N_DEV - 1)), pltpu.SemaphoreType.DMA((n, N_DEV - 1)),
                        pltpu.SemaphoreType.DMA((n,))],
    )(*srcs)


class _Packer:
    def __init__(self, entries, row_multiple):
        self.entries = entries
        self.offsets = {}
        off = 0
        for name, shape in entries:
            self.offsets[name] = off
            off += int(np.prod(shape))
        quantum = PACK_LANES * row_multiple
        self.total = -(-off // quantum) * quantum
        self.used = off
        self.rows = self.total // PACK_LANES

    def pack(self, arrays, dtype, lead=()):
        n = len(lead)
        flat = [arrays[name].astype(dtype).reshape(lead + (-1,)) for name, _ in self.entries]
        flat.append(jnp.zeros(lead + (self.total - self.used,), dtype))
        return jnp.concatenate(flat, axis=n).reshape(lead + (self.rows, PACK_LANES))

    def unpack(self, buf, lead=()):
        flat = buf.reshape(lead + (self.total,))
        out = {}
        for name, shape in self.entries:
            o = self.offsets[name]
            out[name] = lax.slice_in_dim(flat, o, o + int(np.prod(shape)), axis=len(lead)).reshape(lead + tuple(shape))
        return out


def _ada_forward(c_rows, w_ada, b_ada_cols):
    cols = w_ada.shape[-1]
    rows = c_rows.shape[0]

    def body(c_ref, w_ref, b_ref, sc_ref, part_ref):
        cv = c_ref[...]
        sc = cv * _sigmoid(cv)
        sc_ref[...] = sc
        scb = sc.astype(BF16)
        for l in range(DEPTH):
            part_ref[l] = jnp.dot(scb, w_ref[l].astype(BF16), preferred_element_type=F32) + b_ref[l:l + 1, :]

    return pl.pallas_call(
        body, name="ada_forward",
        out_shape=(_sds((rows, D)), _sds((DEPTH, rows, cols))),
        compiler_params=pltpu.CompilerParams(vmem_limit_bytes=VMEM_LIMIT),
    )(c_rows, w_ada, b_ada_cols)


def _ada_backward(sc_t, dmod_cols):
    cols = dmod_cols.shape[-1]

    def body(sc_ref, dm_ref, gw_ref):
        scb = sc_ref[...].astype(BF16)
        for l in range(DEPTH):
            gw_ref[l] = jnp.dot(scb, dm_ref[l].astype(BF16), preferred_element_type=F32)

    return pl.pallas_call(
        body, name="ada_backward",
        out_shape=_sds((DEPTH, D, cols)),
        compiler_params=pltpu.CompilerParams(vmem_limit_bytes=VMEM_LIMIT),
    )(sc_t, dmod_cols)


def _riding(comm, nsteps, c_src, c_out, c_sems, where):
    if not comm.n:
        return
    step = 0 if where == "start" else nsteps - 1

    @pl.when(pl.program_id(0) == step)
    def _():
        (comm.start if where == "start" else comm.finish)(c_src, c_out, c_sems)


def _prenorm_inproj(x, g_pre, mod, w_in_p, name, comm=None):
    S = x.shape[0]
    tm = ROW_TILE
    comm = comm or _Comm([], [])
    nc = comm.n

    def body(*refs):
        x_ref, g_ref, mod_ref, w_ref = refs[:4]
        c_src, z_ref, hb_ref = refs[4:4 + nc], refs[4 + nc], refs[5 + nc]
        c_out, c_sems = refs[6 + nc:6 + 2 * nc], refs[6 + 2 * nc:]
        _riding(comm, S // tm, c_src, c_out, c_sems, "start")
        xv = x_ref[...]
        rstd = lax.rsqrt(jnp.mean(xv * xv, axis=-1, keepdims=True) + EPS)
        hb = ((xv * rstd * g_ref[...]) * (1.0 + mod_ref[1:2, :]) + mod_ref[0:1, :]).astype(BF16)
        hb_ref[...] = hb
        z_ref[...] = jnp.dot(hb, w_ref[...], preferred_element_type=F32).astype(BF16)
        _riding(comm, S // tm, c_src, c_out, c_sems, "finish")

    outs = pl.pallas_call(
        body, name=name, grid=(S // tm,),
        in_specs=[_rows(tm, D), _full((1, D)), _full((3, D)), _full((D, DZ))] + comm.specs,
        out_specs=[_rows(tm, DZ), _rows(tm, D)] + comm.specs,
        out_shape=[_sds((S, DZ), BF16), _sds((S, D), BF16)] + comm.out_shape,
        scratch_shapes=comm.scratch,
        compiler_params=_cparams("arbitrary"),
    )(x, g_pre, mod, w_in_p, *comm.srcs)
    return outs[0], outs[1], outs[2:]


def _att_prep(z, pos, q_g, kv_g, wq_p, w_ukv, rope_rows, name):
    S = z.shape[0]
    tm = ROW_TILE

    def body(z_ref, pos_ref, qg_ref, kvg_ref, wq_ref, wkv_ref, rope_ref, q_ref, k_ref, v_ref):
        zz = z_ref[...].astype(F32)
        ql, kvl, ka = zz[:, 0:Q_RANK], zz[:, Q_RANK:Q_RANK + KV_RANK], zz[:, Q_RANK + KV_RANK:]
        qn = ql * lax.rsqrt(jnp.mean(ql * ql, axis=-1, keepdims=True) + EPS) * qg_ref[...]
        kvn = kvl * lax.rsqrt(jnp.mean(kvl * kvl, axis=-1, keepdims=True) + EPS) * kvg_ref[...]
        q = jnp.dot(qn.astype(BF16), wq_ref[...], preferred_element_type=F32)
        kv = jnp.dot(kvn.astype(BF16), wkv_ref[...], preferred_element_type=F32)
        ct, st = _rope_tables(pos_ref, rope_ref)
        krot = (ka * ct + _swap_halves(ka) * st).astype(BF16)
        for h in range(HEADS):
            b = h * HQ
            q_ref[:, b:b + NOPE] = q[:, b:b + NOPE].astype(BF16)
            a = q[:, b + NOPE:b + HQ]
            q_ref[:, b + NOPE:b + HQ] = (a * ct + _swap_halves(a) * st).astype(BF16)
            k_ref[:, b:b + NOPE] = kv[:, b:b + NOPE].astype(BF16)
            k_ref[:, b + NOPE:b + HQ] = krot
            v_ref[:, h * VDIM:(h + 1) * VDIM] = kv[:, b + NOPE:b + HQ].astype(BF16)

    return pl.pallas_call(
        body, name=name, grid=(S // tm,),
        in_specs=[_rows(tm, 512, 0), _rows(tm, 1), _full((1, Q_RANK)), _full((1, KV_RANK)),
                  _full((Q_RANK, HEADS * HQ)), _full((KV_RANK, HEADS * HQ)), _full((8, 128))],
        out_specs=(_rows(tm, HEADS * HQ), _rows(tm, HEADS * HQ), _rows(tm, ATT_W)),
        out_shape=(_sds((S, HEADS * HQ), BF16), _sds((S, HEADS * HQ), BF16), _sds((S, ATT_W), BF16)),
        compiler_params=_cparams("parallel"),
    )(z, pos, q_g, kv_g, wq_p, w_ukv, rope_rows)


def _flash_forward(q, k, v, name, comm=None):
    S = q.shape[0]
    t = ATT_TILE
    nq = S // t
    nl = t // 128
    comm = comm or _Comm([], [])
    nc = comm.n

    def body(*refs):
        q_ref, k_ref, v_ref = refs[:3]
        c_src = refs[3:3 + nc]
        o_ref, lse_ref = refs[3 + nc:5 + nc]
        c_out = refs[5 + nc:5 + 2 * nc]
        m_sc, l_sc, acc_sc, s_sc, mp_sc = refs[5 + 2 * nc:10 + 2 * nc]
        c_sems = refs[10 + 2 * nc:]
        if nc:
            @pl.when((pl.program_id(0) == 0) & (pl.program_id(1) == 0))
            def _():
                comm.start(c_src, c_out, c_sems)

        qb = pl.program_id(1)
        m_sc[...] = jnp.full(m_sc.shape, NEG_INF, F32)
        l_sc[...] = jnp.zeros(l_sc.shape, F32)
        acc_sc[...] = jnp.zeros(acc_sc.shape, F32)

        def score_phase(kb, slot, diagonal):
            s = lax.dot_general(q_ref[...], k_ref[pl.ds(pl.multiple_of(kb * t, t), t), :], NT,
                                preferred_element_type=F32)
            if diagonal:
                ri = lax.broadcasted_iota(jnp.int32, (t, t), 0)
                ci = lax.broadcasted_iota(jnp.int32, (t, t), 1)
                s = jnp.where(ci <= ri, s, NEG_INF)
            s_sc[slot] = s
            mp = s[:, 0:128]
            for c in range(1, nl):
                mp = jnp.maximum(mp, s[:, c * 128:(c + 1) * 128])
            mp_sc[slot] = mp

        def sum_phase(kb, slot):
            m_prev = m_sc[...]
            m_new = jnp.maximum(m_prev, jnp.max(mp_sc[slot], axis=-1, keepdims=True))
            alpha = jnp.exp2((m_prev - m_new) * EXP2_SCALE)
            p = jnp.exp2(s_sc[slot] * EXP2_SCALE - jnp.tile(m_new * EXP2_SCALE, (1, nl)))
            lp = alpha * l_sc[...]
            for c in range(nl):
                lp = lp + p[:, c * 128:(c + 1) * 128]
            l_sc[...] = lp
            acc_sc[...] = alpha * acc_sc[...] + jnp.dot(p.astype(BF16), v_ref[pl.ds(pl.multiple_of(kb * t, t), t), :],
                                                        preferred_element_type=F32)
            m_sc[...] = m_new

        def tile_at(pos):
            return jnp.where(pos == 0, qb, pos - 1)

        def run(p0, count, final):
            for i in range(count):
                sum_phase(tile_at(p0 + i), i % 2)
                if not (final and i == count - 1):
                    score_phase(p0 + i, (i + 1) % 2, False)

        score_phase(qb, 0, True)

        def trip(u, carry):
            run(FLASH_UNROLL * u, FLASH_UNROLL, False)
            return carry

        lax.fori_loop(0, qb // FLASH_UNROLL, trip, 0)
        for left in range(FLASH_UNROLL):
            @pl.when(qb % FLASH_UNROLL == left)
            def _():
                run(qb - left, left + 1, True)

        l = jnp.sum(l_sc[...], axis=-1, keepdims=True)
        o_ref[...] = acc_sc[...] / l
        lse_ref[0] = jnp.max(m_sc[...], axis=-1, keepdims=True) * ATT_SCALE + jnp.log(l)

        if nc:
            @pl.when((pl.program_id(0) == HEADS - 1) & (pl.program_id(1) == nq - 1))
            def _():
                comm.finish(c_src, c_out, c_sems)

    outs = pl.pallas_call(
        body, name=name, grid=(HEADS, nq),
        in_specs=[pl.BlockSpec((t, HQ), lambda h, i: (i, h)),
                  pl.BlockSpec((S, HQ), lambda h, i: (0, h)),
                  pl.BlockSpec((S, VDIM), lambda h, i: (0, h))] + comm.specs,
        out_specs=[pl.BlockSpec((t, VDIM), lambda h, i: (i, h)),
                   pl.BlockSpec((1, t, 1), lambda h, i: (h, i, 0))] + comm.specs,
        out_shape=[_sds((S, ATT_W)), _sds((HEADS, S, 1))] + comm.out_shape,
        scratch_shapes=[pltpu.VMEM((t, 128), F32), pltpu.VMEM((t, 128), F32), pltpu.VMEM((t, VDIM), F32),
                        pltpu.VMEM((2, t, t), F32), pltpu.VMEM((2, t, 128), F32)] + comm.scratch,
        compiler_params=_cparams("arbitrary", "arbitrary"),
    )(q, k, v, *comm.srcs)
    return outs[0], outs[1], outs[2:]


def _conv_window(win_ref, a_prev, b_prev, a_cur, b_cur, first):
    hp = a_prev * _sigmoid(b_prev)
    win_ref[0:HALO, :] = jnp.where(first, 0.0, hp)
    win_ref[HALO:, :] = a_cur * _sigmoid(b_cur)


def _shifted_copies(win_ref, sh_ref, tm):
    for b in range(1, 8):
        sh_ref[b - 1] = win_ref[pl.ds(b, tm + HALO - 8), :]


def _tap(win_ref, sh_ref, offset, tm):
    a, b = divmod(offset, 8)
    if b == 0:
        return win_ref[pl.ds(8 * a, tm), :]
    return sh_ref[b - 1, pl.ds(8 * a, tm), :]


def _conv_in_specs(tm):
    per = tm // HALO
    prev = lambda col: pl.BlockSpec((HALO, CONV_W), lambda i: (jnp.maximum(i * per - 1, 0), col))
    return [_rows(tm, CONV_W, 4), _rows(tm, CONV_W, 5), prev(4), prev(5)]


def _conv_forward(z, conv_w_p, conv_b, ln_g, ln_b, w_pw2, name):
    S = z.shape[0]
    tm = ROW_TILE

    def body(a_ref, b_ref, ap_ref, bp_ref, w_ref, cb_ref, g_ref, be_ref, pw_ref, cv_ref, y_ref, win, sh):
        _conv_window(win, ap_ref[...].astype(F32), bp_ref[...].astype(F32), a_ref[...].astype(F32),
                     b_ref[...].astype(F32), pl.program_id(0) == 0)
        _shifted_copies(win, sh, tm)
        for r0 in range(0, tm, CONV_ROWS):
            acc = jnp.zeros((CONV_ROWS, CONV_W), F32)
            for kk in range(CONV_K):
                acc = acc + w_ref[kk:kk + 1, :] * _tap(win, sh, r0 + HALO - (CONV_K - 1) + kk, CONV_ROWS)
            cv_ref[r0:r0 + CONV_ROWS, :] = acc + cb_ref[...]
        cv = cv_ref[...]
        mu = jnp.mean(cv, axis=-1, keepdims=True)
        cc = cv - mu
        rstd = lax.rsqrt(jnp.mean(cc * cc, axis=-1, keepdims=True) + EPS)
        n = cc * rstd * g_ref[...] + be_ref[...]
        sl = n * _sigmoid(n)
        y_ref[...] = jnp.dot(sl.astype(BF16), pw_ref[...], preferred_element_type=F32)

    return pl.pallas_call(
        body, name=name, grid=(S // tm,),
        in_specs=_conv_in_specs(tm) + [_full((HALO, CONV_W)), _full((1, CONV_W)), _full((1, CONV_W)),
                                       _full((1, CONV_W)), _full((CONV_W, CONV_W))],
        out_specs=(_rows(tm, CONV_W), _rows(tm, CONV_W)),
        out_shape=(_sds((S, CONV_W)), _sds((S, CONV_W))),
        scratch_shapes=[pltpu.VMEM((tm + HALO, CONV_W), F32), pltpu.VMEM((7, tm + HALO - 8, CONV_W), F32)],
        compiler_params=_cparams("parallel"),
    )(z, z, z, z, conv_w_p, conv_b, ln_g, ln_b, w_pw2)


def _sgu_common(u, v, g_ref, be_ref):
    gu, dgu = _gelu_and_grad(u)
    gv, dgv = _gelu_and_grad(v)
    mu = jnp.mean(gv, axis=-1, keepdims=True)
    cc = gv - mu
    rstd = lax.rsqrt(jnp.mean(cc * cc, axis=-1, keepdims=True) + EPS)
    nh = cc * rstd
    vn = nh * g_ref[...] + be_ref[...]
    return gu, dgu, dgv, rstd, nh, vn


def _sgu_masks():
    lane_group = lax.broadcasted_iota(jnp.int32, (1, SGU_W), 1) // SGU_GD
    ri = lax.broadcasted_iota(jnp.int32, (SGU_T, SGU_T), 0)
    ci = lax.broadcasted_iota(jnp.int32, (SGU_T, SGU_T), 1)
    return [lane_group == g for g in range(SGU_G)], ci <= ri


def _sgu_forward(z, ln_g, ln_b, w_s, bias_full, name):
    S = z.shape[0]
    tm = ROW_TILE

    def body(u_ref, v_ref, g_ref, be_ref, ws_ref, bias_ref, y_ref):
        gmask, tril = _sgu_masks()
        wm = [jnp.where(tril, ws_ref[g], 0.0).astype(BF16) for g in range(SGU_G)]
        for ch in range(tm // SGU_T):
            rows = slice(ch * SGU_T, (ch + 1) * SGU_T)
            gu, _, _, _, _, vn = _sgu_common(u_ref[rows, :].astype(F32), v_ref[rows, :].astype(F32), g_ref, be_ref)
            vb = vn.astype(BF16)
            sv = bias_ref[...]
            for g in range(SGU_G):
                sv = sv + jnp.where(gmask[g], jnp.dot(wm[g], vb, preferred_element_type=F32), 0.0)
            y_ref[rows, :] = gu * sv

    return pl.pallas_call(
        body, name=name, grid=(S // tm,),
        in_specs=[_rows(tm, SGU_W, 7), _rows(tm, SGU_W, 8), _full((1, SGU_W)), _full((1, SGU_W)),
                  _full((SGU_G, SGU_T, SGU_T)), _full((SGU_T, SGU_W))],
        out_specs=_rows(tm, SGU_W), out_shape=_sds((S, SGU_W)),
        compiler_params=_cparams("parallel"),
    )(z, z, ln_g, ln_b, w_s, bias_full)


def _out_proj(x, z, y_att, y_conv, y_sgu, w_out, g_post, mod, name, target=None):
    S = x.shape[0]
    tm = ROW_TILE
    head = target is not None

    def body(*refs):
        x_ref, ga_ref, gc_ref, gs_ref, ya_ref, yc_ref, ys_ref, w_ref, gp_ref, mod_ref = refs[:10]
        t_ref = refs[10] if head else None
        xn_ref, y_ref, cat_ref = refs[10 + head:13 + head]
        ca = (ya_ref[...] * _silu_and_grad(ga_ref[...].astype(F32))[0]).astype(BF16)
        cc = (yc_ref[...] * _silu_and_grad(gc_ref[...].astype(F32))[0]).astype(BF16)
        cs = (ys_ref[...] * _silu_and_grad(gs_ref[...].astype(F32))[0]).astype(BF16)
        cat_ref[:, 0:ATT_W] = ca
        cat_ref[:, ATT_W:ATT_W + CONV_W] = cc
        cat_ref[:, ATT_W + CONV_W:] = cs
        y = (jnp.dot(ca, w_ref[0:ATT_W, :], preferred_element_type=F32)
             + jnp.dot(cc, w_ref[ATT_W:ATT_W + CONV_W, :], preferred_element_type=F32)
             + jnp.dot(cs, w_ref[ATT_W + CONV_W:, :], preferred_element_type=F32))
        y_ref[...] = y
        rstd = lax.rsqrt(jnp.mean(y * y, axis=-1, keepdims=True) + EPS)
        xn = x_ref[...] + mod_ref[2:3, :] * (y * rstd * gp_ref[...])
        if not head:
            xn_ref[...] = xn
            return
        loss_ref = refs[14]

        @pl.when(pl.program_id(0) == 0)
        def _():
            loss_ref[...] = jnp.zeros(loss_ref.shape, F32)

        err = xn - t_ref[...]
        xn_ref[...] = err * (1.0 / D)
        row = jnp.sum(err * err, axis=-1, keepdims=True) * (1.0 / D)
        loss_ref[...] += 0.5 * jnp.sum(row, axis=0, keepdims=True)

    return pl.pallas_call(
        body, name=name, grid=(S // tm,),
        in_specs=[_rows(tm, D), _rows(tm, 512, 1), _rows(tm, 256, 6), _rows(tm, 256, 9),
                  _rows(tm, ATT_W), _rows(tm, CONV_W), _rows(tm, SGU_W),
                  _full((D, D)), _full((1, D)), _full((3, D))] + ([_rows(tm, D)] if head else []),
        out_specs=[_rows(tm, D), _rows(tm, D), _rows(tm, D)] + ([_full((1, 1))] if head else []),
        out_shape=[_sds((S, D)), _sds((S, D)), _sds((S, D), BF16)] + ([_sds((1, 1))] if head else []),
        compiler_params=_cparams("arbitrary" if head else "parallel"),
    )(x, z, z, z, y_att, y_conv, y_sgu, w_out, g_post, mod, *([target] if head else []))


def _matmul_tn(a, b, name, comm=None):
    S, M = a.shape
    pieces = list(b) if isinstance(b, (list, tuple)) else [b]
    nb = len(pieces)
    N = sum(p.shape[1] for p in pieces)
    bk = min(MATMUL_TN_ROWS, S)
    riding = comm is not None
    comm = comm or _Comm([], [])
    nc = comm.n

    def body(*refs):
        a_ref, b_refs = refs[0], refs[1:1 + nb]
        c_src, o_ref = refs[1 + nb:1 + nb + nc], refs[1 + nb + nc]
        c_out = refs[2 + nb + nc:2 + nb + 2 * nc]
        rest = refs[2 + nb + 2 * nc:]
        b_sc, c_sems = (rest[0], rest[1:]) if nb > 1 else (None, rest)
        _riding(comm, S // bk, c_src, c_out, c_sems, "start")

        @pl.when(pl.program_id(0) == 0)
        def _():
            o_ref[...] = jnp.zeros(o_ref.shape, F32)

        if nb > 1:
            _assemble_columns(b_sc, b_refs)
            bv = b_sc[...]
        else:
            bv = b_refs[0][...]
        o_ref[...] += lax.dot_general(a_ref[...], bv, TN, preferred_element_type=F32)
        _riding(comm, S // bk, c_src, c_out, c_sems, "finish")

    outs = pl.pallas_call(
        body, name=name, grid=(S // bk,),
        in_specs=[pl.BlockSpec((bk, M), lambda k: (k, 0))]
                 + [pl.BlockSpec((bk, p.shape[1]), lambda k: (k, 0)) for p in pieces] + comm.specs,
        out_specs=[_full((M, N))] + comm.specs, out_shape=[_sds((M, N))] + comm.out_shape,
        scratch_shapes=([pltpu.VMEM((bk, N), BF16)] if nb > 1 else []) + comm.scratch,
        compiler_params=_cparams("arbitrary"),
    )(a, *pieces, *comm.srcs)
    return (outs[0], outs[1:]) if riding else outs[0]


def _out_proj_backward(dxo, y, z, y_att, y_conv, y_sgu, lse, w_out, g_post, mod, name):
    S = dxo.shape[0]
    tm = ROW_TILE

    def body(dxo_ref, y_ref, ga_ref, gc_ref, gs_ref, ya_ref, yc_ref, ys_ref, lse_ref, w_ref, gp_ref, mod_ref,
             dyb_ref, dob_ref, st_ref, dga_ref, dyc_ref, dgc_ref, dys_ref, dgs_ref, dgate_ref, dgp_ref):
        @pl.when(pl.program_id(0) == 0)
        def _():
            dgate_ref[...] = jnp.zeros(dgate_ref.shape, F32)
            dgp_ref[...] = jnp.zeros(dgp_ref.shape, F32)

        dxo_v = dxo_ref[...]
        yv = y_ref[...]
        gp = gp_ref[...]
        rstd = lax.rsqrt(jnp.mean(yv * yv, axis=-1, keepdims=True) + EPS)
        yhat = yv * rstd
        dgate_ref[...] += jnp.sum(dxo_v * (yhat * gp), axis=0, keepdims=True)
        dr = dxo_v * mod_ref[2:3, :]
        dgp_ref[...] += jnp.sum(dr * yhat, axis=0, keepdims=True)
        dyh = dr * gp
        dy = rstd * (dyh - yhat * jnp.mean(dyh * yhat, axis=-1, keepdims=True))
        dyb = dy.astype(BF16)
        dyb_ref[...] = dyb
        dcat = lax.dot_general(dyb, w_ref[...], NT, preferred_element_type=F32)

        ya = ya_ref[...]
        sil, dsil = _silu_and_grad(ga_ref[...].astype(F32))
        da = dcat[:, 0:ATT_W]
        do = da * sil
        dob_ref[...] = do.astype(BF16)
        dga_ref[...] = (da * ya * dsil).astype(BF16)
        lane = lax.broadcasted_iota(jnp.int32, (1, 128), 1)
        stats = jnp.zeros((tm, 128), F32)
        for h in range(HEADS):
            cols = slice(h * VDIM, (h + 1) * VDIM)
            delta = jnp.sum(do[:, cols] * ya[:, cols], axis=-1, keepdims=True)
            stats = stats + jnp.where(lane == 2 * h, lse_ref[h], 0.0) + jnp.where(lane == 2 * h + 1, delta, 0.0)
        st_ref[...] = stats

        sil, dsil = _silu_and_grad(gc_ref[...].astype(F32))
        dc = dcat[:, ATT_W:ATT_W + CONV_W]
        dyc_ref[...] = dc * sil
        dgc_ref[...] = (dc * yc_ref[...] * dsil).astype(BF16)
        sil, dsil = _silu_and_grad(gs_ref[...].astype(F32))
        dsg = dcat[:, ATT_W + CONV_W:]
        dys_ref[...] = dsg * sil
        dgs_ref[...] = (dsg * ys_ref[...] * dsil).astype(BF16)

    return pl.pallas_call(
        body, name=name, grid=(S // tm,),
        in_specs=[_rows(tm, D), _rows(tm, D), _rows(tm, 512, 1), _rows(tm, 256, 6), _rows(tm, 256, 9),
                  _rows(tm, ATT_W), _rows(tm, CONV_W), _rows(tm, SGU_W),
                  pl.BlockSpec((HEADS, tm, 1), lambda i: (0, i, 0)),
                  _full((D, D)), _full((1, D)), _full((3, D))],
        out_specs=(_rows(tm, D), _rows(tm, ATT_W), _rows(tm, 128), _rows(tm, ATT_W),
                   _rows(tm, CONV_W), _rows(tm, CONV_W), _rows(tm, SGU_W), _rows(tm, SGU_W),
                   _full((1, D)), _full((1, D))),
        out_shape=(_sds((S, D), BF16), _sds((S, ATT_W), BF16), _sds((S, 128)), _sds((S, ATT_W), BF16),
                   _sds((S, CONV_W)), _sds((S, CONV_W), BF16), _sds((S, SGU_W)), _sds((S, SGU_W), BF16),
                   _sds((1, D)), _sds((1, D))),
        compiler_params=_cparams("arbitrary"),
    )(dxo, y, z, z, z, y_att, y_conv, y_sgu, lse, w_out, g_post, mod)


def _flash_backward(q, k, v, do, stats, name, comm=None):
    S = q.shape[0]
    t = ATT_TILE
    tk = 2 * t
    nq = S // t
    comm = comm or _Comm([], [])
    nc = comm.n

    def body(*refs):
        q_ref, do_ref, st_ref, k_ref, v_ref = refs[:5]
        c_src = refs[5:5 + nc]
        dq_ref, dk_ref, dv_ref = refs[5 + nc:8 + nc]
        c_out = refs[8 + nc:8 + 2 * nc]
        dk_sc, dv_sc = refs[8 + 2 * nc:10 + 2 * nc]
        c_sems = refs[10 + 2 * nc:]
        h = pl.program_id(0)
        j = pl.program_id(1)
        if nc:
            @pl.when((h == 0) & (j == 0))
            def _():
                comm.start(c_src, c_out, c_sems)

        @pl.when(j == 0)
        def _():
            dq_ref[...] = jnp.zeros(dq_ref.shape, F32)

        dk_sc[...] = jnp.zeros(dk_sc.shape, F32)
        dv_sc[...] = jnp.zeros(dv_sc.shape, F32)
        lane = lax.broadcasted_iota(jnp.int32, (1, 128), 1)

        def chain(hf, qv, dov, lse2, delta, diagonal):
            kt = k_ref[hf * t:(hf + 1) * t, :]
            s = lax.dot_general(qv, kt, NT, preferred_element_type=F32)
            p = jnp.exp2(s * EXP2_SCALE - lse2)
            if diagonal:
                ri = lax.broadcasted_iota(jnp.int32, (t, t), 0)
                ci = lax.broadcasted_iota(jnp.int32, (t, t), 1)
                p = jnp.where(ci <= ri, p, 0.0)
            dv_sc[hf] += lax.dot_general(p.astype(BF16), dov, TN, preferred_element_type=F32)
            dp = lax.dot_general(dov, v_ref[hf * t:(hf + 1) * t, :], NT, preferred_element_type=F32)
            ds = (p * (dp - delta) * ATT_SCALE).astype(BF16)
            dk_sc[hf] += lax.dot_general(ds, qv, TN, preferred_element_type=F32)
            return jnp.dot(ds, kt, preferred_element_type=F32)

        def q_tile(qb, modes):
            rows = pl.ds(pl.multiple_of(qb * t, t), t)
            qv = q_ref[rows, :]
            dov = do_ref[rows, :]
            st = st_ref[rows, :]
            lse2 = jnp.sum(jnp.where(lane == 2 * h, st, 0.0), axis=-1, keepdims=True) * LOG2E
            delta = jnp.sum(jnp.where(lane == 2 * h + 1, st, 0.0), axis=-1, keepdims=True)
            parts = [chain(hf, qv, dov, lse2, delta, modes[hf]) for hf in range(2) if modes[hf] is not None]
            dq_ref[rows, :] += parts[0] if len(parts) == 1 else parts[0] + parts[1]

        q_tile(2 * j, (True, None))
        q_tile(2 * j + 1, (False, True))

        def loop_body(i, carry):
            q_tile(2 * (j + 1 + i), (False, False))
            q_tile(2 * (j + 1 + i) + 1, (False, False))
            return carry

        lax.fori_loop(0, nq // 2 - j - 1, loop_body, 0)
        for hf in range(2):
            dk_ref[hf * t:(hf + 1) * t, :] = dk_sc[hf]
            dv_ref[hf * t:(hf + 1) * t, :] = dv_sc[hf]

        if nc:
            @pl.when((h == HEADS - 1) & (j == S // tk - 1))
            def _():
                comm.finish(c_src, c_out, c_sems)

    outs = pl.pallas_call(
        body, name=name, grid=(HEADS, S // tk),
        in_specs=[pl.BlockSpec((S, HQ), lambda h, j: (0, h)),
                  pl.BlockSpec((S, VDIM), lambda h, j: (0, h)),
                  pl.BlockSpec((S, 128), lambda h, j: (0, 0)),
                  pl.BlockSpec((tk, HQ), lambda h, j: (j, h)),
                  pl.BlockSpec((tk, VDIM), lambda h, j: (j, h))] + comm.specs,
        out_specs=[pl.BlockSpec((S, HQ), lambda h, j: (0, h)),
                   pl.BlockSpec((tk, HQ), lambda h, j: (j, h)),
                   pl.BlockSpec((tk, VDIM), lambda h, j: (j, h))] + comm.specs,
        out_shape=[_sds((S, HEADS * HQ)), _sds((S, HEADS * HQ)), _sds((S, ATT_W))] + comm.out_shape,
        scratch_shapes=[pltpu.VMEM((2, t, HQ), F32), pltpu.VMEM((2, t, VDIM), F32)] + comm.scratch,
        compiler_params=_cparams("arbitrary", "arbitrary"),
    )(q, do, stats, k, v, *comm.srcs)
    return outs[0], outs[1], outs[2], outs[3:]


def _att_prep_backward(z, pos, dq, dk, dv, q_g, kv_g, wq_p, w_ukv, rope_rows, name):
    S = z.shape[0]
    tm = ROW_TILE

    def body(z_ref, pos_ref, dq_ref, dk_ref, dv_ref, qg_ref, kvg_ref, wq_ref, wkv_ref, rope_ref,
             dz_ref, qn_ref, dqp_ref, kvn_ref, dkv_ref, dqg_ref, dkvg_ref):
        @pl.when(pl.program_id(0) == 0)
        def _():
            dqg_ref[...] = jnp.zeros(dqg_ref.shape, F32)
            dkvg_ref[...] = jnp.zeros(dkvg_ref.shape, F32)

        zz = z_ref[...].astype(F32)
        ql, kvl = zz[:, 0:Q_RANK], zz[:, Q_RANK:Q_RANK + KV_RANK]
        q_rstd = lax.rsqrt(jnp.mean(ql * ql, axis=-1, keepdims=True) + EPS)
        kv_rstd = lax.rsqrt(jnp.mean(kvl * kvl, axis=-1, keepdims=True) + EPS)
        qhat, kvhat = ql * q_rstd, kvl * kv_rstd
        qg, kvg = qg_ref[...], kvg_ref[...]
        qn_ref[...] = (qhat * qg).astype(BF16)
        kvn_ref[...] = (kvhat * kvg).astype(BF16)
        ct, st = _rope_tables(pos_ref, rope_ref)

        def unrotate(d):
            return d * ct + _swap_halves(d * st)

        dkrot = jnp.zeros((tm, NOPE), F32)
        for h in range(HEADS):
            b = h * HQ
            dqp_ref[:, b:b + NOPE] = dq_ref[:, b:b + NOPE].astype(BF16)
            dqp_ref[:, b + NOPE:b + HQ] = unrotate(dq_ref[:, b + NOPE:b + HQ]).astype(BF16)
            dkv_ref[:, b:b + NOPE] = dk_ref[:, b:b + NOPE].astype(BF16)
            dkv_ref[:, b + NOPE:b + HQ] = dv_ref[:, h * VDIM:(h + 1) * VDIM].astype(BF16)
            dkrot = dkrot + dk_ref[:, b + NOPE:b + HQ]
        dqn = lax.dot_general(dqp_ref[...], wq_ref[...], NT, preferred_element_type=F32)
        dkvn = lax.dot_general(dkv_ref[...], wkv_ref[...], NT, preferred_element_type=F32)
        dqg_ref[...] += jnp.sum(dqn * qhat, axis=0, keepdims=True)
        dkvg_ref[...] += jnp.sum(dkvn * kvhat, axis=0, keepdims=True)
        dqh, dkvh = dqn * qg, dkvn * kvg
        dql = q_rstd * (dqh - qhat * jnp.mean(dqh * qhat, axis=-1, keepdims=True))
        dkvl = kv_rstd * (dkvh - kvhat * jnp.mean(dkvh * kvhat, axis=-1, keepdims=True))
        dz_ref[:, 0:Q_RANK] = dql.astype(BF16)
        dz_ref[:, Q_RANK:Q_RANK + KV_RANK] = dkvl.astype(BF16)
        dz_ref[:, Q_RANK + KV_RANK:] = unrotate(dkrot).astype(BF16)

    W = HEADS * HQ
    return pl.pallas_call(
        body, name=name, grid=(S // tm,),
        in_specs=[_rows(tm, 512, 0), _rows(tm, 1), _rows(tm, W), _rows(tm, W), _rows(tm, ATT_W),
                  _full((1, Q_RANK)), _full((1, KV_RANK)), _full((Q_RANK, W)), _full((KV_RANK, W)), _full((8, 128))],
        out_specs=(_rows(tm, 512), _rows(tm, Q_RANK), _rows(tm, W), _rows(tm, KV_RANK), _rows(tm, W),
                   _full((1, Q_RANK)), _full((1, KV_RANK))),
        out_shape=(_sds((S, 512), BF16), _sds((S, Q_RANK), BF16), _sds((S, W), BF16), _sds((S, KV_RANK), BF16),
                   _sds((S, W), BF16), _sds((1, Q_RANK)), _sds((1, KV_RANK))),
        compiler_params=_cparams("arbitrary"),
    )(z, pos, dq, dk, dv, q_g, kv_g, wq_p, w_ukv, rope_rows)


def _conv_norm_backward(dyc, cv, ln_g, ln_b, w_pw2, name):
    S = cv.shape[0]
    tm = ROW_TILE

    def body(dy_ref, cv_ref, g_ref, be_ref, pw_ref, dcv_ref, sl_ref, dyb_ref, dg_ref, db_ref, dcb_ref):
        @pl.when(pl.program_id(0) == 0)
        def _():
            dg_ref[...] = jnp.zeros(dg_ref.shape, F32)
            db_ref[...] = jnp.zeros(db_ref.shape, F32)
            dcb_ref[...] = jnp.zeros(dcb_ref.shape, F32)

        cv_v = cv_ref[...]
        mu = jnp.mean(cv_v, axis=-1, keepdims=True)
        cc = cv_v - mu
        rstd = lax.rsqrt(jnp.mean(cc * cc, axis=-1, keepdims=True) + EPS)
        nh = cc * rstd
        g = g_ref[...]
        n = nh * g + be_ref[...]
        sil, dsil = _silu_and_grad(n)
        sl_ref[...] = sil.astype(BF16)
        dyb = dy_ref[...].astype(BF16)
        dyb_ref[...] = dyb
        dn = lax.dot_general(dyb, pw_ref[...], NT, preferred_element_type=F32) * dsil
        db_ref[...] += jnp.sum(dn, axis=0, keepdims=True)
        dg_ref[...] += jnp.sum(dn * nh, axis=0, keepdims=True)
        dnh = dn * g
        dcv = rstd * (dnh - jnp.mean(dnh, axis=-1, keepdims=True) - nh * jnp.mean(dnh * nh, axis=-1, keepdims=True))
        dcv_ref[...] = dcv
        dcb_ref[...] += jnp.sum(dcv, axis=0, keepdims=True)

    vec = _full((1, CONV_W))
    return pl.pallas_call(
        body, name=name, grid=(S // tm,),
        in_specs=[_rows(tm, CONV_W), _rows(tm, CONV_W), vec, vec, _full((CONV_W, CONV_W))],
        out_specs=(_rows(tm, CONV_W), _rows(tm, CONV_W), _rows(tm, CONV_W), vec, vec, vec),
        out_shape=(_sds((S, CONV_W)), _sds((S, CONV_W), BF16), _sds((S, CONV_W), BF16),
                   _sds((1, CONV_W)), _sds((1, CONV_W)), _sds((1, CONV_W))),
        compiler_params=_cparams("arbitrary"),
    )(dyc, cv, ln_g, ln_b, w_pw2)


def _conv_backward(z, dcv, conv_w_p, name):
    S = z.shape[0]
    tm = ROW_TILE
    per = tm // HALO
    last_halo = S // HALO - 1

    def body(a_ref, b_ref, ap_ref, bp_ref, d_ref, dn_ref, w_ref, da_ref, db_ref, dw_ref, win, dwin, dw_acc, sh, dsh):
        i = pl.program_id(0)

        @pl.when(i == 0)
        def _():
            dw_acc[...] = jnp.zeros(dw_acc.shape, F32)

        av, bv = a_ref[...].astype(F32), b_ref[...].astype(F32)
        _conv_window(win, ap_ref[...].astype(F32), bp_ref[...].astype(F32), av, bv, i == 0)
        dcur = d_ref[...]
        dwin[0:tm, :] = dcur
        dwin[tm:, :] = jnp.where(i == pl.num_programs(0) - 1, 0.0, dn_ref[...])
        _shifted_copies(win, sh, tm)
        _shifted_copies(dwin, dsh, tm)
        for r0 in range(0, tm, CONV_ROWS):
            dchunk = d_ref[r0:r0 + CONV_ROWS, :]
            dh = jnp.zeros((CONV_ROWS, CONV_W), F32)
            for kk in range(CONV_K):
                dh = dh + w_ref[kk:kk + 1, :] * _tap(dwin, dsh, r0 + CONV_K - 1 - kk, CONV_ROWS)
                prod = dchunk * _tap(win, sh, r0 + HALO - (CONV_K - 1) + kk, CONV_ROWS)
                dw_acc[kk] += jnp.sum(prod.reshape(CONV_ROWS // 8, 8, CONV_W), axis=0)
            sb = _sigmoid(b_ref[r0:r0 + CONV_ROWS, :].astype(F32))
            da_ref[r0:r0 + CONV_ROWS, :] = (dh * sb).astype(BF16)
            db_ref[r0:r0 + CONV_ROWS, :] = (dh * a_ref[r0:r0 + CONV_ROWS, :].astype(F32) * sb * (1.0 - sb)).astype(BF16)

        @pl.when(i == pl.num_programs(0) - 1)
        def _():
            dw_ref[...] = jnp.sum(dw_acc[...], axis=1)

    nxt = pl.BlockSpec((HALO, CONV_W), lambda i: (jnp.minimum((i + 1) * per, last_halo), 0))
    return pl.pallas_call(
        body, name=name, grid=(S // tm,),
        in_specs=_conv_in_specs(tm) + [_rows(tm, CONV_W), nxt, _full((HALO, CONV_W))],
        out_specs=(_rows(tm, CONV_W), _rows(tm, CONV_W), _full((HALO, CONV_W))),
        out_shape=(_sds((S, CONV_W), BF16), _sds((S, CONV_W), BF16), _sds((HALO, CONV_W))),
        scratch_shapes=[pltpu.VMEM((tm + HALO, CONV_W), F32), pltpu.VMEM((tm + HALO, CONV_W), F32),
                        pltpu.VMEM((HALO, 8, CONV_W), F32), pltpu.VMEM((7, tm + HALO - 8, CONV_W), F32),
                        pltpu.VMEM((7, tm + HALO - 8, CONV_W), F32)],
        compiler_params=_cparams("arbitrary"),
    )(z, z, z, z, dcv, dcv, conv_w_p)


def _sgu_backward(z, dy, ln_g, ln_b, w_s, bias_full, name):
    S = z.shape[0]
    tm = ROW_TILE

    def body(u_ref, v_ref, dy_ref, g_ref, be_ref, ws_ref, bias_ref, du_ref, dv_ref, dws_ref, dbs_ref, dg_ref, db_ref):
        @pl.when(pl.program_id(0) == 0)
        def _():
            dws_ref[...] = jnp.zeros(dws_ref.shape, F32)
            dbs_ref[...] = jnp.zeros(dbs_ref.shape, F32)
            dg_ref[...] = jnp.zeros(dg_ref.shape, F32)
            db_ref[...] = jnp.zeros(db_ref.shape, F32)

        gmask, tril = _sgu_masks()
        lane = lax.broadcasted_iota(jnp.int32, (1, 128), 1)
        wm = [jnp.where(tril, ws_ref[g], 0.0).astype(BF16) for g in range(SGU_G)]
        gain = g_ref[...]
        for ch in range(tm // SGU_T):
            rows = slice(ch * SGU_T, (ch + 1) * SGU_T)
            gu, dgu, dgv, rstd, nh, vn = _sgu_common(u_ref[rows, :].astype(F32), v_ref[rows, :].astype(F32), g_ref, be_ref)
            vb = vn.astype(BF16)
            sv = bias_ref[...]
            for g in range(SGU_G):
                sv = sv + jnp.where(gmask[g], jnp.dot(wm[g], vb, preferred_element_type=F32), 0.0)
            dyv = dy_ref[rows, :]
            du_ref[rows, :] = (dyv * sv * dgu).astype(BF16)
            dsv = dyv * gu
            dsvb = dsv.astype(BF16)
            dvn = jnp.zeros((SGU_T, SGU_W), F32)
            for g in range(SGU_G):
                dsg = jnp.where(gmask[g], dsv, 0.0)
                dwg = lax.dot_general(dsg.astype(BF16), vb, NT, preferred_element_type=F32)
                dws_ref[g] += jnp.where(tril, dwg, 0.0)
                dvn = dvn + jnp.where(gmask[g], lax.dot_general(wm[g], dsvb, TN, preferred_element_type=F32), 0.0)
                dbs_ref[...] += jnp.where(lane == g, jnp.sum(dsg, axis=-1, keepdims=True), 0.0)
            db_ref[...] += jnp.sum(dvn, axis=0, keepdims=True)
            dg_ref[...] += jnp.sum(dvn * nh, axis=0, keepdims=True)
            dnh = dvn * gain
            dgvv = rstd * (dnh - jnp.mean(dnh, axis=-1, keepdims=True) - nh * jnp.mean(dnh * nh, axis=-1, keepdims=True))
            dv_ref[rows, :] = (dgvv * dgv).astype(BF16)

    vec = _full((1, SGU_W))
    return pl.pallas_call(
        body, name=name, grid=(S // tm,),
        in_specs=[_rows(tm, SGU_W, 7), _rows(tm, SGU_W, 8), _rows(tm, SGU_W), vec, vec,
                  _full((SGU_G, SGU_T, SGU_T)), _full((SGU_T, SGU_W))],
        out_specs=(_rows(tm, SGU_W), _rows(tm, SGU_W), _full((SGU_G, SGU_T, SGU_T)), _full((SGU_T, 128)), vec, vec),
        out_shape=(_sds((S, SGU_W), BF16), _sds((S, SGU_W), BF16), _sds((SGU_G, SGU_T, SGU_T)), _sds((SGU_T, 128)),
                   _sds((1, SGU_W)), _sds((1, SGU_W))),
        compiler_params=_cparams("arbitrary"),
    )(z, z, dy, ln_g, ln_b, w_s, bias_full)


DZ_PIECES = (512, 512, 256, 256, 256, 256, 256, 256)


def _assemble_columns(dst_ref, pieces):
    off = 0
    for piece in pieces:
        wdt = piece.shape[1]
        dst_ref[:, off:off + wdt] = piece[...]
        off += wdt


def _inproj_backward(x, dxo, dz_pieces, g_pre, mod, w_in_p, name, comm=None):
    S = x.shape[0]
    tm = ROW_TILE
    comm = comm or _Comm([], [])
    nc = comm.n
    npc = len(DZ_PIECES)

    def body(*refs):
        x_ref, dxo_ref = refs[:2]
        pieces = refs[2:2 + npc]
        g_ref, mod_ref, w_ref = refs[2 + npc:5 + npc]
        c_src = refs[5 + npc:5 + npc + nc]
        dx_ref, dmod_ref, dg_ref = refs[5 + npc + nc:8 + npc + nc]
        c_out = refs[8 + npc + nc:8 + npc + 2 * nc]
        dz_sc = refs[8 + npc + 2 * nc]
        c_sems = refs[9 + npc + 2 * nc:]
        _riding(comm, S // tm, c_src, c_out, c_sems, "start")

        @pl.when(pl.program_id(0) == 0)
        def _():
            dmod_ref[...] = jnp.zeros(dmod_ref.shape, F32)
            dg_ref[...] = jnp.zeros(dg_ref.shape, F32)

        _assemble_columns(dz_sc, pieces)
        g = g_ref[...]
        one_scale = 1.0 + mod_ref[1:2, :]
        half = tm // 2
        for r0 in (0, half):
            rows = slice(r0, r0 + half)
            dh = lax.dot_general(dz_sc[rows, :], w_ref[...], NT, preferred_element_type=F32)
            xv = x_ref[rows, :]
            rstd = lax.rsqrt(jnp.mean(xv * xv, axis=-1, keepdims=True) + EPS)
            xhat = xv * rstd
            xg = xhat * g
            dmod_ref[0:1, :] += jnp.sum(dh, axis=0, keepdims=True)
            dmod_ref[1:2, :] += jnp.sum(dh * xg, axis=0, keepdims=True)
            dhs = dh * one_scale
            dg_ref[...] += jnp.sum(dhs * xhat, axis=0, keepdims=True)
            dxh = dhs * g
            dx_ref[rows, :] = dxo_ref[rows, :] + rstd * (dxh - xhat * jnp.mean(dxh * xhat, axis=-1, keepdims=True))
        _riding(comm, S // tm, c_src, c_out, c_sems, "finish")

    outs = pl.pallas_call(
        body, name=name, grid=(S // tm,),
        in_specs=[_rows(tm, D), _rows(tm, D)] + [_rows(tm, w) for w in DZ_PIECES]
                 + [_full((1, D)), _full((3, D)), _full((D, DZ))] + comm.specs,
        out_specs=[_rows(tm, D), _full((2, D)), _full((1, D))] + comm.specs,
        out_shape=[_sds((S, D)), _sds((2, D)), _sds((1, D))] + comm.out_shape,
        scratch_shapes=[pltpu.VMEM((tm, DZ), BF16)] + comm.scratch,
        compiler_params=_cparams("arbitrary"),
    )(x, dxo, *dz_pieces, g_pre, mod, w_in_p, *comm.srcs)
    return outs[0], outs[1], outs[2], outs[3:]


def _adamw(w, gparts, m, v, name):
    shape = w.shape
    cols = shape[-1]
    rows = int(np.prod(shape[:-1]))
    parts = gparts.shape[0]
    w2, m2, v2 = (a.reshape(rows, cols) for a in (w, m, v))
    g3 = gparts.reshape(parts, rows, cols)
    tr = rows
    for cand in (256, 128):
        if rows > cand and rows % cand == 0:
            tr = cand
            break

    def body(w_ref, g_ref, m_ref, v_ref, go_ref, d_ref, mo_ref, vo_ref):
        g = g_ref[0].astype(F32)
        for p in range(1, parts):
            g = g + g_ref[p].astype(F32)
        wv = w_ref[...]
        mn = ADAM_B1 * m_ref[...] + (1.0 - ADAM_B1) * g
        vn = ADAM_B2 * v_ref[...] + (1.0 - ADAM_B2) * (g * g)
        m_hat = mn / (1.0 - ADAM_B1 ** ADAM_STEP)
        v_hat = vn / (1.0 - ADAM_B2 ** ADAM_STEP)
        go_ref[...] = g
        d_ref[...] = -ADAM_LR * (m_hat / (jnp.sqrt(v_hat) + ADAM_EPS) + ADAM_WD * wv)
        mo_ref[...] = mn
        vo_ref[...] = vn

    blk = pl.BlockSpec((tr, cols), lambda i: (i, 0))
    outs = pl.pallas_call(
        body, name=name, grid=(rows // tr,),
        in_specs=[blk, pl.BlockSpec((parts, tr, cols), lambda i: (0, i, 0)), blk, blk],
        out_specs=(blk, blk, blk, blk),
        out_shape=tuple(_sds((rows, cols)) for _ in range(4)),
        compiler_params=_cparams("parallel"),
    )(w2, g3, m2, v2)
    return tuple(o.reshape(shape) for o in outs)


_GATHERED = ("w_in", "w_out", "w_uq", "w_ukv", "w_pw2", "conv_w")
_COL_SHARDED = ("w_in", "w_uq", "w_ukv", "conv_w")

_SMALL = (("dmod", (3 * D,)), ("g_pre", (D,)), ("g_post", (D,)), ("q_norm_g", (Q_RANK,)),
          ("kv_norm_g", (KV_RANK,)), ("conv_b", (CONV_W,)), ("conv_ln_g", (CONV_W,)),
          ("conv_ln_b", (CONV_W,)), ("sgu_ln_g", (SGU_W,)), ("sgu_ln_b", (SGU_W,)),
          ("w_s", (SGU_G, SGU_T, SGU_T)), ("b_s", (SGU_G, SGU_T)))


def _assemble(name, parts):
    if name in _COL_SHARDED:
        p = jnp.moveaxis(parts, 0, 1)
        return p.reshape(p.shape[0], p.shape[1] * p.shape[2])
    return parts.reshape(parts.shape[0] * parts.shape[1], parts.shape[2])


def _scatter_layout(name, full):
    if name in _COL_SHARDED:
        return jnp.moveaxis(full.reshape(full.shape[0], N_DEV, full.shape[1] // N_DEV), 1, 0)
    return full.reshape(N_DEV, full.shape[0] // N_DEV, full.shape[1])


def kernel(x, c, positions, w_ada, b_ada, g_pre, g_post, w_in, q_norm_g, w_uq, kv_norm_g, w_ukv, conv_w, conv_b, conv_ln_g, conv_ln_b, w_pw2, sgu_ln_g, sgu_ln_b, w_s, b_s, w_out, loss_target, m_w_ada, m_b_ada, m_g_pre, m_g_post, m_w_in, m_q_norm_g, m_w_uq, m_kv_norm_g, m_w_ukv, m_conv_w, m_conv_b, m_conv_ln_g, m_conv_ln_b, m_w_pw2, m_sgu_ln_g, m_sgu_ln_b, m_w_s, m_b_s, m_w_out, v_w_ada, v_b_ada, v_g_pre, v_g_post, v_w_in, v_q_norm_g, v_w_uq, v_kv_norm_g, v_w_ukv, v_conv_w, v_conv_b, v_conv_ln_g, v_conv_ln_b, v_w_pw2, v_sgu_ln_g, v_sgu_ln_b, v_w_s, v_b_s, v_w_out):
    weights = dict(w_ada=w_ada, b_ada=b_ada, g_pre=g_pre, g_post=g_post, w_in=w_in, q_norm_g=q_norm_g, w_uq=w_uq,
                   kv_norm_g=kv_norm_g, w_ukv=w_ukv, conv_w=conv_w, conv_b=conv_b, conv_ln_g=conv_ln_g,
                   conv_ln_b=conv_ln_b, w_pw2=w_pw2, sgu_ln_g=sgu_ln_g, sgu_ln_b=sgu_ln_b, w_s=w_s, b_s=b_s, w_out=w_out)
    m_in = dict(w_ada=m_w_ada, b_ada=m_b_ada, g_pre=m_g_pre, g_post=m_g_post, w_in=m_w_in, q_norm_g=m_q_norm_g,
                w_uq=m_w_uq, kv_norm_g=m_kv_norm_g, w_ukv=m_w_ukv, conv_w=m_conv_w, conv_b=m_conv_b,
                conv_ln_g=m_conv_ln_g, conv_ln_b=m_conv_ln_b, w_pw2=m_w_pw2, sgu_ln_g=m_sgu_ln_g,
                sgu_ln_b=m_sgu_ln_b, w_s=m_w_s, b_s=m_b_s, w_out=m_w_out)
    v_in = dict(w_ada=v_w_ada, b_ada=v_b_ada, g_pre=v_g_pre, g_post=v_g_post, w_in=v_w_in, q_norm_g=v_q_norm_g,
                w_uq=v_w_uq, kv_norm_g=v_kv_norm_g, w_ukv=v_w_ukv, conv_w=v_conv_w, conv_b=v_conv_b,
                conv_ln_g=v_conv_ln_g, conv_ln_b=v_conv_ln_b, w_pw2=v_w_pw2, sgu_ln_g=v_sgu_ln_g,
                sgu_ln_b=v_sgu_ln_b, w_s=v_w_s, b_s=v_b_s, w_out=v_w_out)
    order = list(weights)

    S = x.shape[1]
    me = 4 * lax.axis_index("x") + 2 * lax.axis_index("y") + lax.axis_index("c")
    x0 = x.reshape(S, D)
    target = loss_target.reshape(S, D)
    pos = positions.reshape(S, 1)

    def shards(l):
        return [weights[n][l].astype(BF16) for n in _GATHERED]

    def w_in_operand(part):
        w_in_f = _assemble("w_in", part)
        return jnp.concatenate([w_in_f[:, :ATT_IN], jnp.zeros((D, PAD_IN), BF16), w_in_f[:, ATT_IN:]], axis=1)

    def other_operands(parts):
        full = {n: _assemble(n, p) for n, p in zip(_GATHERED[1:], parts)}
        wq = jnp.pad(full["w_uq"].reshape(Q_RANK, HEADS, QK), ((0, 0), (0, 0), (0, HQ - QK)))
        return dict(w_uq=wq.reshape(Q_RANK, HEADS * HQ), w_ukv=full["w_ukv"], w_pw2=full["w_pw2"], w_out=full["w_out"],
                    conv_w=jnp.pad(full["conv_w"].astype(F32), ((0, HALO - CONV_K), (0, 0))))

    first_w_in, c_parts = _gather_two_level([shards(0)[0], c.reshape(8, D // 8)], name="gather_w_in_0")
    lw = [dict(w_in=w_in_operand(first_w_in))] + [None] * (DEPTH - 1)
    c_all = c_parts.reshape(N_DEV, D)

    ada_cols = w_ada.shape[-1]
    b_cols = lax.dynamic_slice_in_dim(b_ada, me * ada_cols, ada_cols, axis=1)
    sc_rows, mod_part = _ada_forward(jnp.pad(c_all, ((0, 8), (0, 0))), w_ada, b_cols)
    mod_recv = _exchange([jnp.moveaxis(mod_part[:, :N_DEV], 1, 0)], [True], name="exchange_mod")[0]
    mod = jnp.moveaxis(mod_recv, 0, 1).reshape(DEPTH, 3, D)

    bias_full = jnp.repeat(jnp.swapaxes(b_s, 1, 2), SGU_GD, axis=2)
    inv_freq = ROPE_THETA ** (-jnp.arange(0, ROPE, 2, dtype=F32) / ROPE)
    zeros32 = jnp.zeros((ROPE // 2,), F32)
    ones32 = jnp.ones((ROPE // 2,), F32)
    rope_rows = jnp.zeros((8, 128), F32)
    rope_rows = rope_rows.at[0].set(jnp.concatenate([inv_freq, inv_freq, zeros32, zeros32]))
    rope_rows = rope_rows.at[1].set(jnp.concatenate([ones32, ones32, zeros32, zeros32]))
    rope_rows = rope_rows.at[2].set(jnp.concatenate([-ones32, ones32, zeros32, zeros32]))

    def vec(a, l):
        return a[l].reshape(1, -1)

    saved = []
    xl = x0
    for l in range(DEPTH):
        w = lw[l]
        late = _Comm(shards(0)[1:], [False] * (len(_GATHERED) - 1)) if l == 0 else None
        z, h_b, arrived = _prenorm_inproj(xl, vec(g_pre, l), mod[l], w["w_in"], name=f"prenorm_inproj_{l}", comm=late)
        if late is not None:
            w.update(other_operands(arrived))
        q, k, v = _att_prep(z, pos, vec(q_norm_g, l), vec(kv_norm_g, l), w["w_uq"], w["w_ukv"], rope_rows,
                            name=f"att_prep_{l}")
        ahead = _Comm(shards(l + 1), [False] * len(_GATHERED)) if l + 1 < DEPTH else None
        y_att, lse, arrived = _flash_forward(q, k, v, name=f"flash_forward_{l}", comm=ahead)
        if ahead is not None:
            lw[l + 1] = dict(w_in=w_in_operand(arrived[0]), **other_operands(arrived[1:]))
        cv, y_conv = _conv_forward(z, w["conv_w"], vec(conv_b, l), vec(conv_ln_g, l), vec(conv_ln_b, l),
                                   w["w_pw2"], name=f"conv_forward_{l}")
        y_sgu = _sgu_forward(z, vec(sgu_ln_g, l), vec(sgu_ln_b, l), w_s[l], bias_full[l], name=f"sgu_forward_{l}")
        outs = _out_proj(xl, z, y_att, y_conv, y_sgu, w["w_out"], vec(g_post, l), mod[l], name=f"out_proj_{l}",
                         target=target if l == DEPTH - 1 else None)
        saved.append(dict(x=xl, h=h_b, z=z, q=q, k=k, v=v, y_att=y_att, lse=lse, cv=cv, y_conv=y_conv, y_sgu=y_sgu,
                          y=outs[1], ycat=outs[2]))
        xl = outs[0]

    dx = xl
    loss = lax.psum(outs[3].reshape(()), ("x", "y", "c"))

    spack = _Packer(_SMALL, 8)
    grad_kinds = [True] * len(_GATHERED) + [False]
    received = [None] * DEPTH
    pending = None
    for l in reversed(range(DEPTH)):
        sv = saved[l]
        w = lw[l]
        (dyb, dob, stats, dga, dyc, dgc, dys, dgs, dgate, dgpost) = _out_proj_backward(
            dx, sv["y"], sv["z"], sv["y_att"], sv["y_conv"], sv["y_sgu"], sv["lse"], w["w_out"],
            vec(g_post, l), mod[l], name=f"out_proj_backward_{l}")
        big = dict(w_out=_matmul_tn(sv["ycat"], dyb, name=f"grad_w_out_{l}"))
        riding = _Comm(pending, grad_kinds) if pending is not None else None
        dq, dk, dv, arrived = _flash_backward(sv["q"], sv["k"], sv["v"], dob, stats, name=f"flash_backward_{l}",
                                              comm=riding)
        if riding is not None:
            received[l + 1] = arrived
        dz_att, qn_b, dqp_b, kvn_b, dkv_b, dqg, dkvg = _att_prep_backward(
            sv["z"], pos, dq, dk, dv, vec(q_norm_g, l), vec(kv_norm_g, l), w["w_uq"], w["w_ukv"], rope_rows,
            name=f"att_prep_backward_{l}")
        dwq_p = _matmul_tn(qn_b, dqp_b, name=f"grad_w_uq_{l}")
        big["w_uq"] = dwq_p.reshape(Q_RANK, HEADS, HQ)[:, :, :QK].reshape(Q_RANK, HEADS * QK)
        big["w_ukv"] = _matmul_tn(kvn_b, dkv_b, name=f"grad_w_ukv_{l}")
        dcv, sl_b, dyc_b, dclg, dclb, dcb = _conv_norm_backward(dyc, sv["cv"], vec(conv_ln_g, l), vec(conv_ln_b, l),
                                                              w["w_pw2"], name=f"conv_norm_backward_{l}")
        big["w_pw2"] = _matmul_tn(sl_b, dyc_b, name=f"grad_w_pw2_{l}")
        dca, dcbb, dconvw = _conv_backward(sv["z"], dcv, w["conv_w"], name=f"conv_backward_{l}")
        big["conv_w"] = dconvw[:CONV_K]
        dsu, dsvv, dws, dbs, dslg, dslb = _sgu_backward(sv["z"], dys, vec(sgu_ln_g, l), vec(sgu_ln_b, l), w_s[l],
                                                       bias_full[l], name=f"sgu_backward_{l}")
        dz_pieces = [dz_att, dga, dca, dcbb, dgc, dsu, dsvv, dgs]
        rest = [_scatter_layout(n, big[n]).astype(BF16) for n in _GATHERED[1:]]
        early = _Comm(rest, grad_kinds[1:-1]) if l == 0 else None
        dwin_p = _matmul_tn(sv["h"], dz_pieces, name=f"grad_w_in_{l}", comm=early)
        if early is not None:
            dwin_p, rest_arrived = dwin_p
        dwin = jnp.concatenate([dwin_p[:, :ATT_IN], dwin_p[:, ATT_IN + PAD_IN:]], axis=1)
        dwin_send = _scatter_layout("w_in", dwin).astype(BF16)
        last = _Comm([dwin_send], grad_kinds[:1]) if l == 0 else None
        dx, dmod2, dgpre, dwin_arrived = _inproj_backward(sv["x"], dx, dz_pieces, vec(g_pre, l), mod[l], w["w_in"],
                                                          name=f"inproj_backward_{l}", comm=last)
        small = dict(dmod=jnp.concatenate([dmod2.reshape(-1), dgate.reshape(-1)]), g_pre=dgpre, g_post=dgpost,
                     q_norm_g=dqg, kv_norm_g=dkvg, conv_b=dcb, conv_ln_g=dclg, conv_ln_b=dclb, sgu_ln_g=dslg,
                     sgu_ln_b=dslb, w_s=dws, b_s=jnp.swapaxes(dbs[:, :SGU_G], 0, 1))
        pending = [dwin_send] + rest + [spack.pack(small, F32)]
    grad_x = dx.reshape(1, S, D)
    small_arrived = _gather_two_level(pending[-1:], name="gather_small_grads_0")
    received[0] = list(dwin_arrived) + list(rest_arrived) + list(small_arrived)

    gparts = {n: jnp.stack([received[l][i] for l in range(DEPTH)], axis=1) for i, n in enumerate(_GATHERED)}
    sparts = [spack.unpack(received[l][-1], (N_DEV,)) for l in range(DEPTH)]
    sparts = {n: jnp.stack([sparts[l][n] for l in range(DEPTH)], axis=1) for n, _ in _SMALL}
    dmod_all = sparts["dmod"]
    dmod_cols = lax.dynamic_slice_in_dim(dmod_all, me * ada_cols, ada_cols, axis=2)
    sc_t = jnp.pad(sc_rows[:N_DEV].T, ((0, 0), (0, 128 - N_DEV)))
    dmod_rows = jnp.pad(jnp.moveaxis(dmod_cols, 0, 1), ((0, 0), (0, 128 - N_DEV), (0, 0)))
    gparts["w_ada"] = _ada_backward(sc_t, dmod_rows)[None]
    gparts["b_ada"] = dmod_all
    for n, _ in _SMALL[1:]:
        gparts[n] = sparts[n]

    grads, deltas, new_m, new_v = {}, {}, {}, {}
    for n in order:
        grads[n], deltas[n], new_m[n], new_v[n] = _adamw(weights[n], gparts[n], m_in[n], v_in[n], name=f"adamw_{n}")
    return (loss, grad_x, *[grads[n] for n in order], *[deltas[n] for n in order],
            *[new_m[n] for n in order], *[new_v[n] for n in order])
```

```python
import functools
import math

import numpy as np
import jax
import jax.numpy as jnp
from jax import lax
from jax.experimental import pallas as pl
from jax.experimental.pallas import tpu as pltpu

F32 = jnp.float32
BF16 = jnp.bfloat16

N_DEV = 8
DEPTH = 2
D = 1024
HEADS = 4
NOPE = 128
ROPE = 64
VDIM = 128
QK = NOPE + ROPE
Q_RANK = 256
KV_RANK = 128
ATT_W = HEADS * VDIM
CONV_W = 256
CONV_K = 31
SGU_W = 256
SGU_G = 4
SGU_GD = SGU_W // SGU_G
SGU_T = 128
D_IN = 2496
ATT_IN = Q_RANK + KV_RANK + ROPE
PAD_IN = 64
DZ = D_IN + PAD_IN
HQ = 2 * NOPE
EPS = 1e-6
ROPE_THETA = 10000.0
ATT_SCALE = QK ** -0.5
LOG2E = math.log2(math.e)
EXP2_SCALE = ATT_SCALE * LOG2E
NEG_INF = float("-inf")

ADAM_LR = 0.001
ADAM_B1 = 0.9
ADAM_B2 = 0.999
ADAM_EPS = 1e-08
ADAM_WD = 0.01
ADAM_STEP = 10

VMEM_LIMIT = 56 * 1024 * 1024
ROW_TILE = 512
MATMUL_TN_ROWS = 1024
ATT_TILE = 512
FLASH_UNROLL = 4
HALO = 32
CONV_ROWS = 64
PACK_LANES = 128

MESH = pl.DeviceIdType.MESH
NT = (((1,), (1,)), ((), ()))
TN = (((0,), (0,)), ((), ()))


def _cparams(*sem):
    return pltpu.CompilerParams(dimension_semantics=sem, vmem_limit_bytes=VMEM_LIMIT)


def _sds(shape, dtype=F32):
    return jax.ShapeDtypeStruct(tuple(shape), dtype)


def _rows(tm, width, col=0):
    return pl.BlockSpec((tm, width), lambda i: (i, col))


def _full(shape):
    nd = len(shape)
    return pl.BlockSpec(tuple(shape), lambda *_: (0,) * nd)


def _sigmoid(x):
    return 1.0 / (1.0 + jnp.exp(-x))


def _silu_and_grad(g):
    s = _sigmoid(g)
    return g * s, s * (1.0 + g * (1.0 - s))


def _gelu_and_grad(x):
    cdf = 0.5 * (1.0 + lax.erf(x * (1.0 / math.sqrt(2.0))))
    pdf = jnp.exp(-0.5 * x * x) * (1.0 / math.sqrt(2.0 * math.pi))
    return x * cdf, cdf + x * pdf


def _swap_halves(a):
    lane = lax.broadcasted_iota(jnp.int32, a.shape, 1)
    up = pltpu.roll(a, 32, 1)
    down = pltpu.roll(a, 96, 1)
    return jnp.where(lane < 32, down, jnp.where(lane < 64, up, 0.0))


def _rope_tables(pos_ref, rope_ref):
    ang = pos_ref[...].astype(F32) * rope_ref[0:1, :]
    return jnp.cos(ang) * rope_ref[1:2, :], jnp.sin(ang) * rope_ref[2:3, :]


class _Comm:
    def __init__(self, srcs, kinds):
        self.srcs = list(srcs)
        self.kinds = list(kinds)
        self.n = len(self.srcs)
        self.out_shape = [_sds((N_DEV,) + tuple(s.shape[1:] if k else s.shape), s.dtype)
                          for s, k in zip(self.srcs, self.kinds)]
        self.specs = [pl.BlockSpec(memory_space=pl.ANY)] * self.n
        self.scratch = [pltpu.SemaphoreType.DMA((self.n, N_DEV - 1)), pltpu.SemaphoreType.DMA((self.n, N_DEV - 1)),
                        pltpu.SemaphoreType.DMA((self.n,))] if self.n else []

    def _copies(self, src_refs, out_refs, sems, with_recvs):
        send_sems, recv_sems, local_sems = sems
        x, y, c = lax.axis_index("x"), lax.axis_index("y"), lax.axis_index("c")
        me = 4 * x + 2 * y + c
        local, sends, recvs = [], [], []
        for a in range(self.n):
            def block_for(dest, src_ref=src_refs[a], a2a=self.kinds[a]):
                return src_ref.at[dest] if a2a else src_ref

            local.append(pltpu.make_async_copy(block_for(me), out_refs[a].at[me], local_sems.at[a]))
            for r in range(1, N_DEV):
                px = 1 - x if (r >> 2) & 1 else x
                py = 1 - y if (r >> 1) & 1 else y
                pc = 1 - c if r & 1 else c
                peer = 4 * px + 2 * py + pc
                sends.append(pltpu.make_async_remote_copy(
                    src_ref=block_for(peer), dst_ref=out_refs[a].at[me],
                    send_sem=send_sems.at[a, r - 1], recv_sem=recv_sems.at[a, r - 1],
                    device_id=(px, py, pc), device_id_type=MESH))
                if with_recvs:
                    recvs.append(pltpu.make_async_remote_copy(
                        src_ref=block_for(me), dst_ref=out_refs[a].at[peer],
                        send_sem=send_sems.at[a, r - 1], recv_sem=recv_sems.at[a, r - 1],
                        device_id=(px, py, pc), device_id_type=MESH))
        return local, sends, recvs

    def start(self, src_refs, out_refs, sems):
        local, sends, _ = self._copies(src_refs, out_refs, sems, False)
        for cp in local + sends:
            cp.start()

    def finish(self, src_refs, out_refs, sems):
        local, sends, recvs = self._copies(src_refs, out_refs, sems, True)
        for cp in recvs:
            cp.wait_recv()
        for cp in sends:
            cp.wait_send()
        for cp in local:
            cp.wait()


def _exchange(srcs, kinds, name):
    comm = _Comm(srcs, kinds)
    n = comm.n

    def body(*refs):
        src_refs, out_refs, sems = refs[:n], refs[n:2 * n], refs[2 * n:]
        comm.start(src_refs, out_refs, sems)
        comm.finish(src_refs, out_refs, sems)

    return pl.pallas_call(
        body, name=name, out_shape=comm.out_shape, in_specs=comm.specs, out_specs=comm.specs,
        scratch_shapes=comm.scratch,
    )(*srcs)


def _gather_two_level(srcs, name):
    n = len(srcs)

    def body(*refs):
        src_refs, out_refs = refs[:n], refs[n:2 * n]
        send_sems, recv_sems, local_sems = refs[2 * n:]
        x, y, c = lax.axis_index("x"), lax.axis_index("y"), lax.axis_index("c")
        me, sibling = (x, y, c), (x, y, 1 - c)
        chips = [(1 - x, y), (x, 1 - y), (1 - x, 1 - y)]

        def block(a, dev):
            return out_refs[a].at[4 * dev[0] + 2 * dev[1] + dev[2]]

        def copy(a, k, owner, to, src=None):
            return pltpu.make_async_remote_copy(
                src_ref=block(a, owner) if src is None else src, dst_ref=block(a, owner),
                send_sem=send_sems.at[a, k], recv_sem=recv_sems.at[a, k], device_id=to, device_id_type=MESH)

        mine, first, passed = [], [], []
        for a in range(n):
            mine.append(pltpu.make_async_copy(src_refs[a], block(a, me), local_sems.at[a]))
            first.append(copy(a, 0, me, sibling, src=src_refs[a]))
            first += [copy(a, 1 + j, me, (*chip, c), src=src_refs[a]) for j, chip in enumerate(chips)]
        for cp in mine + first:
            cp.start()
        for a in range(n):
            for j, chip in enumerate(chips):
                copy(a, 1 + j, (*chip, c), me).wait_recv()
                fwd = copy(a, 4 + j, (*chip, c), sibling)
                fwd.start()
                passed.append(fwd)
        for a in range(n):
            copy(a, 0, sibling, me).wait_recv()
            for j, chip in enumerate(chips):
                copy(a, 4 + j, (*chip, 1 - c), me).wait_recv()
        for cp in first + passed:
            cp.wait_send()
        for cp in mine:
            cp.wait()

    return pl.pallas_call(
        body, name=name,
        out_shape=[_sds((N_DEV,) + tuple(s.shape), s.dtype) for s in srcs],
        in_specs=[pl.BlockSpec(memory_space=pl.ANY)] * n, out_specs=[pl.BlockSpec(memory_space=pl.ANY)] * n,
        scratch_shapes=[pltpu.SemaphoreType.DMA((n, N_DEV - 1)), pltpu.SemaphoreType.DMA((n, N_DEV - 1)),
                        pltpu.SemaphoreType.DMA((n,))],
    )(*srcs)


class _Packer:
    def __init__(self, entries, row_multiple):
        self.entries = entries
        self.offsets = {}
        off = 0
        for name, shape in entries:
            self.offsets[name] = off
            off += int(np.prod(shape))
        quantum = PACK_LANES * row_multiple
        self.total = -(-off // quantum) * quantum
        self.used = off
        self.rows = self.total // PACK_LANES

    def pack(self, arrays, dtype, lead=()):
        n = len(lead)
        flat = [arrays[name].astype(dtype).reshape(lead + (-1,)) for name, _ in self.entries]
        flat.append(jnp.zeros(lead + (self.total - self.used,), dtype))
        return jnp.concatenate(flat, axis=n).reshape(lead + (self.rows, PACK_LANES))

    def unpack(self, buf, lead=()):
        flat = buf.reshape(lead + (self.total,))
        out = {}
        for name, shape in self.entries:
            o = self.offsets[name]
            out[name] = lax.slice_in_dim(flat, o, o + int(np.prod(shape)), axis=len(lead)).reshape(lead + tuple(shape))
        return out


def _ada_forward(c_rows, w_ada, b_ada_cols):
    cols = w_ada.shape[-1]
    rows = c_rows.shape[0]

    def body(c_ref, w_ref, b_ref, sc_ref, part_ref):
        cv = c_ref[...]
        sc = cv * _sigmoid(cv)
        sc_ref[...] = sc
        scb = sc.astype(BF16)
        for l in range(DEPTH):
            part_ref[l] = jnp.dot(scb, w_ref[l].astype(BF16), preferred_element_type=F32) + b_ref[l:l + 1, :]

    return pl.pallas_call(
        body, name="ada_forward",
        out_shape=(_sds((rows, D)), _sds((DEPTH, rows, cols))),
        compiler_params=pltpu.CompilerParams(vmem_limit_bytes=VMEM_LIMIT),
    )(c_rows, w_ada, b_ada_cols)


def _ada_backward(sc_t, dmod_cols):
    cols = dmod_cols.shape[-1]

    def body(sc_ref, dm_ref, gw_ref):
        scb = sc_ref[...].astype(BF16)
        for l in range(DEPTH):
            gw_ref[l] = jnp.dot(scb, dm_ref[l].astype(BF16), preferred_element_type=F32)

    return pl.pallas_call(
        body, name="ada_backward",
        out_shape=_sds((DEPTH, D, cols)),
        compiler_params=pltpu.CompilerParams(vmem_limit_bytes=VMEM_LIMIT),
    )(sc_t, dmod_cols)


def _riding(comm, nsteps, c_src, c_out, c_sems, where):
    if not comm.n:
        return
    step = 0 if where == "start" else nsteps - 1

    @pl.when(pl.program_id(0) == step)
    def _():
        (comm.start if where == "start" else comm.finish)(c_src, c_out, c_sems)


def _prenorm_inproj(x, g_pre, mod, w_in_p, name, comm=None):
    S = x.shape[0]
    tm = ROW_TILE
    comm = comm or _Comm([], [])
    nc = comm.n

    def body(*refs):
        x_ref, g_ref, mod_ref, w_ref = refs[:4]
        c_src, z_ref, hb_ref = refs[4:4 + nc], refs[4 + nc], refs[5 + nc]
        c_out, c_sems = refs[6 + nc:6 + 2 * nc], refs[6 + 2 * nc:]
        _riding(comm, S // tm, c_src, c_out, c_sems, "start")
        xv = x_ref[...]
        rstd = lax.rsqrt(jnp.mean(xv * xv, axis=-1, keepdims=True) + EPS)
        hb = ((xv * rstd * g_ref[...]) * (1.0 + mod_ref[1:2, :]) + mod_ref[0:1, :]).astype(BF16)
        hb_ref[...] = hb
        z_ref[...] = jnp.dot(hb, w_ref[...], preferred_element_type=F32).astype(BF16)
        _riding(comm, S // tm, c_src, c_out, c_sems, "finish")

    outs = pl.pallas_call(
        body, name=name, grid=(S // tm,),
        in_specs=[_rows(tm, D), _full((1, D)), _full((3, D)), _full((D, DZ))] + comm.specs,
        out_specs=[_rows(tm, DZ), _rows(tm, D)] + comm.specs,
        out_shape=[_sds((S, DZ), BF16), _sds((S, D), BF16)] + comm.out_shape,
        scratch_shapes=comm.scratch,
        compiler_params=_cparams("arbitrary"),
    )(x, g_pre, mod, w_in_p, *comm.srcs)
    return outs[0], outs[1], outs[2:]


def _att_prep(z, pos, q_g, kv_g, wq_p, w_ukv, rope_rows, name):
    S = z.shape[0]
    tm = ROW_TILE

    def body(z_ref, pos_ref, qg_ref, kvg_ref, wq_ref, wkv_ref, rope_ref, q_ref, k_ref, v_ref):
        zz = z_ref[...].astype(F32)
        ql, kvl, ka = zz[:, 0:Q_RANK], zz[:, Q_RANK:Q_RANK + KV_RANK], zz[:, Q_RANK + KV_RANK:]
        qn = ql * lax.rsqrt(jnp.mean(ql * ql, axis=-1, keepdims=True) + EPS) * qg_ref[...]
        kvn = kvl * lax.rsqrt(jnp.mean(kvl * kvl, axis=-1, keepdims=True) + EPS) * kvg_ref[...]
        q = jnp.dot(qn.astype(BF16), wq_ref[...], preferred_element_type=F32)
        kv = jnp.dot(kvn.astype(BF16), wkv_ref[...], preferred_element_type=F32)
        ct, st = _rope_tables(pos_ref, rope_ref)
        krot = (ka * ct + _swap_halves(ka) * st).astype(BF16)
        for h in range(HEADS):
            b = h * HQ
            q_ref[:, b:b + NOPE] = q[:, b:b + NOPE].astype(BF16)
            a = q[:, b + NOPE:b + HQ]
            q_ref[:, b + NOPE:b + HQ] = (a * ct + _swap_halves(a) * st).astype(BF16)
            k_ref[:, b:b + NOPE] = kv[:, b:b + NOPE].astype(BF16)
            k_ref[:, b + NOPE:b + HQ] = krot
            v_ref[:, h * VDIM:(h + 1) * VDIM] = kv[:, b + NOPE:b + HQ].astype(BF16)

    return pl.pallas_call(
        body, name=name, grid=(S // tm,),
        in_specs=[_rows(tm, 512, 0), _rows(tm, 1), _full((1, Q_RANK)), _full((1, KV_RANK)),
                  _full((Q_RANK, HEADS * HQ)), _full((KV_RANK, HEADS * HQ)), _full((8, 128))],
        out_specs=(_rows(tm, HEADS * HQ), _rows(tm, HEADS * HQ), _rows(tm, ATT_W)),
        out_shape=(_sds((S, HEADS * HQ), BF16), _sds((S, HEADS * HQ), BF16), _sds((S, ATT_W), BF16)),
        compiler_params=_cparams("parallel"),
    )(z, pos, q_g, kv_g, wq_p, w_ukv, rope_rows)


def _flash_forward(q, k, v, name, comm=None):
    S = q.shape[0]
    t = ATT_TILE
    nq = S // t
    nl = t // 128
    comm = comm or _Comm([], [])
    nc = comm.n

    def body(*refs):
        q_ref, k_ref, v_ref = refs[:3]
        c_src = refs[3:3 + nc]
        o_ref, lse_ref = refs[3 + nc:5 + nc]
        c_out = refs[5 + nc:5 + 2 * nc]
        m_sc, l_sc, acc_sc, s_sc, mp_sc = refs[5 + 2 * nc:10 + 2 * nc]
        c_sems = refs[10 + 2 * nc:]
        if nc:
            @pl.when((pl.program_id(0) == 0) & (pl.program_id(1) == 0))
            def _():
                comm.start(c_src, c_out, c_sems)

        qb = pl.program_id(1)
        m_sc[...] = jnp.full(m_sc.shape, NEG_INF, F32)
        l_sc[...] = jnp.zeros(l_sc.shape, F32)
        acc_sc[...] = jnp.zeros(acc_sc.shape, F32)

        def score_phase(kb, slot, diagonal):
            s = lax.dot_general(q_ref[...], k_ref[pl.ds(pl.multiple_of(kb * t, t), t), :], NT,
                                preferred_element_type=F32)
            if diagonal:
                ri = lax.broadcasted_iota(jnp.int32, (t, t), 0)
                ci = lax.broadcasted_iota(jnp.int32, (t, t), 1)
                s = jnp.where(ci <= ri, s, NEG_INF)
            s_sc[slot] = s
            mp = s[:, 0:128]
            for c in range(1, nl):
                mp = jnp.maximum(mp, s[:, c * 128:(c + 1) * 128])
            mp_sc[slot] = mp

        def sum_phase(kb, slot):
            m_prev = m_sc[...]
            m_new = jnp.maximum(m_prev, jnp.max(mp_sc[slot], axis=-1, keepdims=True))
            alpha = jnp.exp2((m_prev - m_new) * EXP2_SCALE)
            p = jnp.exp2(s_sc[slot] * EXP2_SCALE - jnp.tile(m_new * EXP2_SCALE, (1, nl)))
            lp = alpha * l_sc[...]
            for c in range(nl):
                lp = lp + p[:, c * 128:(c + 1) * 128]
            l_sc[...] = lp
            acc_sc[...] = alpha * acc_sc[...] + jnp.dot(p.astype(BF16), v_ref[pl.ds(pl.multiple_of(kb * t, t), t), :],
                                                        preferred_element_type=F32)
            m_sc[...] = m_new

        def tile_at(pos):
            return jnp.where(pos == 0, qb, pos - 1)

        def run(p0, count, final):
            for i in range(count):
                sum_phase(tile_at(p0 + i), i % 2)
                if not (final and i == count - 1):
                    score_phase(p0 + i, (i + 1) % 2, False)

        score_phase(qb, 0, True)

        def trip(u, carry):
            run(FLASH_UNROLL * u, FLASH_UNROLL, False)
            return carry

        lax.fori_loop(0, qb // FLASH_UNROLL, trip, 0)
        for left in range(FLASH_UNROLL):
            @pl.when(qb % FLASH_UNROLL == left)
            def _():
                run(qb - left, left + 1, True)

        l = jnp.sum(l_sc[...], axis=-1, keepdims=True)
        o_ref[...] = acc_sc[...] / l
        lse_ref[0] = jnp.max(m_sc[...], axis=-1, keepdims=True) * ATT_SCALE + jnp.log(l)

        if nc:
            @pl.when((pl.program_id(0) == HEADS - 1) & (pl.program_id(1) == nq - 1))
            def _():
                comm.finish(c_src, c_out, c_sems)

    outs = pl.pallas_call(
        body, name=name, grid=(HEADS, nq),
        in_specs=[pl.BlockSpec((t, HQ), lambda h, i: (i, h)),
                  pl.BlockSpec((S, HQ), lambda h, i: (0, h)),
                  pl.BlockSpec((S, VDIM), lambda h, i: (0, h))] + comm.specs,
        out_specs=[pl.BlockSpec((t, VDIM), lambda h, i: (i, h)),
                   pl.BlockSpec((1, t, 1), lambda h, i: (h, i, 0))] + comm.specs,
        out_shape=[_sds((S, ATT_W)), _sds((HEADS, S, 1))] + comm.out_shape,
        scratch_shapes=[pltpu.VMEM((t, 128), F32), pltpu.VMEM((t, 128), F32), pltpu.VMEM((t, VDIM), F32),
                        pltpu.VMEM((2, t, t), F32), pltpu.VMEM((2, t, 128), F32)] + comm.scratch,
        compiler_params=_cparams("arbitrary", "arbitrary"),
    )(q, k, v, *comm.srcs)
    return outs[0], outs[1], outs[2:]


def _conv_window(win_ref, a_prev, b_prev, a_cur, b_cur, first):
    hp = a_prev * _sigmoid(b_prev)
    win_ref[0:HALO, :] = jnp.where(first, 0.0, hp)
    win_ref[HALO:, :] = a_cur * _sigmoid(b_cur)


def _shifted_copies(win_ref, sh_ref, tm):
    for b in range(1, 8):
        sh_ref[b - 1] = win_ref[pl.ds(b, tm + HALO - 8), :]


def _tap(win_ref, sh_ref, offset, tm):
    a, b = divmod(offset, 8)
    if b == 0:
        return win_ref[pl.ds(8 * a, tm), :]
    return sh_ref[b - 1, pl.ds(8 * a, tm), :]


def _conv_in_specs(tm):
    per = tm // HALO
    prev = lambda col: pl.BlockSpec((HALO, CONV_W), lambda i: (jnp.maximum(i * per - 1, 0), col))
    return [_rows(tm, CONV_W, 4), _rows(tm, CONV_W, 5), prev(4), prev(5)]


def _conv_forward(z, conv_w_p, conv_b, ln_g, ln_b, w_pw2, name):
    S = z.shape[0]
    tm = ROW_TILE

    def body(a_ref, b_ref, ap_ref, bp_ref, w_ref, cb_ref, g_ref, be_ref, pw_ref, cv_ref, y_ref, win, sh):
        _conv_window(win, ap_ref[...].astype(F32), bp_ref[...].astype(F32), a_ref[...].astype(F32),
                     b_ref[...].astype(F32), pl.program_id(0) == 0)
        _shifted_copies(win, sh, tm)
        for r0 in range(0, tm, CONV_ROWS):
            acc = jnp.zeros((CONV_ROWS, CONV_W), F32)
            for kk in range(CONV_K):
                acc = acc + w_ref[kk:kk + 1, :] * _tap(win, sh, r0 + HALO - (CONV_K - 1) + kk, CONV_ROWS)
            cv_ref[r0:r0 + CONV_ROWS, :] = acc + cb_ref[...]
        cv = cv_ref[...]
        mu = jnp.mean(cv, axis=-1, keepdims=True)
        cc = cv - mu
        rstd = lax.rsqrt(jnp.mean(cc * cc, axis=-1, keepdims=True) + EPS)
        n = cc * rstd * g_ref[...] + be_ref[...]
        sl = n * _sigmoid(n)
        y_ref[...] = jnp.dot(sl.astype(BF16), pw_ref[...], preferred_element_type=F32)

    return pl.pallas_call(
        body, name=name, grid=(S // tm,),
        in_specs=_conv_in_specs(tm) + [_full((HALO, CONV_W)), _full((1, CONV_W)), _full((1, CONV_W)),
                                       _full((1, CONV_W)), _full((CONV_W, CONV_W))],
        out_specs=(_rows(tm, CONV_W), _rows(tm, CONV_W)),
        out_shape=(_sds((S, CONV_W)), _sds((S, CONV_W))),
        scratch_shapes=[pltpu.VMEM((tm + HALO, CONV_W), F32), pltpu.VMEM((7, tm + HALO - 8, CONV_W), F32)],
        compiler_params=_cparams("parallel"),
    )(z, z, z, z, conv_w_p, conv_b, ln_g, ln_b, w_pw2)


def _sgu_common(u, v, g_ref, be_ref):
    gu, dgu = _gelu_and_grad(u)
    gv, dgv = _gelu_and_grad(v)
    mu = jnp.mean(gv, axis=-1, keepdims=True)
    cc = gv - mu
    rstd = lax.rsqrt(jnp.mean(cc * cc, axis=-1, keepdims=True) + EPS)
    nh = cc * rstd
    vn = nh * g_ref[...] + be_ref[...]
    return gu, dgu, dgv, rstd, nh, vn


def _sgu_masks():
    lane_group = lax.broadcasted_iota(jnp.int32, (1, SGU_W), 1) // SGU_GD
    ri = lax.broadcasted_iota(jnp.int32, (SGU_T, SGU_T), 0)
    ci = lax.broadcasted_iota(jnp.int32, (SGU_T, SGU_T), 1)
    return [lane_group == g for g in range(SGU_G)], ci <= ri


def _sgu_forward(z, ln_g, ln_b, w_s, bias_full, name):
    S = z.shape[0]
    tm = ROW_TILE

    def body(u_ref, v_ref, g_ref, be_ref, ws_ref, bias_ref, y_ref):
        gmask, tril = _sgu_masks()
        wm = [jnp.where(tril, ws_ref[g], 0.0).astype(BF16) for g in range(SGU_G)]
        for ch in range(tm // SGU_T):
            rows = slice(ch * SGU_T, (ch + 1) * SGU_T)
            gu, _, _, _, _, vn = _sgu_common(u_ref[rows, :].astype(F32), v_ref[rows, :].astype(F32), g_ref, be_ref)
            vb = vn.astype(BF16)
            sv = bias_ref[...]
            for g in range(SGU_G):
                sv = sv + jnp.where(gmask[g], jnp.dot(wm[g], vb, preferred_element_type=F32), 0.0)
            y_ref[rows, :] = gu * sv

    return pl.pallas_call(
        body, name=name, grid=(S // tm,),
        in_specs=[_rows(tm, SGU_W, 7), _rows(tm, SGU_W, 8), _full((1, SGU_W)), _full((1, SGU_W)),
                  _full((SGU_G, SGU_T, SGU_T)), _full((SGU_T, SGU_W))],
        out_specs=_rows(tm, SGU_W), out_shape=_sds((S, SGU_W)),
        compiler_params=_cparams("parallel"),
    )(z, z, ln_g, ln_b, w_s, bias_full)


def _out_proj(x, z, y_att, y_conv, y_sgu, w_out, g_post, mod, name, target=None):
    S = x.shape[0]
    tm = ROW_TILE
    head = target is not None

    def body(*refs):
        x_ref, ga_ref, gc_ref, gs_ref, ya_ref, yc_ref, ys_ref, w_ref, gp_ref, mod_ref = refs[:10]
        t_ref = refs[10] if head else None
        xn_ref, y_ref, cat_ref = refs[10 + head:13 + head]
        ca = (ya_ref[...] * _silu_and_grad(ga_ref[...].astype(F32))[0]).astype(BF16)
        cc = (yc_ref[...] * _silu_and_grad(gc_ref[...].astype(F32))[0]).astype(BF16)
        cs = (ys_ref[...] * _silu_and_grad(gs_ref[...].astype(F32))[0]).astype(BF16)
        cat_ref[:, 0:ATT_W] = ca
        cat_ref[:, ATT_W:ATT_W + CONV_W] = cc
        cat_ref[:, ATT_W + CONV_W:] = cs
        y = (jnp.dot(ca, w_ref[0:ATT_W, :], preferred_element_type=F32)
             + jnp.dot(cc, w_ref[ATT_W:ATT_W + CONV_W, :], preferred_element_type=F32)
             + jnp.dot(cs, w_ref[ATT_W + CONV_W:, :], preferred_element_type=F32))
        y_ref[...] = y
        rstd = lax.rsqrt(jnp.mean(y * y, axis=-1, keepdims=True) + EPS)
        xn = x_ref[...] + mod_ref[2:3, :] * (y * rstd * gp_ref[...])
        if not head:
            xn_ref[...] = xn
            return
        loss_ref = refs[14]

        @pl.when(pl.program_id(0) == 0)
        def _():
            loss_ref[...] = jnp.zeros(loss_ref.shape, F32)

        err = xn - t_ref[...]
        xn_ref[...] = err * (1.0 / D)
        row = jnp.sum(err * err, axis=-1, keepdims=True) * (1.0 / D)
        loss_ref[...] += 0.5 * jnp.sum(row, axis=0, keepdims=True)

    return pl.pallas_call(
        body, name=name, grid=(S // tm,),
        in_specs=[_rows(tm, D), _rows(tm, 512, 1), _rows(tm, 256, 6), _rows(tm, 256, 9),
                  _rows(tm, ATT_W), _rows(tm, CONV_W), _rows(tm, SGU_W),
                  _full((D, D)), _full((1, D)), _full((3, D))] + ([_rows(tm, D)] if head else []),
        out_specs=[_rows(tm, D), _rows(tm, D), _rows(tm, D)] + ([_full((1, 1))] if head else []),
        out_shape=[_sds((S, D)), _sds((S, D)), _sds((S, D), BF16)] + ([_sds((1, 1))] if head else []),
        compiler_params=_cparams("arbitrary" if head else "parallel"),
    )(x, z, z, z, y_att, y_conv, y_sgu, w_out, g_post, mod, *([target] if head else []))


def _matmul_tn(a, b, name, comm=None):
    S, M = a.shape
    pieces = list(b) if isinstance(b, (list, tuple)) else [b]
    nb = len(pieces)
    N = sum(p.shape[1] for p in pieces)
    bk = min(MATMUL_TN_ROWS, S)
    riding = comm is not None
    comm = comm or _Comm([], [])
    nc = comm.n

    def body(*refs):
        a_ref, b_refs = refs[0], refs[1:1 + nb]
        c_src, o_ref = refs[1 + nb:1 + nb + nc], refs[1 + nb + nc]
        c_out = refs[2 + nb + nc:2 + nb + 2 * nc]
        rest = refs[2 + nb + 2 * nc:]
        b_sc, c_sems = (rest[0], rest[1:]) if nb > 1 else (None, rest)
        _riding(comm, S // bk, c_src, c_out, c_sems, "start")

        @pl.when(pl.program_id(0) == 0)
        def _():
            o_ref[...] = jnp.zeros(o_ref.shape, F32)

        if nb > 1:
            _assemble_columns(b_sc, b_refs)
            bv = b_sc[...]
        else:
            bv = b_refs[0][...]
        o_ref[...] += lax.dot_general(a_ref[...], bv, TN, preferred_element_type=F32)
        _riding(comm, S // bk, c_src, c_out, c_sems, "finish")

    outs = pl.pallas_call(
        body, name=name, grid=(S // bk,),
        in_specs=[pl.BlockSpec((bk, M), lambda k: (k, 0))]
                 + [pl.BlockSpec((bk, p.shape[1]), lambda k: (k, 0)) for p in pieces] + comm.specs,
        out_specs=[_full((M, N))] + comm.specs, out_shape=[_sds((M, N))] + comm.out_shape,
        scratch_shapes=([pltpu.VMEM((bk, N), BF16)] if nb > 1 else []) + comm.scratch,
        compiler_params=_cparams("arbitrary"),
    )(a, *pieces, *comm.srcs)
    return (outs[0], outs[1:]) if riding else outs[0]


def _out_proj_backward(dxo, y, z, y_att, y_conv, y_sgu, lse, w_out, g_post, mod, name):
    S = dxo.shape[0]
    tm = ROW_TILE

    def body(dxo_ref, y_ref, ga_ref, gc_ref, gs_ref, ya_ref, yc_ref, ys_ref, lse_ref, w_ref, gp_ref, mod_ref,
             dyb_ref, dob_ref, st_ref, dga_ref, dyc_ref, dgc_ref, dys_ref, dgs_ref, dgate_ref, dgp_ref):
        @pl.when(pl.program_id(0) == 0)
        def _():
            dgate_ref[...] = jnp.zeros(dgate_ref.shape, F32)
            dgp_ref[...] = jnp.zeros(dgp_ref.shape, F32)

        dxo_v = dxo_ref[...]
        yv = y_ref[...]
        gp = gp_ref[...]
        rstd = lax.rsqrt(jnp.mean(yv * yv, axis=-1, keepdims=True) + EPS)
        yhat = yv * rstd
        dgate_ref[...] += jnp.sum(dxo_v * (yhat * gp), axis=0, keepdims=True)
        dr = dxo_v * mod_ref[2:3, :]
        dgp_ref[...] += jnp.sum(dr * yhat, axis=0, keepdims=True)
        dyh = dr * gp
        dy = rstd * (dyh - yhat * jnp.mean(dyh * yhat, axis=-1, keepdims=True))
        dyb = dy.astype(BF16)
        dyb_ref[...] = dyb
        dcat = lax.dot_general(dyb, w_ref[...], NT, preferred_element_type=F32)

        ya = ya_ref[...]
        sil, dsil = _silu_and_grad(ga_ref[...].astype(F32))
        da = dcat[:, 0:ATT_W]
        do = da * sil
        dob_ref[...] = do.astype(BF16)
        dga_ref[...] = (da * ya * dsil).astype(BF16)
        lane = lax.broadcasted_iota(jnp.int32, (1, 128), 1)
        stats = jnp.zeros((tm, 128), F32)
        for h in range(HEADS):
            cols = slice(h * VDIM, (h + 1) * VDIM)
            delta = jnp.sum(do[:, cols] * ya[:, cols], axis=-1, keepdims=True)
            stats = stats + jnp.where(lane == 2 * h, lse_ref[h], 0.0) + jnp.where(lane == 2 * h + 1, delta, 0.0)
        st_ref[...] = stats

        sil, dsil = _silu_and_grad(gc_ref[...].astype(F32))
        dc = dcat[:, ATT_W:ATT_W + CONV_W]
        dyc_ref[...] = dc * sil
        dgc_ref[...] = (dc * yc_ref[...] * dsil).astype(BF16)
        sil, dsil = _silu_and_grad(gs_ref[...].astype(F32))
        dsg = dcat[:, ATT_W + CONV_W:]
        dys_ref[...] = dsg * sil
        dgs_ref[...] = (dsg * ys_ref[...] * dsil).astype(BF16)

    return pl.pallas_call(
        body, name=name, grid=(S // tm,),
        in_specs=[_rows(tm, D), _rows(tm, D), _rows(tm, 512, 1), _rows(tm, 256, 6), _rows(tm, 256, 9),
                  _rows(tm, ATT_W), _rows(tm, CONV_W), _rows(tm, SGU_W),
                  pl.BlockSpec((HEADS, tm, 1), lambda i: (0, i, 0)),
                  _full((D, D)), _full((1, D)), _full((3, D))],
        out_specs=(_rows(tm, D), _rows(tm, ATT_W), _rows(tm, 128), _rows(tm, ATT_W),
                   _rows(tm, CONV_W), _rows(tm, CONV_W), _rows(tm, SGU_W), _rows(tm, SGU_W),
                   _full((1, D)), _full((1, D))),
        out_shape=(_sds((S, D), BF16), _sds((S, ATT_W), BF16), _sds((S, 128)), _sds((S, ATT_W), BF16),
                   _sds((S, CONV_W)), _sds((S, CONV_W), BF16), _sds((S, SGU_W)), _sds((S, SGU_W), BF16),
                   _sds((1, D)), _sds((1, D))),
        compiler_params=_cparams("arbitrary"),
    )(dxo, y, z, z, z, y_att, y_conv, y_sgu, lse, w_out, g_post, mod)


def _flash_backward(q, k, v, do, stats, name, comm=None):
    S = q.shape[0]
    t = ATT_TILE
    tk = 2 * t
    nq = S // t
    comm = comm or _Comm([], [])
    nc = comm.n

    def body(*refs):
        q_ref, do_ref, st_ref, k_ref, v_ref = refs[:5]
        c_src = refs[5:5 + nc]
        dq_ref, dk_ref, dv_ref = refs[5 + nc:8 + nc]
        c_out = refs[8 + nc:8 + 2 * nc]
        dk_sc, dv_sc = refs[8 + 2 * nc:10 + 2 * nc]
        c_sems = refs[10 + 2 * nc:]
        h = pl.program_id(0)
        j = pl.program_id(1)
        if nc:
            @pl.when((h == 0) & (j == 0))
            def _():
                comm.start(c_src, c_out, c_sems)

        @pl.when(j == 0)
        def _():
            dq_ref[...] = jnp.zeros(dq_ref.shape, F32)

        dk_sc[...] = jnp.zeros(dk_sc.shape, F32)
        dv_sc[...] = jnp.zeros(dv_sc.shape, F32)
        lane = lax.broadcasted_iota(jnp.int32, (1, 128), 1)

        def chain(hf, qv, dov, lse2, delta, diagonal):
            kt = k_ref[hf * t:(hf + 1) * t, :]
            s = lax.dot_general(qv, kt, NT, preferred_element_type=F32)
            p = jnp.exp2(s * EXP2_SCALE - lse2)
            if diagonal:
                ri = lax.broadcasted_iota(jnp.int32, (t, t), 0)
                ci = lax.broadcasted_iota(jnp.int32, (t, t), 1)
                p = jnp.where(ci <= ri, p, 0.0)
            dv_sc[hf] += lax.dot_general(p.astype(BF16), dov, TN, preferred_element_type=F32)
            dp = lax.dot_general(dov, v_ref[hf * t:(hf + 1) * t, :], NT, preferred_element_type=F32)
            ds = (p * (dp - delta) * ATT_SCALE).astype(BF16)
            dk_sc[hf] += lax.dot_general(ds, qv, TN, preferred_element_type=F32)
            return jnp.dot(ds, kt, preferred_element_type=F32)

        def q_tile(qb, modes):
            rows = pl.ds(pl.multiple_of(qb * t, t), t)
            qv = q_ref[rows, :]
            dov = do_ref[rows, :]
            st = st_ref[rows, :]
            lse2 = jnp.sum(jnp.where(lane == 2 * h, st, 0.0), axis=-1, keepdims=True) * LOG2E
            delta = jnp.sum(jnp.where(lane == 2 * h + 1, st, 0.0), axis=-1, keepdims=True)
            parts = [chain(hf, qv, dov, lse2, delta, modes[hf]) for hf in range(2) if modes[hf] is not None]
            dq_ref[rows, :] += parts[0] if len(parts) == 1 else parts[0] + parts[1]

        q_tile(2 * j, (True, None))
        q_tile(2 * j + 1, (False, True))

        def pair(qb0):
            q_tile(qb0, (False, False))
            q_tile(qb0 + 1, (False, False))

        def loop_body(i, carry):
            pair(2 * (j + 1) + 4 * i)
            pair(2 * (j + 1) + 4 * i + 2)
            return carry

        pairs = nq // 2 - j - 1
        lax.fori_loop(0, pairs // 2, loop_body, 0)

        @pl.when(pairs % 2 == 1)
        def _():
            pair(2 * (j + 1) + 4 * (pairs // 2))
        for hf in range(2):
            dk_ref[hf * t:(hf + 1) * t, :] = dk_sc[hf]
            dv_ref[hf * t:(hf + 1) * t, :] = dv_sc[hf]

        if nc:
            @pl.when((h == HEADS - 1) & (j == S // tk - 1))
            def _():
                comm.finish(c_src, c_out, c_sems)

    outs = pl.pallas_call(
        body, name=name, grid=(HEADS, S // tk),
        in_specs=[pl.BlockSpec((S, HQ), lambda h, j: (0, h)),
                  pl.BlockSpec((S, VDIM), lambda h, j: (0, h)),
                  pl.BlockSpec((S, 128), lambda h, j: (0, 0)),
                  pl.BlockSpec((tk, HQ), lambda h, j: (j, h)),
                  pl.BlockSpec((tk, VDIM), lambda h, j: (j, h))] + comm.specs,
        out_specs=[pl.BlockSpec((S, HQ), lambda h, j: (0, h)),
                   pl.BlockSpec((tk, HQ), lambda h, j: (j, h)),
                   pl.BlockSpec((tk, VDIM), lambda h, j: (j, h))] + comm.specs,
        out_shape=[_sds((S, HEADS * HQ)), _sds((S, HEADS * HQ)), _sds((S, ATT_W))] + comm.out_shape,
        scratch_shapes=[pltpu.VMEM((2, t, HQ), F32), pltpu.VMEM((2, t, VDIM), F32)] + comm.scratch,
        compiler_params=_cparams("arbitrary", "arbitrary"),
    )(q, do, stats, k, v, *comm.srcs)
    return outs[0], outs[1], outs[2], outs[3:]


def _att_prep_backward(z, pos, dq, dk, dv, q_g, kv_g, wq_p, w_ukv, rope_rows, name):
    S = z.shape[0]
    tm = ROW_TILE

    def body(z_ref, pos_ref, dq_ref, dk_ref, dv_ref, qg_ref, kvg_ref, wq_ref, wkv_ref, rope_ref,
             dz_ref, qn_ref, dqp_ref, kvn_ref, dkv_ref, dqg_ref, dkvg_ref):
        @pl.when(pl.program_id(0) == 0)
        def _():
            dqg_ref[...] = jnp.zeros(dqg_ref.shape, F32)
            dkvg_ref[...] = jnp.zeros(dkvg_ref.shape, F32)

        zz = z_ref[...].astype(F32)
        ql, kvl = zz[:, 0:Q_RANK], zz[:, Q_RANK:Q_RANK + KV_RANK]
        q_rstd = lax.rsqrt(jnp.mean(ql * ql, axis=-1, keepdims=True) + EPS)
        kv_rstd = lax.rsqrt(jnp.mean(kvl * kvl, axis=-1, keepdims=True) + EPS)
        qhat, kvhat = ql * q_rstd, kvl * kv_rstd
        qg, kvg = qg_ref[...], kvg_ref[...]
        qn_ref[...] = (qhat * qg).astype(BF16)
        kvn_ref[...] = (kvhat * kvg).astype(BF16)
        ct, st = _rope_tables(pos_ref, rope_ref)

        def unrotate(d):
            return d * ct + _swap_halves(d * st)

        dkrot = jnp.zeros((tm, NOPE), F32)
        for h in range(HEADS):
            b = h * HQ
            dqp_ref[:, b:b + NOPE] = dq_ref[:, b:b + NOPE].astype(BF16)
            dqp_ref[:, b + NOPE:b + HQ] = unrotate(dq_ref[:, b + NOPE:b + HQ]).astype(BF16)
            dkv_ref[:, b:b + NOPE] = dk_ref[:, b:b + NOPE].astype(BF16)
            dkv_ref[:, b + NOPE:b + HQ] = dv_ref[:, h * VDIM:(h + 1) * VDIM].astype(BF16)
            dkrot = dkrot + dk_ref[:, b + NOPE:b + HQ]
        dqn = lax.dot_general(dqp_ref[...], wq_ref[...], NT, preferred_element_type=F32)
        dkvn = lax.dot_general(dkv_ref[...], wkv_ref[...], NT, preferred_element_type=F32)
        dqg_ref[...] += jnp.sum(dqn * qhat, axis=0, keepdims=True)
        dkvg_ref[...] += jnp.sum(dkvn * kvhat, axis=0, keepdims=True)
        dqh, dkvh = dqn * qg, dkvn * kvg
        dql = q_rstd * (dqh - qhat * jnp.mean(dqh * qhat, axis=-1, keepdims=True))
        dkvl = kv_rstd * (dkvh - kvhat * jnp.mean(dkvh * kvhat, axis=-1, keepdims=True))
        dz_ref[:, 0:Q_RANK] = dql.astype(BF16)
        dz_ref[:, Q_RANK:Q_RANK + KV_RANK] = dkvl.astype(BF16)
        dz_ref[:, Q_RANK + KV_RANK:] = unrotate(dkrot).astype(BF16)

    W = HEADS * HQ
    return pl.pallas_call(
        body, name=name, grid=(S // tm,),
        in_specs=[_rows(tm, 512, 0), _rows(tm, 1), _rows(tm, W), _rows(tm, W), _rows(tm, ATT_W),
                  _full((1, Q_RANK)), _full((1, KV_RANK)), _full((Q_RANK, W)), _full((KV_RANK, W)), _full((8, 128))],
        out_specs=(_rows(tm, 512), _rows(tm, Q_RANK), _rows(tm, W), _rows(tm, KV_RANK), _rows(tm, W),
                   _full((1, Q_RANK)), _full((1, KV_RANK))),
        out_shape=(_sds((S, 512), BF16), _sds((S, Q_RANK), BF16), _sds((S, W), BF16), _sds((S, KV_RANK), BF16),
                   _sds((S, W), BF16), _sds((1, Q_RANK)), _sds((1, KV_RANK))),
        compiler_params=_cparams("arbitrary"),
    )(z, pos, dq, dk, dv, q_g, kv_g, wq_p, w_ukv, rope_rows)


def _conv_norm_backward(dyc, cv, ln_g, ln_b, w_pw2, name):
    S = cv.shape[0]
    tm = ROW_TILE

    def body(dy_ref, cv_ref, g_ref, be_ref, pw_ref, dcv_ref, sl_ref, dyb_ref, dg_ref, db_ref, dcb_ref):
        @pl.when(pl.program_id(0) == 0)
        def _():
            dg_ref[...] = jnp.zeros(dg_ref.shape, F32)
            db_ref[...] = jnp.zeros(db_ref.shape, F32)
            dcb_ref[...] = jnp.zeros(dcb_ref.shape, F32)

        cv_v = cv_ref[...]
        mu = jnp.mean(cv_v, axis=-1, keepdims=True)
        cc = cv_v - mu
        rstd = lax.rsqrt(jnp.mean(cc * cc, axis=-1, keepdims=True) + EPS)
        nh = cc * rstd
        g = g_ref[...]
        n = nh * g + be_ref[...]
        sil, dsil = _silu_and_grad(n)
        sl_ref[...] = sil.astype(BF16)
        dyb = dy_ref[...].astype(BF16)
        dyb_ref[...] = dyb
        dn = lax.dot_general(dyb, pw_ref[...], NT, preferred_element_type=F32) * dsil
        db_ref[...] += jnp.sum(dn, axis=0, keepdims=True)
        dg_ref[...] += jnp.sum(dn * nh, axis=0, keepdims=True)
        dnh = dn * g
        dcv = rstd * (dnh - jnp.mean(dnh, axis=-1, keepdims=True) - nh * jnp.mean(dnh * nh, axis=-1, keepdims=True))
        dcv_ref[...] = dcv
        dcb_ref[...] += jnp.sum(dcv, axis=0, keepdims=True)

    vec = _full((1, CONV_W))
    return pl.pallas_call(
        body, name=name, grid=(S // tm,),
        in_specs=[_rows(tm, CONV_W), _rows(tm, CONV_W), vec, vec, _full((CONV_W, CONV_W))],
        out_specs=(_rows(tm, CONV_W), _rows(tm, CONV_W), _rows(tm, CONV_W), vec, vec, vec),
        out_shape=(_sds((S, CONV_W)), _sds((S, CONV_W), BF16), _sds((S, CONV_W), BF16),
                   _sds((1, CONV_W)), _sds((1, CONV_W)), _sds((1, CONV_W))),
        compiler_params=_cparams("arbitrary"),
    )(dyc, cv, ln_g, ln_b, w_pw2)


def _conv_backward(z, dcv, conv_w_p, name):
    S = z.shape[0]
    tm = ROW_TILE
    per = tm // HALO
    last_halo = S // HALO - 1

    def body(a_ref, b_ref, ap_ref, bp_ref, d_ref, dn_ref, w_ref, da_ref, db_ref, dw_ref, win, dwin, dw_acc, sh, dsh):
        i = pl.program_id(0)

        @pl.when(i == 0)
        def _():
            dw_acc[...] = jnp.zeros(dw_acc.shape, F32)

        av, bv = a_ref[...].astype(F32), b_ref[...].astype(F32)
        _conv_window(win, ap_ref[...].astype(F32), bp_ref[...].astype(F32), av, bv, i == 0)
        dcur = d_ref[...]
        dwin[0:tm, :] = dcur
        dwin[tm:, :] = jnp.where(i == pl.num_programs(0) - 1, 0.0, dn_ref[...])
        _shifted_copies(win, sh, tm)
        _shifted_copies(dwin, dsh, tm)
        for r0 in range(0, tm, CONV_ROWS):
            dchunk = d_ref[r0:r0 + CONV_ROWS, :]
            dh = jnp.zeros((CONV_ROWS, CONV_W), F32)
            for kk in range(CONV_K):
                dh = dh + w_ref[kk:kk + 1, :] * _tap(dwin, dsh, r0 + CONV_K - 1 - kk, CONV_ROWS)
                prod = dchunk * _tap(win, sh, r0 + HALO - (CONV_K - 1) + kk, CONV_ROWS)
                dw_acc[kk] += jnp.sum(prod.reshape(CONV_ROWS // 8, 8, CONV_W), axis=0)
            sb = _sigmoid(b_ref[r0:r0 + CONV_ROWS, :].astype(F32))
            da_ref[r0:r0 + CONV_ROWS, :] = (dh * sb).astype(BF16)
            db_ref[r0:r0 + CONV_ROWS, :] = (dh * a_ref[r0:r0 + CONV_ROWS, :].astype(F32) * sb * (1.0 - sb)).astype(BF16)

        @pl.when(i == pl.num_programs(0) - 1)
        def _():
            dw_ref[...] = jnp.sum(dw_acc[...], axis=1)

    nxt = pl.BlockSpec((HALO, CONV_W), lambda i: (jnp.minimum((i + 1) * per, last_halo), 0))
    return pl.pallas_call(
        body, name=name, grid=(S // tm,),
        in_specs=_conv_in_specs(tm) + [_rows(tm, CONV_W), nxt, _full((HALO, CONV_W))],
        out_specs=(_rows(tm, CONV_W), _rows(tm, CONV_W), _full((HALO, CONV_W))),
        out_shape=(_sds((S, CONV_W), BF16), _sds((S, CONV_W), BF16), _sds((HALO, CONV_W))),
        scratch_shapes=[pltpu.VMEM((tm + HALO, CONV_W), F32), pltpu.VMEM((tm + HALO, CONV_W), F32),
                        pltpu.VMEM((HALO, 8, CONV_W), F32), pltpu.VMEM((7, tm + HALO - 8, CONV_W), F32),
                        pltpu.VMEM((7, tm + HALO - 8, CONV_W), F32)],
        compiler_params=_cparams("arbitrary"),
    )(z, z, z, z, dcv, dcv, conv_w_p)


def _sgu_backward(z, dy, ln_g, ln_b, w_s, bias_full, name):
    S = z.shape[0]
    tm = ROW_TILE

    def body(u_ref, v_ref, dy_ref, g_ref, be_ref, ws_ref, bias_ref, du_ref, dv_ref, dws_ref, dbs_ref, dg_ref, db_ref):
        @pl.when(pl.program_id(0) == 0)
        def _():
            dws_ref[...] = jnp.zeros(dws_ref.shape, F32)
            dbs_ref[...] = jnp.zeros(dbs_ref.shape, F32)
            dg_ref[...] = jnp.zeros(dg_ref.shape, F32)
            db_ref[...] = jnp.zeros(db_ref.shape, F32)

        gmask, tril = _sgu_masks()
        lane = lax.broadcasted_iota(jnp.int32, (1, 128), 1)
        wm = [jnp.where(tril, ws_ref[g], 0.0).astype(BF16) for g in range(SGU_G)]
        gain = g_ref[...]
        for ch in range(tm // SGU_T):
            rows = slice(ch * SGU_T, (ch + 1) * SGU_T)
            gu, dgu, dgv, rstd, nh, vn = _sgu_common(u_ref[rows, :].astype(F32), v_ref[rows, :].astype(F32), g_ref, be_ref)
            vb = vn.astype(BF16)
            sv = bias_ref[...]
            for g in range(SGU_G):
                sv = sv + jnp.where(gmask[g], jnp.dot(wm[g], vb, preferred_element_type=F32), 0.0)
            dyv = dy_ref[rows, :]
            du_ref[rows, :] = (dyv * sv * dgu).astype(BF16)
            dsv = dyv * gu
            dsvb = dsv.astype(BF16)
            dvn = jnp.zeros((SGU_T, SGU_W), F32)
            for g in range(SGU_G):
                dsg = jnp.where(gmask[g], dsv, 0.0)
                dwg = lax.dot_general(dsg.astype(BF16), vb, NT, preferred_element_type=F32)
                dws_ref[g] += jnp.where(tril, dwg, 0.0)
                dvn = dvn + jnp.where(gmask[g], lax.dot_general(wm[g], dsvb, TN, preferred_element_type=F32), 0.0)
                dbs_ref[...] += jnp.where(lane == g, jnp.sum(dsg, axis=-1, keepdims=True), 0.0)
            db_ref[...] += jnp.sum(dvn, axis=0, keepdims=True)
            dg_ref[...] += jnp.sum(dvn * nh, axis=0, keepdims=True)
            dnh = dvn * gain
            dgvv = rstd * (dnh - jnp.mean(dnh, axis=-1, keepdims=True) - nh * jnp.mean(dnh * nh, axis=-1, keepdims=True))
            dv_ref[rows, :] = (dgvv * dgv).astype(BF16)

    vec = _full((1, SGU_W))
    return pl.pallas_call(
        body, name=name, grid=(S // tm,),
        in_specs=[_rows(tm, SGU_W, 7), _rows(tm, SGU_W, 8), _rows(tm, SGU_W), vec, vec,
                  _full((SGU_G, SGU_T, SGU_T)), _full((SGU_T, SGU_W))],
        out_specs=(_rows(tm, SGU_W), _rows(tm, SGU_W), _full((SGU_G, SGU_T, SGU_T)), _full((SGU_T, 128)), vec, vec),
        out_shape=(_sds((S, SGU_W), BF16), _sds((S, SGU_W), BF16), _sds((SGU_G, SGU_T, SGU_T)), _sds((SGU_T, 128)),
                   _sds((1, SGU_W)), _sds((1, SGU_W))),
        compiler_params=_cparams("arbitrary"),
    )(z, z, dy, ln_g, ln_b, w_s, bias_full)


DZ_PIECES = (512, 512, 256, 256, 256, 256, 256, 256)


def _assemble_columns(dst_ref, pieces):
    off = 0
    for piece in pieces:
        wdt = piece.shape[1]
        dst_ref[:, off:off + wdt] = piece[...]
        off += wdt


def _inproj_backward(x, dxo, dz_pieces, g_pre, mod, w_in_p, name, comm=None):
    S = x.shape[0]
    tm = ROW_TILE
    comm = comm or _Comm([], [])
    nc = comm.n
    npc = len(DZ_PIECES)

    def body(*refs):
        x_ref, dxo_ref = refs[:2]
        pieces = refs[2:2 + npc]
        g_ref, mod_ref, w_ref = refs[2 + npc:5 + npc]
        c_src = refs[5 + npc:5 + npc + nc]
        dx_ref, dmod_ref, dg_ref = refs[5 + npc + nc:8 + npc + nc]
        c_out = refs[8 + npc + nc:8 + npc + 2 * nc]
        dz_sc = refs[8 + npc + 2 * nc]
        c_sems = refs[9 + npc + 2 * nc:]
        _riding(comm, S // tm, c_src, c_out, c_sems, "start")

        @pl.when(pl.program_id(0) == 0)
        def _():
            dmod_ref[...] = jnp.zeros(dmod_ref.shape, F32)
            dg_ref[...] = jnp.zeros(dg_ref.shape, F32)

        _assemble_columns(dz_sc, pieces)
        g = g_ref[...]
        one_scale = 1.0 + mod_ref[1:2, :]
        half = tm // 2
        for r0 in (0, half):
            rows = slice(r0, r0 + half)
            dh = lax.dot_general(dz_sc[rows, :], w_ref[...], NT, preferred_element_type=F32)
            xv = x_ref[rows, :]
            rstd = lax.rsqrt(jnp.mean(xv * xv, axis=-1, keepdims=True) + EPS)
            xhat = xv * rstd
            xg = xhat * g
            dmod_ref[0:1, :] += jnp.sum(dh, axis=0, keepdims=True)
            dmod_ref[1:2, :] += jnp.sum(dh * xg, axis=0, keepdims=True)
            dhs = dh * one_scale
            dg_ref[...] += jnp.sum(dhs * xhat, axis=0, keepdims=True)
            dxh = dhs * g
            dx_ref[rows, :] = dxo_ref[rows, :] + rstd * (dxh - xhat * jnp.mean(dxh * xhat, axis=-1, keepdims=True))
        _riding(comm, S // tm, c_src, c_out, c_sems, "finish")

    outs = pl.pallas_call(
        body, name=name, grid=(S // tm,),
        in_specs=[_rows(tm, D), _rows(tm, D)] + [_rows(tm, w) for w in DZ_PIECES]
                 + [_full((1, D)), _full((3, D)), _full((D, DZ))] + comm.specs,
        out_specs=[_rows(tm, D), _full((2, D)), _full((1, D))] + comm.specs,
        out_shape=[_sds((S, D)), _sds((2, D)), _sds((1, D))] + comm.out_shape,
        scratch_shapes=[pltpu.VMEM((tm, DZ), BF16)] + comm.scratch,
        compiler_params=_cparams("arbitrary"),
    )(x, dxo, *dz_pieces, g_pre, mod, w_in_p, *comm.srcs)
    return outs[0], outs[1], outs[2], outs[3:]


def _adamw(w, gparts, m, v, name):
    shape = w.shape
    cols = shape[-1]
    rows = int(np.prod(shape[:-1]))
    parts = gparts.shape[0]
    w2, m2, v2 = (a.reshape(rows, cols) for a in (w, m, v))
    g3 = gparts.reshape(parts, rows, cols)
    tr = rows
    for cand in (256, 128):
        if rows > cand and rows % cand == 0:
            tr = cand
            break

    def body(w_ref, g_ref, m_ref, v_ref, go_ref, d_ref, mo_ref, vo_ref):
        g = g_ref[0].astype(F32)
        for p in range(1, parts):
            g = g + g_ref[p].astype(F32)
        wv = w_ref[...]
        mn = ADAM_B1 * m_ref[...] + (1.0 - ADAM_B1) * g
        vn = ADAM_B2 * v_ref[...] + (1.0 - ADAM_B2) * (g * g)
        m_hat = mn / (1.0 - ADAM_B1 ** ADAM_STEP)
        v_hat = vn / (1.0 - ADAM_B2 ** ADAM_STEP)
        go_ref[...] = g
        d_ref[...] = -ADAM_LR * (m_hat / (jnp.sqrt(v_hat) + ADAM_EPS) + ADAM_WD * wv)
        mo_ref[...] = mn
        vo_ref[...] = vn

    blk = pl.BlockSpec((tr, cols), lambda i: (i, 0))
    outs = pl.pallas_call(
        body, name=name, grid=(rows // tr,),
        in_specs=[blk, pl.BlockSpec((parts, tr, cols), lambda i: (0, i, 0)), blk, blk],
        out_specs=(blk, blk, blk, blk),
        out_shape=tuple(_sds((rows, cols)) for _ in range(4)),
        compiler_params=_cparams("parallel"),
    )(w2, g3, m2, v2)
    return tuple(o.reshape(shape) for o in outs)


_GATHERED = ("w_in", "w_out", "w_uq", "w_ukv", "w_pw2", "conv_w")
_COL_SHARDED = ("w_in", "w_uq", "w_ukv", "conv_w")

_SMALL = (("dmod", (3 * D,)), ("g_pre", (D,)), ("g_post", (D,)), ("q_norm_g", (Q_RANK,)),
          ("kv_norm_g", (KV_RANK,)), ("conv_b", (CONV_W,)), ("conv_ln_g", (CONV_W,)),
          ("conv_ln_b", (CONV_W,)), ("sgu_ln_g", (SGU_W,)), ("sgu_ln_b", (SGU_W,)),
          ("w_s", (SGU_G, SGU_T, SGU_T)), ("b_s", (SGU_G, SGU_T)))


def _assemble(name, parts):
    if name in _COL_SHARDED:
        p = jnp.moveaxis(parts, 0, 1)
        return p.reshape(p.shape[0], p.shape[1] * p.shape[2])
    return parts.reshape(parts.shape[0] * parts.shape[1], parts.shape[2])


def _scatter_layout(name, full):
    if name in _COL_SHARDED:
        return jnp.moveaxis(full.reshape(full.shape[0], N_DEV, full.shape[1] // N_DEV), 1, 0)
    return full.reshape(N_DEV, full.shape[0] // N_DEV, full.shape[1])


def kernel(x, c, positions, w_ada, b_ada, g_pre, g_post, w_in, q_norm_g, w_uq, kv_norm_g, w_ukv, conv_w, conv_b, conv_ln_g, conv_ln_b, w_pw2, sgu_ln_g, sgu_ln_b, w_s, b_s, w_out, loss_target, m_w_ada, m_b_ada, m_g_pre, m_g_post, m_w_in, m_q_norm_g, m_w_uq, m_kv_norm_g, m_w_ukv, m_conv_w, m_conv_b, m_conv_ln_g, m_conv_ln_b, m_w_pw2, m_sgu_ln_g, m_sgu_ln_b, m_w_s, m_b_s, m_w_out, v_w_ada, v_b_ada, v_g_pre, v_g_post, v_w_in, v_q_norm_g, v_w_uq, v_kv_norm_g, v_w_ukv, v_conv_w, v_conv_b, v_conv_ln_g, v_conv_ln_b, v_w_pw2, v_sgu_ln_g, v_sgu_ln_b, v_w_s, v_b_s, v_w_out):
    weights = dict(w_ada=w_ada, b_ada=b_ada, g_pre=g_pre, g_post=g_post, w_in=w_in, q_norm_g=q_norm_g, w_uq=w_uq,
                   kv_norm_g=kv_norm_g, w_ukv=w_ukv, conv_w=conv_w, conv_b=conv_b, conv_ln_g=conv_ln_g,
                   conv_ln_b=conv_ln_b, w_pw2=w_pw2, sgu_ln_g=sgu_ln_g, sgu_ln_b=sgu_ln_b, w_s=w_s, b_s=b_s, w_out=w_out)
    m_in = dict(w_ada=m_w_ada, b_ada=m_b_ada, g_pre=m_g_pre, g_post=m_g_post, w_in=m_w_in, q_norm_g=m_q_norm_g,
                w_uq=m_w_uq, kv_norm_g=m_kv_norm_g, w_ukv=m_w_ukv, conv_w=m_conv_w, conv_b=m_conv_b,
                conv_ln_g=m_conv_ln_g, conv_ln_b=m_conv_ln_b, w_pw2=m_w_pw2, sgu_ln_g=m_sgu_ln_g,
                sgu_ln_b=m_sgu_ln_b, w_s=m_w_s, b_s=m_b_s, w_out=m_w_out)
    v_in = dict(w_ada=v_w_ada, b_ada=v_b_ada, g_pre=v_g_pre, g_post=v_g_post, w_in=v_w_in, q_norm_g=v_q_norm_g,
                w_uq=v_w_uq, kv_norm_g=v_kv_norm_g, w_ukv=v_w_ukv, conv_w=v_conv_w, conv_b=v_conv_b,
                conv_ln_g=v_conv_ln_g, conv_ln_b=v_conv_ln_b, w_pw2=v_w_pw2, sgu_ln_g=v_sgu_ln_g,
                sgu_ln_b=v_sgu_ln_b, w_s=v_w_s, b_s=v_b_s, w_out=v_w_out)
    order = list(weights)

    S = x.shape[1]
    me = 4 * lax.axis_index("x") + 2 * lax.axis_index("y") + lax.axis_index("c")
    x0 = x.reshape(S, D)
    target = loss_target.reshape(S, D)
    pos = positions.reshape(S, 1)

    def shards(l):
        return [weights[n][l].astype(BF16) for n in _GATHERED]

    def w_in_operand(part):
        w_in_f = _assemble("w_in", part)
        return jnp.concatenate([w_in_f[:, :ATT_IN], jnp.zeros((D, PAD_IN), BF16), w_in_f[:, ATT_IN:]], axis=1)

    def other_operands(parts):
        full = {n: _assemble(n, p) for n, p in zip(_GATHERED[1:], parts)}
        wq = jnp.pad(full["w_uq"].reshape(Q_RANK, HEADS, QK), ((0, 0), (0, 0), (0, HQ - QK)))
        return dict(w_uq=wq.reshape(Q_RANK, HEADS * HQ), w_ukv=full["w_ukv"], w_pw2=full["w_pw2"], w_out=full["w_out"],
                    conv_w=jnp.pad(full["conv_w"].astype(F32), ((0, HALO - CONV_K), (0, 0))))

    first_w_in, c_parts = _gather_two_level([shards(0)[0], c.reshape(8, D // 8)], name="gather_w_in_0")
    lw = [dict(w_in=w_in_operand(first_w_in))] + [None] * (DEPTH - 1)
    c_all = c_parts.reshape(N_DEV, D)

    ada_cols = w_ada.shape[-1]
    b_cols = lax.dynamic_slice_in_dim(b_ada, me * ada_cols, ada_cols, axis=1)
    sc_rows, mod_part = _ada_forward(jnp.pad(c_all, ((0, 8), (0, 0))), w_ada, b_cols)
    mod_recv = _exchange([jnp.moveaxis(mod_part[:, :N_DEV], 1, 0)], [True], name="exchange_mod")[0]
    mod = jnp.moveaxis(mod_recv, 0, 1).reshape(DEPTH, 3, D)

    bias_full = jnp.repeat(jnp.swapaxes(b_s, 1, 2), SGU_GD, axis=2)
    inv_freq = ROPE_THETA ** (-jnp.arange(0, ROPE, 2, dtype=F32) / ROPE)
    zeros32 = jnp.zeros((ROPE // 2,), F32)
    ones32 = jnp.ones((ROPE // 2,), F32)
    rope_rows = jnp.zeros((8, 128), F32)
    rope_rows = rope_rows.at[0].set(jnp.concatenate([inv_freq, inv_freq, zeros32, zeros32]))
    rope_rows = rope_rows.at[1].set(jnp.concatenate([ones32, ones32, zeros32, zeros32]))
    rope_rows = rope_rows.at[2].set(jnp.concatenate([-ones32, ones32, zeros32, zeros32]))

    def vec(a, l):
        return a[l].reshape(1, -1)

    saved = []
    xl = x0
    for l in range(DEPTH):
        w = lw[l]
        late = _Comm(shards(0)[1:], [False] * (len(_GATHERED) - 1)) if l == 0 else None
        z, h_b, arrived = _prenorm_inproj(xl, vec(g_pre, l), mod[l], w["w_in"], name=f"prenorm_inproj_{l}", comm=late)
        if late is not None:
            w.update(other_operands(arrived))
        q, k, v = _att_prep(z, pos, vec(q_norm_g, l), vec(kv_norm_g, l), w["w_uq"], w["w_ukv"], rope_rows,
                            name=f"att_prep_{l}")
        ahead = _Comm(shards(l + 1), [False] * len(_GATHERED)) if l + 1 < DEPTH else None
        y_att, lse, arrived = _flash_forward(q, k, v, name=f"flash_forward_{l}", comm=ahead)
        if ahead is not None:
            lw[l + 1] = dict(w_in=w_in_operand(arrived[0]), **other_operands(arrived[1:]))
        cv, y_conv = _conv_forward(z, w["conv_w"], vec(conv_b, l), vec(conv_ln_g, l), vec(conv_ln_b, l),
                                   w["w_pw2"], name=f"conv_forward_{l}")
        y_sgu = _sgu_forward(z, vec(sgu_ln_g, l), vec(sgu_ln_b, l), w_s[l], bias_full[l], name=f"sgu_forward_{l}")
        outs = _out_proj(xl, z, y_att, y_conv, y_sgu, w["w_out"], vec(g_post, l), mod[l], name=f"out_proj_{l}",
                         target=target if l == DEPTH - 1 else None)
        saved.append(dict(x=xl, h=h_b, z=z, q=q, k=k, v=v, y_att=y_att, lse=lse, cv=cv, y_conv=y_conv, y_sgu=y_sgu,
                          y=outs[1], ycat=outs[2]))
        xl = outs[0]

    dx = xl
    loss = lax.psum(outs[3].reshape(()), ("x", "y", "c"))

    spack = _Packer(_SMALL, 8)
    grad_kinds = [True] * len(_GATHERED) + [False]
    received = [None] * DEPTH
    pending = None
    for l in reversed(range(DEPTH)):
        sv = saved[l]
        w = lw[l]
        (dyb, dob, stats, dga, dyc, dgc, dys, dgs, dgate, dgpost) = _out_proj_backward(
            dx, sv["y"], sv["z"], sv["y_att"], sv["y_conv"], sv["y_sgu"], sv["lse"], w["w_out"],
            vec(g_post, l), mod[l], name=f"out_proj_backward_{l}")
        big = dict(w_out=_matmul_tn(sv["ycat"], dyb, name=f"grad_w_out_{l}"))
        riding = _Comm(pending, grad_kinds) if pending is not None else None
        dq, dk, dv, arrived = _flash_backward(sv["q"], sv["k"], sv["v"], dob, stats, name=f"flash_backward_{l}",
                                              comm=riding)
        if riding is not None:
            received[l + 1] = arrived
        dz_att, qn_b, dqp_b, kvn_b, dkv_b, dqg, dkvg = _att_prep_backward(
            sv["z"], pos, dq, dk, dv, vec(q_norm_g, l), vec(kv_norm_g, l), w["w_uq"], w["w_ukv"], rope_rows,
            name=f"att_prep_backward_{l}")
        dwq_p = _matmul_tn(qn_b, dqp_b, name=f"grad_w_uq_{l}")
        big["w_uq"] = dwq_p.reshape(Q_RANK, HEADS, HQ)[:, :, :QK].reshape(Q_RANK, HEADS * QK)
        big["w_ukv"] = _matmul_tn(kvn_b, dkv_b, name=f"grad_w_ukv_{l}")
        dcv, sl_b, dyc_b, dclg, dclb, dcb = _conv_norm_backward(dyc, sv["cv"], vec(conv_ln_g, l), vec(conv_ln_b, l),
                                                              w["w_pw2"], name=f"conv_norm_backward_{l}")
        big["w_pw2"] = _matmul_tn(sl_b, dyc_b, name=f"grad_w_pw2_{l}")
        dca, dcbb, dconvw = _conv_backward(sv["z"], dcv, w["conv_w"], name=f"conv_backward_{l}")
        big["conv_w"] = dconvw[:CONV_K]
        dsu, dsvv, dws, dbs, dslg, dslb = _sgu_backward(sv["z"], dys, vec(sgu_ln_g, l), vec(sgu_ln_b, l), w_s[l],
                                                       bias_full[l], name=f"sgu_backward_{l}")
        dz_pieces = [dz_att, dga, dca, dcbb, dgc, dsu, dsvv, dgs]
        rest = [_scatter_layout(n, big[n]).astype(BF16) for n in _GATHERED[1:]]
        early = _Comm(rest, grad_kinds[1:-1]) if l == 0 else None
        dwin_p = _matmul_tn(sv["h"], dz_pieces, name=f"grad_w_in_{l}", comm=early)
        if early is not None:
            dwin_p, rest_arrived = dwin_p
        dwin = jnp.concatenate([dwin_p[:, :ATT_IN], dwin_p[:, ATT_IN + PAD_IN:]], axis=1)
        dwin_send = _scatter_layout("w_in", dwin).astype(BF16)
        last = _Comm([dwin_send], grad_kinds[:1]) if l == 0 else None
        dx, dmod2, dgpre, dwin_arrived = _inproj_backward(sv["x"], dx, dz_pieces, vec(g_pre, l), mod[l], w["w_in"],
                                                          name=f"inproj_backward_{l}", comm=last)
        small = dict(dmod=jnp.concatenate([dmod2.reshape(-1), dgate.reshape(-1)]), g_pre=dgpre, g_post=dgpost,
                     q_norm_g=dqg, kv_norm_g=dkvg, conv_b=dcb, conv_ln_g=dclg, conv_ln_b=dclb, sgu_ln_g=dslg,
                     sgu_ln_b=dslb, w_s=dws, b_s=jnp.swapaxes(dbs[:, :SGU_G], 0, 1))
        pending = [dwin_send] + rest + [spack.pack(small, F32)]
    grad_x = dx.reshape(1, S, D)
    small_arrived = _gather_two_level(pending[-1:], name="gather_small_grads_0")
    received[0] = list(dwin_arrived) + list(rest_arrived) + list(small_arrived)

    gparts = {n: jnp.stack([received[l][i] for l in range(DEPTH)], axis=1) for i, n in enumerate(_GATHERED)}
    sparts = [spack.unpack(received[l][-1], (N_DEV,)) for l in range(DEPTH)]
    sparts = {n: jnp.stack([sparts[l][n] for l in range(DEPTH)], axis=1) for n, _ in _SMALL}
    dmod_all = sparts["dmod"]
    dmod_cols = lax.dynamic_slice_in_dim(dmod_all, me * ada_cols, ada_cols, axis=2)
    sc_t = jnp.pad(sc_rows[:N_DEV].T, ((0, 0), (0, 128 - N_DEV)))
    dmod_rows = jnp.pad(jnp.moveaxis(dmod_cols, 0, 1), ((0, 0), (0, 128 - N_DEV), (0, 0)))
    gparts["w_ada"] = _ada_backward(sc_t, dmod_rows)[None]
    gparts["b_ada"] = dmod_all
    for n, _ in _SMALL[1:]:
        gparts[n] = sparts[n]

    grads, deltas, new_m, new_v = {}, {}, {}, {}
    for n in order:
        grads[n], deltas[n], new_m[n], new_v[n] = _adamw(weights[n], gparts[n], m_in[n], v_in[n], name=f"adamw_{n}")
    return (loss, grad_x, *[grads[n] for n in order], *[deltas[n] for n in order],
            *[new_m[n] for n in order], *[new_v[n] for n in order])
```
